```python
import math
import jax, jax.numpy as jnp
from jax import lax
import numpy as np

D_MODEL = 1024
BATCH = 1
SEQ = 16384
DEPTH = 4

GRID_W = 64
CTX_LEN = 256
N_MIXERS = 4
N_GM = (DEPTH + 3) // 4
N_AT = (DEPTH + 2) // 4
N_HY = (DEPTH + 1) // 4
N_SSD = DEPTH // 4
NORM_EPS = 1e-6
FFN_HIDDEN = -(-8 * D_MODEL // (3 * 256)) * 256

GM_CHUNK = 128
GM_WIDTH = 2 * D_MODEL
GM_GROUPS = 8

AT_HEAD_DIM = 64
AT_Q_HEADS = D_MODEL // AT_HEAD_DIM
AT_KV_HEADS = 4
AT_Q_BLOCK = 128
ROPE_THETA = 10000.0

HY_ORDER = 2
HY_SHORT = 3
HY_BANDS = 16
HY_EMB = 1 + 2 * HY_BANDS
HY_FILT_W = 64
HY_FILT_OUT = 2 * (HY_ORDER - 1) * D_MODEL
HY_TARGET = 1e-2
HY_FAST_PCT = 0.3
HY_SLOW_PCT = 1.5
HY_MAX_DECAY = math.log(HY_TARGET) / HY_FAST_PCT
HY_MIN_DECAY = math.log(HY_TARGET) / HY_SLOW_PCT

SSD_INNER = 2 * D_MODEL
SSD_HEAD_DIM = 64
SSD_HEADS = SSD_INNER // SSD_HEAD_DIM
SSD_GROUPS = 4
SSD_STATE = 128
SSD_CONV = 3
SSD_CHUNK = 128
SSD_CONV_DIM = SSD_INNER + 2 * SSD_GROUPS * SSD_STATE
SSD_PROJ = SSD_INNER + SSD_CONV_DIM + 2 * SSD_HEADS

kernel_name = "hybrid_interleaved_dit_trunk"

F32 = jnp.float32


def rms_norm(x, g):
    xf = x.astype(F32)
    y = xf * lax.rsqrt(jnp.mean(xf * xf, axis=-1, keepdims=True) + NORM_EPS)
    return (y * g.astype(F32)).astype(x.dtype)


def layer_norm(x, g, b):
    xf = x.astype(F32)
    mu = jnp.mean(xf, axis=-1, keepdims=True)
    var = jnp.mean(jnp.square(xf - mu), axis=-1, keepdims=True)
    y = (xf - mu) * lax.rsqrt(var + NORM_EPS) * g.astype(F32) + b.astype(F32)
    return y.astype(x.dtype)


def modulate(h, shift, scale):
    return h * (1.0 + scale) + shift


def dwconv_centred(x, w, b):
    K = w.shape[0]
    L = x.shape[1]
    pad = K // 2
    xp = jnp.pad(x, ((0, 0), (pad, pad), (0, 0)))
    return sum(xp[:, k:k + L] * w[k] for k in range(K)) + b


def swiglu(h, w_in, w_out):
    g, u = jnp.split(h @ w_in, 2, axis=-1)
    return (jax.nn.silu(g) * u) @ w_out


def gmlp_mixer(h, w_in, ln_g, ln_b, ws, bs, w_out):
    B_, L, _ = h.shape
    u, v = jnp.split(jax.nn.gelu(h @ w_in, approximate=False), 2, axis=-1)
    v = layer_norm(v, ln_g, ln_b)
    v = v.reshape(B_, L // GM_CHUNK, GM_CHUNK, GM_GROUPS, GM_WIDTH // GM_GROUPS)
    v = jnp.einsum('gpq,bnqgc->bnpgc', ws, v) + bs.T[None, None, :, :, None]
    return (u * v.reshape(B_, L, GM_WIDTH)) @ w_out


def axial_rope(L):
    rows = L // GRID_W
    row = jnp.repeat(jnp.arange(rows, dtype=F32), GRID_W)
    col = jnp.tile(jnp.arange(GRID_W, dtype=F32), rows)
    n = AT_HEAD_DIM // 4
    inv = ROPE_THETA ** (-jnp.arange(n, dtype=F32) / n)
    ang = jnp.concatenate([row[:, None] * inv, col[:, None] * inv], axis=-1)
    return jnp.cos(ang), jnp.sin(ang)


def apply_rope(x, cos, sin):
    half = x.shape[-1] // 2
    x1, x2 = x[..., :half], x[..., half:]
    cs = cos[None, :, None, :].astype(x.dtype)
    sn = sin[None, :, None, :].astype(x.dtype)
    return jnp.concatenate([x1 * cs - x2 * sn, x2 * cs + x1 * sn], axis=-1)


def block_attention(q, k, v):
    B_, Lq = q.shape[:2]
    G = AT_Q_HEADS // AT_KV_HEADS
    nb = Lq // AT_Q_BLOCK
    qb = q.reshape(B_, nb, AT_Q_BLOCK, AT_KV_HEADS, G, AT_HEAD_DIM).transpose(1, 0, 2, 3, 4, 5)
    scale = AT_HEAD_DIM ** -0.5

    def one_block(qi):
        s = jnp.einsum('bqkgd,bskd->bkgqs', qi, k, preferred_element_type=F32) * scale
        p = jax.nn.softmax(s, axis=-1).astype(v.dtype)
        return jnp.einsum('bkgqs,bskd->bqkgd', p, v)

    o = lax.map(one_block, qb)
    return o.transpose(1, 0, 2, 3, 4, 5).reshape(B_, Lq, AT_Q_HEADS, AT_HEAD_DIM)


def attention_mixer(h_lat, h_ctx, w_qkv, q_g, k_g, w_out, want_ctx):
    def project(h):
        B_, L, _ = h.shape
        q, k, v = jnp.split(h @ w_qkv, [AT_Q_HEADS * AT_HEAD_DIM, (AT_Q_HEADS + AT_KV_HEADS) * AT_HEAD_DIM], axis=-1)
        q = rms_norm(q.reshape(B_, L, AT_Q_HEADS, AT_HEAD_DIM), q_g)
        k = rms_norm(k.reshape(B_, L, AT_KV_HEADS, AT_HEAD_DIM), k_g)
        return q, k, v.reshape(B_, L, AT_KV_HEADS, AT_HEAD_DIM)

    B_, L, _ = h_lat.shape
    q_l, k_l, v_l = project(h_lat)
    cos, sin = axial_rope(L)
    q_l, k_l = apply_rope(q_l, cos, sin), apply_rope(k_l, cos, sin)
    q_c, k_c, v_c = project(h_ctx)
    k_all = jnp.concatenate([k_c, k_l], axis=1)
    v_all = jnp.concatenate([v_c, v_l], axis=1)
    y_l = block_attention(q_l, k_all, v_all).reshape(B_, L, AT_Q_HEADS * AT_HEAD_DIM) @ w_out
    y_c = None
    if want_ctx:
        y_c = block_attention(q_c, k_c, v_c).reshape(B_, h_ctx.shape[1], AT_Q_HEADS * AT_HEAD_DIM) @ w_out
    return y_l, y_c


def hyena_filters(L, w1, b1, w2, b2, w3, freq):
    t_idx = jnp.arange(L, dtype=F32)
    t = t_idx / (L - 1)
    w = 2.0 * math.pi * t_idx / L
    f = jnp.linspace(1e-4, HY_BANDS - 1, HY_BANDS, dtype=F32)
    zf = w[:, None] * f[None, :]
    feats = jnp.concatenate([t[:, None], jnp.cos(zf), -jnp.sin(zf)], axis=-1)
    freq = freq.astype(F32)
    hid = jnp.sin(freq * (feats @ w1.astype(F32) + b1.astype(F32)))
    hid = jnp.sin(freq * (hid @ w2.astype(F32) + b2.astype(F32)))
    k = hid @ w3.astype(F32)
    deltas = jnp.abs(jnp.linspace(HY_MIN_DECAY, HY_MAX_DECAY, D_MODEL, dtype=F32))
    decay = jnp.exp(-t[:, None] * jnp.tile(deltas, 2)[None, :])
    return k * decay


def hyena_mixer(h, w_in, conv_w, conv_b, w1, b1, w2, b2, w3, freq, skip, w_out):
    B_, L, D = h.shape
    p = dwconv_centred(h @ w_in, conv_w, conv_b)
    x0, x1, v = jnp.split(p, 3, axis=-1)
    k = hyena_filters(L, w1, b1, w2, b2, w3, freq)
    k_circ = jnp.concatenate([k[:, :D], jnp.zeros((1, D), F32), jnp.flip(k[1:, D:], axis=0)], axis=0)
    u = (x1 * v).astype(F32)
    n = 2 * L
    y = jnp.fft.irfft(jnp.fft.rfft(u, n=n, axis=1) * jnp.fft.rfft(k_circ, n=n, axis=0)[None], n=n, axis=1)[:, :L]
    y = y + u * skip.astype(F32)
    return (x0 * y.astype(h.dtype)) @ w_out


def ssd_scan(x, dt, A, bm, cm, h0, need_y):
    B_, L = x.shape[:2]
    Q, G, R = SSD_CHUNK, SSD_GROUPS, SSD_HEADS // SSD_GROUPS
    nc = L // Q
    xf = x.astype(F32).reshape(B_, nc, Q, G, R, SSD_HEAD_DIM)
    dtf = dt.reshape(B_, nc, Q, G, R)
    bf = bm.astype(F32).reshape(B_, nc, Q, G, SSD_STATE)
    cf = cm.astype(F32).reshape(B_, nc, Q, G, SSD_STATE)
    acs = jnp.cumsum(dtf * A.reshape(G, R), axis=2)
    last = acs[:, :, -1]
    w_end = jnp.exp(last[:, :, None] - acs) * dtf
    states = jnp.einsum('bcjgn,bcjgr,bcjgrp->bcgrpn', bf, w_end, xf)

    def step(h, inp):
        st, dec = inp
        return h * jnp.exp(dec)[..., None, None] + st, h

    h_fin, h_start = lax.scan(step, h0, (jnp.moveaxis(states, 1, 0), jnp.moveaxis(last, 1, 0)))
    if not need_y:
        return None, h_fin
    h_start = jnp.moveaxis(h_start, 0, 1)
    order = jnp.tril(jnp.ones((Q, Q), bool))
    seg = acs[:, :, :, None] - acs[:, :, None, :]
    lmat = jnp.exp(jnp.where(order[:, :, None, None], seg, -jnp.inf))
    cb = jnp.einsum('bcign,bcjgn->bcijg', cf, bf)
    y_diag = jnp.einsum('bcijgr,bcjgrp->bcigrp', cb[..., None] * lmat * dtf[:, :, None], xf)
    y_off = jnp.einsum('bcign,bcgrpn->bcigrp', cf, h_start) * jnp.exp(acs)[..., None]
    return (y_diag + y_off).reshape(B_, L, SSD_HEADS, SSD_HEAD_DIM), h_fin


def ssd_mixer(h_lat, h_ctx, w_in, conv_w, conv_b, a_log, dt_bias, d_skip, norm_g, w_out, want_ctx):
    A = -jnp.exp(a_log.astype(F32))

    def prep(h):
        B_, L, _ = h.shape
        zg, xbc, dt = jnp.split(h @ w_in, [SSD_INNER, SSD_INNER + SSD_CONV_DIM], axis=-1)
        xbc = jax.nn.silu(dwconv_centred(xbc, conv_w, conv_b))
        xs, bm, cm = jnp.split(xbc, [SSD_INNER, SSD_INNER + SSD_GROUPS * SSD_STATE], axis=-1)
        dt = jax.nn.softplus(dt.reshape(B_, L, 2, SSD_HEADS).astype(F32) + dt_bias.astype(F32))
        return (zg, xs.reshape(B_, L, SSD_HEADS, SSD_HEAD_DIM),
                bm.reshape(B_, L, SSD_GROUPS, SSD_STATE), cm.reshape(B_, L, SSD_GROUPS, SSD_STATE), dt)

    z_l, x_l, b_l, c_l, dt_l = prep(h_lat)
    z_c, x_c, b_c, c_c, dt_c = prep(h_ctx)
    B_ = h_lat.shape[0]
    h0 = jnp.zeros((B_, SSD_GROUPS, SSD_HEADS // SSD_GROUPS, SSD_HEAD_DIM, SSD_STATE), F32)
    y_l, y_c = 0.0, 0.0
    for d in range(2):
        fl = (lambda t: jnp.flip(t, axis=1)) if d == 1 else (lambda t: t)
        yc_d, s_ctx = ssd_scan(fl(x_c), fl(dt_c[:, :, d]), A[d], fl(b_c), fl(c_c), h0, want_ctx)
        yl_d, _ = ssd_scan(fl(x_l), fl(dt_l[:, :, d]), A[d], fl(b_l), fl(c_l), s_ctx, True)
        y_l = y_l + fl(yl_d)
        if want_ctx:
            y_c = y_c + fl(yc_d)

    def finish(y, xs, zg):
        B2, L = xs.shape[:2]
        y = (y + xs.astype(F32) * d_skip.astype(F32)[:, None]).reshape(B2, L, SSD_INNER).astype(zg.dtype)
        y = (y * jax.nn.silu(zg)).reshape(B2, L, SSD_GROUPS, SSD_INNER // SSD_GROUPS)
        y = rms_norm(y, norm_g.reshape(SSD_GROUPS, SSD_INNER // SSD_GROUPS)).reshape(B2, L, SSD_INNER)
        return y @ w_out

    out_c = finish(y_c, x_c, z_c) if want_ctx else None
    return finish(y_l, x_l, z_l), out_c


def setup_inputs(seed: int = 0) -> dict:
    key = jax.random.key(seed)
    ks = iter(jax.random.split(key, 64))

    def nrm(shape, scale):
        return jax.random.normal(next(ks), shape, F32) * scale

    def gain(shape):
        return 1.0 + nrm(shape, 0.02)

    D = D_MODEL
    inp = {}
    inp["x"] = nrm((BATCH, SEQ, D), 1.0)
    inp["c"] = nrm((BATCH, D), 1.0)
    inp["ctx"] = nrm((BATCH, CTX_LEN, D), 1.0)
    inp["c_ctx"] = nrm((D,), 1.0)
    inp["norm1_g"] = gain((DEPTH, D))
    inp["norm2_g"] = gain((DEPTH, D))
    inp["mod_w"] = nrm((DEPTH, D, 6 * D), D ** -0.5)
    inp["mod_b"] = nrm((DEPTH, 6 * D), 0.02)
    inp["ffn_w_in"] = nrm((DEPTH, D, 2 * FFN_HIDDEN), D ** -0.5)
    inp["ffn_w_out"] = nrm((DEPTH, FFN_HIDDEN, D), FFN_HIDDEN ** -0.5)
    inp["final_g"] = gain((D,))
    inp["gm_w_in"] = nrm((N_GM, D, 2 * GM_WIDTH), D ** -0.5)
    inp["gm_ln_g"] = gain((N_GM, GM_WIDTH))
    inp["gm_ln_b"] = nrm((N_GM, GM_WIDTH), 0.02)
    inp["gm_ws"] = nrm((N_GM, GM_GROUPS, GM_CHUNK, GM_CHUNK), GM_CHUNK ** -0.5)
    inp["gm_bs"] = gain((N_GM, GM_GROUPS, GM_CHUNK))
    inp["gm_w_out"] = nrm((N_GM, GM_WIDTH, D), GM_WIDTH ** -0.5)
    inp["at_w_qkv"] = nrm((N_AT, D, (AT_Q_HEADS + 2 * AT_KV_HEADS) * AT_HEAD_DIM), D ** -0.5)
    inp["at_q_g"] = gain((N_AT, AT_HEAD_DIM))
    inp["at_k_g"] = gain((N_AT, AT_HEAD_DIM))
    inp["at_w_out"] = nrm((N_AT, AT_Q_HEADS * AT_HEAD_DIM, D), (AT_Q_HEADS * AT_HEAD_DIM) ** -0.5)
    inp["hy_w_in"] = nrm((N_HY, D, (HY_ORDER + 1) * D), D ** -0.5)
    inp["hy_conv_w"] = nrm((N_HY, HY_SHORT, (HY_ORDER + 1) * D), HY_SHORT ** -0.5)
    inp["hy_conv_b"] = nrm((N_HY, (HY_ORDER + 1) * D), 0.02)
    inp["hy_filt_w1"] = nrm((N_HY, HY_EMB, HY_FILT_W), HY_EMB ** -0.5)
    inp["hy_filt_b1"] = nrm((N_HY, HY_FILT_W), 0.1)
    inp["hy_filt_w2"] = nrm((N_HY, HY_FILT_W, HY_FILT_W), HY_FILT_W ** -0.5)
    inp["hy_filt_b2"] = nrm((N_HY, HY_FILT_W), 0.1)
    inp["hy_filt_w3"] = nrm((N_HY, HY_FILT_W, HY_FILT_OUT), 0.004)
    inp["hy_filt_freq"] = gain((N_HY, HY_FILT_W))
    inp["hy_skip"] = nrm((N_HY, D), 0.5)
    inp["hy_w_out"] = nrm((N_HY, D, D), D ** -0.5)
    inp["ssd_w_in"] = nrm((N_SSD, D, SSD_PROJ), D ** -0.5)
    inp["ssd_conv_w"] = nrm((N_SSD, SSD_CONV, SSD_CONV_DIM), SSD_CONV ** -0.5)
    inp["ssd_conv_b"] = nrm((N_SSD, SSD_CONV_DIM), 0.02)
    inp["ssd_a_log"] = jnp.log(jax.random.uniform(next(ks), (N_SSD, 2, SSD_HEADS), F32, 1.0, 16.0))
    dt0 = jnp.exp(jax.random.uniform(next(ks), (N_SSD, 2, SSD_HEADS), F32, math.log(1e-3), math.log(1e-1)))
    inp["ssd_dt_bias"] = dt0 + jnp.log(-jnp.expm1(-dt0))
    inp["ssd_d_skip"] = gain((N_SSD, SSD_HEADS))
    inp["ssd_norm_g"] = gain((N_SSD, SSD_INNER))
    inp["ssd_w_out"] = nrm((N_SSD, SSD_INNER, D), SSD_INNER ** -0.5)
    return inp


def reference(x, c, ctx, c_ctx, norm1_g, norm2_g, mod_w, mod_b, ffn_w_in, ffn_w_out, final_g,
              gm_w_in, gm_ln_g, gm_ln_b, gm_ws, gm_bs, gm_w_out,
              at_w_qkv, at_q_g, at_k_g, at_w_out,
              hy_w_in, hy_conv_w, hy_conv_b, hy_filt_w1, hy_filt_b1, hy_filt_w2, hy_filt_b2,
              hy_filt_w3, hy_filt_freq, hy_skip, hy_w_out,
              ssd_w_in, ssd_conv_w, ssd_conv_b, ssd_a_log, ssd_dt_bias, ssd_d_skip, ssd_norm_g, ssd_w_out):
    z = ctx
    for i in range(DEPTH):
        m, j = i % N_MIXERS, i // N_MIXERS
        want_ctx = i < DEPTH - 1
        ctx_reads = want_ctx or m in (1, 3)
        mod_l = (jax.nn.silu(c) @ mod_w[i] + mod_b[i])[:, None, :]
        mod_c = (jax.nn.silu(c_ctx) @ mod_w[i] + mod_b[i])[None, None, :]
        sh1, sc1, g1, sh2, sc2, g2 = jnp.split(mod_l, 6, axis=-1)
        csh1, csc1, cg1, csh2, csc2, cg2 = jnp.split(mod_c, 6, axis=-1)
        h_l = modulate(rms_norm(x, norm1_g[i]), sh1, sc1)
        h_c = modulate(rms_norm(z, norm1_g[i]), csh1, csc1) if ctx_reads else None
        if m == 0:
            p = (gm_w_in[j], gm_ln_g[j], gm_ln_b[j], gm_ws[j], gm_bs[j], gm_w_out[j])
            y_l = gmlp_mixer(h_l, *p)
            y_c = gmlp_mixer(h_c, *p) if want_ctx else None
        elif m == 1:
            y_l, y_c = attention_mixer(h_l, h_c, at_w_qkv[j], at_q_g[j], at_k_g[j], at_w_out[j], want_ctx)
        elif m == 2:
            p = (hy_w_in[j], hy_conv_w[j], hy_conv_b[j], hy_filt_w1[j], hy_filt_b1[j], hy_filt_w2[j],
                 hy_filt_b2[j], hy_filt_w3[j], hy_filt_freq[j], hy_skip[j], hy_w_out[j])
            y_l = hyena_mixer(h_l, *p)
            y_c = hyena_mixer(h_c, *p) if want_ctx else None
        else:
            y_l, y_c = ssd_mixer(h_l, h_c, ssd_w_in[j], ssd_conv_w[j], ssd_conv_b[j], ssd_a_log[j],
                                 ssd_dt_bias[j], ssd_d_skip[j], ssd_norm_g[j], ssd_w_out[j], want_ctx)
        x = x + g1 * y_l
        x = x + g2 * swiglu(modulate(rms_norm(x, norm2_g[i]), sh2, sc2), ffn_w_in[i], ffn_w_out[i])
        if want_ctx:
            z = z + cg1 * y_c
            z = z + cg2 * swiglu(modulate(rms_norm(z, norm2_g[i]), csh2, csc2), ffn_w_in[i], ffn_w_out[i])
    return rms_norm(x, final_g)
```

```python
import functools
import math

import numpy as np
import jax
import jax.numpy as jnp
from jax import lax
from jax.experimental import pallas as pl
from jax.experimental.pallas import tpu as pltpu

F32 = jnp.float32
BF16 = jnp.bfloat16
HIGHEST = lax.Precision.HIGHEST

D = 1024
DEPTH = 4
GRID_W = 64
NORM_EPS = 1e-6
FFN_HIDDEN = 2816
GM_CHUNK = 128
GM_WIDTH = 2 * D
GM_GROUPS = 8
GM_GW = GM_WIDTH // GM_GROUPS
HD = 64
QH = D // HD
KVH = 4
ROPE_THETA = 10000.0
HY_BANDS = 16
HY_EMB = 1 + 2 * HY_BANDS
HY_FILT_W = 64
HY_MAX_DECAY = math.log(1e-2) / 0.3
HY_MIN_DECAY = math.log(1e-2) / 1.5
SSD_INNER = 2 * D
SSD_P = 64
SSD_HEADS = SSD_INNER // SSD_P
SSD_GROUPS = 4
SSD_STATE = 128
SSD_CHUNK = 128
SSD_BC = SSD_GROUPS * SSD_STATE
SSD_CONV_DIM = SSD_INNER + 2 * SSD_BC

LANES = 128
SUBLANES = 8
VMEM_LIMIT_BYTES = 56 * 1024 * 1024
DFT_N2 = 128


def _cparams(*sem):
    return pltpu.CompilerParams(dimension_semantics=sem, vmem_limit_bytes=VMEM_LIMIT_BYTES)


def _row(v):
    return v.reshape(1, -1)


def _normmod(x, g, shift, scale):
    ms = jnp.mean(x * x, axis=-1, keepdims=True)
    return x * lax.rsqrt(ms + NORM_EPS) * g * (1.0 + scale) + shift


def _silu(x):
    return x * jax.nn.sigmoid(x)


def _mod_kernel(cl_ref, cc_ref, w_ref, b_ref, o_ref):
    w = w_ref[0]
    for r, c_ref in enumerate((cl_ref, cc_ref)):
        a = _silu(c_ref[...])
        o_ref[0, r:r + 1, :] = jnp.sum(a * w, axis=0, keepdims=True) + b_ref[0]


def _modulation(c, c_ctx, mod_w, mod_b):
    tn = 1536
    n6 = 6 * D
    depth = mod_w.shape[0]
    return pl.pallas_call(
        _mod_kernel,
        out_shape=jax.ShapeDtypeStruct((depth, 2, n6), F32),
        grid=(depth, n6 // tn),
        in_specs=[
            pl.BlockSpec((D, 1), lambda i, n: (0, 0)),
            pl.BlockSpec((D, 1), lambda i, n: (0, 0)),
            pl.BlockSpec((1, D, tn), lambda i, n: (i, 0, n)),
            pl.BlockSpec((1, 1, tn), lambda i, n: (i, 0, n)),
        ],
        out_specs=pl.BlockSpec((1, 2, tn), lambda i, n: (i, 0, n)),
        compiler_params=_cparams("parallel", "parallel"),
        name="modulation",
    )(c.reshape(D, 1), c_ctx.reshape(D, 1), mod_w, mod_b.reshape(depth, 1, n6))


def _ffn_kernel(x_ref, g_ref, sh_ref, sc_ref, gate_ref, wg_ref, wu_ref, wo_ref, fg_ref, o_ref, h_scr, acc_scr,
                *, nk, final):
    k = pl.program_id(1)

    @pl.when(k == 0)
    def _():
        h_scr[...] = _normmod(x_ref[...], g_ref[...], sh_ref[...], sc_ref[...]).astype(BF16)
        acc_scr[...] = jnp.zeros_like(acc_scr)

    h = h_scr[...]
    a = jnp.dot(h, wg_ref[...], preferred_element_type=F32)
    u = jnp.dot(h, wu_ref[...], preferred_element_type=F32)
    act = (_silu(a) * u).astype(BF16)
    acc_scr[...] += jnp.dot(act, wo_ref[...], preferred_element_type=F32)

    @pl.when(k == nk - 1)
    def _():
        y = x_ref[...] + gate_ref[...] * acc_scr[...]
        if final:
            ms = jnp.mean(y * y, axis=-1, keepdims=True)
            y = y * lax.rsqrt(ms + NORM_EPS) * fg_ref[...]
        o_ref[...] = y


def _ffn(x, g, sh, sc, gate, w_in, w_out, final_g, final):
    lx = x.shape[0]
    tm = min(1024, lx)
    tf = 256
    nk = FFN_HIDDEN // tf
    vec = pl.BlockSpec((1, D), lambda i, k: (0, 0))
    return pl.pallas_call(
        functools.partial(_ffn_kernel, nk=nk, final=final),
        out_shape=jax.ShapeDtypeStruct((lx, D), F32),
        grid=(lx // tm, nk),
        in_specs=[
            pl.BlockSpec((tm, D), lambda i, k: (i, 0)),
            vec, vec, vec, vec,
            pl.BlockSpec((D, tf), lambda i, k: (0, k)),
            pl.BlockSpec((D, tf), lambda i, k: (0, nk + k)),
            pl.BlockSpec((tf, D), lambda i, k: (k, 0)),
            vec,
        ],
        out_specs=pl.BlockSpec((tm, D), lambda i, k: (i, 0)),
        scratch_shapes=[pltpu.VMEM((tm, D), BF16), pltpu.VMEM((tm, D), F32)],
        compiler_params=_cparams("parallel", "arbitrary"),
        name="ffn",
    )(x, _row(g), _row(sh), _row(sc), _row(gate), w_in, w_in, w_out, _row(final_g))


def _outproj_kernel(x_ref, a_ref, w_ref, gate_ref, o_ref):
    y = jnp.dot(a_ref[...], w_ref[...], preferred_element_type=F32)
    o_ref[...] = x_ref[...] + gate_ref[...] * y


def _outproj(x, a, w, gate):
    lx, kin = a.shape
    tm = min(512, lx)
    return pl.pallas_call(
        _outproj_kernel,
        out_shape=jax.ShapeDtypeStruct((lx, D), F32),
        grid=(lx // tm,),
        in_specs=[
            pl.BlockSpec((tm, D), lambda i: (i, 0)),
            pl.BlockSpec((tm, kin), lambda i: (i, 0)),
            pl.BlockSpec((kin, D), lambda i: (0, 0)),
            pl.BlockSpec((1, D), lambda i: (0, 0)),
        ],
        out_specs=pl.BlockSpec((tm, D), lambda i: (i, 0)),
        compiler_params=_cparams("parallel"),
        name="outproj",
    )(x, a, w, _row(gate))


def _gmlp_kernel(x_ref, g_ref, sh_ref, sc_ref, gate_ref, win_ref, lng_ref, lnb_ref, ws_ref, bs_ref, wout_ref,
                 o_ref, *, tm):
    x = x_ref[...]
    h = _normmod(x, g_ref[...], sh_ref[...], sc_ref[...]).astype(BF16)
    t = jnp.dot(h, win_ref[...], preferred_element_type=F32)
    t = 0.5 * t * (1.0 + lax.erf(t * (1.0 / math.sqrt(2.0))))
    u = t[:, :GM_WIDTH]
    v = t[:, GM_WIDTH:]
    mu = jnp.mean(v, axis=-1, keepdims=True)
    vc = v - mu
    var = jnp.mean(vc * vc, axis=-1, keepdims=True)
    v = (vc * lax.rsqrt(var + NORM_EPS) * lng_ref[...] + lnb_ref[...]).astype(BF16)
    rows = []
    for q in range(tm // GM_CHUNK):
        cols = []
        for gidx in range(GM_GROUPS):
            vq = v[q * GM_CHUNK:(q + 1) * GM_CHUNK, gidx * GM_GW:(gidx + 1) * GM_GW]
            bias = bs_ref[gidx]
            m = jnp.dot(ws_ref[gidx], vq, preferred_element_type=F32)
            cols.append(m + jnp.concatenate([bias] * (GM_GW // LANES), axis=1))
        rows.append(jnp.concatenate(cols, axis=1))
    vm = jnp.concatenate(rows, axis=0)
    gated = (u * vm).astype(BF16)
    y = jnp.dot(gated, wout_ref[...], preferred_element_type=F32)
    o_ref[...] = x + gate_ref[...] * y


def _gmlp(x, g, sh, sc, gate, w_in, ln_g, ln_b, ws, bs, w_out):
    lx = x.shape[0]
    tm = min(256, lx)
    vec = pl.BlockSpec((1, D), lambda i: (0, 0))
    vecw = pl.BlockSpec((1, GM_WIDTH), lambda i: (0, 0))
    bsb = jnp.broadcast_to(bs[:, :, None], (GM_GROUPS, GM_CHUNK, LANES))
    return pl.pallas_call(
        functools.partial(_gmlp_kernel, tm=tm),
        out_shape=jax.ShapeDtypeStruct((lx, D), F32),
        grid=(lx // tm,),
        in_specs=[
            pl.BlockSpec((tm, D), lambda i: (i, 0)),
            vec, vec, vec, vec,
            pl.BlockSpec((D, 2 * GM_WIDTH), lambda i: (0, 0)),
            vecw, vecw,
            pl.BlockSpec((GM_GROUPS, GM_CHUNK, GM_CHUNK), lambda i: (0, 0, 0)),
            pl.BlockSpec((GM_GROUPS, GM_CHUNK, LANES), lambda i: (0, 0, 0)),
            pl.BlockSpec((GM_WIDTH, D), lambda i: (0, 0)),
        ],
        out_specs=pl.BlockSpec((tm, D), lambda i: (i, 0)),
        compiler_params=_cparams("parallel"),
        name="gmlp",
    )(x, _row(g), _row(sh), _row(sc), _row(gate), w_in, _row(ln_g), _row(ln_b), ws, bsb, w_out)


def _group_sumsq(t, e_ref):
    sq = t * t
    hi = sq.astype(BF16)
    lo = (sq - hi.astype(F32)).astype(BF16)
    outs = []
    for j in range(t.shape[1] // LANES):
        sl = slice(j * LANES, (j + 1) * LANES)
        outs.append(jnp.dot(hi[:, sl], e_ref[...], preferred_element_type=F32)
                    + jnp.dot(lo[:, sl], e_ref[...], preferred_element_type=F32))
    return jnp.concatenate(outs, axis=1)


def _rope(t, cosf, sinf):
    w = t.shape[1]
    lane = lax.broadcasted_iota(jnp.int32, t.shape, 1)
    first = (lane % HD) < (HD // 2)
    partner = jnp.where(first, pltpu.roll(t, w - HD // 2, axis=1), pltpu.roll(t, HD // 2, axis=1))
    reps = w // LANES
    c = jnp.concatenate([cosf] * reps, axis=1)
    s = jnp.concatenate([sinf] * reps, axis=1)
    return t * c + partner * s


def _qkv_kernel(x_ref, g_ref, sh_ref, sc_ref, w_ref, qg_ref, kg_ref, e_ref, cos_ref, sin_ref,
                q_ref, kt_ref, v_ref, *, rope):
    h = _normmod(x_ref[...], g_ref[...], sh_ref[...], sc_ref[...]).astype(BF16)
    qkv = jnp.dot(h, w_ref[...], preferred_element_type=F32)
    q = qkv[:, :D]
    k = qkv[:, D:D + KVH * HD]
    v = qkv[:, D + KVH * HD:]
    q = q * lax.rsqrt(_group_sumsq(q, e_ref) * (1.0 / HD) + NORM_EPS) * qg_ref[...]
    k = k * lax.rsqrt(_group_sumsq(k, e_ref) * (1.0 / HD) + NORM_EPS) * kg_ref[...]
    if rope:
        q = _rope(q, cos_ref[...], sin_ref[...])
        k = _rope(k, cos_ref[...], sin_ref[...])
    q_ref[...] = (q * (HD ** -0.5)).astype(BF16)
    kt_ref[...] = k.T.astype(BF16)
    v_ref[...] = v.astype(BF16)


def _qkv(x, g, sh, sc, w_qkv, q_g, k_g, rope):
    lx = x.shape[0]
    tm = min(256, lx)
    vec = pl.BlockSpec((1, D), lambda i: (0, 0))
    kvw = KVH * HD
    rows = lx // GRID_W
    row = jnp.repeat(jnp.arange(rows, dtype=F32), GRID_W)
    col = jnp.tile(jnp.arange(GRID_W, dtype=F32), rows)
    n = HD // 4
    inv = ROPE_THETA ** (-jnp.arange(n, dtype=F32) / n)
    ang = jnp.concatenate([row[:, None] * inv, col[:, None] * inv], axis=-1)
    cos, sin = jnp.cos(ang), jnp.sin(ang)
    cosf = jnp.tile(jnp.concatenate([cos, cos], axis=-1), (1, LANES // HD))
    sinf = jnp.tile(jnp.concatenate([-sin, sin], axis=-1), (1, LANES // HD))
    eblk = jnp.asarray(np.kron(np.eye(LANES // HD), np.ones((HD, HD))), BF16)
    tab = pl.BlockSpec((tm, LANES), lambda i: (i, 0))
    return pl.pallas_call(
        functools.partial(_qkv_kernel, rope=rope),
        out_shape=(jax.ShapeDtypeStruct((lx, D), BF16),
                   jax.ShapeDtypeStruct((kvw, lx), BF16),
                   jax.ShapeDtypeStruct((lx, kvw), BF16)),
        grid=(lx // tm,),
        in_specs=[
            pl.BlockSpec((tm, D), lambda i: (i, 0)),
            vec, vec, vec,
            pl.BlockSpec((D, D + 2 * kvw), lambda i: (0, 0)),
            vec,
            pl.BlockSpec((1, kvw), lambda i: (0, 0)),
            pl.BlockSpec((LANES, LANES), lambda i: (0, 0)),
            tab, tab,
        ],
        out_specs=(pl.BlockSpec((tm, D), lambda i: (i, 0)),
                   pl.BlockSpec((kvw, tm), lambda i: (0, i)),
                   pl.BlockSpec((tm, kvw), lambda i: (i, 0))),
        compiler_params=_cparams("parallel"),
        name="qkv_proj",
    )(x, _row(g), _row(sh), _row(sc), w_qkv, _row(jnp.tile(q_g, QH)), _row(jnp.tile(k_g, KVH)), eblk, cosf, sinf)


def _flash_kernel(q_ref, kt_ref, v_ref, o_ref, qs_scr, m_scr, l_scr, acc_scr, *, tq, ts, nkv):
    j = pl.program_id(1)
    gq = QH // KVH
    mrows = gq * tq

    @pl.when(j == 0)
    def _():
        q = q_ref[...]
        for h in range(QH):
            qs_scr[h // gq, (h % gq) * tq:(h % gq + 1) * tq, :] = q[:, h * HD:(h + 1) * HD]
        m_scr[...] = jnp.full(m_scr.shape, -jnp.inf, F32)
        l_scr[...] = jnp.zeros_like(l_scr)
        acc_scr[...] = jnp.zeros_like(acc_scr)

    for g in range(KVH):
        s = jnp.dot(qs_scr[g], kt_ref[g * HD:(g + 1) * HD, :], preferred_element_type=F32)
        m_prev = m_scr[g]
        m_new = jnp.maximum(m_prev, jnp.max(s, axis=1, keepdims=True))
        alpha = jnp.exp(m_prev - m_new)
        p = jnp.exp(s - pltpu.repeat(m_new, ts // LANES, axis=1))
        l_scr[g] = alpha * l_scr[g] + jnp.sum(p, axis=1, keepdims=True)
        vp = v_ref[:, (g // 2) * LANES:(g // 2 + 1) * LANES]
        acc_scr[g] = alpha * acc_scr[g] + jnp.dot(p.astype(BF16), vp, preferred_element_type=F32)
        m_scr[g] = m_new

    @pl.when(j == nkv - 1)
    def _():
        for g in range(KVH):
            o = acc_scr[g] / l_scr[g]
            o = o[:, (g % 2) * HD:(g % 2 + 1) * HD]
            for r in range(gq):
                h = g * gq + r
                o_ref[:, h * HD:(h + 1) * HD] = o[r * tq:(r + 1) * tq, :].astype(o_ref.dtype)


def _flash(q, kt, v, s_lo, s_len, ts):
    lq = q.shape[0]
    tq = min(128, lq)
    nkv = s_len // ts
    off = s_lo // ts
    gq = QH // KVH
    kvw = KVH * HD
    return pl.pallas_call(
        functools.partial(_flash_kernel, tq=tq, ts=ts, nkv=nkv),
        out_shape=jax.ShapeDtypeStruct((lq, D), BF16),
        grid=(lq // tq, nkv),
        in_specs=[
            pl.BlockSpec((tq, D), lambda i, j: (i, 0)),
            pl.BlockSpec((kvw, ts), lambda i, j: (0, off + j)),
            pl.BlockSpec((ts, kvw), lambda i, j: (off + j, 0)),
        ],
        out_specs=pl.BlockSpec((tq, D), lambda i, j: (i, 0)),
        scratch_shapes=[
            pltpu.VMEM((KVH, gq * tq, HD), BF16),
            pltpu.VMEM((KVH, gq * tq, LANES), F32),
            pltpu.VMEM((KVH, gq * tq, LANES), F32),
            pltpu.VMEM((KVH, gq * tq, LANES), F32),
        ],
        compiler_params=_cparams("parallel", "arbitrary"),
        name="flash_attn",
    )(q, kt, v)


def _halo_specs(tm, lx):
    nb = lx // SUBLANES
    step = tm // SUBLANES
    prev = pl.BlockSpec((SUBLANES, D), lambda i: (jnp.maximum(i * step - 1, 0), 0))
    nxt = pl.BlockSpec((SUBLANES, D), lambda i: (jnp.minimum((i + 1) * step, nb - 1), 0))
    return prev, nxt


def _conv3(p_main, p_halo, cw, cb, first, last):
    tm = p_main.shape[0]
    rid = lax.broadcasted_iota(jnp.int32, p_main.shape, 0)
    before = jnp.where(first, 0.0, p_halo[SUBLANES - 1:SUBLANES, :])
    after = jnp.where(last, 0.0, p_halo[SUBLANES:SUBLANES + 1, :])
    up = jnp.where(rid == 0, before, pltpu.roll(p_main, 1, axis=0))
    dn = jnp.where(rid == tm - 1, after, pltpu.roll(p_main, tm - 1, axis=0))
    return cw[0:1, :] * up + cw[1:2, :] * p_main + cw[2:3, :] * dn + cb


def _norm_halo(xm_ref, xp_ref, xn_ref, g_ref, sh_ref, sc_ref):
    g, sh, sc = g_ref[...], sh_ref[...], sc_ref[...]
    h = _normmod(xm_ref[...], g, sh, sc).astype(BF16)
    hh = jnp.concatenate([_normmod(xp_ref[...], g, sh, sc), _normmod(xn_ref[...], g, sh, sc)], axis=0).astype(BF16)
    return h, hh


def _hy_in_kernel(xm_ref, xp_ref, xn_ref, g_ref, sh_ref, sc_ref, w_ref, cw_ref, cb_ref, x0_ref, ut_ref, *, nt):
    i = pl.program_id(0)
    first, last = i == 0, i == nt - 1
    h, hh = _norm_halo(xm_ref, xp_ref, xn_ref, g_ref, sh_ref, sc_ref)

    def branch(b):
        sl = slice(b * D, (b + 1) * D)
        pm = jnp.dot(h, w_ref[:, sl], preferred_element_type=F32)
        ph = jnp.dot(hh, w_ref[:, sl], preferred_element_type=F32)
        return _conv3(pm, ph, cw_ref[:, sl], cb_ref[:, sl], first, last)

    x0_ref[...] = branch(0)
    u = branch(1) * branch(2)
    ut_ref[...] = u.T


def _hy_in(x, g, sh, sc, w_in, conv_w, conv_b):
    lx = x.shape[0]
    tm = min(512, lx)
    nt = lx // tm
    vec = pl.BlockSpec((1, D), lambda i: (0, 0))
    prev, nxt = _halo_specs(tm, lx)
    return pl.pallas_call(
        functools.partial(_hy_in_kernel, nt=nt),
        out_shape=(jax.ShapeDtypeStruct((lx, D), F32), jax.ShapeDtypeStruct((D, lx), F32)),
        grid=(nt,),
        in_specs=[
            pl.BlockSpec((tm, D), lambda i: (i, 0)), prev, nxt,
            vec, vec, vec,
            pl.BlockSpec((D, 3 * D), lambda i: (0, 0)),
            pl.BlockSpec((3, 3 * D), lambda i: (0, 0)),
            pl.BlockSpec((1, 3 * D), lambda i: (0, 0)),
        ],
        out_specs=(pl.BlockSpec((tm, D), lambda i: (i, 0)), pl.BlockSpec((D, tm), lambda i: (0, i))),
        compiler_params=_cparams("parallel"),
        name="hyena_in",
    )(x, x, x, _row(g), _row(sh), _row(sc), w_in, conv_w, _row(conv_b))


def _hy_filter_kernel(w1_ref, b1_ref, w2_ref, b2_ref, w3_ref, fr_ref, dl_ref, sk_ref, kt_ref, *, tm, ltrue):
    i = pl.program_id(0)
    shape = (tm, LANES)
    pos = (lax.broadcasted_iota(jnp.int32, shape, 0) + i * tm).astype(F32)
    lane = lax.broadcasted_iota(jnp.int32, shape, 1)
    t = pos / (ltrue - 1)
    w = 2.0 * math.pi * pos / ltrue
    band = jnp.where(lane <= HY_BANDS, lane - 1, lane - 1 - HY_BANDS).astype(F32)
    f = 1e-4 + band * ((HY_BANDS - 1 - 1e-4) / (HY_BANDS - 1))
    zf = w * f
    feats = jnp.where(lane == 0, t, jnp.where(lane <= HY_BANDS, jnp.cos(zf),
                                              jnp.where(lane <= 2 * HY_BANDS, -jnp.sin(zf), 0.0)))
    fr = fr_ref[...]
    hid = jnp.sin(fr * (jnp.dot(feats, w1_ref[...], preferred_element_type=F32, precision=HIGHEST) + b1_ref[...]))
    hid = jnp.sin(fr * (jnp.dot(hid, w2_ref[...], preferred_element_type=F32, precision=HIGHEST) + b2_ref[...]))
    k = jnp.dot(hid, w3_ref[...], preferred_element_type=F32, precision=HIGHEST)
    k = k * jnp.exp(-t[:, 0:1] * dl_ref[...])
    rid = lax.broadcasted_iota(jnp.int32, k.shape, 0) + i * tm
    cid = lax.broadcasted_iota(jnp.int32, k.shape, 1)
    k = jnp.where(rid == 0, jnp.where(cid < D, k + sk_ref[...], 0.0), k)
    k = jnp.where(rid < ltrue, k, 0.0)
    kt_ref[...] = k.T


def _hy_filter(ltrue, lpad, w1, b1, w2, b2, w3, freq, skip):
    tm = min(512, lpad)
    w1p = jnp.zeros((LANES, HY_FILT_W), F32).at[:HY_EMB].set(w1)
    deltas = jnp.abs(jnp.linspace(HY_MIN_DECAY, HY_MAX_DECAY, D, dtype=F32))
    full = lambda a: pl.BlockSpec(a.shape, lambda i: (0,) * a.ndim)
    args = (w1p, _row(b1), w2, _row(b2), w3, _row(freq), _row(jnp.tile(deltas, 2)),
            _row(jnp.concatenate([skip, jnp.zeros((D,), F32)])))
    return pl.pallas_call(
        functools.partial(_hy_filter_kernel, tm=tm, ltrue=ltrue),
        out_shape=jax.ShapeDtypeStruct((2 * D, lpad), F32),
        grid=(lpad // tm,),
        in_specs=[full(a) for a in args],
        out_specs=pl.BlockSpec((2 * D, tm), lambda i: (0, i)),
        compiler_params=_cparams("parallel"),
        name="hyena_filter",
    )(*args)


def _dft_consts(nh):
    n1 = 2 * nh
    n = n1 * DFT_N2
    k1 = np.arange(n1)[:, None].astype(np.float64)
    a1 = 2.0 * np.pi * k1 * np.arange(nh)[None, :] / n1
    f1 = np.concatenate([np.cos(a1), -np.sin(a1)], axis=0)
    at = 2.0 * np.pi * ((np.arange(n1)[:, None] * np.arange(DFT_N2)[None, :]) % n) / n
    a2 = 2.0 * np.pi * ((np.arange(DFT_N2)[:, None] * np.arange(DFT_N2)[None, :]) % DFT_N2) / DFT_N2
    c2, s2 = np.cos(a2), np.sin(a2)
    f2 = np.block([[c2, -s2], [s2, c2]])
    g2 = np.block([[c2, s2], [-s2, c2]])
    g1 = np.concatenate([np.cos(a1).T, -np.sin(a1).T], axis=1) / n
    as32 = lambda a: jnp.asarray(a, F32)
    return as32(f1), as32(np.cos(at)), as32(np.sin(at)), as32(f2), as32(g2), as32(g1)


def _dft_fwd(xs, f1, twc, tws, f2, cb, n1):
    xcat = jnp.concatenate(xs, axis=1)
    a = jnp.dot(f1, xcat, preferred_element_type=F32, precision=HIGHEST)
    rows = []
    for c in range(cb):
        ar = a[:n1, c * DFT_N2:(c + 1) * DFT_N2]
        ai = a[n1:, c * DFT_N2:(c + 1) * DFT_N2]
        rows.append(jnp.concatenate([ar * twc + ai * tws, ai * twc - ar * tws], axis=1))
    return jnp.dot(jnp.concatenate(rows, axis=0), f2, preferred_element_type=F32, precision=HIGHEST)


def _hy_spec_kernel(x_ref, f1_ref, twc_ref, tws_ref, f2_ref, o_ref, *, cb, n1):
    xs = [x_ref[c] for c in range(cb)]
    spec = _dft_fwd(xs, f1_ref[...], twc_ref[...], tws_ref[...], f2_ref[...], cb, n1)
    o_ref[...] = spec.reshape(cb, n1, 2 * DFT_N2)


def _hy_conv_kernel(x_ref, hf_ref, hb_ref, f1_ref, twc_ref, tws_ref, f2_ref, g2_ref, g1_ref, o_ref, *, cb, n1):
    twc, tws = twc_ref[...], tws_ref[...]
    xs = [x_ref[c] for c in range(cb)]
    spec = _dft_fwd(xs, f1_ref[...], twc, tws, f2_ref[...], cb, n1)
    hf = hf_ref[...].reshape(cb * n1, 2 * DFT_N2)
    hb = hb_ref[...].reshape(cb * n1, 2 * DFT_N2)
    hr = hf[:, :DFT_N2] + hb[:, :DFT_N2]
    hi = hf[:, DFT_N2:] - hb[:, DFT_N2:]
    xr, xi = spec[:, :DFT_N2], spec[:, DFT_N2:]
    y = jnp.concatenate([xr * hr - xi * hi, xr * hi + xi * hr], axis=1)
    b = jnp.dot(y, g2_ref[...], preferred_element_type=F32, precision=HIGHEST)
    cols = []
    for c in range(cb):
        br = b[c * n1:(c + 1) * n1, :DFT_N2]
        bi = b[c * n1:(c + 1) * n1, DFT_N2:]
        cols.append(jnp.concatenate([br * twc - bi * tws, bi * twc + br * tws], axis=0))
    out = jnp.dot(g1_ref[...], jnp.concatenate(cols, axis=1), preferred_element_type=F32, precision=HIGHEST)
    for c in range(cb):
        o_ref[c] = out[:, c * DFT_N2:(c + 1) * DFT_N2]


def _hy_longconv(ut, kt, lpad):
    nh = lpad // DFT_N2
    n1 = 2 * nh
    cb = 8
    f1, twc, tws, f2, g2, g1 = _dft_consts(nh)
    consts = (f1, twc, tws, f2)
    full = lambda a: pl.BlockSpec(a.shape, lambda i: (0,) * a.ndim)
    k3 = kt.reshape(2 * D, nh, DFT_N2)
    hspec = pl.pallas_call(
        functools.partial(_hy_spec_kernel, cb=cb, n1=n1),
        out_shape=jax.ShapeDtypeStruct((2 * D, n1, 2 * DFT_N2), F32),
        grid=(2 * D // cb,),
        in_specs=[pl.BlockSpec((cb, nh, DFT_N2), lambda i: (i, 0, 0))] + [full(a) for a in consts],
        out_specs=pl.BlockSpec((cb, n1, 2 * DFT_N2), lambda i: (i, 0, 0)),
        compiler_params=_cparams("parallel"),
        name="hyena_filter_spectrum",
    )(k3, *consts)
    u3 = ut.reshape(D, nh, DFT_N2)
    nb = D // cb
    y3 = pl.pallas_call(
        functools.partial(_hy_conv_kernel, cb=cb, n1=n1),
        out_shape=jax.ShapeDtypeStruct((D, nh, DFT_N2), F32),
        grid=(nb,),
        in_specs=[pl.BlockSpec((cb, nh, DFT_N2), lambda i: (i, 0, 0)),
                  pl.BlockSpec((cb, n1, 2 * DFT_N2), lambda i: (i, 0, 0)),
                  pl.BlockSpec((cb, n1, 2 * DFT_N2), lambda i: (i + nb, 0, 0))]
                 + [full(a) for a in consts + (g2, g1)],
        out_specs=pl.BlockSpec((cb, nh, DFT_N2), lambda i: (i, 0, 0)),
        compiler_params=_cparams("parallel"),
        name="hyena_longconv",
    )(u3, hspec, hspec, *consts, g2, g1)
    return y3.reshape(D, lpad)


def _hy_out_kernel(x_ref, x0_ref, yt_ref, w_ref, gate_ref, o_ref):
    a = (x0_ref[...] * yt_ref[...].T).astype(BF16)
    y = jnp.dot(a, w_ref[...], preferred_element_type=F32)
    o_ref[...] = x_ref[...] + gate_ref[...] * y


def _hy_out(x, x0, yt, w_out, gate):
    lx = x.shape[0]
    tm = min(512, lx)
    return pl.pallas_call(
        _hy_out_kernel,
        out_shape=jax.ShapeDtypeStruct((lx, D), F32),
        grid=(lx // tm,),
        in_specs=[
            pl.BlockSpec((tm, D), lambda i: (i, 0)),
            pl.BlockSpec((tm, D), lambda i: (i, 0)),
            pl.BlockSpec((D, tm), lambda i: (0, i)),
            pl.BlockSpec((D, D), lambda i: (0, 0)),
            pl.BlockSpec((1, D), lambda i: (0, 0)),
        ],
        out_specs=pl.BlockSpec((tm, D), lambda i: (i, 0)),
        compiler_params=_cparams("parallel"),
        name="hyena_out",
    )(x, x0, yt, w_out, _row(gate))


def _hyena(x, g, sh, sc, gate, w_in, conv_w, conv_b, w1, b1, w2, b2, w3, freq, skip, w_out):
    lx = x.shape[0]
    lpad = max(lx, SUBLANES * DFT_N2)
    x0, ut = _hy_in(x, g, sh, sc, w_in, conv_w, conv_b)
    if lpad != lx:
        ut = jnp.pad(ut, ((0, 0), (0, lpad - lx)))
    kt = _hy_filter(lx, lpad, w1, b1, w2, b2, w3, freq, skip)
    yt = _hy_longconv(ut, kt, lpad)[:, :lx]
    return _hy_out(x, x0, yt, w_out, gate)


def _ssd_in_kernel(xm_ref, xp_ref, xn_ref, g_ref, sh_ref, sc_ref, wz_ref, wx_ref, wd_ref, cw_ref, cb_ref, db_ref,
                   zg_ref, xs_ref, bm_ref, cm_ref, dt_ref, *, nt):
    i = pl.program_id(0)
    first, last = i == 0, i == nt - 1
    h, hh = _norm_halo(xm_ref, xp_ref, xn_ref, g_ref, sh_ref, sc_ref)
    zg_ref[...] = jnp.dot(h, wz_ref[...], preferred_element_type=F32)
    pm = jnp.dot(h, wx_ref[...], preferred_element_type=F32)
    ph = jnp.dot(hh, wx_ref[...], preferred_element_type=F32)
    xbc = _silu(_conv3(pm, ph, cw_ref[...], cb_ref[...], first, last))
    xs_ref[...] = xbc[:, :SSD_INNER]
    bm_ref[...] = xbc[:, SSD_INNER:SSD_INNER + SSD_BC]
    cm_ref[...] = xbc[:, SSD_INNER + SSD_BC:]
    dt = jnp.dot(h, wd_ref[...], preferred_element_type=F32) + db_ref[...]
    dt = jnp.maximum(dt, 0.0) + jnp.log1p(jnp.exp(-jnp.abs(dt)))
    lane = lax.broadcasted_iota(jnp.int32, dt.shape, 1)
    dt = jnp.where((lane % LANES) < SSD_HEADS, dt, 0.0)
    dt_ref[0] = dt[:, :LANES]
    dt_ref[1] = dt[:, LANES:]


def _ssd_in(x, g, sh, sc, w_in, conv_w, conv_b, dt_bias):
    lx = x.shape[0]
    tm = min(256, lx)
    nt = lx // tm
    vec = pl.BlockSpec((1, D), lambda i: (0, 0))
    prev, nxt = _halo_specs(tm, lx)
    wz = w_in[:, :SSD_INNER]
    wx = w_in[:, SSD_INNER:SSD_INNER + SSD_CONV_DIM]
    wdt = w_in[:, SSD_INNER + SSD_CONV_DIM:]
    pad = LANES - SSD_HEADS
    wd = jnp.concatenate([jnp.pad(wdt[:, :SSD_HEADS], ((0, 0), (0, pad))),
                          jnp.pad(wdt[:, SSD_HEADS:], ((0, 0), (0, pad)))], axis=1)
    db = jnp.pad(dt_bias, ((0, 0), (0, pad))).reshape(1, 2 * LANES)
    full = lambda a: pl.BlockSpec(a.shape, lambda i: (0,) * a.ndim)
    rowblk = lambda w: pl.BlockSpec((tm, w), lambda i: (i, 0))
    return pl.pallas_call(
        functools.partial(_ssd_in_kernel, nt=nt),
        out_shape=(jax.ShapeDtypeStruct((lx, SSD_INNER), F32), jax.ShapeDtypeStruct((lx, SSD_INNER), F32),
                   jax.ShapeDtypeStruct((lx, SSD_BC), F32), jax.ShapeDtypeStruct((lx, SSD_BC), F32),
                   jax.ShapeDtypeStruct((2, lx, LANES), F32)),
        grid=(nt,),
        in_specs=[rowblk(D), prev, nxt, vec, vec, vec, full(wz), full(wx), full(wd),
                  pl.BlockSpec((3, SSD_CONV_DIM), lambda i: (0, 0)),
                  pl.BlockSpec((1, SSD_CONV_DIM), lambda i: (0, 0)),
                  pl.BlockSpec((1, 2 * LANES), lambda i: (0, 0))],
        out_specs=(rowblk(SSD_INNER), rowblk(SSD_INNER), rowblk(SSD_BC), rowblk(SSD_BC),
                   pl.BlockSpec((2, tm, LANES), lambda i: (0, i, 0))),
        compiler_params=_cparams("parallel"),
        name="ssd_in",
    )(x, x, x, _row(g), _row(sh), _row(sc), wz, wx, wd, conv_w, _row(conv_b), db)


def _ssd_scan_kernel(xs_ref, bm_ref, cm_ref, dt_ref, a_ref, tri_ref, h0_ref, y_ref, hfin_ref, h_scr, *, nc, need_y):
    s = pl.program_id(1)
    q = SSD_CHUNK
    npair = SSD_HEADS // 2
    ppg = npair // SSD_GROUPS

    @pl.when(s == 0)
    def _():
        h_scr[...] = h0_ref[0]

    tri = tri_ref[0]
    keep = tri > 0.5
    dt = dt_ref[0]
    a = dt * a_ref[0]
    acs = jnp.dot(tri, a, preferred_element_type=F32, precision=HIGHEST)
    total = jnp.sum(a, axis=0, keepdims=True)
    wend = jnp.exp(total - acs) * dt
    eacs = jnp.exp(acs)
    etot = jnp.exp(total)
    acs_t = acs.T
    dt_t = dt.T
    lane = lax.broadcasted_iota(jnp.int32, (q, LANES), 1)
    sub = lax.broadcasted_iota(jnp.int32, (q, LANES), 0)
    left = lane < SSD_P
    top = sub < SSD_P

    def colb(arr, hd):
        return jnp.broadcast_to(arr[:, hd:hd + 1], (q, LANES))

    for g in range(SSD_GROUPS):
        bg = bm_ref[:, g * SSD_STATE:(g + 1) * SSD_STATE]
        cg = cm_ref[:, g * SSD_STATE:(g + 1) * SSD_STATE].astype(BF16)
        bgb = bg.astype(BF16)
        if need_y:
            cb = lax.dot_general(cg, bgb, (((1,), (1,)), ((), ())), preferred_element_type=F32)
        for r in range(ppg):
            pidx = g * ppg + r
            h0i, h1i = 2 * pidx, 2 * pidx + 1
            xp = xs_ref[:, pidx * LANES:(pidx + 1) * LANES]
            hs = h_scr[pidx]
            if need_y:
                xpb = xp.astype(BF16)
                yd = []
                for hd in (h0i, h1i):
                    seg = colb(acs, hd) - acs_t[hd:hd + 1, :]
                    lm = jnp.exp(jnp.where(keep, seg, -jnp.inf))
                    m = (cb * lm * dt_t[hd:hd + 1, :]).astype(BF16)
                    yd.append(jnp.dot(m, xpb, preferred_element_type=F32))
                ydiag = jnp.where(left, yd[0], yd[1])
                yoff = lax.dot_general(cg, hs.astype(BF16), (((1,), (1,)), ((), ())), preferred_element_type=F32)
                yoff = yoff * jnp.where(left, colb(eacs, h0i), colb(eacs, h1i))
                y_ref[0, :, pidx * LANES:(pidx + 1) * LANES] = ydiag + yoff
            xw = (xp * jnp.where(left, colb(wend, h0i), colb(wend, h1i))).astype(BF16)
            st = lax.dot_general(xw, bgb, (((0,), (0,)), ((), ())), preferred_element_type=F32)
            dec = jnp.where(top, jnp.broadcast_to(etot[:, h0i:h0i + 1], (q, LANES)),
                            jnp.broadcast_to(etot[:, h1i:h1i + 1], (q, LANES)))
            h_scr[pidx] = hs * dec + st

    @pl.when(s == nc - 1)
    def _():
        hfin_ref[0] = h_scr[...]


def _ssd_scan(xs, bm, cm, dt2, a_log, h0, need_y):
    lx = xs.shape[0]
    q = SSD_CHUNK
    nc = lx // q
    npair = SSD_HEADS // 2
    a = -jnp.exp(a_log.astype(F32))
    a_pad = jnp.pad(a, ((0, 0), (0, LANES - SSD_HEADS))).reshape(2, 1, LANES)
    lower = np.tril(np.ones((q, q), np.float32))
    tri = jnp.asarray(np.stack([lower, lower.T]))
    chunk = lambda d, s: jnp.where(d == 0, s, nc - 1 - s)
    out_shape = [jax.ShapeDtypeStruct((2, lx if need_y else q, SSD_INNER), F32),
                 jax.ShapeDtypeStruct((2, npair, 2 * SSD_P, SSD_STATE), F32)]
    y_spec = (pl.BlockSpec((1, q, SSD_INNER), lambda d, s: (d, chunk(d, s), 0)) if need_y
              else pl.BlockSpec((1, q, SSD_INNER), lambda d, s: (d, 0, 0)))
    st_spec = pl.BlockSpec((1, npair, 2 * SSD_P, SSD_STATE), lambda d, s: (d, 0, 0, 0))
    y, hfin = pl.pallas_call(
        functools.partial(_ssd_scan_kernel, nc=nc, need_y=need_y),
        out_shape=out_shape,
        grid=(2, nc),
        in_specs=[
            pl.BlockSpec((q, SSD_INNER), lambda d, s: (chunk(d, s), 0)),
            pl.BlockSpec((q, SSD_BC), lambda d, s: (chunk(d, s), 0)),
            pl.BlockSpec((q, SSD_BC), lambda d, s: (chunk(d, s), 0)),
            pl.BlockSpec((1, q, LANES), lambda d, s: (d, chunk(d, s), 0)),
            pl.BlockSpec((1, 1, LANES), lambda d, s: (d, 0, 0)),
            pl.BlockSpec((1, q, q), lambda d, s: (d, 0, 0)),
            st_spec,
        ],
        out_specs=[y_spec, st_spec],
        scratch_shapes=[pltpu.VMEM((npair, 2 * SSD_P, SSD_STATE), F32)],
        compiler_params=_cparams("arbitrary", "arbitrary"),
        name="ssd_scan",
    )(xs, bm, cm, dt2, a_pad, tri, h0)
    return (y if need_y else None), hfin


def _ssd_out_kernel(x_ref, y_ref, xs_ref, zg_ref, dsk_ref, ng_ref, w_ref, gate_ref, o_ref):
    y = y_ref[0] + y_ref[1] + xs_ref[...] * dsk_ref[...]
    y = y * _silu(zg_ref[...])
    gw = SSD_INNER // SSD_GROUPS
    parts = []
    for g in range(SSD_GROUPS):
        yg = y[:, g * gw:(g + 1) * gw]
        ms = jnp.mean(yg * yg, axis=-1, keepdims=True)
        parts.append(yg * lax.rsqrt(ms + NORM_EPS) * ng_ref[:, g * gw:(g + 1) * gw])
    yn = jnp.concatenate(parts, axis=1).astype(BF16)
    o_ref[...] = x_ref[...] + gate_ref[...] * jnp.dot(yn, w_ref[...], preferred_element_type=F32)


def _ssd_out(x, y2, xs, zg, d_skip, norm_g, w_out, gate):
    lx = x.shape[0]
    tm = min(256, lx)
    rowblk = lambda w: pl.BlockSpec((tm, w), lambda i: (i, 0))
    vecw = pl.BlockSpec((1, SSD_INNER), lambda i: (0, 0))
    return pl.pallas_call(
        _ssd_out_kernel,
        out_shape=jax.ShapeDtypeStruct((lx, D), F32),
        grid=(lx // tm,),
        in_specs=[rowblk(D), pl.BlockSpec((2, tm, SSD_INNER), lambda i: (0, i, 0)), rowblk(SSD_INNER),
                  rowblk(SSD_INNER), vecw, vecw, pl.BlockSpec((SSD_INNER, D), lambda i: (0, 0)),
                  pl.BlockSpec((1, D), lambda i: (0, 0))],
        out_specs=rowblk(D),
        compiler_params=_cparams("parallel"),
        name="ssd_out",
    )(x, y2, xs, zg, _row(jnp.repeat(d_skip, SSD_P)), _row(norm_g), w_out, _row(gate))


def kernel(x, c, ctx, c_ctx, norm1_g, norm2_g, mod_w, mod_b, ffn_w_in, ffn_w_out, final_g, gm_w_in, gm_ln_g, gm_ln_b, gm_ws, gm_bs, gm_w_out, at_w_qkv, at_q_g, at_k_g, at_w_out, hy_w_in, hy_conv_w, hy_conv_b, hy_filt_w1, hy_filt_b1, hy_filt_w2, hy_filt_b2, hy_filt_w3, hy_filt_freq, hy_skip, hy_w_out, ssd_w_in, ssd_conv_w, ssd_conv_b, ssd_a_log, ssd_dt_bias, ssd_d_skip, ssd_norm_g, ssd_w_out):
    batch, seq, _ = x.shape
    assert batch == 1, "kernels are written for a single sequence"
    nctx = ctx.shape[1]
    xl = x[0]
    z = ctx[0]
    mods = _modulation(c[0], c_ctx, mod_w, mod_b)
    bf = lambda w: w.astype(BF16)

    for i in range(DEPTH):
        m, j = i % 4, i // 4
        want_ctx = i < DEPTH - 1
        ml = [mods[i, 0, k * D:(k + 1) * D] for k in range(6)]
        mc = [mods[i, 1, k * D:(k + 1) * D] for k in range(6)]
        n1 = norm1_g[i]
        if m == 0:
            p = (bf(gm_w_in[j]), gm_ln_g[j], gm_ln_b[j], bf(gm_ws[j]), gm_bs[j], bf(gm_w_out[j]))
            xl = _gmlp(xl, n1, ml[0], ml[1], ml[2], *p)
            if want_ctx:
                z = _gmlp(z, n1, mc[0], mc[1], mc[2], *p)
        elif m == 1:
            wq, wo = bf(at_w_qkv[j]), bf(at_w_out[j])
            q_l, kt_l, v_l = _qkv(xl, n1, ml[0], ml[1], wq, at_q_g[j], at_k_g[j], rope=True)
            q_c, kt_c, v_c = _qkv(z, n1, mc[0], mc[1], wq, at_q_g[j], at_k_g[j], rope=False)
            kt_all = jnp.concatenate([kt_c, kt_l], axis=1)
            v_all = jnp.concatenate([v_c, v_l], axis=0)
            stot = nctx + seq
            ts = next(t for t in (1280, 1024, 512, 256) if stot % t == 0)
            o_l = _flash(q_l, kt_all, v_all, 0, stot, ts)
            xl = _outproj(xl, o_l, wo, ml[2])
            if want_ctx:
                o_c = _flash(q_c, kt_all, v_all, 0, nctx, nctx)
                z = _outproj(z, o_c, wo, mc[2])
        elif m == 2:
            p = (bf(hy_w_in[j]), hy_conv_w[j], hy_conv_b[j], hy_filt_w1[j], hy_filt_b1[j], hy_filt_w2[j],
                 hy_filt_b2[j], hy_filt_w3[j], hy_filt_freq[j], hy_skip[j], bf(hy_w_out[j]))
            xl = _hyena(xl, n1, ml[0], ml[1], ml[2], *p)
            if want_ctx:
                z = _hyena(z, n1, mc[0], mc[1], mc[2], *p)
        else:
            win, wo = bf(ssd_w_in[j]), bf(ssd_w_out[j])
            pin = (win, ssd_conv_w[j], ssd_conv_b[j], ssd_dt_bias[j])
            zg_c, xs_c, bm_c, cm_c, dt_c = _ssd_in(z, n1, mc[0], mc[1], *pin)
            zg_l, xs_l, bm_l, cm_l, dt_l = _ssd_in(xl, n1, ml[0], ml[1], *pin)
            h0 = jnp.zeros((2, SSD_HEADS // 2, 2 * SSD_P, SSD_STATE), F32)
            y_c, h_ctx = _ssd_scan(xs_c, bm_c, cm_c, dt_c, ssd_a_log[j], h0, want_ctx)
            y_l, _ = _ssd_scan(xs_l, bm_l, cm_l, dt_l, ssd_a_log[j], h_ctx, True)
            xl = _ssd_out(xl, y_l, xs_l, zg_l, ssd_d_skip[j], ssd_norm_g[j], wo, ml[2])
            if want_ctx:
                z = _ssd_out(z, y_c, xs_c, zg_c, ssd_d_skip[j], ssd_norm_g[j], wo, mc[2])
        wi, wo2 = bf(ffn_w_in[i]), bf(ffn_w_out[i])
        xl = _ffn(xl, norm2_g[i], ml[3], ml[4], ml[5], wi, wo2, final_g, final=(i == DEPTH - 1))
        if want_ctx:
            z = _ffn(z, norm2_g[i], mc[3], mc[4], mc[5], wi, wo2, final_g, final=False)
    return xl[None]
```

```python
import functools
import math

import numpy as np
import jax
import jax.numpy as jnp
from jax import lax
from jax.experimental import pallas as pl
from jax.experimental.pallas import tpu as pltpu

F32 = jnp.float32
BF16 = jnp.bfloat16
HIGHEST = lax.Precision.HIGHEST

D = 1024
DEPTH = 4
GRID_W = 64
NORM_EPS = 1e-6
FFN_HIDDEN = 2816
GM_CHUNK = 128
GM_WIDTH = 2 * D
GM_GROUPS = 8
GM_GW = GM_WIDTH // GM_GROUPS
HD = 64
QH = D // HD
KVH = 4
ROPE_THETA = 10000.0
LOG2E = math.log2(math.e)
FLASH_LOOKAHEAD = 2
VT_ROWS = HD + 16
HY_BANDS = 16
HY_EMB = 1 + 2 * HY_BANDS
HY_FILT_W = 64
HY_MAX_DECAY = math.log(1e-2) / 0.3
HY_MIN_DECAY = math.log(1e-2) / 1.5
SSD_INNER = 2 * D
SSD_P = 64
SSD_HEADS = SSD_INNER // SSD_P
SSD_GROUPS = 4
SSD_STATE = 128
SSD_CHUNK = 128
SSD_BC = SSD_GROUPS * SSD_STATE
SSD_CONV_DIM = SSD_INNER + 2 * SSD_BC

LANES = 128
SUBLANES = 8
VMEM_LIMIT_BYTES = 56 * 1024 * 1024
DFT_N2 = 128
DFT_SPLIT = 1


def _cparams(*sem):
    return pltpu.CompilerParams(dimension_semantics=sem, vmem_limit_bytes=VMEM_LIMIT_BYTES)


def _row(v):
    return v.reshape(1, -1)


def _normmod(x, g, shift, scale):
    ms = jnp.mean(x * x, axis=-1, keepdims=True)
    return x * lax.rsqrt(ms + NORM_EPS) * g * (1.0 + scale) + shift


def _silu(x):
    return x * jax.nn.sigmoid(x)


def _mod_kernel(cl_ref, cc_ref, w_ref, b_ref, o_ref):
    w = w_ref[0]
    for r, c_ref in enumerate((cl_ref, cc_ref)):
        a = _silu(c_ref[...])
        o_ref[0, r:r + 1, :] = jnp.sum(a * w, axis=0, keepdims=True) + b_ref[0]


def _modulation(c, c_ctx, mod_w, mod_b):
    tn = 1536
    n6 = 6 * D
    depth = mod_w.shape[0]
    return pl.pallas_call(
        _mod_kernel,
        out_shape=jax.ShapeDtypeStruct((depth, 2, n6), F32),
        grid=(depth, n6 // tn),
        in_specs=[
            pl.BlockSpec((D, 1), lambda i, n: (0, 0)),
            pl.BlockSpec((D, 1), lambda i, n: (0, 0)),
            pl.BlockSpec((1, D, tn), lambda i, n: (i, 0, n)),
            pl.BlockSpec((1, 1, tn), lambda i, n: (i, 0, n)),
        ],
        out_specs=pl.BlockSpec((1, 2, tn), lambda i, n: (i, 0, n)),
        compiler_params=_cparams("parallel", "parallel"),
        name="modulation",
    )(c.reshape(D, 1), c_ctx.reshape(D, 1), mod_w, mod_b.reshape(depth, 1, n6))


def _ffn_kernel(x_ref, g_ref, sh_ref, sc_ref, gate_ref, wi_ref, wo_ref, fg_ref, o_ref, *, final):
    x = x_ref[...]
    h = _normmod(x, g_ref[...], sh_ref[...], sc_ref[...]).astype(BF16)
    a = jnp.dot(h, wi_ref[:, :FFN_HIDDEN], preferred_element_type=F32)
    u = jnp.dot(h, wi_ref[:, FFN_HIDDEN:], preferred_element_type=F32)
    act = (_silu(a) * u).astype(BF16)
    y = x + gate_ref[...] * jnp.dot(act, wo_ref[...], preferred_element_type=F32)
    if final:
        ms = jnp.mean(y * y, axis=-1, keepdims=True)
        y = y * lax.rsqrt(ms + NORM_EPS) * fg_ref[...]
    o_ref[...] = y


def _resident(shape):
    return pl.BlockSpec(shape, lambda *_: (0,) * len(shape), pipeline_mode=pl.Buffered(1))


def _ffn(x, g, sh, sc, gate, w_in, w_out, final_g, final):
    lx = x.shape[0]
    tm = min(512, lx)
    vec = pl.BlockSpec((1, D), lambda i: (0, 0))
    return pl.pallas_call(
        functools.partial(_ffn_kernel, final=final),
        out_shape=jax.ShapeDtypeStruct((lx, D), F32),
        grid=(lx // tm,),
        in_specs=[
            pl.BlockSpec((tm, D), lambda i: (i, 0)),
            vec, vec, vec, vec,
            _resident((D, 2 * FFN_HIDDEN)),
            _resident((FFN_HIDDEN, D)),
            vec,
        ],
        out_specs=pl.BlockSpec((tm, D), lambda i: (i, 0)),
        compiler_params=_cparams("parallel"),
        name="ffn",
    )(x, _row(g), _row(sh), _row(sc), _row(gate), w_in, w_out, _row(final_g))


def _outproj_kernel(x_ref, a_ref, w_ref, gate_ref, o_ref):
    y = jnp.dot(a_ref[...], w_ref[...], preferred_element_type=F32)
    o_ref[...] = x_ref[...] + gate_ref[...] * y


def _outproj(x, a, w, gate):
    lx, kin = a.shape
    tm = min(512, lx)
    return pl.pallas_call(
        _outproj_kernel,
        out_shape=jax.ShapeDtypeStruct((lx, D), F32),
        grid=(lx // tm,),
        in_specs=[
            pl.BlockSpec((tm, D), lambda i: (i, 0)),
            pl.BlockSpec((tm, kin), lambda i: (i, 0)),
            _resident((kin, D)),
            pl.BlockSpec((1, D), lambda i: (0, 0)),
        ],
        out_specs=pl.BlockSpec((tm, D), lambda i: (i, 0)),
        compiler_params=_cparams("parallel"),
        name="outproj",
    )(x, a, w, _row(gate))


def _gmlp_kernel(x_ref, g_ref, sh_ref, sc_ref, gate_ref, win_ref, lng_ref, lnb_ref, ws_ref, bs_ref, wout_ref,
                 o_ref, *, tm):
    x = x_ref[...]
    h = _normmod(x, g_ref[...], sh_ref[...], sc_ref[...]).astype(BF16)
    t = jnp.dot(h, win_ref[...], preferred_element_type=F32)
    t = 0.5 * t * (1.0 + lax.erf(t * (1.0 / math.sqrt(2.0))))
    u = t[:, :GM_WIDTH]
    v = t[:, GM_WIDTH:]
    mu = jnp.mean(v, axis=-1, keepdims=True)
    vc = v - mu
    var = jnp.mean(vc * vc, axis=-1, keepdims=True)
    v = (vc * lax.rsqrt(var + NORM_EPS) * lng_ref[...] + lnb_ref[...]).astype(BF16)
    rows = []
    for q in range(tm // GM_CHUNK):
        cols = []
        for gidx in range(GM_GROUPS):
            vq = v[q * GM_CHUNK:(q + 1) * GM_CHUNK, gidx * GM_GW:(gidx + 1) * GM_GW]
            bias = bs_ref[gidx]
            m = jnp.dot(ws_ref[gidx], vq, preferred_element_type=F32)
            cols.append(m + jnp.concatenate([bias] * (GM_GW // LANES), axis=1))
        rows.append(jnp.concatenate(cols, axis=1))
    vm = jnp.concatenate(rows, axis=0)
    gated = (u * vm).astype(BF16)
    y = jnp.dot(gated, wout_ref[...], preferred_element_type=F32)
    o_ref[...] = x + gate_ref[...] * y


def _gmlp(x, g, sh, sc, gate, w_in, ln_g, ln_b, ws, bs, w_out):
    lx = x.shape[0]
    tm = min(256, lx)
    vec = pl.BlockSpec((1, D), lambda i: (0, 0))
    vecw = pl.BlockSpec((1, GM_WIDTH), lambda i: (0, 0))
    bsb = jnp.broadcast_to(bs[:, :, None], (GM_GROUPS, GM_CHUNK, LANES))
    return pl.pallas_call(
        functools.partial(_gmlp_kernel, tm=tm),
        out_shape=jax.ShapeDtypeStruct((lx, D), F32),
        grid=(lx // tm,),
        in_specs=[
            pl.BlockSpec((tm, D), lambda i: (i, 0)),
            vec, vec, vec, vec,
            _resident((D, 2 * GM_WIDTH)),
            vecw, vecw,
            pl.BlockSpec((GM_GROUPS, GM_CHUNK, GM_CHUNK), lambda i: (0, 0, 0)),
            pl.BlockSpec((GM_GROUPS, GM_CHUNK, LANES), lambda i: (0, 0, 0)),
            _resident((GM_WIDTH, D)),
        ],
        out_specs=pl.BlockSpec((tm, D), lambda i: (i, 0)),
        compiler_params=_cparams("parallel"),
        name="gmlp",
    )(x, _row(g), _row(sh), _row(sc), _row(gate), w_in, _row(ln_g), _row(ln_b), ws, bsb, w_out)


def _group_sumsq(t, e_ref):
    sq = t * t
    hi = sq.astype(BF16)
    lo = (sq - hi.astype(F32)).astype(BF16)
    outs = []
    for j in range(t.shape[1] // LANES):
        sl = slice(j * LANES, (j + 1) * LANES)
        outs.append(jnp.dot(hi[:, sl], e_ref[...], preferred_element_type=F32)
                    + jnp.dot(lo[:, sl], e_ref[...], preferred_element_type=F32))
    return jnp.concatenate(outs, axis=1)


def _rope(t, cosf, sinf):
    w = t.shape[1]
    lane = lax.broadcasted_iota(jnp.int32, t.shape, 1)
    first = (lane % HD) < (HD // 2)
    partner = jnp.where(first, pltpu.roll(t, w - HD // 2, axis=1), pltpu.roll(t, HD // 2, axis=1))
    reps = w // LANES
    c = jnp.concatenate([cosf] * reps, axis=1)
    s = jnp.concatenate([sinf] * reps, axis=1)
    return t * c + partner * s


def _qkv_kernel(x_ref, g_ref, sh_ref, sc_ref, w_ref, qg_ref, kg_ref, e_ref, cos_ref, sin_ref,
                qt_ref, k_ref, vt_ref, *, rope):
    h = _normmod(x_ref[...], g_ref[...], sh_ref[...], sc_ref[...]).astype(BF16)
    qkv = jnp.dot(h, w_ref[...], preferred_element_type=F32)
    q = qkv[:, :D]
    k = qkv[:, D:D + KVH * HD]
    v = qkv[:, D + KVH * HD:]
    q = q * lax.rsqrt(_group_sumsq(q, e_ref) * (1.0 / HD) + NORM_EPS) * qg_ref[...]
    k = k * lax.rsqrt(_group_sumsq(k, e_ref) * (1.0 / HD) + NORM_EPS) * kg_ref[...]
    if rope:
        q = _rope(q, cos_ref[...], sin_ref[...])
        k = _rope(k, cos_ref[...], sin_ref[...])
    qt_ref[...] = (q * (HD ** -0.5 * LOG2E)).T.astype(BF16)
    k_ref[...] = k.astype(BF16)
    vt = v.T
    tm = vt.shape[1]
    ones_blk = (lax.broadcasted_iota(jnp.int32, (VT_ROWS - HD, tm), 0) == 0).astype(F32)
    parts = []
    for gidx in range(KVH):
        parts += [vt[gidx * HD:(gidx + 1) * HD, :], ones_blk]
    vt_ref[...] = jnp.concatenate(parts, axis=0).astype(BF16)


def _qkv(x, g, sh, sc, w_qkv, q_g, k_g, rope):
    lx = x.shape[0]
    tm = min(256, lx)
    vec = pl.BlockSpec((1, D), lambda i: (0, 0))
    kvw = KVH * HD
    rows = lx // GRID_W
    row = jnp.repeat(jnp.arange(rows, dtype=F32), GRID_W)
    col = jnp.tile(jnp.arange(GRID_W, dtype=F32), rows)
    n = HD // 4
    inv = ROPE_THETA ** (-jnp.arange(n, dtype=F32) / n)
    ang = jnp.concatenate([row[:, None] * inv, col[:, None] * inv], axis=-1)
    cos, sin = jnp.cos(ang), jnp.sin(ang)
    cosf = jnp.tile(jnp.concatenate([cos, cos], axis=-1), (1, LANES // HD))
    sinf = jnp.tile(jnp.concatenate([-sin, sin], axis=-1), (1, LANES // HD))
    eblk = jnp.asarray(np.kron(np.eye(LANES // HD), np.ones((HD, HD))), BF16)
    tab = pl.BlockSpec((tm, LANES), lambda i: (i, 0))
    return pl.pallas_call(
        functools.partial(_qkv_kernel, rope=rope),
        out_shape=(jax.ShapeDtypeStruct((D, lx), BF16),
                   jax.ShapeDtypeStruct((lx, kvw), BF16),
                   jax.ShapeDtypeStruct((KVH * VT_ROWS, lx), BF16)),
        grid=(lx // tm,),
        in_specs=[
            pl.BlockSpec((tm, D), lambda i: (i, 0)),
            vec, vec, vec,
            _resident((D, D + 2 * kvw)),
            vec,
            pl.BlockSpec((1, kvw), lambda i: (0, 0)),
            pl.BlockSpec((LANES, LANES), lambda i: (0, 0)),
            tab, tab,
        ],
        out_specs=(pl.BlockSpec((D, tm), lambda i: (0, i)),
                   pl.BlockSpec((tm, kvw), lambda i: (i, 0)),
                   pl.BlockSpec((KVH * VT_ROWS, tm), lambda i: (0, i))),
        compiler_params=_cparams("parallel"),
        name="qkv_proj",
    )(x, _row(g), _row(sh), _row(sc), w_qkv, _row(jnp.tile(q_g, QH)), _row(jnp.tile(k_g, KVH)), eblk, cosf, sinf)


def _flash_kernel(qt_ref, k_ref, vt_ref, o_ref, qbd_scr, m_scr, acc_scr, *, tq, ts, tc, nkv):
    j = pl.program_id(1)
    gq = QH // KVH
    mcols = gq * tq

    @pl.when(j == 0)
    def _():
        qbd_scr[...] = jnp.zeros_like(qbd_scr)
        for h in range(QH):
            g = h // gq
            qbd_scr[g * HD:(g + 1) * HD, h * tq:(h + 1) * tq] = qt_ref[h * HD:(h + 1) * HD, :]
        m_scr[...] = jnp.full(m_scr.shape, -jnp.inf, F32)
        acc_scr[...] = jnp.zeros_like(acc_scr)

    stages = [(g, c) for c in range(ts // tc) for g in range(KVH)]

    def scores(g, c):
        return jnp.dot(k_ref[c * tc:(c + 1) * tc, :], qbd_scr[:, g * mcols:(g + 1) * mcols],
                       preferred_element_type=F32)

    pending = [scores(*st) for st in stages[:FLASH_LOOKAHEAD]]
    for idx, (g, c) in enumerate(stages):
        s = pending.pop(0)
        if idx + FLASH_LOOKAHEAD < len(stages):
            pending.append(scores(*stages[idx + FLASH_LOOKAHEAD]))
        m_prev = m_scr[g]
        m_new = jnp.maximum(m_prev, jnp.max(s, axis=0, keepdims=True))
        alpha = jnp.exp2(m_prev - m_new)
        p = jnp.exp2(s - m_new).astype(BF16)
        pv = jnp.dot(vt_ref[g * VT_ROWS:(g + 1) * VT_ROWS, c * tc:(c + 1) * tc], p,
                     preferred_element_type=F32)
        acc_scr[g] = alpha * acc_scr[g] + pv
        m_scr[g] = m_new

    @pl.when(j == nkv - 1)
    def _():
        rows = []
        for g in range(KVH):
            acc = acc_scr[g]
            o = acc[:HD, :] / acc[HD:HD + 1, :]
            rows += [o[:, r * tq:(r + 1) * tq] for r in range(gq)]
        o_ref[...] = jnp.concatenate(rows, axis=0).T.astype(o_ref.dtype)


def _flash(qt, k, vt, s_len, ts):
    lq = qt.shape[1]
    tq = min(128, lq)
    nkv = s_len // ts
    gq = QH // KVH
    kvw = KVH * HD
    tc = 2 * LANES if ts % (2 * LANES) == 0 else LANES
    return pl.pallas_call(
        functools.partial(_flash_kernel, tq=tq, ts=ts, tc=tc, nkv=nkv),
        out_shape=jax.ShapeDtypeStruct((lq, D), BF16),
        grid=(lq // tq, nkv),
        in_specs=[
            pl.BlockSpec((D, tq), lambda i, j: (0, i)),
            pl.BlockSpec((ts, kvw), lambda i, j: (j, 0)),
            pl.BlockSpec((KVH * VT_ROWS, ts), lambda i, j: (0, j)),
        ],
        out_specs=pl.BlockSpec((tq, D), lambda i, j: (i, 0)),
        scratch_shapes=[
            pltpu.VMEM((kvw, QH * tq), BF16),
            pltpu.VMEM((KVH, 1, gq * tq), F32),
            pltpu.VMEM((KVH, VT_ROWS, gq * tq), F32),
        ],
        compiler_params=_cparams("parallel", "arbitrary"),
        name="flash_attn",
    )(qt, k, vt)


def _halo_specs(tm, lx):
    nb = lx // SUBLANES
    step = tm // SUBLANES
    prev = pl.BlockSpec((SUBLANES, D), lambda i: (jnp.maximum(i * step - 1, 0), 0))
    nxt = pl.BlockSpec((SUBLANES, D), lambda i: (jnp.minimum((i + 1) * step, nb - 1), 0))
    return prev, nxt


def _conv3(p_main, p_halo, cw, cb, first, last):
    tm = p_main.shape[0]
    rid = lax.broadcasted_iota(jnp.int32, p_main.shape, 0)
    before = jnp.where(first, 0.0, p_halo[SUBLANES - 1:SUBLANES, :])
    after = jnp.where(last, 0.0, p_halo[SUBLANES:SUBLANES + 1, :])
    up = jnp.where(rid == 0, before, pltpu.roll(p_main, 1, axis=0))
    dn = jnp.where(rid == tm - 1, after, pltpu.roll(p_main, tm - 1, axis=0))
    return cw[0:1, :] * up + cw[1:2, :] * p_main + cw[2:3, :] * dn + cb


def _norm_halo(xm_ref, xp_ref, xn_ref, g_ref, sh_ref, sc_ref):
    g, sh, sc = g_ref[...], sh_ref[...], sc_ref[...]
    h = _normmod(xm_ref[...], g, sh, sc).astype(BF16)
    hh = jnp.concatenate([_normmod(xp_ref[...], g, sh, sc), _normmod(xn_ref[...], g, sh, sc)], axis=0).astype(BF16)
    return h, hh


def _hy_in_kernel(xm_ref, xp_ref, xn_ref, g_ref, sh_ref, sc_ref, w_ref, cw_ref, cb_ref, x0_ref, ut_ref, *, nt):
    i = pl.program_id(0)
    first, last = i == 0, i == nt - 1
    h, hh = _norm_halo(xm_ref, xp_ref, xn_ref, g_ref, sh_ref, sc_ref)

    def branch(b):
        sl = slice(b * D, (b + 1) * D)
        pm = jnp.dot(h, w_ref[:, sl], preferred_element_type=F32)
        ph = jnp.dot(hh, w_ref[:, sl], preferred_element_type=F32)
        return _conv3(pm, ph, cw_ref[:, sl], cb_ref[:, sl], first, last)

    x0_ref[...] = branch(0)
    u = branch(1) * branch(2)
    ut_ref[...] = u.T


def _hy_in(x, g, sh, sc, w_in, conv_w, conv_b):
    lx = x.shape[0]
    tm = min(512, lx)
    nt = lx // tm
    vec = pl.BlockSpec((1, D), lambda i: (0, 0))
    prev, nxt = _halo_specs(tm, lx)
    return pl.pallas_call(
        functools.partial(_hy_in_kernel, nt=nt),
        out_shape=(jax.ShapeDtypeStruct((lx, D), F32), jax.ShapeDtypeStruct((D, lx), F32)),
        grid=(nt,),
        in_specs=[
            pl.BlockSpec((tm, D), lambda i: (i, 0)), prev, nxt,
            vec, vec, vec,
            _resident((D, 3 * D)),
            pl.BlockSpec((3, 3 * D), lambda i: (0, 0)),
            pl.BlockSpec((1, 3 * D), lambda i: (0, 0)),
        ],
        out_specs=(pl.BlockSpec((tm, D), lambda i: (i, 0)), pl.BlockSpec((D, tm), lambda i: (0, i))),
        compiler_params=_cparams("parallel"),
        name="hyena_in",
    )(x, x, x, _row(g), _row(sh), _row(sc), w_in, conv_w, _row(conv_b))


def _hy_filter_kernel(w1_ref, b1_ref, w2_ref, b2_ref, w3_ref, fr_ref, dl_ref, sk_ref, kt_ref, *, tm, ltrue):
    i = pl.program_id(0)
    shape = (tm, LANES)
    pos = (lax.broadcasted_iota(jnp.int32, shape, 0) + i * tm).astype(F32)
    lane = lax.broadcasted_iota(jnp.int32, shape, 1)
    t = pos / (ltrue - 1)
    w = 2.0 * math.pi * pos / ltrue
    band = jnp.where(lane <= HY_BANDS, lane - 1, lane - 1 - HY_BANDS).astype(F32)
    f = 1e-4 + band * ((HY_BANDS - 1 - 1e-4) / (HY_BANDS - 1))
    zf = w * f
    feats = jnp.where(lane == 0, t, jnp.where(lane <= HY_BANDS, jnp.cos(zf),
                                              jnp.where(lane <= 2 * HY_BANDS, -jnp.sin(zf), 0.0)))
    fr = fr_ref[...]
    hid = jnp.sin(fr * (jnp.dot(feats, w1_ref[...], preferred_element_type=F32, precision=HIGHEST) + b1_ref[...]))
    hid = jnp.sin(fr * (jnp.dot(hid, w2_ref[...], preferred_element_type=F32, precision=HIGHEST) + b2_ref[...]))
    k = jnp.dot(hid, w3_ref[...], preferred_element_type=F32, precision=HIGHEST)
    k = k * jnp.exp(-t[:, 0:1] * dl_ref[...])
    rid = lax.broadcasted_iota(jnp.int32, k.shape, 0) + i * tm
    cid = lax.broadcasted_iota(jnp.int32, k.shape, 1)
    k = jnp.where(rid == 0, jnp.where(cid < D, k + sk_ref[...], 0.0), k)
    k = jnp.where(rid < ltrue, k, 0.0)
    kt_ref[...] = k.T


def _hy_filter(ltrue, lpad, w1, b1, w2, b2, w3, freq, skip):
    tm = min(512, lpad)
    w1p = jnp.zeros((LANES, HY_FILT_W), F32).at[:HY_EMB].set(w1)
    deltas = jnp.abs(jnp.linspace(HY_MIN_DECAY, HY_MAX_DECAY, D, dtype=F32))
    full = lambda a: pl.BlockSpec(a.shape, lambda i: (0,) * a.ndim)
    args = (w1p, _row(b1), w2, _row(b2), w3, _row(freq), _row(jnp.tile(deltas, 2)),
            _row(jnp.concatenate([skip, jnp.zeros((D,), F32)])))
    return pl.pallas_call(
        functools.partial(_hy_filter_kernel, tm=tm, ltrue=ltrue),
        out_shape=jax.ShapeDtypeStruct((2 * D, lpad), F32),
        grid=(lpad // tm,),
        in_specs=[full(a) for a in args],
        out_specs=pl.BlockSpec((2 * D, tm), lambda i: (0, i)),
        compiler_params=_cparams("parallel"),
        name="hyena_filter",
    )(*args)


def _dft_consts(nh):
    n1 = 2 * nh
    n = n1 * DFT_N2
    k1 = np.arange(n1)[:, None].astype(np.float64)
    a1 = 2.0 * np.pi * k1 * np.arange(nh)[None, :] / n1
    f1 = np.concatenate([np.cos(a1), -np.sin(a1)], axis=0)
    at = 2.0 * np.pi * ((np.arange(n1)[:, None] * np.arange(DFT_N2)[None, :]) % n) / n
    a2 = 2.0 * np.pi * ((np.arange(DFT_N2)[:, None] * np.arange(DFT_N2)[None, :]) % DFT_N2) / DFT_N2
    c2, s2 = np.cos(a2), np.sin(a2)
    f2 = np.block([[c2, -s2], [s2, c2]])
    g2 = np.block([[c2, s2], [-s2, c2]])
    g1 = np.concatenate([np.cos(a1).T, -np.sin(a1).T], axis=1) / n
    as32 = lambda a: jnp.asarray(a, F32)

    def parts(a):
        hi = a.astype(BF16)
        if DFT_SPLIT == 1:
            return (jnp.asarray(hi),)
        return (jnp.asarray(hi), jnp.asarray((a - hi.astype(np.float64)).astype(BF16)))

    return parts(f1), as32(np.cos(at)), as32(np.sin(at)), parts(f2), parts(g2), parts(g1)


def _split_bf16(a):
    hi = a.astype(BF16)
    if DFT_SPLIT == 1:
        return (hi,)
    return (hi, (a - hi.astype(F32)).astype(BF16))


def _split_dot(a, b):
    out = jnp.dot(a[0], b[0], preferred_element_type=F32)
    if DFT_SPLIT > 1:
        out = out + jnp.dot(a[1], b[0], preferred_element_type=F32) + jnp.dot(a[0], b[1], preferred_element_type=F32)
    return out


def _load_parts(refs):
    return tuple(r[...] for r in refs)


def _dft_fwd(xs, f1, twc, tws, f2, cb, n1):
    xcat = jnp.concatenate(xs, axis=1)
    a = _split_dot(f1, _split_bf16(xcat))
    rows = []
    for c in range(cb):
        ar = a[:n1, c * DFT_N2:(c + 1) * DFT_N2]
        ai = a[n1:, c * DFT_N2:(c + 1) * DFT_N2]
        rows.append(jnp.concatenate([ar * twc + ai * tws, ai * twc - ar * tws], axis=1))
    return _split_dot(_split_bf16(jnp.concatenate(rows, axis=0)), f2)


def _hy_spec_kernel(x_ref, *refs, cb, n1):
    ns = DFT_SPLIT
    f1, (twc_ref, tws_ref), f2, o_ref = refs[:ns], refs[ns:ns + 2], refs[ns + 2:2 * ns + 2], refs[2 * ns + 2]
    xs = [x_ref[c] for c in range(cb)]
    spec = _dft_fwd(xs, _load_parts(f1), twc_ref[...], tws_ref[...], _load_parts(f2), cb, n1)
    o_ref[...] = spec.reshape(cb, n1, 2 * DFT_N2)


def _hy_conv_kernel(x_ref, hf_ref, hb_ref, *refs, cb, n1):
    ns = DFT_SPLIT
    f1, (twc_ref, tws_ref), f2 = refs[:ns], refs[ns:ns + 2], refs[ns + 2:2 * ns + 2]
    g2, g1, o_ref = refs[2 * ns + 2:3 * ns + 2], refs[3 * ns + 2:4 * ns + 2], refs[4 * ns + 2]
    twc, tws = twc_ref[...], tws_ref[...]
    xs = [x_ref[c] for c in range(cb)]
    spec = _dft_fwd(xs, _load_parts(f1), twc, tws, _load_parts(f2), cb, n1)
    hf = hf_ref[...].reshape(cb * n1, 2 * DFT_N2)
    hb = hb_ref[...].reshape(cb * n1, 2 * DFT_N2)
    hr = hf[:, :DFT_N2] + hb[:, :DFT_N2]
    hi = hf[:, DFT_N2:] - hb[:, DFT_N2:]
    xr, xi = spec[:, :DFT_N2], spec[:, DFT_N2:]
    y = jnp.concatenate([xr * hr - xi * hi, xr * hi + xi * hr], axis=1)
    b = _split_dot(_split_bf16(y), _load_parts(g2))
    cols = []
    for c in range(cb):
        br = b[c * n1:(c + 1) * n1, :DFT_N2]
        bi = b[c * n1:(c + 1) * n1, DFT_N2:]
        cols.append(jnp.concatenate([br * twc - bi * tws, bi * twc + br * tws], axis=0))
    out = _split_dot(_load_parts(g1), _split_bf16(jnp.concatenate(cols, axis=1)))
    for c in range(cb):
        o_ref[c] = out[:, c * DFT_N2:(c + 1) * DFT_N2]


def _hy_longconv(ut, kt, lpad):
    nh = lpad // DFT_N2
    n1 = 2 * nh
    cb = 8
    f1, twc, tws, f2, g2, g1 = _dft_consts(nh)
    consts = f1 + (twc, tws) + f2
    full = lambda a: pl.BlockSpec(a.shape, lambda i: (0,) * a.ndim)
    k3 = kt.reshape(2 * D, nh, DFT_N2)
    hspec = pl.pallas_call(
        functools.partial(_hy_spec_kernel, cb=cb, n1=n1),
        out_shape=jax.ShapeDtypeStruct((2 * D, n1, 2 * DFT_N2), F32),
        grid=(2 * D // cb,),
        in_specs=[pl.BlockSpec((cb, nh, DFT_N2), lambda i: (i, 0, 0))] + [full(a) for a in consts],
        out_specs=pl.BlockSpec((cb, n1, 2 * DFT_N2), lambda i: (i, 0, 0)),
        compiler_params=_cparams("parallel"),
        name="hyena_filter_spectrum",
    )(k3, *consts)
    u3 = ut.reshape(D, nh, DFT_N2)
    nb = D // cb
    y3 = pl.pallas_call(
        functools.partial(_hy_conv_kernel, cb=cb, n1=n1),
        out_shape=jax.ShapeDtypeStruct((D, nh, DFT_N2), F32),
        grid=(nb,),
        in_specs=[pl.BlockSpec((cb, nh, DFT_N2), lambda i: (i, 0, 0)),
                  pl.BlockSpec((cb, n1, 2 * DFT_N2), lambda i: (i, 0, 0)),
                  pl.BlockSpec((cb, n1, 2 * DFT_N2), lambda i: (i + nb, 0, 0))]
                 + [full(a) for a in consts + g2 + g1],
        out_specs=pl.BlockSpec((cb, nh, DFT_N2), lambda i: (i, 0, 0)),
        compiler_params=_cparams("parallel"),
        name="hyena_longconv",
    )(u3, hspec, hspec, *consts, *g2, *g1)
    return y3.reshape(D, lpad)


def _hy_out_kernel(x_ref, x0_ref, yt_ref, w_ref, gate_ref, o_ref):
    a = (x0_ref[...] * yt_ref[...].T).astype(BF16)
    y = jnp.dot(a, w_ref[...], preferred_element_type=F32)
    o_ref[...] = x_ref[...] + gate_ref[...] * y


def _hy_out(x, x0, yt, w_out, gate):
    lx = x.shape[0]
    tm = min(512, lx)
    return pl.pallas_call(
        _hy_out_kernel,
        out_shape=jax.ShapeDtypeStruct((lx, D), F32),
        grid=(lx // tm,),
        in_specs=[
            pl.BlockSpec((tm, D), lambda i: (i, 0)),
            pl.BlockSpec((tm, D), lambda i: (i, 0)),
            pl.BlockSpec((D, tm), lambda i: (0, i)),
            _resident((D, D)),
            pl.BlockSpec((1, D), lambda i: (0, 0)),
        ],
        out_specs=pl.BlockSpec((tm, D), lambda i: (i, 0)),
        compiler_params=_cparams("parallel"),
        name="hyena_out",
    )(x, x0, yt, w_out, _row(gate))


def _hyena(x, g, sh, sc, gate, w_in, conv_w, conv_b, w1, b1, w2, b2, w3, freq, skip, w_out):
    lx = x.shape[0]
    lpad = max(lx, SUBLANES * DFT_N2)
    x0, ut = _hy_in(x, g, sh, sc, w_in, conv_w, conv_b)
    if lpad != lx:
        ut = jnp.pad(ut, ((0, 0), (0, lpad - lx)))
    kt = _hy_filter(lx, lpad, w1, b1, w2, b2, w3, freq, skip)
    yt = _hy_longconv(ut, kt, lpad)[:, :lx]
    return _hy_out(x, x0, yt, w_out, gate)


def _ssd_in_kernel(xm_ref, xp_ref, xn_ref, g_ref, sh_ref, sc_ref, wz_ref, wx_ref, wd_ref, cw_ref, cb_ref, db_ref,
                   zg_ref, xs_ref, bm_ref, cm_ref, dt_ref, *, nt):
    i = pl.program_id(0)
    first, last = i == 0, i == nt - 1
    h, hh = _norm_halo(xm_ref, xp_ref, xn_ref, g_ref, sh_ref, sc_ref)
    zg_ref[...] = jnp.dot(h, wz_ref[...], preferred_element_type=F32)
    pm = jnp.dot(h, wx_ref[...], preferred_element_type=F32)
    ph = jnp.dot(hh, wx_ref[...], preferred_element_type=F32)
    xbc = _silu(_conv3(pm, ph, cw_ref[...], cb_ref[...], first, last))
    xs_ref[...] = xbc[:, :SSD_INNER]
    bm_ref[...] = xbc[:, SSD_INNER:SSD_INNER + SSD_BC]
    cm_ref[...] = xbc[:, SSD_INNER + SSD_BC:]
    dt = jnp.dot(h, wd_ref[...], preferred_element_type=F32) + db_ref[...]
    dt = jnp.maximum(dt, 0.0) + jnp.log1p(jnp.exp(-jnp.abs(dt)))
    lane = lax.broadcasted_iota(jnp.int32, dt.shape, 1)
    dt = jnp.where((lane % LANES) < SSD_HEADS, dt, 0.0)
    dt_ref[0] = dt[:, :LANES]
    dt_ref[1] = dt[:, LANES:]


def _ssd_in(x, g, sh, sc, w_in, conv_w, conv_b, dt_bias):
    lx = x.shape[0]
    tm = min(256, lx)
    nt = lx // tm
    vec = pl.BlockSpec((1, D), lambda i: (0, 0))
    prev, nxt = _halo_specs(tm, lx)
    wz = w_in[:, :SSD_INNER]
    wx = w_in[:, SSD_INNER:SSD_INNER + SSD_CONV_DIM]
    wdt = w_in[:, SSD_INNER + SSD_CONV_DIM:]
    pad = LANES - SSD_HEADS
    wd = jnp.concatenate([jnp.pad(wdt[:, :SSD_HEADS], ((0, 0), (0, pad))),
                          jnp.pad(wdt[:, SSD_HEADS:], ((0, 0), (0, pad)))], axis=1)
    db = jnp.pad(dt_bias, ((0, 0), (0, pad))).reshape(1, 2 * LANES)
    full = lambda a: pl.BlockSpec(a.shape, lambda i: (0,) * a.ndim)
    rowblk = lambda w: pl.BlockSpec((tm, w), lambda i: (i, 0))
    return pl.pallas_call(
        functools.partial(_ssd_in_kernel, nt=nt),
        out_shape=(jax.ShapeDtypeStruct((lx, SSD_INNER), F32), jax.ShapeDtypeStruct((lx, SSD_INNER), F32),
                   jax.ShapeDtypeStruct((lx, SSD_BC), F32), jax.ShapeDtypeStruct((lx, SSD_BC), F32),
                   jax.ShapeDtypeStruct((2, lx, LANES), F32)),
        grid=(nt,),
        in_specs=[rowblk(D), prev, nxt, vec, vec, vec, _resident(wz.shape), _resident(wx.shape), _resident(wd.shape),
                  pl.BlockSpec((3, SSD_CONV_DIM), lambda i: (0, 0)),
                  pl.BlockSpec((1, SSD_CONV_DIM), lambda i: (0, 0)),
                  pl.BlockSpec((1, 2 * LANES), lambda i: (0, 0))],
        out_specs=(rowblk(SSD_INNER), rowblk(SSD_INNER), rowblk(SSD_BC), rowblk(SSD_BC),
                   pl.BlockSpec((2, tm, LANES), lambda i: (0, i, 0))),
        compiler_params=_cparams("parallel"),
        name="ssd_in",
    )(x, x, x, _row(g), _row(sh), _row(sc), wz, wx, wd, conv_w, _row(conv_b), db)


def _ssd_scan_kernel(xs_ref, bm_ref, cm_ref, dt_ref, a_ref, tri_ref, h0_ref, y_ref, hfin_ref, h_scr, *, nc, need_y):
    s = pl.program_id(1)
    q = SSD_CHUNK
    npair = SSD_HEADS // 2
    ppg = npair // SSD_GROUPS

    @pl.when(s == 0)
    def _():
        h_scr[...] = h0_ref[0]

    tri = tri_ref[0]
    keep = tri > 0.5
    dt = dt_ref[0]
    a = dt * a_ref[0]
    acs = jnp.dot(tri, a, preferred_element_type=F32, precision=HIGHEST)
    total = jnp.sum(a, axis=0, keepdims=True)
    wend = jnp.exp(total - acs) * dt
    eacs = jnp.exp(acs)
    etot = jnp.exp(total)
    acs_t = acs.T
    dt_t = dt.T
    lane = lax.broadcasted_iota(jnp.int32, (q, LANES), 1)
    sub = lax.broadcasted_iota(jnp.int32, (q, LANES), 0)
    left = lane < SSD_P
    top = sub < SSD_P

    def colb(arr, hd):
        return jnp.broadcast_to(arr[:, hd:hd + 1], (q, LANES))

    for g in range(SSD_GROUPS):
        bg = bm_ref[:, g * SSD_STATE:(g + 1) * SSD_STATE]
        cg = cm_ref[:, g * SSD_STATE:(g + 1) * SSD_STATE].astype(BF16)
        bgb = bg.astype(BF16)
        if need_y:
            cb = lax.dot_general(cg, bgb, (((1,), (1,)), ((), ())), preferred_element_type=F32)
        for r in range(ppg):
            pidx = g * ppg + r
            h0i, h1i = 2 * pidx, 2 * pidx + 1
            xp = xs_ref[:, pidx * LANES:(pidx + 1) * LANES]
            hs = h_scr[pidx]
            if need_y:
                xpb = xp.astype(BF16)
                yd = []
                for hd in (h0i, h1i):
                    seg = colb(acs, hd) - acs_t[hd:hd + 1, :]
                    lm = jnp.exp(jnp.where(keep, seg, -jnp.inf))
                    m = (cb * lm * dt_t[hd:hd + 1, :]).astype(BF16)
                    yd.append(jnp.dot(m, xpb, preferred_element_type=F32))
                ydiag = jnp.where(left, yd[0], yd[1])
                yoff = lax.dot_general(cg, hs.astype(BF16), (((1,), (1,)), ((), ())), preferred_element_type=F32)
                yoff = yoff * jnp.where(left, colb(eacs, h0i), colb(eacs, h1i))
                y_ref[0, :, pidx * LANES:(pidx + 1) * LANES] = ydiag + yoff
            xw = (xp * jnp.where(left, colb(wend, h0i), colb(wend, h1i))).astype(BF16)
            st = lax.dot_general(xw, bgb, (((0,), (0,)), ((), ())), preferred_element_type=F32)
            dec = jnp.where(top, jnp.broadcast_to(etot[:, h0i:h0i + 1], (q, LANES)),
                            jnp.broadcast_to(etot[:, h1i:h1i + 1], (q, LANES)))
            h_scr[pidx] = hs * dec + st

    @pl.when(s == nc - 1)
    def _():
        hfin_ref[0] = h_scr[...]


def _ssd_scan(xs, bm, cm, dt2, a_log, h0, need_y):
    lx = xs.shape[0]
    q = SSD_CHUNK
    nc = lx // q
    npair = SSD_HEADS // 2
    a = -jnp.exp(a_log.astype(F32))
    a_pad = jnp.pad(a, ((0, 0), (0, LANES - SSD_HEADS))).reshape(2, 1, LANES)
    lower = np.tril(np.ones((q, q), np.float32))
    tri = jnp.asarray(np.stack([lower, lower.T]))
    chunk = lambda d, s: jnp.where(d == 0, s, nc - 1 - s)
    out_shape = [jax.ShapeDtypeStruct((2, lx if need_y else q, SSD_INNER), F32),
                 jax.ShapeDtypeStruct((2, npair, 2 * SSD_P, SSD_STATE), F32)]
    y_spec = (pl.BlockSpec((1, q, SSD_INNER), lambda d, s: (d, chunk(d, s), 0)) if need_y
              else pl.BlockSpec((1, q, SSD_INNER), lambda d, s: (d, 0, 0)))
    st_spec = pl.BlockSpec((1, npair, 2 * SSD_P, SSD_STATE), lambda d, s: (d, 0, 0, 0))
    y, hfin = pl.pallas_call(
        functools.partial(_ssd_scan_kernel, nc=nc, need_y=need_y),
        out_shape=out_shape,
        grid=(2, nc),
        in_specs=[
            pl.BlockSpec((q, SSD_INNER), lambda d, s: (chunk(d, s), 0)),
            pl.BlockSpec((q, SSD_BC), lambda d, s: (chunk(d, s), 0)),
            pl.BlockSpec((q, SSD_BC), lambda d, s: (chunk(d, s), 0)),
            pl.BlockSpec((1, q, LANES), lambda d, s: (d, chunk(d, s), 0)),
            pl.BlockSpec((1, 1, LANES), lambda d, s: (d, 0, 0)),
            pl.BlockSpec((1, q, q), lambda d, s: (d, 0, 0)),
            st_spec,
        ],
        out_specs=[y_spec, st_spec],
        scratch_shapes=[pltpu.VMEM((npair, 2 * SSD_P, SSD_STATE), F32)],
        compiler_params=_cparams("arbitrary", "arbitrary"),
        name="ssd_scan",
    )(xs, bm, cm, dt2, a_pad, tri, h0)
    return (y if need_y else None), hfin


def _ssd_out_kernel(x_ref, y_ref, xs_ref, zg_ref, dsk_ref, ng_ref, w_ref, gate_ref, o_ref):
    y = y_ref[0] + y_ref[1] + xs_ref[...] * dsk_ref[...]
    y = y * _silu(zg_ref[...])
    gw = SSD_INNER // SSD_GROUPS
    parts = []
    for g in range(SSD_GROUPS):
        yg = y[:, g * gw:(g + 1) * gw]
        ms = jnp.mean(yg * yg, axis=-1, keepdims=True)
        parts.append(yg * lax.rsqrt(ms + NORM_EPS) * ng_ref[:, g * gw:(g + 1) * gw])
    yn = jnp.concatenate(parts, axis=1).astype(BF16)
    o_ref[...] = x_ref[...] + gate_ref[...] * jnp.dot(yn, w_ref[...], preferred_element_type=F32)


def _ssd_out(x, y2, xs, zg, d_skip, norm_g, w_out, gate):
    lx = x.shape[0]
    tm = min(256, lx)
    rowblk = lambda w: pl.BlockSpec((tm, w), lambda i: (i, 0))
    vecw = pl.BlockSpec((1, SSD_INNER), lambda i: (0, 0))
    return pl.pallas_call(
        _ssd_out_kernel,
        out_shape=jax.ShapeDtypeStruct((lx, D), F32),
        grid=(lx // tm,),
        in_specs=[rowblk(D), pl.BlockSpec((2, tm, SSD_INNER), lambda i: (0, i, 0)), rowblk(SSD_INNER),
                  rowblk(SSD_INNER), vecw, vecw, _resident((SSD_INNER, D)),
                  pl.BlockSpec((1, D), lambda i: (0, 0))],
        out_specs=rowblk(D),
        compiler_params=_cparams("parallel"),
        name="ssd_out",
    )(x, y2, xs, zg, _row(jnp.repeat(d_skip, SSD_P)), _row(norm_g), w_out, _row(gate))


def kernel(x, c, ctx, c_ctx, norm1_g, norm2_g, mod_w, mod_b, ffn_w_in, ffn_w_out, final_g, gm_w_in, gm_ln_g, gm_ln_b, gm_ws, gm_bs, gm_w_out, at_w_qkv, at_q_g, at_k_g, at_w_out, hy_w_in, hy_conv_w, hy_conv_b, hy_filt_w1, hy_filt_b1, hy_filt_w2, hy_filt_b2, hy_filt_w3, hy_filt_freq, hy_skip, hy_w_out, ssd_w_in, ssd_conv_w, ssd_conv_b, ssd_a_log, ssd_dt_bias, ssd_d_skip, ssd_norm_g, ssd_w_out):
    batch, seq, _ = x.shape
    assert batch == 1, "kernels are written for a single sequence"
    nctx = ctx.shape[1]
    xl = x[0]
    z = ctx[0]
    mods = _modulation(c[0], c_ctx, mod_w, mod_b)
    bf = lambda w: w.astype(BF16)

    for i in range(DEPTH):
        m, j = i % 4, i // 4
        want_ctx = i < DEPTH - 1
        ml = [mods[i, 0, k * D:(k + 1) * D] for k in range(6)]
        mc = [mods[i, 1, k * D:(k + 1) * D] for k in range(6)]
        n1 = norm1_g[i]
        if m == 0:
            p = (bf(gm_w_in[j]), gm_ln_g[j], gm_ln_b[j], bf(gm_ws[j]), gm_bs[j], bf(gm_w_out[j]))
            xl = _gmlp(xl, n1, ml[0], ml[1], ml[2], *p)
            if want_ctx:
                z = _gmlp(z, n1, mc[0], mc[1], mc[2], *p)
        elif m == 1:
            wq, wo = bf(at_w_qkv[j]), bf(at_w_out[j])
            qt_l, k_l, vt_l = _qkv(xl, n1, ml[0], ml[1], wq, at_q_g[j], at_k_g[j], rope=True)
            qt_c, k_c, vt_c = _qkv(z, n1, mc[0], mc[1], wq, at_q_g[j], at_k_g[j], rope=False)
            k_all = jnp.concatenate([k_c, k_l], axis=0)
            vt_all = jnp.concatenate([vt_c, vt_l], axis=1)
            stot = nctx + seq
            ts = next(t for t in (3328, 1280, 1024, 512, 256) if stot % t == 0)
            o_l = _flash(qt_l, k_all, vt_all, stot, ts)
            xl = _outproj(xl, o_l, wo, ml[2])
            if want_ctx:
                o_c = _flash(qt_c, k_all, vt_all, nctx, nctx)
                z = _outproj(z, o_c, wo, mc[2])
        elif m == 2:
            p = (bf(hy_w_in[j]), hy_conv_w[j], hy_conv_b[j], hy_filt_w1[j], hy_filt_b1[j], hy_filt_w2[j],
                 hy_filt_b2[j], hy_filt_w3[j], hy_filt_freq[j], hy_skip[j], bf(hy_w_out[j]))
            xl = _hyena(xl, n1, ml[0], ml[1], ml[2], *p)
            if want_ctx:
                z = _hyena(z, n1, mc[0], mc[1], mc[2], *p)
        else:
            win, wo = bf(ssd_w_in[j]), bf(ssd_w_out[j])
            pin = (win, ssd_conv_w[j], ssd_conv_b[j], ssd_dt_bias[j])
            zg_c, xs_c, bm_c, cm_c, dt_c = _ssd_in(z, n1, mc[0], mc[1], *pin)
            zg_l, xs_l, bm_l, cm_l, dt_l = _ssd_in(xl, n1, ml[0], ml[1], *pin)
            h0 = jnp.zeros((2, SSD_HEADS // 2, 2 * SSD_P, SSD_STATE), F32)
            y_c, h_ctx = _ssd_scan(xs_c, bm_c, cm_c, dt_c, ssd_a_log[j], h0, want_ctx)
            y_l, _ = _ssd_scan(xs_l, bm_l, cm_l, dt_l, ssd_a_log[j], h_ctx, True)
            xl = _ssd_out(xl, y_l, xs_l, zg_l, ssd_d_skip[j], ssd_norm_g[j], wo, ml[2])
            if want_ctx:
                z = _ssd_out(z, y_c, xs_c, zg_c, ssd_d_skip[j], ssd_norm_g[j], wo, mc[2])
        wi, wo2 = bf(ffn_w_in[i]), bf(ffn_w_out[i])
        xl = _ffn(xl, norm2_g[i], ml[3], ml[4], ml[5], wi, wo2, final_g, final=(i == DEPTH - 1))
        if want_ctx:
            z = _ffn(z, norm2_g[i], mc[3], mc[4], mc[5], wi, wo2, final_g, final=False)
    return xl[None]
```

```python
import functools
import math

import numpy as np
import jax
import jax.numpy as jnp
from jax import lax
from jax.experimental import pallas as pl
from jax.experimental.pallas import tpu as pltpu

F32 = jnp.float32
BF16 = jnp.bfloat16
HIGHEST = lax.Precision.HIGHEST

D = 1024
DEPTH = 4
GRID_W = 64
NORM_EPS = 1e-6
FFN_HIDDEN = 2816
GM_CHUNK = 128
GM_WIDTH = 2 * D
GM_GROUPS = 8
GM_GW = GM_WIDTH // GM_GROUPS
HD = 64
QH = D // HD
KVH = 4
ROPE_THETA = 10000.0
LOG2E = math.log2(math.e)
FLASH_SCORE_BOUND = 30.0
FLASH_LOOKAHEAD = 2
VT_ROWS = HD + 16
HY_BANDS = 16
HY_EMB = 1 + 2 * HY_BANDS
HY_FILT_W = 64
HY_MAX_DECAY = math.log(1e-2) / 0.3
HY_MIN_DECAY = math.log(1e-2) / 1.5
SSD_INNER = 2 * D
SSD_P = 64
SSD_HEADS = SSD_INNER // SSD_P
SSD_GROUPS = 4
SSD_STATE = 128
SSD_CHUNK = 128
SSD_BC = SSD_GROUPS * SSD_STATE
SSD_CONV_DIM = SSD_INNER + 2 * SSD_BC

LANES = 128
SUBLANES = 8
VMEM_LIMIT_BYTES = 56 * 1024 * 1024
DFT_N2 = 128
DFT_SPLIT = 1


def _cparams(*sem):
    return pltpu.CompilerParams(dimension_semantics=sem, vmem_limit_bytes=VMEM_LIMIT_BYTES)


def _row(v):
    return v.reshape(1, -1)


def _normmod(x, g, shift, scale):
    ms = jnp.mean(x * x, axis=-1, keepdims=True)
    return x * lax.rsqrt(ms + NORM_EPS) * g * (1.0 + scale) + shift


def _silu(x):
    return x * jax.nn.sigmoid(x)


def _mod_kernel(cl_ref, cc_ref, w_ref, b_ref, o_ref):
    w = w_ref[0]
    for r, c_ref in enumerate((cl_ref, cc_ref)):
        a = _silu(c_ref[...])
        o_ref[0, r:r + 1, :] = jnp.sum(a * w, axis=0, keepdims=True) + b_ref[0]


def _modulation(c, c_ctx, mod_w, mod_b):
    tn = 1536
    n6 = 6 * D
    depth = mod_w.shape[0]
    return pl.pallas_call(
        _mod_kernel,
        out_shape=jax.ShapeDtypeStruct((depth, 2, n6), F32),
        grid=(depth, n6 // tn),
        in_specs=[
            pl.BlockSpec((D, 1), lambda i, n: (0, 0)),
            pl.BlockSpec((D, 1), lambda i, n: (0, 0)),
            pl.BlockSpec((1, D, tn), lambda i, n: (i, 0, n)),
            pl.BlockSpec((1, 1, tn), lambda i, n: (i, 0, n)),
        ],
        out_specs=pl.BlockSpec((1, 2, tn), lambda i, n: (i, 0, n)),
        compiler_params=_cparams("parallel", "parallel"),
        name="modulation",
    )(c.reshape(D, 1), c_ctx.reshape(D, 1), mod_w, mod_b.reshape(depth, 1, n6))


def _ffn_kernel(x_ref, g_ref, sh_ref, sc_ref, gate_ref, wi_ref, wo_ref, fg_ref, o_ref, *, final):
    x = x_ref[...]
    h = _normmod(x, g_ref[...], sh_ref[...], sc_ref[...]).astype(BF16)
    a = jnp.dot(h, wi_ref[:, :FFN_HIDDEN], preferred_element_type=F32)
    u = jnp.dot(h, wi_ref[:, FFN_HIDDEN:], preferred_element_type=F32)
    act = (_silu(a) * u).astype(BF16)
    y = x + gate_ref[...] * jnp.dot(act, wo_ref[...], preferred_element_type=F32)
    if final:
        ms = jnp.mean(y * y, axis=-1, keepdims=True)
        y = y * lax.rsqrt(ms + NORM_EPS) * fg_ref[...]
    o_ref[...] = y


def _resident(shape):
    return pl.BlockSpec(shape, lambda *_: (0,) * len(shape), pipeline_mode=pl.Buffered(1))


def _ffn(x, g, sh, sc, gate, w_in, w_out, final_g, final):
    lx = x.shape[0]
    tm = min(512, lx)
    vec = pl.BlockSpec((1, D), lambda i: (0, 0))
    return pl.pallas_call(
        functools.partial(_ffn_kernel, final=final),
        out_shape=jax.ShapeDtypeStruct((lx, D), F32),
        grid=(lx // tm,),
        in_specs=[
            pl.BlockSpec((tm, D), lambda i: (i, 0)),
            vec, vec, vec, vec,
            _resident((D, 2 * FFN_HIDDEN)),
            _resident((FFN_HIDDEN, D)),
            vec,
        ],
        out_specs=pl.BlockSpec((tm, D), lambda i: (i, 0)),
        compiler_params=_cparams("parallel"),
        name="ffn",
    )(x, _row(g), _row(sh), _row(sc), _row(gate), w_in, w_out, _row(final_g))


def _outproj_kernel(x_ref, a_ref, w_ref, gate_ref, o_ref):
    y = jnp.dot(a_ref[...], w_ref[...], preferred_element_type=F32)
    o_ref[...] = x_ref[...] + gate_ref[...] * y


def _outproj(x, a, w, gate):
    lx, kin = a.shape
    tm = min(512, lx)
    return pl.pallas_call(
        _outproj_kernel,
        out_shape=jax.ShapeDtypeStruct((lx, D), F32),
        grid=(lx // tm,),
        in_specs=[
            pl.BlockSpec((tm, D), lambda i: (i, 0)),
            pl.BlockSpec((tm, kin), lambda i: (i, 0)),
            _resident((kin, D)),
            pl.BlockSpec((1, D), lambda i: (0, 0)),
        ],
        out_specs=pl.BlockSpec((tm, D), lambda i: (i, 0)),
        compiler_params=_cparams("parallel"),
        name="outproj",
    )(x, a, w, _row(gate))


def _gmlp_kernel(x_ref, g_ref, sh_ref, sc_ref, gate_ref, win_ref, lng_ref, lnb_ref, ws_ref, bs_ref, wout_ref,
                 o_ref, *, tm):
    x = x_ref[...]
    h = _normmod(x, g_ref[...], sh_ref[...], sc_ref[...]).astype(BF16)
    t = jnp.dot(h, win_ref[...], preferred_element_type=F32)
    t = 0.5 * t * (1.0 + lax.erf(t * (1.0 / math.sqrt(2.0))))
    u = t[:, :GM_WIDTH]
    v = t[:, GM_WIDTH:]
    mu = jnp.mean(v, axis=-1, keepdims=True)
    vc = v - mu
    var = jnp.mean(vc * vc, axis=-1, keepdims=True)
    v = (vc * lax.rsqrt(var + NORM_EPS) * lng_ref[...] + lnb_ref[...]).astype(BF16)
    rows = []
    for q in range(tm // GM_CHUNK):
        cols = []
        for gidx in range(GM_GROUPS):
            vq = v[q * GM_CHUNK:(q + 1) * GM_CHUNK, gidx * GM_GW:(gidx + 1) * GM_GW]
            bias = bs_ref[gidx]
            m = jnp.dot(ws_ref[gidx], vq, preferred_element_type=F32)
            cols.append(m + jnp.concatenate([bias] * (GM_GW // LANES), axis=1))
        rows.append(jnp.concatenate(cols, axis=1))
    vm = jnp.concatenate(rows, axis=0)
    gated = (u * vm).astype(BF16)
    y = jnp.dot(gated, wout_ref[...], preferred_element_type=F32)
    o_ref[...] = x + gate_ref[...] * y


def _gmlp(x, g, sh, sc, gate, w_in, ln_g, ln_b, ws, bs, w_out):
    lx = x.shape[0]
    tm = min(256, lx)
    vec = pl.BlockSpec((1, D), lambda i: (0, 0))
    vecw = pl.BlockSpec((1, GM_WIDTH), lambda i: (0, 0))
    bsb = jnp.broadcast_to(bs[:, :, None], (GM_GROUPS, GM_CHUNK, LANES))
    return pl.pallas_call(
        functools.partial(_gmlp_kernel, tm=tm),
        out_shape=jax.ShapeDtypeStruct((lx, D), F32),
        grid=(lx // tm,),
        in_specs=[
            pl.BlockSpec((tm, D), lambda i: (i, 0)),
            vec, vec, vec, vec,
            _resident((D, 2 * GM_WIDTH)),
            vecw, vecw,
            pl.BlockSpec((GM_GROUPS, GM_CHUNK, GM_CHUNK), lambda i: (0, 0, 0)),
            pl.BlockSpec((GM_GROUPS, GM_CHUNK, LANES), lambda i: (0, 0, 0)),
            _resident((GM_WIDTH, D)),
        ],
        out_specs=pl.BlockSpec((tm, D), lambda i: (i, 0)),
        compiler_params=_cparams("parallel"),
        name="gmlp",
    )(x, _row(g), _row(sh), _row(sc), _row(gate), w_in, _row(ln_g), _row(ln_b), ws, bsb, w_out)


def _group_sumsq(t, e_ref):
    sq = t * t
    hi = sq.astype(BF16)
    lo = (sq - hi.astype(F32)).astype(BF16)
    outs = []
    for j in range(t.shape[1] // LANES):
        sl = slice(j * LANES, (j + 1) * LANES)
        outs.append(jnp.dot(hi[:, sl], e_ref[...], preferred_element_type=F32)
                    + jnp.dot(lo[:, sl], e_ref[...], preferred_element_type=F32))
    return jnp.concatenate(outs, axis=1)


def _rope(t, cosf, sinf):
    w = t.shape[1]
    lane = lax.broadcasted_iota(jnp.int32, t.shape, 1)
    first = (lane % HD) < (HD // 2)
    partner = jnp.where(first, pltpu.roll(t, w - HD // 2, axis=1), pltpu.roll(t, HD // 2, axis=1))
    reps = w // LANES
    c = jnp.concatenate([cosf] * reps, axis=1)
    s = jnp.concatenate([sinf] * reps, axis=1)
    return t * c + partner * s


def _qkv_kernel(x_ref, g_ref, sh_ref, sc_ref, w_ref, qg_ref, kg_ref, e_ref, cos_ref, sin_ref,
                qt_ref, k_ref, vt_ref, *, rope):
    h = _normmod(x_ref[...], g_ref[...], sh_ref[...], sc_ref[...]).astype(BF16)
    qkv = jnp.dot(h, w_ref[...], preferred_element_type=F32)
    q = qkv[:, :D]
    k = qkv[:, D:D + KVH * HD]
    v = qkv[:, D + KVH * HD:]
    q = q * lax.rsqrt(_group_sumsq(q, e_ref) * (1.0 / HD) + NORM_EPS) * qg_ref[...]
    k = k * lax.rsqrt(_group_sumsq(k, e_ref) * (1.0 / HD) + NORM_EPS) * kg_ref[...]
    if rope:
        q = _rope(q, cos_ref[...], sin_ref[...])
        k = _rope(k, cos_ref[...], sin_ref[...])
    qt_ref[...] = (q * (HD ** -0.5 * LOG2E)).T.astype(BF16)
    k_ref[...] = k.astype(BF16)
    vt = v.T
    tm = vt.shape[1]
    ones_blk = (lax.broadcasted_iota(jnp.int32, (VT_ROWS - HD, tm), 0) == 0).astype(F32)
    parts = []
    for gidx in range(KVH):
        parts += [vt[gidx * HD:(gidx + 1) * HD, :], ones_blk]
    vt_ref[...] = jnp.concatenate(parts, axis=0).astype(BF16)


def _qkv(x, g, sh, sc, w_qkv, q_g, k_g, rope):
    lx = x.shape[0]
    tm = min(256, lx)
    vec = pl.BlockSpec((1, D), lambda i: (0, 0))
    kvw = KVH * HD
    rows = lx // GRID_W
    row = jnp.repeat(jnp.arange(rows, dtype=F32), GRID_W)
    col = jnp.tile(jnp.arange(GRID_W, dtype=F32), rows)
    n = HD // 4
    inv = ROPE_THETA ** (-jnp.arange(n, dtype=F32) / n)
    ang = jnp.concatenate([row[:, None] * inv, col[:, None] * inv], axis=-1)
    cos, sin = jnp.cos(ang), jnp.sin(ang)
    cosf = jnp.tile(jnp.concatenate([cos, cos], axis=-1), (1, LANES // HD))
    sinf = jnp.tile(jnp.concatenate([-sin, sin], axis=-1), (1, LANES // HD))
    eblk = jnp.asarray(np.kron(np.eye(LANES // HD), np.ones((HD, HD))), BF16)
    tab = pl.BlockSpec((tm, LANES), lambda i: (i, 0))
    return pl.pallas_call(
        functools.partial(_qkv_kernel, rope=rope),
        out_shape=(jax.ShapeDtypeStruct((D, lx), BF16),
                   jax.ShapeDtypeStruct((lx, kvw), BF16),
                   jax.ShapeDtypeStruct((KVH * VT_ROWS, lx), BF16)),
        grid=(lx // tm,),
        in_specs=[
            pl.BlockSpec((tm, D), lambda i: (i, 0)),
            vec, vec, vec,
            _resident((D, D + 2 * kvw)),
            vec,
            pl.BlockSpec((1, kvw), lambda i: (0, 0)),
            pl.BlockSpec((LANES, LANES), lambda i: (0, 0)),
            tab, tab,
        ],
        out_specs=(pl.BlockSpec((D, tm), lambda i: (0, i)),
                   pl.BlockSpec((tm, kvw), lambda i: (i, 0)),
                   pl.BlockSpec((KVH * VT_ROWS, tm), lambda i: (0, i))),
        compiler_params=_cparams("parallel"),
        name="qkv_proj",
    )(x, _row(g), _row(sh), _row(sc), w_qkv, _row(jnp.tile(q_g, QH)), _row(jnp.tile(k_g, KVH)), eblk, cosf, sinf)


def _flash_kernel(qt_ref, k_ref, vt_ref, o_ref, qbd_scr, m_scr, acc_scr, *, tq, ts, tc, nkv, bounded):
    j = pl.program_id(1)
    gq = QH // KVH
    mcols = gq * tq

    @pl.when(j == 0)
    def _():
        qbd_scr[...] = jnp.zeros_like(qbd_scr)
        for h in range(QH):
            g = h // gq
            qbd_scr[g * HD:(g + 1) * HD, h * tq:(h + 1) * tq] = qt_ref[h * HD:(h + 1) * HD, :]
        m_scr[...] = jnp.full(m_scr.shape, -jnp.inf, F32)
        acc_scr[...] = jnp.zeros_like(acc_scr)

    stages = [(g, c) for c in range(ts // tc) for g in range(KVH)]

    def scores(g, c):
        return jnp.dot(k_ref[c * tc:(c + 1) * tc, :], qbd_scr[:, g * mcols:(g + 1) * mcols],
                       preferred_element_type=F32)

    pending = [scores(*st) for st in stages[:FLASH_LOOKAHEAD]]
    for idx, (g, c) in enumerate(stages):
        s = pending.pop(0)
        if idx + FLASH_LOOKAHEAD < len(stages):
            pending.append(scores(*stages[idx + FLASH_LOOKAHEAD]))
        vt = vt_ref[g * VT_ROWS:(g + 1) * VT_ROWS, c * tc:(c + 1) * tc]
        if bounded:
            acc_scr[g] += jnp.dot(vt, jnp.exp2(s).astype(BF16), preferred_element_type=F32)
        else:
            m_prev = m_scr[g]
            m_new = jnp.maximum(m_prev, jnp.max(s, axis=0, keepdims=True))
            alpha = jnp.exp2(m_prev - m_new)
            p = jnp.exp2(s - m_new).astype(BF16)
            acc_scr[g] = alpha * acc_scr[g] + jnp.dot(vt, p, preferred_element_type=F32)
            m_scr[g] = m_new

    @pl.when(j == nkv - 1)
    def _():
        rows = []
        for g in range(KVH):
            acc = acc_scr[g]
            o = acc[:HD, :] / acc[HD:HD + 1, :]
            rows += [o[:, r * tq:(r + 1) * tq] for r in range(gq)]
        o_ref[...] = jnp.concatenate(rows, axis=0).T.astype(o_ref.dtype)


def _flash(qt, k, vt, s_len, ts, bounded=False):
    lq = qt.shape[1]
    tq = min(128, lq)
    nkv = s_len // ts
    gq = QH // KVH
    kvw = KVH * HD
    tc = 2 * LANES if ts % (2 * LANES) == 0 else LANES
    return pl.pallas_call(
        functools.partial(_flash_kernel, tq=tq, ts=ts, tc=tc, nkv=nkv, bounded=bounded),
        out_shape=jax.ShapeDtypeStruct((lq, D), BF16),
        grid=(lq // tq, nkv),
        in_specs=[
            pl.BlockSpec((D, tq), lambda i, j: (0, i)),
            pl.BlockSpec((ts, kvw), lambda i, j: (j, 0)),
            pl.BlockSpec((KVH * VT_ROWS, ts), lambda i, j: (0, j)),
        ],
        out_specs=pl.BlockSpec((tq, D), lambda i, j: (i, 0)),
        scratch_shapes=[
            pltpu.VMEM((kvw, QH * tq), BF16),
            pltpu.VMEM((KVH, 1, gq * tq), F32),
            pltpu.VMEM((KVH, VT_ROWS, gq * tq), F32),
        ],
        compiler_params=_cparams("parallel", "arbitrary"),
        name="flash_attn",
    )(qt, k, vt)


def _halo_specs(tm, lx):
    nb = lx // SUBLANES
    step = tm // SUBLANES
    prev = pl.BlockSpec((SUBLANES, D), lambda i: (jnp.maximum(i * step - 1, 0), 0))
    nxt = pl.BlockSpec((SUBLANES, D), lambda i: (jnp.minimum((i + 1) * step, nb - 1), 0))
    return prev, nxt


def _conv3(p_main, p_halo, cw, cb, first, last):
    tm = p_main.shape[0]
    rid = lax.broadcasted_iota(jnp.int32, p_main.shape, 0)
    before = jnp.where(first, 0.0, p_halo[SUBLANES - 1:SUBLANES, :])
    after = jnp.where(last, 0.0, p_halo[SUBLANES:SUBLANES + 1, :])
    up = jnp.where(rid == 0, before, pltpu.roll(p_main, 1, axis=0))
    dn = jnp.where(rid == tm - 1, after, pltpu.roll(p_main, tm - 1, axis=0))
    return cw[0:1, :] * up + cw[1:2, :] * p_main + cw[2:3, :] * dn + cb


def _norm_halo(xm_ref, xp_ref, xn_ref, g_ref, sh_ref, sc_ref):
    g, sh, sc = g_ref[...], sh_ref[...], sc_ref[...]
    h = _normmod(xm_ref[...], g, sh, sc).astype(BF16)
    hh = jnp.concatenate([_normmod(xp_ref[...], g, sh, sc), _normmod(xn_ref[...], g, sh, sc)], axis=0).astype(BF16)
    return h, hh


def _hy_in_kernel(xm_ref, xp_ref, xn_ref, g_ref, sh_ref, sc_ref, w_ref, cw_ref, cb_ref, x0_ref, ut_ref, *, nt):
    i = pl.program_id(0)
    first, last = i == 0, i == nt - 1
    h, hh = _norm_halo(xm_ref, xp_ref, xn_ref, g_ref, sh_ref, sc_ref)

    def branch(b):
        sl = slice(b * D, (b + 1) * D)
        pm = jnp.dot(h, w_ref[:, sl], preferred_element_type=F32)
        ph = jnp.dot(hh, w_ref[:, sl], preferred_element_type=F32)
        return _conv3(pm, ph, cw_ref[:, sl], cb_ref[:, sl], first, last)

    x0_ref[...] = branch(0)
    u = branch(1) * branch(2)
    ut_ref[...] = u.T


def _hy_in(x, g, sh, sc, w_in, conv_w, conv_b):
    lx = x.shape[0]
    tm = min(512, lx)
    nt = lx // tm
    vec = pl.BlockSpec((1, D), lambda i: (0, 0))
    prev, nxt = _halo_specs(tm, lx)
    return pl.pallas_call(
        functools.partial(_hy_in_kernel, nt=nt),
        out_shape=(jax.ShapeDtypeStruct((lx, D), F32), jax.ShapeDtypeStruct((D, lx), F32)),
        grid=(nt,),
        in_specs=[
            pl.BlockSpec((tm, D), lambda i: (i, 0)), prev, nxt,
            vec, vec, vec,
            _resident((D, 3 * D)),
            pl.BlockSpec((3, 3 * D), lambda i: (0, 0)),
            pl.BlockSpec((1, 3 * D), lambda i: (0, 0)),
        ],
        out_specs=(pl.BlockSpec((tm, D), lambda i: (i, 0)), pl.BlockSpec((D, tm), lambda i: (0, i))),
        compiler_params=_cparams("parallel"),
        name="hyena_in",
    )(x, x, x, _row(g), _row(sh), _row(sc), w_in, conv_w, _row(conv_b))


def _hy_filter_kernel(w1_ref, b1_ref, w2_ref, b2_ref, w3_ref, fr_ref, dl_ref, sk_ref, kt_ref, *, tm, ltrue):
    i = pl.program_id(0)
    shape = (tm, LANES)
    pos = (lax.broadcasted_iota(jnp.int32, shape, 0) + i * tm).astype(F32)
    lane = lax.broadcasted_iota(jnp.int32, shape, 1)
    t = pos / (ltrue - 1)
    w = 2.0 * math.pi * pos / ltrue
    band = jnp.where(lane <= HY_BANDS, lane - 1, lane - 1 - HY_BANDS).astype(F32)
    f = 1e-4 + band * ((HY_BANDS - 1 - 1e-4) / (HY_BANDS - 1))
    zf = w * f
    feats = jnp.where(lane == 0, t, jnp.where(lane <= HY_BANDS, jnp.cos(zf),
                                              jnp.where(lane <= 2 * HY_BANDS, -jnp.sin(zf), 0.0)))
    fr = fr_ref[...]
    hid = jnp.sin(fr * (jnp.dot(feats, w1_ref[...], preferred_element_type=F32, precision=HIGHEST) + b1_ref[...]))
    hid = jnp.sin(fr * (jnp.dot(hid, w2_ref[...], preferred_element_type=F32, precision=HIGHEST) + b2_ref[...]))
    k = jnp.dot(hid, w3_ref[...], preferred_element_type=F32, precision=HIGHEST)
    k = k * jnp.exp(-t[:, 0:1] * dl_ref[...])
    rid = lax.broadcasted_iota(jnp.int32, k.shape, 0) + i * tm
    cid = lax.broadcasted_iota(jnp.int32, k.shape, 1)
    k = jnp.where(rid == 0, jnp.where(cid < D, k + sk_ref[...], 0.0), k)
    k = jnp.where(rid < ltrue, k, 0.0)
    kt_ref[...] = k.T


def _hy_filter(ltrue, lpad, w1, b1, w2, b2, w3, freq, skip):
    tm = min(512, lpad)
    w1p = jnp.zeros((LANES, HY_FILT_W), F32).at[:HY_EMB].set(w1)
    deltas = jnp.abs(jnp.linspace(HY_MIN_DECAY, HY_MAX_DECAY, D, dtype=F32))
    full = lambda a: pl.BlockSpec(a.shape, lambda i: (0,) * a.ndim)
    args = (w1p, _row(b1), w2, _row(b2), w3, _row(freq), _row(jnp.tile(deltas, 2)),
            _row(jnp.concatenate([skip, jnp.zeros((D,), F32)])))
    return pl.pallas_call(
        functools.partial(_hy_filter_kernel, tm=tm, ltrue=ltrue),
        out_shape=jax.ShapeDtypeStruct((2 * D, lpad), F32),
        grid=(lpad // tm,),
        in_specs=[full(a) for a in args],
        out_specs=pl.BlockSpec((2 * D, tm), lambda i: (0, i)),
        compiler_params=_cparams("parallel"),
        name="hyena_filter",
    )(*args)


def _dft_consts(nh):
    n1 = 2 * nh
    n = n1 * DFT_N2
    k1 = np.arange(n1)[:, None].astype(np.float64)
    a1 = 2.0 * np.pi * k1 * np.arange(nh)[None, :] / n1
    f1 = np.concatenate([np.cos(a1), -np.sin(a1)], axis=0)
    at = 2.0 * np.pi * ((np.arange(n1)[:, None] * np.arange(DFT_N2)[None, :]) % n) / n
    a2 = 2.0 * np.pi * ((np.arange(DFT_N2)[:, None] * np.arange(DFT_N2)[None, :]) % DFT_N2) / DFT_N2
    c2, s2 = np.cos(a2), np.sin(a2)
    f2 = np.block([[c2, -s2], [s2, c2]])
    g2 = np.block([[c2, s2], [-s2, c2]])
    g1 = np.concatenate([np.cos(a1).T, -np.sin(a1).T], axis=1) / n
    as32 = lambda a: jnp.asarray(a, F32)

    def parts(a):
        hi = a.astype(BF16)
        if DFT_SPLIT == 1:
            return (jnp.asarray(hi),)
        return (jnp.asarray(hi), jnp.asarray((a - hi.astype(np.float64)).astype(BF16)))

    return parts(f1), as32(np.cos(at)), as32(np.sin(at)), parts(f2), parts(g2), parts(g1)


def _split_bf16(a):
    hi = a.astype(BF16)
    if DFT_SPLIT == 1:
        return (hi,)
    return (hi, (a - hi.astype(F32)).astype(BF16))


def _split_dot(a, b):
    out = jnp.dot(a[0], b[0], preferred_element_type=F32)
    if DFT_SPLIT > 1:
        out = out + jnp.dot(a[1], b[0], preferred_element_type=F32) + jnp.dot(a[0], b[1], preferred_element_type=F32)
    return out


def _load_parts(refs):
    return tuple(r[...] for r in refs)


def _dft_fwd(xs, f1, twc, tws, f2, cb, n1):
    xcat = jnp.concatenate(xs, axis=1)
    a = _split_dot(f1, _split_bf16(xcat))
    rows = []
    for c in range(cb):
        ar = a[:n1, c * DFT_N2:(c + 1) * DFT_N2]
        ai = a[n1:, c * DFT_N2:(c + 1) * DFT_N2]
        rows.append(jnp.concatenate([ar * twc + ai * tws, ai * twc - ar * tws], axis=1))
    return _split_dot(_split_bf16(jnp.concatenate(rows, axis=0)), f2)


def _hy_conv_kernel(x_ref, kf_ref, kb_ref, *refs, cb, n1):
    ns = DFT_SPLIT
    f1, (twc_ref, tws_ref), f2 = refs[:ns], refs[ns:ns + 2], refs[ns + 2:2 * ns + 2]
    g2, g1, o_ref = refs[2 * ns + 2:3 * ns + 2], refs[3 * ns + 2:4 * ns + 2], refs[4 * ns + 2]
    twc, tws = twc_ref[...], tws_ref[...]
    xs = [r[c] for r in (x_ref, kf_ref, kb_ref) for c in range(cb)]
    spec_all = _dft_fwd(xs, _load_parts(f1), twc, tws, _load_parts(f2), 3 * cb, n1)
    rows = cb * n1
    spec, hf, hb = spec_all[:rows], spec_all[rows:2 * rows], spec_all[2 * rows:]
    hr = hf[:, :DFT_N2] + hb[:, :DFT_N2]
    hi = hf[:, DFT_N2:] - hb[:, DFT_N2:]
    xr, xi = spec[:, :DFT_N2], spec[:, DFT_N2:]
    y = jnp.concatenate([xr * hr - xi * hi, xr * hi + xi * hr], axis=1)
    b = _split_dot(_split_bf16(y), _load_parts(g2))
    cols = []
    for c in range(cb):
        br = b[c * n1:(c + 1) * n1, :DFT_N2]
        bi = b[c * n1:(c + 1) * n1, DFT_N2:]
        cols.append(jnp.concatenate([br * twc - bi * tws, bi * twc + br * tws], axis=0))
    out = _split_dot(_load_parts(g1), _split_bf16(jnp.concatenate(cols, axis=1)))
    for c in range(cb):
        o_ref[c] = out[:, c * DFT_N2:(c + 1) * DFT_N2]


def _hy_longconv(ut, kt, lpad):
    nh = lpad // DFT_N2
    n1 = 2 * nh
    cb = max(8, min(64, 2048 // n1))
    f1, twc, tws, f2, g2, g1 = _dft_consts(nh)
    consts = f1 + (twc, tws) + f2 + g2 + g1
    full = lambda a: pl.BlockSpec(a.shape, lambda i: (0,) * a.ndim)
    k3 = kt.reshape(2 * D, nh, DFT_N2)
    u3 = ut.reshape(D, nh, DFT_N2)
    nb = D // cb
    y3 = pl.pallas_call(
        functools.partial(_hy_conv_kernel, cb=cb, n1=n1),
        out_shape=jax.ShapeDtypeStruct((D, nh, DFT_N2), F32),
        grid=(nb,),
        in_specs=[pl.BlockSpec((cb, nh, DFT_N2), lambda i: (i, 0, 0)),
                  pl.BlockSpec((cb, nh, DFT_N2), lambda i: (i, 0, 0)),
                  pl.BlockSpec((cb, nh, DFT_N2), lambda i: (i + nb, 0, 0))]
                 + [full(a) for a in consts],
        out_specs=pl.BlockSpec((cb, nh, DFT_N2), lambda i: (i, 0, 0)),
        compiler_params=_cparams("parallel"),
        name="hyena_longconv",
    )(u3, k3, k3, *consts)
    return y3.reshape(D, lpad)


def _hy_out_kernel(x_ref, x0_ref, yt_ref, w_ref, gate_ref, o_ref):
    a = (x0_ref[...] * yt_ref[...].T).astype(BF16)
    y = jnp.dot(a, w_ref[...], preferred_element_type=F32)
    o_ref[...] = x_ref[...] + gate_ref[...] * y


def _hy_out(x, x0, yt, w_out, gate):
    lx = x.shape[0]
    tm = min(512, lx)
    return pl.pallas_call(
        _hy_out_kernel,
        out_shape=jax.ShapeDtypeStruct((lx, D), F32),
        grid=(lx // tm,),
        in_specs=[
            pl.BlockSpec((tm, D), lambda i: (i, 0)),
            pl.BlockSpec((tm, D), lambda i: (i, 0)),
            pl.BlockSpec((D, tm), lambda i: (0, i)),
            _resident((D, D)),
            pl.BlockSpec((1, D), lambda i: (0, 0)),
        ],
        out_specs=pl.BlockSpec((tm, D), lambda i: (i, 0)),
        compiler_params=_cparams("parallel"),
        name="hyena_out",
    )(x, x0, yt, w_out, _row(gate))


def _hyena(x, g, sh, sc, gate, w_in, conv_w, conv_b, w1, b1, w2, b2, w3, freq, skip, w_out):
    lx = x.shape[0]
    lpad = max(lx, SUBLANES * DFT_N2)
    x0, ut = _hy_in(x, g, sh, sc, w_in, conv_w, conv_b)
    if lpad != lx:
        ut = jnp.pad(ut, ((0, 0), (0, lpad - lx)))
    kt = _hy_filter(lx, lpad, w1, b1, w2, b2, w3, freq, skip)
    yt = _hy_longconv(ut, kt, lpad)[:, :lx]
    return _hy_out(x, x0, yt, w_out, gate)


def _ssd_in_kernel(xm_ref, xp_ref, xn_ref, g_ref, sh_ref, sc_ref, wz_ref, wx_ref, wd_ref, cw_ref, cb_ref, db_ref,
                   zg_ref, xs_ref, bm_ref, cm_ref, dt_ref, *, nt):
    i = pl.program_id(0)
    first, last = i == 0, i == nt - 1
    h, hh = _norm_halo(xm_ref, xp_ref, xn_ref, g_ref, sh_ref, sc_ref)
    zg_ref[...] = jnp.dot(h, wz_ref[...], preferred_element_type=F32)
    pm = jnp.dot(h, wx_ref[...], preferred_element_type=F32)
    ph = jnp.dot(hh, wx_ref[...], preferred_element_type=F32)
    xbc = _silu(_conv3(pm, ph, cw_ref[...], cb_ref[...], first, last))
    xs_ref[...] = xbc[:, :SSD_INNER]
    bm_ref[...] = xbc[:, SSD_INNER:SSD_INNER + SSD_BC]
    cm_ref[...] = xbc[:, SSD_INNER + SSD_BC:]
    dt = jnp.dot(h, wd_ref[...], preferred_element_type=F32) + db_ref[...]
    dt = jnp.maximum(dt, 0.0) + jnp.log1p(jnp.exp(-jnp.abs(dt)))
    lane = lax.broadcasted_iota(jnp.int32, dt.shape, 1)
    dt = jnp.where((lane % LANES) < SSD_HEADS, dt, 0.0)
    dt_ref[0] = dt[:, :LANES]
    dt_ref[1] = dt[:, LANES:]


def _ssd_in(x, g, sh, sc, w_in, conv_w, conv_b, dt_bias):
    lx = x.shape[0]
    tm = min(256, lx)
    nt = lx // tm
    vec = pl.BlockSpec((1, D), lambda i: (0, 0))
    prev, nxt = _halo_specs(tm, lx)
    wz = w_in[:, :SSD_INNER]
    wx = w_in[:, SSD_INNER:SSD_INNER + SSD_CONV_DIM]
    wdt = w_in[:, SSD_INNER + SSD_CONV_DIM:]
    pad = LANES - SSD_HEADS
    wd = jnp.concatenate([jnp.pad(wdt[:, :SSD_HEADS], ((0, 0), (0, pad))),
                          jnp.pad(wdt[:, SSD_HEADS:], ((0, 0), (0, pad)))], axis=1)
    db = jnp.pad(dt_bias, ((0, 0), (0, pad))).reshape(1, 2 * LANES)
    full = lambda a: pl.BlockSpec(a.shape, lambda i: (0,) * a.ndim)
    rowblk = lambda w: pl.BlockSpec((tm, w), lambda i: (i, 0))
    return pl.pallas_call(
        functools.partial(_ssd_in_kernel, nt=nt),
        out_shape=(jax.ShapeDtypeStruct((lx, SSD_INNER), F32), jax.ShapeDtypeStruct((lx, SSD_INNER), F32),
                   jax.ShapeDtypeStruct((lx, SSD_BC), F32), jax.ShapeDtypeStruct((lx, SSD_BC), F32),
                   jax.ShapeDtypeStruct((2, lx, LANES), F32)),
        grid=(nt,),
        in_specs=[rowblk(D), prev, nxt, vec, vec, vec, _resident(wz.shape), _resident(wx.shape), _resident(wd.shape),
                  pl.BlockSpec((3, SSD_CONV_DIM), lambda i: (0, 0)),
                  pl.BlockSpec((1, SSD_CONV_DIM), lambda i: (0, 0)),
                  pl.BlockSpec((1, 2 * LANES), lambda i: (0, 0))],
        out_specs=(rowblk(SSD_INNER), rowblk(SSD_INNER), rowblk(SSD_BC), rowblk(SSD_BC),
                   pl.BlockSpec((2, tm, LANES), lambda i: (0, i, 0))),
        compiler_params=_cparams("parallel"),
        name="ssd_in",
    )(x, x, x, _row(g), _row(sh), _row(sc), wz, wx, wd, conv_w, _row(conv_b), db)


def _expand_heads(arr, e_ref):
    hi = arr.astype(BF16)
    lo = (arr - hi.astype(F32)).astype(BF16)
    e = e_ref[...]
    return jnp.dot(hi, e, preferred_element_type=F32) + jnp.dot(lo, e, preferred_element_type=F32)


def _ssd_scan_kernel(xs_ref, bm_ref, cm_ref, dt_ref, a_ref, tri_ref, e_ref, h0_ref, y_ref, hfin_ref, h_scr,
                     *, nc, need_y):
    s = pl.program_id(1)
    q = SSD_CHUNK
    npair = SSD_HEADS // 2
    ppg = npair // SSD_GROUPS

    @pl.when(s == 0)
    def _():
        h_scr[...] = h0_ref[0]

    tri = tri_ref[0]
    keep = tri > 0.5
    dt = dt_ref[0]
    a = dt * a_ref[0]
    acs = jnp.dot(tri, a, preferred_element_type=F32, precision=HIGHEST)
    total = jnp.sum(a, axis=0, keepdims=True)
    wend_x = _expand_heads(jnp.exp(total - acs) * dt, e_ref)
    etot_x = _expand_heads(jnp.broadcast_to(jnp.exp(total), (SUBLANES, LANES)), e_ref)[0:1, :]
    if need_y:
        eacs_x = _expand_heads(jnp.exp(acs), e_ref)
        acs_t = acs.T
        dt_t = dt.T
    lane = lax.broadcasted_iota(jnp.int32, (q, LANES), 1)
    left = lane < SSD_P

    for g in range(SSD_GROUPS):
        bg = bm_ref[:, g * SSD_STATE:(g + 1) * SSD_STATE]
        cg = cm_ref[:, g * SSD_STATE:(g + 1) * SSD_STATE].astype(BF16)
        bgb = bg.astype(BF16)
        bgt = bg.T.astype(BF16)
        if need_y:
            cb = lax.dot_general(cg, bgb, (((1,), (1,)), ((), ())), preferred_element_type=F32)
        for r in range(ppg):
            pidx = g * ppg + r
            psl = slice(pidx * LANES, (pidx + 1) * LANES)
            xp = xs_ref[:, psl]
            hs = h_scr[pidx]
            if need_y:
                xpb = xp.astype(BF16)
                yd = []
                for hd in (2 * pidx, 2 * pidx + 1):
                    seg = jnp.broadcast_to(acs[:, hd:hd + 1], (q, LANES)) - acs_t[hd:hd + 1, :]
                    lm = jnp.exp(jnp.where(keep, seg, -jnp.inf))
                    m = (cb * lm * dt_t[hd:hd + 1, :]).astype(BF16)
                    yd.append(jnp.dot(m, xpb, preferred_element_type=F32))
                ydiag = jnp.where(left, yd[0], yd[1])
                yoff = jnp.dot(cg, hs.astype(BF16), preferred_element_type=F32) * eacs_x[:, psl]
                y_ref[0, :, psl] = ydiag + yoff
            xw = (xp * wend_x[:, psl]).astype(BF16)
            st = jnp.dot(bgt, xw, preferred_element_type=F32)
            h_scr[pidx] = hs * etot_x[:, psl] + st

    @pl.when(s == nc - 1)
    def _():
        hfin_ref[0] = h_scr[...]


def _ssd_scan(xs, bm, cm, dt2, a_log, h0, need_y):
    lx = xs.shape[0]
    q = SSD_CHUNK
    nc = lx // q
    npair = SSD_HEADS // 2
    a = -jnp.exp(a_log.astype(F32))
    a_pad = jnp.pad(a, ((0, 0), (0, LANES - SSD_HEADS))).reshape(2, 1, LANES)
    lower = np.tril(np.ones((q, q), np.float32))
    tri = jnp.asarray(np.stack([lower, lower.T]))
    chunk = lambda d, s: jnp.where(d == 0, s, nc - 1 - s)
    out_shape = [jax.ShapeDtypeStruct((2, lx if need_y else q, SSD_INNER), F32),
                 jax.ShapeDtypeStruct((2, npair, SSD_STATE, 2 * SSD_P), F32)]
    y_spec = (pl.BlockSpec((1, q, SSD_INNER), lambda d, s: (d, chunk(d, s), 0)) if need_y
              else pl.BlockSpec((1, q, SSD_INNER), lambda d, s: (d, 0, 0)))
    st_spec = pl.BlockSpec((1, npair, SSD_STATE, 2 * SSD_P), lambda d, s: (d, 0, 0, 0))
    expand = jnp.asarray(np.kron(np.eye(LANES)[:, :SSD_HEADS], np.ones((1, SSD_P))), BF16)
    y, hfin = pl.pallas_call(
        functools.partial(_ssd_scan_kernel, nc=nc, need_y=need_y),
        out_shape=out_shape,
        grid=(2, nc),
        in_specs=[
            pl.BlockSpec((q, SSD_INNER), lambda d, s: (chunk(d, s), 0)),
            pl.BlockSpec((q, SSD_BC), lambda d, s: (chunk(d, s), 0)),
            pl.BlockSpec((q, SSD_BC), lambda d, s: (chunk(d, s), 0)),
            pl.BlockSpec((1, q, LANES), lambda d, s: (d, chunk(d, s), 0)),
            pl.BlockSpec((1, 1, LANES), lambda d, s: (d, 0, 0)),
            pl.BlockSpec((1, q, q), lambda d, s: (d, 0, 0)),
            pl.BlockSpec((LANES, SSD_INNER), lambda d, s: (0, 0)),
            st_spec,
        ],
        out_specs=[y_spec, st_spec],
        scratch_shapes=[pltpu.VMEM((npair, SSD_STATE, 2 * SSD_P), F32)],
        compiler_params=_cparams("arbitrary", "arbitrary"),
        name="ssd_scan",
    )(xs, bm, cm, dt2, a_pad, tri, expand, h0)
    return (y if need_y else None), hfin


def _ssd_out_kernel(x_ref, y_ref, xs_ref, zg_ref, dsk_ref, ng_ref, w_ref, gate_ref, o_ref):
    y = y_ref[0] + y_ref[1] + xs_ref[...] * dsk_ref[...]
    y = y * _silu(zg_ref[...])
    gw = SSD_INNER // SSD_GROUPS
    parts = []
    for g in range(SSD_GROUPS):
        yg = y[:, g * gw:(g + 1) * gw]
        ms = jnp.mean(yg * yg, axis=-1, keepdims=True)
        parts.append(yg * lax.rsqrt(ms + NORM_EPS) * ng_ref[:, g * gw:(g + 1) * gw])
    yn = jnp.concatenate(parts, axis=1).astype(BF16)
    o_ref[...] = x_ref[...] + gate_ref[...] * jnp.dot(yn, w_ref[...], preferred_element_type=F32)


def _ssd_out(x, y2, xs, zg, d_skip, norm_g, w_out, gate):
    lx = x.shape[0]
    tm = min(256, lx)
    rowblk = lambda w: pl.BlockSpec((tm, w), lambda i: (i, 0))
    vecw = pl.BlockSpec((1, SSD_INNER), lambda i: (0, 0))
    return pl.pallas_call(
        _ssd_out_kernel,
        out_shape=jax.ShapeDtypeStruct((lx, D), F32),
        grid=(lx // tm,),
        in_specs=[rowblk(D), pl.BlockSpec((2, tm, SSD_INNER), lambda i: (0, i, 0)), rowblk(SSD_INNER),
                  rowblk(SSD_INNER), vecw, vecw, _resident((SSD_INNER, D)),
                  pl.BlockSpec((1, D), lambda i: (0, 0))],
        out_specs=rowblk(D),
        compiler_params=_cparams("parallel"),
        name="ssd_out",
    )(x, y2, xs, zg, _row(jnp.repeat(d_skip, SSD_P)), _row(norm_g), w_out, _row(gate))


def kernel(x, c, ctx, c_ctx, norm1_g, norm2_g, mod_w, mod_b, ffn_w_in, ffn_w_out, final_g, gm_w_in, gm_ln_g, gm_ln_b, gm_ws, gm_bs, gm_w_out, at_w_qkv, at_q_g, at_k_g, at_w_out, hy_w_in, hy_conv_w, hy_conv_b, hy_filt_w1, hy_filt_b1, hy_filt_w2, hy_filt_b2, hy_filt_w3, hy_filt_freq, hy_skip, hy_w_out, ssd_w_in, ssd_conv_w, ssd_conv_b, ssd_a_log, ssd_dt_bias, ssd_d_skip, ssd_norm_g, ssd_w_out):
    batch, seq, _ = x.shape
    assert batch == 1, "kernels are written for a single sequence"
    nctx = ctx.shape[1]
    xl = x[0]
    z = ctx[0]
    mods = _modulation(c[0], c_ctx, mod_w, mod_b)
    bf = lambda w: w.astype(BF16)

    for i in range(DEPTH):
        m, j = i % 4, i // 4
        want_ctx = i < DEPTH - 1
        ml = [mods[i, 0, k * D:(k + 1) * D] for k in range(6)]
        mc = [mods[i, 1, k * D:(k + 1) * D] for k in range(6)]
        n1 = norm1_g[i]
        if m == 0:
            p = (bf(gm_w_in[j]), gm_ln_g[j], gm_ln_b[j], bf(gm_ws[j]), gm_bs[j], bf(gm_w_out[j]))
            xl = _gmlp(xl, n1, ml[0], ml[1], ml[2], *p)
            if want_ctx:
                z = _gmlp(z, n1, mc[0], mc[1], mc[2], *p)
        elif m == 1:
            wq, wo = bf(at_w_qkv[j]), bf(at_w_out[j])
            qt_l, k_l, vt_l = _qkv(xl, n1, ml[0], ml[1], wq, at_q_g[j], at_k_g[j], rope=True)
            qt_c, k_c, vt_c = _qkv(z, n1, mc[0], mc[1], wq, at_q_g[j], at_k_g[j], rope=False)
            k_all = jnp.concatenate([k_c, k_l], axis=0)
            vt_all = jnp.concatenate([vt_c, vt_l], axis=1)
            stot = nctx + seq
            ts = next(t for t in (3328, 1280, 1024, 512, 256) if stot % t == 0)
            score_bound = (HD ** 0.5 * LOG2E) * jnp.max(jnp.abs(at_q_g[j])) * jnp.max(jnp.abs(at_k_g[j]))
            o_l = lax.cond(score_bound <= FLASH_SCORE_BOUND,
                           lambda: _flash(qt_l, k_all, vt_all, stot, ts, bounded=True),
                           lambda: _flash(qt_l, k_all, vt_all, stot, ts, bounded=False))
            xl = _outproj(xl, o_l, wo, ml[2])
            if want_ctx:
                o_c = _flash(qt_c, k_all, vt_all, nctx, nctx)
                z = _outproj(z, o_c, wo, mc[2])
        elif m == 2:
            p = (bf(hy_w_in[j]), hy_conv_w[j], hy_conv_b[j], hy_filt_w1[j], hy_filt_b1[j], hy_filt_w2[j],
                 hy_filt_b2[j], hy_filt_w3[j], hy_filt_freq[j], hy_skip[j], bf(hy_w_out[j]))
            xl = _hyena(xl, n1, ml[0], ml[1], ml[2], *p)
            if want_ctx:
                z = _hyena(z, n1, mc[0], mc[1], mc[2], *p)
        else:
            win, wo = bf(ssd_w_in[j]), bf(ssd_w_out[j])
            pin = (win, ssd_conv_w[j], ssd_conv_b[j], ssd_dt_bias[j])
            zg_c, xs_c, bm_c, cm_c, dt_c = _ssd_in(z, n1, mc[0], mc[1], *pin)
            zg_l, xs_l, bm_l, cm_l, dt_l = _ssd_in(xl, n1, ml[0], ml[1], *pin)
            h0 = jnp.zeros((2, SSD_HEADS // 2, SSD_STATE, 2 * SSD_P), F32)
            y_c, h_ctx = _ssd_scan(xs_c, bm_c, cm_c, dt_c, ssd_a_log[j], h0, want_ctx)
            y_l, _ = _ssd_scan(xs_l, bm_l, cm_l, dt_l, ssd_a_log[j], h_ctx, True)
            xl = _ssd_out(xl, y_l, xs_l, zg_l, ssd_d_skip[j], ssd_norm_g[j], wo, ml[2])
            if want_ctx:
                z = _ssd_out(z, y_c, xs_c, zg_c, ssd_d_skip[j], ssd_norm_g[j], wo, mc[2])
        wi, wo2 = bf(ffn_w_in[i]), bf(ffn_w_out[i])
        xl = _ffn(xl, norm2_g[i], ml[3], ml[4], ml[5], wi, wo2, final_g, final=(i == DEPTH - 1))
        if want_ctx:
            z = _ffn(z, norm2_g[i], mc[3], mc[4], mc[5], wi, wo2, final_g, final=False)
    return xl[None]
```

```python
import functools
import math

import numpy as np
import jax
import jax.numpy as jnp
from jax import lax
from jax.experimental import pallas as pl
from jax.experimental.pallas import tpu as pltpu

F32 = jnp.float32
BF16 = jnp.bfloat16
HIGHEST = lax.Precision.HIGHEST

D = 1024
DEPTH = 4
GRID_W = 64
NORM_EPS = 1e-6
FFN_HIDDEN = 2816
GM_CHUNK = 128
GM_WIDTH = 2 * D
GM_GROUPS = 8
GM_GW = GM_WIDTH // GM_GROUPS
HD = 64
QH = D // HD
KVH = 4
ROPE_THETA = 10000.0
LOG2E = math.log2(math.e)
FLASH_SCORE_BOUND = 30.0
FLASH_LOOKAHEAD = 2
HY_BANDS = 16
HY_EMB = 1 + 2 * HY_BANDS
HY_FILT_W = 64
HY_MAX_DECAY = math.log(1e-2) / 0.3
HY_MIN_DECAY = math.log(1e-2) / 1.5
SSD_INNER = 2 * D
SSD_P = 64
SSD_HEADS = SSD_INNER // SSD_P
SSD_GROUPS = 4
SSD_STATE = 128
SSD_CHUNK = 128
SSD_BC = SSD_GROUPS * SSD_STATE
SSD_CONV_DIM = SSD_INNER + 2 * SSD_BC

LANES = 128
SUBLANES = 8
VMEM_LIMIT_BYTES = 56 * 1024 * 1024
DFT_N2 = 128
DFT_SPLIT = 1


def _cparams(*sem):
    return pltpu.CompilerParams(dimension_semantics=sem, vmem_limit_bytes=VMEM_LIMIT_BYTES)


def _row(v):
    return v.reshape(1, -1)


def _normmod(x, g, shift, scale):
    ms = jnp.mean(x * x, axis=-1, keepdims=True)
    return x * lax.rsqrt(ms + NORM_EPS) * g * (1.0 + scale) + shift


def _silu(x):
    return x * jax.nn.sigmoid(x)


def _mod_kernel(cl_ref, cc_ref, w_ref, b_ref, o_ref):
    w = w_ref[0]
    for r, c_ref in enumerate((cl_ref, cc_ref)):
        a = _silu(c_ref[...])
        o_ref[0, r:r + 1, :] = jnp.sum(a * w, axis=0, keepdims=True) + b_ref[0]


def _modulation(c, c_ctx, mod_w, mod_b):
    tn = 1536
    n6 = 6 * D
    depth = mod_w.shape[0]
    return pl.pallas_call(
        _mod_kernel,
        out_shape=jax.ShapeDtypeStruct((depth, 2, n6), F32),
        grid=(depth, n6 // tn),
        in_specs=[
            pl.BlockSpec((D, 1), lambda i, n: (0, 0)),
            pl.BlockSpec((D, 1), lambda i, n: (0, 0)),
            pl.BlockSpec((1, D, tn), lambda i, n: (i, 0, n)),
            pl.BlockSpec((1, 1, tn), lambda i, n: (i, 0, n)),
        ],
        out_specs=pl.BlockSpec((1, 2, tn), lambda i, n: (i, 0, n)),
        compiler_params=_cparams("parallel", "parallel"),
        name="modulation",
    )(c.reshape(D, 1), c_ctx.reshape(D, 1), mod_w, mod_b.reshape(depth, 1, n6))


def _ffn_kernel(x_ref, g_ref, sh_ref, sc_ref, gate_ref, wi_ref, wo_ref, fg_ref, o_ref, *, final):
    x = x_ref[...]
    h = _normmod(x, g_ref[...], sh_ref[...], sc_ref[...]).astype(BF16)
    a = jnp.dot(h, wi_ref[:, :FFN_HIDDEN], preferred_element_type=F32)
    u = jnp.dot(h, wi_ref[:, FFN_HIDDEN:], preferred_element_type=F32)
    act = (_silu(a) * u).astype(BF16)
    y = x + gate_ref[...] * jnp.dot(act, wo_ref[...], preferred_element_type=F32)
    if final:
        ms = jnp.mean(y * y, axis=-1, keepdims=True)
        y = y * lax.rsqrt(ms + NORM_EPS) * fg_ref[...]
    o_ref[...] = y


def _resident(shape):
    return pl.BlockSpec(shape, lambda *_: (0,) * len(shape), pipeline_mode=pl.Buffered(1))


def _ffn(x, g, sh, sc, gate, w_in, w_out, final_g, final):
    lx = x.shape[0]
    tm = min(512, lx)
    vec = pl.BlockSpec((1, D), lambda i: (0, 0))
    return pl.pallas_call(
        functools.partial(_ffn_kernel, final=final),
        out_shape=jax.ShapeDtypeStruct((lx, D), F32),
        grid=(lx // tm,),
        in_specs=[
            pl.BlockSpec((tm, D), lambda i: (i, 0)),
            vec, vec, vec, vec,
            _resident((D, 2 * FFN_HIDDEN)),
            _resident((FFN_HIDDEN, D)),
            vec,
        ],
        out_specs=pl.BlockSpec((tm, D), lambda i: (i, 0)),
        compiler_params=_cparams("parallel"),
        name="ffn",
    )(x, _row(g), _row(sh), _row(sc), _row(gate), w_in, w_out, _row(final_g))


def _outproj_kernel(x_ref, a_ref, w_ref, gate_ref, o_ref):
    y = jnp.dot(a_ref[...], w_ref[...], preferred_element_type=F32)
    o_ref[...] = x_ref[...] + gate_ref[...] * y


def _outproj(x, a, w, gate):
    lx, kin = a.shape
    tm = min(512, lx)
    return pl.pallas_call(
        _outproj_kernel,
        out_shape=jax.ShapeDtypeStruct((lx, D), F32),
        grid=(lx // tm,),
        in_specs=[
            pl.BlockSpec((tm, D), lambda i: (i, 0)),
            pl.BlockSpec((tm, kin), lambda i: (i, 0)),
            _resident((kin, D)),
            pl.BlockSpec((1, D), lambda i: (0, 0)),
        ],
        out_specs=pl.BlockSpec((tm, D), lambda i: (i, 0)),
        compiler_params=_cparams("parallel"),
        name="outproj",
    )(x, a, w, _row(gate))


def _gmlp_kernel(x_ref, g_ref, sh_ref, sc_ref, gate_ref, win_ref, lng_ref, lnb_ref, ws_ref, bs_ref, wout_ref,
                 o_ref, *, tm):
    x = x_ref[...]
    h = _normmod(x, g_ref[...], sh_ref[...], sc_ref[...]).astype(BF16)
    t = jnp.dot(h, win_ref[...], preferred_element_type=F32)
    t = 0.5 * t * (1.0 + lax.erf(t * (1.0 / math.sqrt(2.0))))
    u = t[:, :GM_WIDTH]
    v = t[:, GM_WIDTH:]
    mu = jnp.mean(v, axis=-1, keepdims=True)
    vc = v - mu
    var = jnp.mean(vc * vc, axis=-1, keepdims=True)
    v = (vc * lax.rsqrt(var + NORM_EPS) * lng_ref[...] + lnb_ref[...]).astype(BF16)
    rows = []
    for q in range(tm // GM_CHUNK):
        cols = []
        for gidx in range(GM_GROUPS):
            vq = v[q * GM_CHUNK:(q + 1) * GM_CHUNK, gidx * GM_GW:(gidx + 1) * GM_GW]
            bias = bs_ref[gidx]
            m = jnp.dot(ws_ref[gidx], vq, preferred_element_type=F32)
            cols.append(m + jnp.concatenate([bias] * (GM_GW // LANES), axis=1))
        rows.append(jnp.concatenate(cols, axis=1))
    vm = jnp.concatenate(rows, axis=0)
    gated = (u * vm).astype(BF16)
    y = jnp.dot(gated, wout_ref[...], preferred_element_type=F32)
    o_ref[...] = x + gate_ref[...] * y


def _gmlp(x, g, sh, sc, gate, w_in, ln_g, ln_b, ws, bs, w_out):
    lx = x.shape[0]
    tm = min(256, lx)
    vec = pl.BlockSpec((1, D), lambda i: (0, 0))
    vecw = pl.BlockSpec((1, GM_WIDTH), lambda i: (0, 0))
    bsb = jnp.broadcast_to(bs[:, :, None], (GM_GROUPS, GM_CHUNK, LANES))
    return pl.pallas_call(
        functools.partial(_gmlp_kernel, tm=tm),
        out_shape=jax.ShapeDtypeStruct((lx, D), F32),
        grid=(lx // tm,),
        in_specs=[
            pl.BlockSpec((tm, D), lambda i: (i, 0)),
            vec, vec, vec, vec,
            _resident((D, 2 * GM_WIDTH)),
            vecw, vecw,
            pl.BlockSpec((GM_GROUPS, GM_CHUNK, GM_CHUNK), lambda i: (0, 0, 0)),
            pl.BlockSpec((GM_GROUPS, GM_CHUNK, LANES), lambda i: (0, 0, 0)),
            _resident((GM_WIDTH, D)),
        ],
        out_specs=pl.BlockSpec((tm, D), lambda i: (i, 0)),
        compiler_params=_cparams("parallel"),
        name="gmlp",
    )(x, _row(g), _row(sh), _row(sc), _row(gate), w_in, _row(ln_g), _row(ln_b), ws, bsb, w_out)


def _group_sumsq(t, e_ref):
    sq = t * t
    hi = sq.astype(BF16)
    lo = (sq - hi.astype(F32)).astype(BF16)
    outs = []
    for j in range(t.shape[1] // LANES):
        sl = slice(j * LANES, (j + 1) * LANES)
        outs.append(jnp.dot(hi[:, sl], e_ref[...], preferred_element_type=F32)
                    + jnp.dot(lo[:, sl], e_ref[...], preferred_element_type=F32))
    return jnp.concatenate(outs, axis=1)


def _rope(t, cosf, sinf):
    w = t.shape[1]
    lane = lax.broadcasted_iota(jnp.int32, t.shape, 1)
    first = (lane % HD) < (HD // 2)
    partner = jnp.where(first, pltpu.roll(t, w - HD // 2, axis=1), pltpu.roll(t, HD // 2, axis=1))
    reps = w // LANES
    c = jnp.concatenate([cosf] * reps, axis=1)
    s = jnp.concatenate([sinf] * reps, axis=1)
    return t * c + partner * s


def _qkv_kernel(x_ref, g_ref, sh_ref, sc_ref, w_ref, qg_ref, kg_ref, e_ref, cos_ref, sin_ref,
                qt_ref, k_ref, vt_ref, *, rope):
    h = _normmod(x_ref[...], g_ref[...], sh_ref[...], sc_ref[...]).astype(BF16)
    qkv = jnp.dot(h, w_ref[...], preferred_element_type=F32)
    q = qkv[:, :D]
    k = qkv[:, D:D + KVH * HD]
    v = qkv[:, D + KVH * HD:]
    q = q * lax.rsqrt(_group_sumsq(q, e_ref) * (1.0 / HD) + NORM_EPS) * qg_ref[...]
    k = k * lax.rsqrt(_group_sumsq(k, e_ref) * (1.0 / HD) + NORM_EPS) * kg_ref[...]
    if rope:
        q = _rope(q, cos_ref[...], sin_ref[...])
        k = _rope(k, cos_ref[...], sin_ref[...])
    qt_ref[...] = (q * (HD ** -0.5 * LOG2E)).T.astype(BF16)
    k_ref[...] = k.astype(BF16)
    vt_ref[...] = v.T.astype(BF16)


def _qkv(x, g, sh, sc, w_qkv, q_g, k_g, rope):
    lx = x.shape[0]
    tm = min(256, lx)
    vec = pl.BlockSpec((1, D), lambda i: (0, 0))
    kvw = KVH * HD
    rows = lx // GRID_W
    row = jnp.repeat(jnp.arange(rows, dtype=F32), GRID_W)
    col = jnp.tile(jnp.arange(GRID_W, dtype=F32), rows)
    n = HD // 4
    inv = ROPE_THETA ** (-jnp.arange(n, dtype=F32) / n)
    ang = jnp.concatenate([row[:, None] * inv, col[:, None] * inv], axis=-1)
    cos, sin = jnp.cos(ang), jnp.sin(ang)
    cosf = jnp.tile(jnp.concatenate([cos, cos], axis=-1), (1, LANES // HD))
    sinf = jnp.tile(jnp.concatenate([-sin, sin], axis=-1), (1, LANES // HD))
    eblk = jnp.asarray(np.kron(np.eye(LANES // HD), np.ones((HD, HD))), BF16)
    tab = pl.BlockSpec((tm, LANES), lambda i: (i, 0))
    return pl.pallas_call(
        functools.partial(_qkv_kernel, rope=rope),
        out_shape=(jax.ShapeDtypeStruct((D, lx), BF16),
                   jax.ShapeDtypeStruct((lx, kvw), BF16),
                   jax.ShapeDtypeStruct((kvw, lx), BF16)),
        grid=(lx // tm,),
        in_specs=[
            pl.BlockSpec((tm, D), lambda i: (i, 0)),
            vec, vec, vec,
            _resident((D, D + 2 * kvw)),
            vec,
            pl.BlockSpec((1, kvw), lambda i: (0, 0)),
            pl.BlockSpec((LANES, LANES), lambda i: (0, 0)),
            tab, tab,
        ],
        out_specs=(pl.BlockSpec((D, tm), lambda i: (0, i)),
                   pl.BlockSpec((tm, kvw), lambda i: (i, 0)),
                   pl.BlockSpec((kvw, tm), lambda i: (0, i))),
        compiler_params=_cparams("parallel"),
        name="qkv_proj",
    )(x, _row(g), _row(sh), _row(sc), w_qkv, _row(jnp.tile(q_g, QH)), _row(jnp.tile(k_g, KVH)), eblk, cosf, sinf)


def _flash_kernel(qt_ref, k_ref, vt_ref, o_ref, qbd_scr, m_scr, l_scr, acc_scr, *, tq, ts, tc, nkv, bounded):
    j = pl.program_id(1)
    gq = QH // KVH
    mcols = gq * tq

    @pl.when(j == 0)
    def _():
        qbd_scr[...] = jnp.zeros_like(qbd_scr)
        for h in range(QH):
            g = h // gq
            qbd_scr[g * HD:(g + 1) * HD, h * tq:(h + 1) * tq] = qt_ref[h * HD:(h + 1) * HD, :]
        m_scr[...] = jnp.full(m_scr.shape, -jnp.inf, F32)
        l_scr[...] = jnp.zeros_like(l_scr)
        acc_scr[...] = jnp.zeros_like(acc_scr)

    stages = [(g, c) for c in range(ts // tc) for g in range(KVH)]

    def scores(g, c):
        return jnp.dot(k_ref[c * tc:(c + 1) * tc, :], qbd_scr[:, g * mcols:(g + 1) * mcols],
                       preferred_element_type=F32)

    pending = [scores(*st) for st in stages[:FLASH_LOOKAHEAD]]
    for idx, (g, c) in enumerate(stages):
        s = pending.pop(0)
        if idx + FLASH_LOOKAHEAD < len(stages):
            pending.append(scores(*stages[idx + FLASH_LOOKAHEAD]))
        vt = vt_ref[g * HD:(g + 1) * HD, c * tc:(c + 1) * tc]
        if bounded:
            p = jnp.exp2(s)
            l_scr[g] += jnp.sum(p, axis=0, keepdims=True)
            acc_scr[g] += jnp.dot(vt, p.astype(BF16), preferred_element_type=F32)
        else:
            m_prev = m_scr[g]
            m_new = jnp.maximum(m_prev, jnp.max(s, axis=0, keepdims=True))
            alpha = jnp.exp2(m_prev - m_new)
            p = jnp.exp2(s - m_new)
            l_scr[g] = alpha * l_scr[g] + jnp.sum(p, axis=0, keepdims=True)
            acc_scr[g] = alpha * acc_scr[g] + jnp.dot(vt, p.astype(BF16), preferred_element_type=F32)
            m_scr[g] = m_new

    @pl.when(j == nkv - 1)
    def _():
        rows = []
        for g in range(KVH):
            o = acc_scr[g] / l_scr[g]
            rows += [o[:, r * tq:(r + 1) * tq] for r in range(gq)]
        o_ref[...] = jnp.concatenate(rows, axis=0).T.astype(o_ref.dtype)


def _flash(qt, k, vt, s_len, ts, bounded=False):
    lq = qt.shape[1]
    tq = min(128, lq)
    nkv = s_len // ts
    gq = QH // KVH
    kvw = KVH * HD
    tc = 2 * LANES if ts % (2 * LANES) == 0 else LANES
    return pl.pallas_call(
        functools.partial(_flash_kernel, tq=tq, ts=ts, tc=tc, nkv=nkv, bounded=bounded),
        out_shape=jax.ShapeDtypeStruct((lq, D), BF16),
        grid=(lq // tq, nkv),
        in_specs=[
            pl.BlockSpec((D, tq), lambda i, j: (0, i)),
            pl.BlockSpec((ts, kvw), lambda i, j: (j, 0)),
            pl.BlockSpec((kvw, ts), lambda i, j: (0, j)),
        ],
        out_specs=pl.BlockSpec((tq, D), lambda i, j: (i, 0)),
        scratch_shapes=[
            pltpu.VMEM((kvw, QH * tq), BF16),
            pltpu.VMEM((KVH, 1, gq * tq), F32),
            pltpu.VMEM((KVH, 1, gq * tq), F32),
            pltpu.VMEM((KVH, HD, gq * tq), F32),
        ],
        compiler_params=_cparams("parallel", "arbitrary"),
        name="flash_attn",
    )(qt, k, vt)


def _halo_specs(tm, lx):
    nb = lx // SUBLANES
    step = tm // SUBLANES
    prev = pl.BlockSpec((SUBLANES, D), lambda i: (jnp.maximum(i * step - 1, 0), 0))
    nxt = pl.BlockSpec((SUBLANES, D), lambda i: (jnp.minimum((i + 1) * step, nb - 1), 0))
    return prev, nxt


def _conv3(p_main, p_halo, cw, cb, first, last):
    tm = p_main.shape[0]
    rid = lax.broadcasted_iota(jnp.int32, p_main.shape, 0)
    before = jnp.where(first, 0.0, p_halo[SUBLANES - 1:SUBLANES, :])
    after = jnp.where(last, 0.0, p_halo[SUBLANES:SUBLANES + 1, :])
    up = jnp.where(rid == 0, before, pltpu.roll(p_main, 1, axis=0))
    dn = jnp.where(rid == tm - 1, after, pltpu.roll(p_main, tm - 1, axis=0))
    return cw[0:1, :] * up + cw[1:2, :] * p_main + cw[2:3, :] * dn + cb


def _norm_halo(xm_ref, xp_ref, xn_ref, g_ref, sh_ref, sc_ref):
    g, sh, sc = g_ref[...], sh_ref[...], sc_ref[...]
    h = _normmod(xm_ref[...], g, sh, sc).astype(BF16)
    hh = jnp.concatenate([_normmod(xp_ref[...], g, sh, sc), _normmod(xn_ref[...], g, sh, sc)], axis=0).astype(BF16)
    return h, hh


def _hy_in_kernel(xm_ref, xp_ref, xn_ref, g_ref, sh_ref, sc_ref, w_ref, cw_ref, cb_ref, x0_ref, ut_ref, *, nt):
    i = pl.program_id(0)
    first, last = i == 0, i == nt - 1
    h, hh = _norm_halo(xm_ref, xp_ref, xn_ref, g_ref, sh_ref, sc_ref)

    def branch(b):
        sl = slice(b * D, (b + 1) * D)
        pm = jnp.dot(h, w_ref[:, sl], preferred_element_type=F32)
        ph = jnp.dot(hh, w_ref[:, sl], preferred_element_type=F32)
        return _conv3(pm, ph, cw_ref[:, sl], cb_ref[:, sl], first, last)

    x0_ref[...] = branch(0)
    u = branch(1) * branch(2)
    ut_ref[...] = u.T


def _hy_in(x, g, sh, sc, w_in, conv_w, conv_b):
    lx = x.shape[0]
    tm = min(512, lx)
    nt = lx // tm
    vec = pl.BlockSpec((1, D), lambda i: (0, 0))
    prev, nxt = _halo_specs(tm, lx)
    return pl.pallas_call(
        functools.partial(_hy_in_kernel, nt=nt),
        out_shape=(jax.ShapeDtypeStruct((lx, D), F32), jax.ShapeDtypeStruct((D, lx), F32)),
        grid=(nt,),
        in_specs=[
            pl.BlockSpec((tm, D), lambda i: (i, 0)), prev, nxt,
            vec, vec, vec,
            _resident((D, 3 * D)),
            pl.BlockSpec((3, 3 * D), lambda i: (0, 0)),
            pl.BlockSpec((1, 3 * D), lambda i: (0, 0)),
        ],
        out_specs=(pl.BlockSpec((tm, D), lambda i: (i, 0)), pl.BlockSpec((D, tm), lambda i: (0, i))),
        compiler_params=_cparams("parallel"),
        name="hyena_in",
    )(x, x, x, _row(g), _row(sh), _row(sc), w_in, conv_w, _row(conv_b))


def _hy_filter_kernel(ft_ref, w1_ref, b1_ref, w2_ref, b2_ref, w3h_ref, w3l_ref, fr_ref, dl_ref, sk_ref, kt_ref,
                      *, tm, ltrue):
    i = pl.program_id(0)
    feats = ft_ref[...]
    t = feats[:, 0:1]
    fr = fr_ref[...]
    hid = jnp.sin(fr * (jnp.dot(feats, w1_ref[...], preferred_element_type=F32, precision=HIGHEST) + b1_ref[...]))
    hid = jnp.sin(fr * (jnp.dot(hid, w2_ref[...], preferred_element_type=F32, precision=HIGHEST) + b2_ref[...]))
    hh = hid.astype(BF16)
    hl = (hid - hh.astype(F32)).astype(BF16)
    w3h = w3h_ref[...]
    k = (jnp.dot(hh, w3h, preferred_element_type=F32) + jnp.dot(hl, w3h, preferred_element_type=F32)
         + jnp.dot(hh, w3l_ref[...], preferred_element_type=F32))
    k = k * jnp.exp(-t * dl_ref[...])
    rid = lax.broadcasted_iota(jnp.int32, k.shape, 0) + i * tm
    cid = lax.broadcasted_iota(jnp.int32, k.shape, 1)
    k = jnp.where(rid == 0, jnp.where(cid < D, k + sk_ref[...], 0.0), k)
    k = jnp.where(rid < ltrue, k, 0.0)
    kt_ref[...] = k.T


def _hy_filter(ltrue, lpad, w1, b1, w2, b2, w3, freq, skip):
    tm = min(512, lpad)
    w1p = jnp.zeros((LANES, HY_FILT_W), F32).at[:HY_EMB].set(w1)
    deltas = jnp.abs(jnp.linspace(HY_MIN_DECAY, HY_MAX_DECAY, D, dtype=F32))
    pos = jnp.arange(lpad, dtype=F32)
    zf = (2.0 * math.pi * pos / ltrue)[:, None] * jnp.linspace(1e-4, HY_BANDS - 1, HY_BANDS, dtype=F32)[None, :]
    feats = jnp.concatenate([(pos / (ltrue - 1))[:, None], jnp.cos(zf), -jnp.sin(zf),
                             jnp.zeros((lpad, LANES - HY_EMB), F32)], axis=-1)
    w3h = w3.astype(BF16)
    w3l = (w3 - w3h.astype(F32)).astype(BF16)
    full = lambda a: pl.BlockSpec(a.shape, lambda i: (0,) * a.ndim)
    args = (w1p, _row(b1), w2, _row(b2), w3h, w3l, _row(freq), _row(jnp.tile(deltas, 2)),
            _row(jnp.concatenate([skip, jnp.zeros((D,), F32)])))
    return pl.pallas_call(
        functools.partial(_hy_filter_kernel, tm=tm, ltrue=ltrue),
        out_shape=jax.ShapeDtypeStruct((2 * D, lpad), F32),
        grid=(lpad // tm,),
        in_specs=[pl.BlockSpec((tm, LANES), lambda i: (i, 0))] + [full(a) for a in args],
        out_specs=pl.BlockSpec((2 * D, tm), lambda i: (0, i)),
        compiler_params=_cparams("parallel"),
        name="hyena_filter",
    )(feats, *args)


def _dft_consts(nh):
    n1 = 2 * nh
    n = n1 * DFT_N2
    k1 = np.arange(n1)[:, None].astype(np.float64)
    a1 = 2.0 * np.pi * k1 * np.arange(nh)[None, :] / n1
    f1 = np.concatenate([np.cos(a1), -np.sin(a1)], axis=0)
    at = 2.0 * np.pi * ((np.arange(n1)[:, None] * np.arange(DFT_N2)[None, :]) % n) / n
    a2 = 2.0 * np.pi * ((np.arange(DFT_N2)[:, None] * np.arange(DFT_N2)[None, :]) % DFT_N2) / DFT_N2
    c2, s2 = np.cos(a2), np.sin(a2)
    f2 = np.block([[c2, -s2], [s2, c2]])
    g2 = np.block([[c2, s2], [-s2, c2]])
    g1 = np.concatenate([np.cos(a1).T, -np.sin(a1).T], axis=1) / n
    as32 = lambda a: jnp.asarray(a, F32)

    def parts(a):
        hi = a.astype(BF16)
        if DFT_SPLIT == 1:
            return (jnp.asarray(hi),)
        return (jnp.asarray(hi), jnp.asarray((a - hi.astype(np.float64)).astype(BF16)))

    return parts(f1), as32(np.cos(at)), as32(np.sin(at)), parts(f2), parts(g2), parts(g1)


def _split_bf16(a):
    hi = a.astype(BF16)
    if DFT_SPLIT == 1:
        return (hi,)
    return (hi, (a - hi.astype(F32)).astype(BF16))


def _split_dot(a, b):
    out = jnp.dot(a[0], b[0], preferred_element_type=F32)
    if DFT_SPLIT > 1:
        out = out + jnp.dot(a[1], b[0], preferred_element_type=F32) + jnp.dot(a[0], b[1], preferred_element_type=F32)
    return out


def _load_parts(refs):
    return tuple(r[...] for r in refs)


def _dft_fwd(xs, f1, twc, tws, f2, cb, n1):
    xcat = jnp.concatenate(xs, axis=1)
    a = _split_dot(f1, _split_bf16(xcat))
    rows = []
    for c in range(cb):
        ar = a[:n1, c * DFT_N2:(c + 1) * DFT_N2]
        ai = a[n1:, c * DFT_N2:(c + 1) * DFT_N2]
        rows.append(jnp.concatenate([ar * twc + ai * tws, ai * twc - ar * tws], axis=1))
    return _split_dot(_split_bf16(jnp.concatenate(rows, axis=0)), f2)


def _hy_conv_kernel(x_ref, kf_ref, kb_ref, *refs, cb, n1):
    ns = DFT_SPLIT
    f1, (twc_ref, tws_ref), f2 = refs[:ns], refs[ns:ns + 2], refs[ns + 2:2 * ns + 2]
    g2, g1, o_ref = refs[2 * ns + 2:3 * ns + 2], refs[3 * ns + 2:4 * ns + 2], refs[4 * ns + 2]
    twc, tws = twc_ref[...], tws_ref[...]
    xs = [r[c] for r in (x_ref, kf_ref, kb_ref) for c in range(cb)]
    spec_all = _dft_fwd(xs, _load_parts(f1), twc, tws, _load_parts(f2), 3 * cb, n1)
    rows = cb * n1
    spec, hf, hb = spec_all[:rows], spec_all[rows:2 * rows], spec_all[2 * rows:]
    hr = hf[:, :DFT_N2] + hb[:, :DFT_N2]
    hi = hf[:, DFT_N2:] - hb[:, DFT_N2:]
    xr, xi = spec[:, :DFT_N2], spec[:, DFT_N2:]
    y = jnp.concatenate([xr * hr - xi * hi, xr * hi + xi * hr], axis=1)
    b = _split_dot(_split_bf16(y), _load_parts(g2))
    cols = []
    for c in range(cb):
        br = b[c * n1:(c + 1) * n1, :DFT_N2]
        bi = b[c * n1:(c + 1) * n1, DFT_N2:]
        cols.append(jnp.concatenate([br * twc - bi * tws, bi * twc + br * tws], axis=0))
    out = _split_dot(_load_parts(g1), _split_bf16(jnp.concatenate(cols, axis=1)))
    for c in range(cb):
        o_ref[c] = out[:, c * DFT_N2:(c + 1) * DFT_N2]


def _hy_longconv(ut, kt, lpad):
    nh = lpad // DFT_N2
    n1 = 2 * nh
    cb = max(8, min(64, 2048 // n1))
    f1, twc, tws, f2, g2, g1 = _dft_consts(nh)
    consts = f1 + (twc, tws) + f2 + g2 + g1
    full = lambda a: pl.BlockSpec(a.shape, lambda i: (0,) * a.ndim)
    k3 = kt.reshape(2 * D, nh, DFT_N2)
    u3 = ut.reshape(D, nh, DFT_N2)
    nb = D // cb
    y3 = pl.pallas_call(
        functools.partial(_hy_conv_kernel, cb=cb, n1=n1),
        out_shape=jax.ShapeDtypeStruct((D, nh, DFT_N2), F32),
        grid=(nb,),
        in_specs=[pl.BlockSpec((cb, nh, DFT_N2), lambda i: (i, 0, 0)),
                  pl.BlockSpec((cb, nh, DFT_N2), lambda i: (i, 0, 0)),
                  pl.BlockSpec((cb, nh, DFT_N2), lambda i: (i + nb, 0, 0))]
                 + [full(a) for a in consts],
        out_specs=pl.BlockSpec((cb, nh, DFT_N2), lambda i: (i, 0, 0)),
        compiler_params=_cparams("parallel"),
        name="hyena_longconv",
    )(u3, k3, k3, *consts)
    return y3.reshape(D, lpad)


def _hy_out_kernel(x_ref, x0_ref, yt_ref, w_ref, gate_ref, o_ref):
    a = (x0_ref[...] * yt_ref[...].T).astype(BF16)
    y = jnp.dot(a, w_ref[...], preferred_element_type=F32)
    o_ref[...] = x_ref[...] + gate_ref[...] * y


def _hy_out(x, x0, yt, w_out, gate):
    lx = x.shape[0]
    tm = min(512, lx)
    return pl.pallas_call(
        _hy_out_kernel,
        out_shape=jax.ShapeDtypeStruct((lx, D), F32),
        grid=(lx // tm,),
        in_specs=[
            pl.BlockSpec((tm, D), lambda i: (i, 0)),
            pl.BlockSpec((tm, D), lambda i: (i, 0)),
            pl.BlockSpec((D, tm), lambda i: (0, i)),
            _resident((D, D)),
            pl.BlockSpec((1, D), lambda i: (0, 0)),
        ],
        out_specs=pl.BlockSpec((tm, D), lambda i: (i, 0)),
        compiler_params=_cparams("parallel"),
        name="hyena_out",
    )(x, x0, yt, w_out, _row(gate))


def _hyena(x, g, sh, sc, gate, w_in, conv_w, conv_b, w1, b1, w2, b2, w3, freq, skip, w_out):
    lx = x.shape[0]
    lpad = max(lx, SUBLANES * DFT_N2)
    x0, ut = _hy_in(x, g, sh, sc, w_in, conv_w, conv_b)
    if lpad != lx:
        ut = jnp.pad(ut, ((0, 0), (0, lpad - lx)))
    kt = _hy_filter(lx, lpad, w1, b1, w2, b2, w3, freq, skip)
    yt = _hy_longconv(ut, kt, lpad)[:, :lx]
    return _hy_out(x, x0, yt, w_out, gate)


def _ssd_in_kernel(xm_ref, xp_ref, xn_ref, g_ref, sh_ref, sc_ref, wz_ref, wx_ref, wd_ref, cw_ref, cb_ref, db_ref,
                   zg_ref, xs_ref, bm_ref, cm_ref, dt_ref, *, nt):
    i = pl.program_id(0)
    first, last = i == 0, i == nt - 1
    h, hh = _norm_halo(xm_ref, xp_ref, xn_ref, g_ref, sh_ref, sc_ref)
    zg_ref[...] = jnp.dot(h, wz_ref[...], preferred_element_type=F32).astype(zg_ref.dtype)
    pm = jnp.dot(h, wx_ref[...], preferred_element_type=F32)
    ph = jnp.dot(hh, wx_ref[...], preferred_element_type=F32)
    xbc = _silu(_conv3(pm, ph, cw_ref[...], cb_ref[...], first, last))
    xs_ref[...] = xbc[:, :SSD_INNER].astype(xs_ref.dtype)
    bm_ref[...] = xbc[:, SSD_INNER:SSD_INNER + SSD_BC]
    cm_ref[...] = xbc[:, SSD_INNER + SSD_BC:]
    dt = jnp.dot(h, wd_ref[...], preferred_element_type=F32) + db_ref[...]
    dt = jnp.maximum(dt, 0.0) + jnp.log1p(jnp.exp(-jnp.abs(dt)))
    lane = lax.broadcasted_iota(jnp.int32, dt.shape, 1)
    dt = jnp.where((lane % LANES) < SSD_HEADS, dt, 0.0)
    dt_ref[0] = dt[:, :LANES]
    dt_ref[1] = dt[:, LANES:]


def _ssd_in(x, g, sh, sc, w_in, conv_w, conv_b, dt_bias):
    lx = x.shape[0]
    tm = min(256, lx)
    nt = lx // tm
    vec = pl.BlockSpec((1, D), lambda i: (0, 0))
    prev, nxt = _halo_specs(tm, lx)
    wz = w_in[:, :SSD_INNER]
    wx = w_in[:, SSD_INNER:SSD_INNER + SSD_CONV_DIM]
    wdt = w_in[:, SSD_INNER + SSD_CONV_DIM:]
    pad = LANES - SSD_HEADS
    wd = jnp.concatenate([jnp.pad(wdt[:, :SSD_HEADS], ((0, 0), (0, pad))),
                          jnp.pad(wdt[:, SSD_HEADS:], ((0, 0), (0, pad)))], axis=1)
    db = jnp.pad(dt_bias, ((0, 0), (0, pad))).reshape(1, 2 * LANES)
    full = lambda a: pl.BlockSpec(a.shape, lambda i: (0,) * a.ndim)
    rowblk = lambda w: pl.BlockSpec((tm, w), lambda i: (i, 0))
    return pl.pallas_call(
        functools.partial(_ssd_in_kernel, nt=nt),
        out_shape=(jax.ShapeDtypeStruct((lx, SSD_INNER), BF16), jax.ShapeDtypeStruct((lx, SSD_INNER), BF16),
                   jax.ShapeDtypeStruct((lx, SSD_BC), F32), jax.ShapeDtypeStruct((lx, SSD_BC), F32),
                   jax.ShapeDtypeStruct((2, lx, LANES), F32)),
        grid=(nt,),
        in_specs=[rowblk(D), prev, nxt, vec, vec, vec, _resident(wz.shape), _resident(wx.shape), _resident(wd.shape),
                  pl.BlockSpec((3, SSD_CONV_DIM), lambda i: (0, 0)),
                  pl.BlockSpec((1, SSD_CONV_DIM), lambda i: (0, 0)),
                  pl.BlockSpec((1, 2 * LANES), lambda i: (0, 0))],
        out_specs=(rowblk(SSD_INNER), rowblk(SSD_INNER), rowblk(SSD_BC), rowblk(SSD_BC),
                   pl.BlockSpec((2, tm, LANES), lambda i: (0, i, 0))),
        compiler_params=_cparams("parallel"),
        name="ssd_in",
    )(x, x, x, _row(g), _row(sh), _row(sc), wz, wx, wd, conv_w, _row(conv_b), db)


def _expand_heads(arr, e_ref):
    hi = arr.astype(BF16)
    lo = (arr - hi.astype(F32)).astype(BF16)
    e = e_ref[...]
    return jnp.dot(hi, e, preferred_element_type=F32) + jnp.dot(lo, e, preferred_element_type=F32)


def _ssd_prologue(dt_ref, a_row, tri, e_ref, need_y):
    dt = dt_ref[0]
    a = dt * a_row
    acs = jnp.dot(tri, a, preferred_element_type=F32, precision=HIGHEST)
    total = jnp.sum(a, axis=0, keepdims=True)
    ctx = dict(keep=tri > 0.5, acs=acs)
    ctx["wend_x"] = _expand_heads(jnp.exp(total - acs) * dt, e_ref)
    ctx["etot_x"] = _expand_heads(jnp.broadcast_to(jnp.exp(total), (SUBLANES, LANES)), e_ref)[0:1, :]
    if need_y:
        ctx["eacs_x"] = _expand_heads(jnp.exp(acs), e_ref)
        ctx["acs_t"] = acs.T
        ctx["dt_t"] = dt.T
    return ctx


def _ssd_prepare(ctx, xs_ref, bm_ref, cm_ref, g, need_y):
    q = SSD_CHUNK
    ppg = SSD_HEADS // 2 // SSD_GROUPS
    bg = bm_ref[:, g * SSD_STATE:(g + 1) * SSD_STATE]
    ops = dict(cg=cm_ref[:, g * SSD_STATE:(g + 1) * SSD_STATE].astype(BF16), bgt=bg.T.astype(BF16), xp=[], xw=[], m=[])
    if need_y:
        cb = lax.dot_general(ops["cg"], bg.astype(BF16), (((1,), (1,)), ((), ())), preferred_element_type=F32)
    for r in range(ppg):
        pidx = g * ppg + r
        psl = slice(pidx * LANES, (pidx + 1) * LANES)
        xp = xs_ref[:, psl]
        ops["xp"].append(xp)
        ops["xw"].append((xp.astype(F32) * ctx["wend_x"][:, psl]).astype(BF16))
        if need_y:
            for hd in (2 * pidx, 2 * pidx + 1):
                seg = jnp.broadcast_to(ctx["acs"][:, hd:hd + 1], (q, LANES)) - ctx["acs_t"][hd:hd + 1, :]
                lm = jnp.exp(jnp.where(ctx["keep"], seg, -jnp.inf))
                ops["m"].append((cb * lm * ctx["dt_t"][hd:hd + 1, :]).astype(BF16))
    return ops


def _ssd_issue(ctx, ops, g, h_scr, y_ref, need_y):
    ppg = SSD_HEADS // 2 // SSD_GROUPS
    left = lax.broadcasted_iota(jnp.int32, (SSD_CHUNK, LANES), 1) < SSD_P
    for r in range(ppg):
        pidx = g * ppg + r
        psl = slice(pidx * LANES, (pidx + 1) * LANES)
        hs = h_scr[pidx]
        if need_y:
            yd = [jnp.dot(ops["m"][2 * r + e], ops["xp"][r], preferred_element_type=F32) for e in range(2)]
            yoff = jnp.dot(ops["cg"], hs.astype(BF16), preferred_element_type=F32) * ctx["eacs_x"][:, psl]
            y_ref[:, psl] = (jnp.where(left, yd[0], yd[1]) + yoff).astype(y_ref.dtype)
        st = jnp.dot(ops["bgt"], ops["xw"][r], preferred_element_type=F32)
        h_scr[pidx] = hs * ctx["etot_x"][:, psl] + st


def _ssd_scan_kernel(xsf_ref, xsb_ref, bmf_ref, bmb_ref, cmf_ref, cmb_ref, dtf_ref, dtb_ref, a_ref, tri_ref, e_ref,
                     h0_ref, yf_ref, yb_ref, hfin_ref, h_scr, *, nc, need_y):
    s = pl.program_id(0)

    @pl.when(s == 0)
    def _():
        h_scr[...] = h0_ref[...]

    dirs = ((xsf_ref, bmf_ref, cmf_ref, dtf_ref, yf_ref), (xsb_ref, bmb_ref, cmb_ref, dtb_ref, yb_ref))
    ctxs = [_ssd_prologue(dirs[d][3], a_ref[d], tri_ref[d], e_ref, need_y) for d in range(2)]
    stages = [(d, g) for g in range(SSD_GROUPS) for d in range(2)]
    prep = lambda d, g: _ssd_prepare(ctxs[d], dirs[d][0], dirs[d][1], dirs[d][2], g, need_y)
    pending = prep(*stages[0])
    for idx, (d, g) in enumerate(stages):
        ops = pending
        if idx + 1 < len(stages):
            pending = prep(*stages[idx + 1])
        _ssd_issue(ctxs[d], ops, g, h_scr.at[d], dirs[d][4], need_y)

    @pl.when(s == nc - 1)
    def _():
        hfin_ref[...] = h_scr[...]


def _ssd_scan(xs, bm, cm, dt2, a_log, h0, need_y):
    lx = xs.shape[0]
    q = SSD_CHUNK
    nc = lx // q
    npair = SSD_HEADS // 2
    a = -jnp.exp(a_log.astype(F32))
    a_pad = jnp.pad(a, ((0, 0), (0, LANES - SSD_HEADS))).reshape(2, 1, LANES)
    lower = np.tril(np.ones((q, q), np.float32))
    tri = jnp.asarray(np.stack([lower, lower.T]))
    expand = jnp.asarray(np.kron(np.eye(LANES)[:, :SSD_HEADS], np.ones((1, SSD_P))), BF16)
    fwd = lambda s: s
    bwd = lambda s: nc - 1 - s
    rows = lambda w, idx: pl.BlockSpec((q, w), lambda s: (idx(s), 0))
    full = lambda a_: pl.BlockSpec(a_.shape, lambda s: (0,) * a_.ndim)
    ylen = lx if need_y else q
    yspec = (lambda idx: rows(SSD_INNER, idx)) if need_y else (lambda idx: pl.BlockSpec((q, SSD_INNER), lambda s: (0, 0)))
    yf, yb, hfin = pl.pallas_call(
        functools.partial(_ssd_scan_kernel, nc=nc, need_y=need_y),
        out_shape=[jax.ShapeDtypeStruct((ylen, SSD_INNER), BF16), jax.ShapeDtypeStruct((ylen, SSD_INNER), BF16),
                   jax.ShapeDtypeStruct(h0.shape, F32)],
        grid=(nc,),
        in_specs=[
            rows(SSD_INNER, fwd), rows(SSD_INNER, bwd), rows(SSD_BC, fwd), rows(SSD_BC, bwd),
            rows(SSD_BC, fwd), rows(SSD_BC, bwd),
            pl.BlockSpec((1, q, LANES), lambda s: (0, s, 0)),
            pl.BlockSpec((1, q, LANES), lambda s: (1, nc - 1 - s, 0)),
            full(a_pad), full(tri), full(expand), full(h0),
        ],
        out_specs=[yspec(fwd), yspec(bwd), full(h0)],
        scratch_shapes=[pltpu.VMEM((2, npair, SSD_STATE, 2 * SSD_P), F32)],
        compiler_params=_cparams("arbitrary"),
        name="ssd_scan",
    )(xs, xs, bm, bm, cm, cm, dt2, dt2, a_pad, tri, expand, h0)
    return ((yf, yb) if need_y else None), hfin


def _ssd_out_kernel(x_ref, yf_ref, yb_ref, xs_ref, zg_ref, dsk_ref, ng_ref, w_ref, gate_ref, o_ref):
    y = yf_ref[...].astype(F32) + yb_ref[...].astype(F32) + xs_ref[...].astype(F32) * dsk_ref[...]
    y = y * _silu(zg_ref[...].astype(F32))
    gw = SSD_INNER // SSD_GROUPS
    parts = []
    for g in range(SSD_GROUPS):
        yg = y[:, g * gw:(g + 1) * gw]
        ms = jnp.mean(yg * yg, axis=-1, keepdims=True)
        parts.append(yg * lax.rsqrt(ms + NORM_EPS) * ng_ref[:, g * gw:(g + 1) * gw])
    yn = jnp.concatenate(parts, axis=1).astype(BF16)
    o_ref[...] = x_ref[...] + gate_ref[...] * jnp.dot(yn, w_ref[...], preferred_element_type=F32)


def _ssd_out(x, yfb, xs, zg, d_skip, norm_g, w_out, gate):
    lx = x.shape[0]
    tm = min(256, lx)
    rowblk = lambda w: pl.BlockSpec((tm, w), lambda i: (i, 0))
    vecw = pl.BlockSpec((1, SSD_INNER), lambda i: (0, 0))
    return pl.pallas_call(
        _ssd_out_kernel,
        out_shape=jax.ShapeDtypeStruct((lx, D), F32),
        grid=(lx // tm,),
        in_specs=[rowblk(D), rowblk(SSD_INNER), rowblk(SSD_INNER), rowblk(SSD_INNER),
                  rowblk(SSD_INNER), vecw, vecw, _resident((SSD_INNER, D)),
                  pl.BlockSpec((1, D), lambda i: (0, 0))],
        out_specs=rowblk(D),
        compiler_params=_cparams("parallel"),
        name="ssd_out",
    )(x, yfb[0], yfb[1], xs, zg, _row(jnp.repeat(d_skip, SSD_P)), _row(norm_g), w_out, _row(gate))


def kernel(x, c, ctx, c_ctx, norm1_g, norm2_g, mod_w, mod_b, ffn_w_in, ffn_w_out, final_g, gm_w_in, gm_ln_g, gm_ln_b, gm_ws, gm_bs, gm_w_out, at_w_qkv, at_q_g, at_k_g, at_w_out, hy_w_in, hy_conv_w, hy_conv_b, hy_filt_w1, hy_filt_b1, hy_filt_w2, hy_filt_b2, hy_filt_w3, hy_filt_freq, hy_skip, hy_w_out, ssd_w_in, ssd_conv_w, ssd_conv_b, ssd_a_log, ssd_dt_bias, ssd_d_skip, ssd_norm_g, ssd_w_out):
    batch, seq, _ = x.shape
    assert batch == 1, "kernels are written for a single sequence"
    nctx = ctx.shape[1]
    xl = x[0]
    z = ctx[0]
    mods = _modulation(c[0], c_ctx, mod_w, mod_b)
    bf = lambda w: w.astype(BF16)

    for i in range(DEPTH):
        m, j = i % 4, i // 4
        want_ctx = i < DEPTH - 1
        ml = [mods[i, 0, k * D:(k + 1) * D] for k in range(6)]
        mc = [mods[i, 1, k * D:(k + 1) * D] for k in range(6)]
        n1 = norm1_g[i]
        if m == 0:
            p = (bf(gm_w_in[j]), gm_ln_g[j], gm_ln_b[j], bf(gm_ws[j]), gm_bs[j], bf(gm_w_out[j]))
            xl = _gmlp(xl, n1, ml[0], ml[1], ml[2], *p)
            if want_ctx:
                z = _gmlp(z, n1, mc[0], mc[1], mc[2], *p)
        elif m == 1:
            wq, wo = bf(at_w_qkv[j]), bf(at_w_out[j])
            qt_l, k_l, vt_l = _qkv(xl, n1, ml[0], ml[1], wq, at_q_g[j], at_k_g[j], rope=True)
            qt_c, k_c, vt_c = _qkv(z, n1, mc[0], mc[1], wq, at_q_g[j], at_k_g[j], rope=False)
            k_all = jnp.concatenate([k_c, k_l], axis=0)
            vt_all = jnp.concatenate([vt_c, vt_l], axis=1)
            stot = nctx + seq
            ts = next(t for t in (3328, 1280, 1024, 512, 256) if stot % t == 0)
            score_bound = (HD ** 0.5 * LOG2E) * jnp.max(jnp.abs(at_q_g[j])) * jnp.max(jnp.abs(at_k_g[j]))
            o_l = lax.cond(score_bound <= FLASH_SCORE_BOUND,
                           lambda: _flash(qt_l, k_all, vt_all, stot, ts, bounded=True),
                           lambda: _flash(qt_l, k_all, vt_all, stot, ts, bounded=False))
            xl = _outproj(xl, o_l, wo, ml[2])
            if want_ctx:
                o_c = _flash(qt_c, k_all, vt_all, nctx, nctx)
                z = _outproj(z, o_c, wo, mc[2])
        elif m == 2:
            p = (bf(hy_w_in[j]), hy_conv_w[j], hy_conv_b[j], hy_filt_w1[j], hy_filt_b1[j], hy_filt_w2[j],
                 hy_filt_b2[j], hy_filt_w3[j], hy_filt_freq[j], hy_skip[j], bf(hy_w_out[j]))
            xl = _hyena(xl, n1, ml[0], ml[1], ml[2], *p)
            if want_ctx:
                z = _hyena(z, n1, mc[0], mc[1], mc[2], *p)
        else:
            win, wo = bf(ssd_w_in[j]), bf(ssd_w_out[j])
            pin = (win, ssd_conv_w[j], ssd_conv_b[j], ssd_dt_bias[j])
            zg_c, xs_c, bm_c, cm_c, dt_c = _ssd_in(z, n1, mc[0], mc[1], *pin)
            zg_l, xs_l, bm_l, cm_l, dt_l = _ssd_in(xl, n1, ml[0], ml[1], *pin)
            h0 = jnp.zeros((2, SSD_HEADS // 2, SSD_STATE, 2 * SSD_P), F32)
            y_c, h_ctx = _ssd_scan(xs_c, bm_c, cm_c, dt_c, ssd_a_log[j], h0, want_ctx)
            y_l, _ = _ssd_scan(xs_l, bm_l, cm_l, dt_l, ssd_a_log[j], h_ctx, True)
            xl = _ssd_out(xl, y_l, xs_l, zg_l, ssd_d_skip[j], ssd_norm_g[j], wo, ml[2])
            if want_ctx:
                z = _ssd_out(z, y_c, xs_c, zg_c, ssd_d_skip[j], ssd_norm_g[j], wo, mc[2])
        wi, wo2 = bf(ffn_w_in[i]), bf(ffn_w_out[i])
        xl = _ffn(xl, norm2_g[i], ml[3], ml[4], ml[5], wi, wo2, final_g, final=(i == DEPTH - 1))
        if want_ctx:
            z = _ffn(z, norm2_g[i], mc[3], mc[4], mc[5], wi, wo2, final_g, final=False)
    return xl[None]
```

```python
import functools
import math

import numpy as np
import jax
import jax.numpy as jnp
from jax import lax
from jax.experimental import pallas as pl
from jax.experimental.pallas import tpu as pltpu

F32 = jnp.float32
BF16 = jnp.bfloat16
HIGHEST = lax.Precision.HIGHEST

D = 1024
DEPTH = 4
GRID_W = 64
NORM_EPS = 1e-6
FFN_HIDDEN = 2816
GM_CHUNK = 128
GM_WIDTH = 2 * D
GM_GROUPS = 8
GM_GW = GM_WIDTH // GM_GROUPS
HD = 64
QH = D // HD
KVH = 4
ROPE_THETA = 10000.0
LOG2E = math.log2(math.e)
FLASH_SCORE_BOUND = 30.0
FLASH_LOOKAHEAD = 2
HY_BANDS = 16
HY_EMB = 1 + 2 * HY_BANDS
HY_FILT_W = 64
HY_MAX_DECAY = math.log(1e-2) / 0.3
HY_MIN_DECAY = math.log(1e-2) / 1.5
SSD_INNER = 2 * D
SSD_P = 64
SSD_HEADS = SSD_INNER // SSD_P
SSD_GROUPS = 4
SSD_STATE = 128
SSD_CHUNK = 128
SSD_BC = SSD_GROUPS * SSD_STATE
SSD_CONV_DIM = SSD_INNER + 2 * SSD_BC

LANES = 128
SUBLANES = 8
VMEM_LIMIT_BYTES = 56 * 1024 * 1024
DFT_N2 = 128
DFT_SPLIT = 1


def _cparams(*sem):
    return pltpu.CompilerParams(dimension_semantics=sem, vmem_limit_bytes=VMEM_LIMIT_BYTES)


def _row(v):
    return v.reshape(1, -1)


def _normmod(x, g, shift, scale):
    ms = jnp.mean(x * x, axis=-1, keepdims=True)
    return x * lax.rsqrt(ms + NORM_EPS) * g * (1.0 + scale) + shift


def _silu(x):
    return x * jax.nn.sigmoid(x)


def _mod_kernel(cl_ref, cc_ref, w_ref, b_ref, o_ref):
    w = w_ref[0]
    for r, c_ref in enumerate((cl_ref, cc_ref)):
        a = _silu(c_ref[...])
        o_ref[0, r:r + 1, :] = jnp.sum(a * w, axis=0, keepdims=True) + b_ref[0]


def _modulation(c, c_ctx, mod_w, mod_b):
    tn = 1536
    n6 = 6 * D
    depth = mod_w.shape[0]
    return pl.pallas_call(
        _mod_kernel,
        out_shape=jax.ShapeDtypeStruct((depth, 2, n6), F32),
        grid=(depth, n6 // tn),
        in_specs=[
            pl.BlockSpec((D, 1), lambda i, n: (0, 0)),
            pl.BlockSpec((D, 1), lambda i, n: (0, 0)),
            pl.BlockSpec((1, D, tn), lambda i, n: (i, 0, n)),
            pl.BlockSpec((1, 1, tn), lambda i, n: (i, 0, n)),
        ],
        out_specs=pl.BlockSpec((1, 2, tn), lambda i, n: (i, 0, n)),
        compiler_params=_cparams("parallel", "parallel"),
        name="modulation",
    )(c.reshape(D, 1), c_ctx.reshape(D, 1), mod_w, mod_b.reshape(depth, 1, n6))


def _ffn_kernel(x_ref, g_ref, sh_ref, sc_ref, gate_ref, wi_ref, wo_ref, fg_ref, o_ref, *, final):
    x = x_ref[...]
    h = _normmod(x, g_ref[...], sh_ref[...], sc_ref[...]).astype(BF16)
    a = jnp.dot(h, wi_ref[:, :FFN_HIDDEN], preferred_element_type=F32)
    u = jnp.dot(h, wi_ref[:, FFN_HIDDEN:], preferred_element_type=F32)
    act = (_silu(a) * u).astype(BF16)
    y = x + gate_ref[...] * jnp.dot(act, wo_ref[...], preferred_element_type=F32)
    if final:
        ms = jnp.mean(y * y, axis=-1, keepdims=True)
        y = y * lax.rsqrt(ms + NORM_EPS) * fg_ref[...]
    o_ref[...] = y


def _resident(shape):
    return pl.BlockSpec(shape, lambda *_: (0,) * len(shape), pipeline_mode=pl.Buffered(1))


def _ffn(x, g, sh, sc, gate, w_in, w_out, final_g, final):
    lx = x.shape[0]
    tm = min(512, lx)
    vec = pl.BlockSpec((1, D), lambda i: (0, 0))
    return pl.pallas_call(
        functools.partial(_ffn_kernel, final=final),
        out_shape=jax.ShapeDtypeStruct((lx, D), F32),
        grid=(lx // tm,),
        in_specs=[
            pl.BlockSpec((tm, D), lambda i: (i, 0)),
            vec, vec, vec, vec,
            _resident((D, 2 * FFN_HIDDEN)),
            _resident((FFN_HIDDEN, D)),
            vec,
        ],
        out_specs=pl.BlockSpec((tm, D), lambda i: (i, 0)),
        compiler_params=_cparams("parallel"),
        name="ffn",
    )(x, _row(g), _row(sh), _row(sc), _row(gate), w_in, w_out, _row(final_g))


def _outproj_kernel(x_ref, a_ref, w_ref, gate_ref, o_ref):
    y = jnp.dot(a_ref[...], w_ref[...], preferred_element_type=F32)
    o_ref[...] = x_ref[...] + gate_ref[...] * y


def _outproj(x, a, w, gate):
    lx, kin = a.shape
    tm = min(512, lx)
    return pl.pallas_call(
        _outproj_kernel,
        out_shape=jax.ShapeDtypeStruct((lx, D), F32),
        grid=(lx // tm,),
        in_specs=[
            pl.BlockSpec((tm, D), lambda i: (i, 0)),
            pl.BlockSpec((tm, kin), lambda i: (i, 0)),
            _resident((kin, D)),
            pl.BlockSpec((1, D), lambda i: (0, 0)),
        ],
        out_specs=pl.BlockSpec((tm, D), lambda i: (i, 0)),
        compiler_params=_cparams("parallel"),
        name="outproj",
    )(x, a, w, _row(gate))


def _gmlp_kernel(x_ref, g_ref, sh_ref, sc_ref, gate_ref, win_ref, lng_ref, lnb_ref, ws_ref, bs_ref, wout_ref,
                 o_ref, *, tm):
    x = x_ref[...]
    h = _normmod(x, g_ref[...], sh_ref[...], sc_ref[...]).astype(BF16)
    t = jnp.dot(h, win_ref[...], preferred_element_type=F32)
    t = 0.5 * t * (1.0 + lax.erf(t * (1.0 / math.sqrt(2.0))))
    u = t[:, :GM_WIDTH]
    v = t[:, GM_WIDTH:]
    mu = jnp.mean(v, axis=-1, keepdims=True)
    vc = v - mu
    var = jnp.mean(vc * vc, axis=-1, keepdims=True)
    v = (vc * lax.rsqrt(var + NORM_EPS) * lng_ref[...] + lnb_ref[...]).astype(BF16)
    rows = []
    for q in range(tm // GM_CHUNK):
        cols = []
        for gidx in range(GM_GROUPS):
            vq = v[q * GM_CHUNK:(q + 1) * GM_CHUNK, gidx * GM_GW:(gidx + 1) * GM_GW]
            bias = bs_ref[gidx]
            m = jnp.dot(ws_ref[gidx], vq, preferred_element_type=F32)
            cols.append(m + jnp.concatenate([bias] * (GM_GW // LANES), axis=1))
        rows.append(jnp.concatenate(cols, axis=1))
    vm = jnp.concatenate(rows, axis=0)
    gated = (u * vm).astype(BF16)
    y = jnp.dot(gated, wout_ref[...], preferred_element_type=F32)
    o_ref[...] = x + gate_ref[...] * y


def _gmlp(x, g, sh, sc, gate, w_in, ln_g, ln_b, ws, bs, w_out):
    lx = x.shape[0]
    tm = min(256, lx)
    vec = pl.BlockSpec((1, D), lambda i: (0, 0))
    vecw = pl.BlockSpec((1, GM_WIDTH), lambda i: (0, 0))
    bsb = jnp.broadcast_to(bs[:, :, None], (GM_GROUPS, GM_CHUNK, LANES))
    return pl.pallas_call(
        functools.partial(_gmlp_kernel, tm=tm),
        out_shape=jax.ShapeDtypeStruct((lx, D), F32),
        grid=(lx // tm,),
        in_specs=[
            pl.BlockSpec((tm, D), lambda i: (i, 0)),
            vec, vec, vec, vec,
            _resident((D, 2 * GM_WIDTH)),
            vecw, vecw,
            pl.BlockSpec((GM_GROUPS, GM_CHUNK, GM_CHUNK), lambda i: (0, 0, 0)),
            pl.BlockSpec((GM_GROUPS, GM_CHUNK, LANES), lambda i: (0, 0, 0)),
            _resident((GM_WIDTH, D)),
        ],
        out_specs=pl.BlockSpec((tm, D), lambda i: (i, 0)),
        compiler_params=_cparams("parallel"),
        name="gmlp",
    )(x, _row(g), _row(sh), _row(sc), _row(gate), w_in, _row(ln_g), _row(ln_b), ws, bsb, w_out)


def _group_sumsq(t, e_ref):
    sq = t * t
    hi = sq.astype(BF16)
    lo = (sq - hi.astype(F32)).astype(BF16)
    outs = []
    for j in range(t.shape[1] // LANES):
        sl = slice(j * LANES, (j + 1) * LANES)
        outs.append(jnp.dot(hi[:, sl], e_ref[...], preferred_element_type=F32)
                    + jnp.dot(lo[:, sl], e_ref[...], preferred_element_type=F32))
    return jnp.concatenate(outs, axis=1)


def _rope(t, cosf, sinf):
    w = t.shape[1]
    lane = lax.broadcasted_iota(jnp.int32, t.shape, 1)
    first = (lane % HD) < (HD // 2)
    partner = jnp.where(first, pltpu.roll(t, w - HD // 2, axis=1), pltpu.roll(t, HD // 2, axis=1))
    reps = w // LANES
    c = jnp.concatenate([cosf] * reps, axis=1)
    s = jnp.concatenate([sinf] * reps, axis=1)
    return t * c + partner * s


def _qkv_kernel(x_ref, g_ref, sh_ref, sc_ref, w_ref, qg_ref, kg_ref, e_ref, cos_ref, sin_ref,
                qt_ref, k_ref, vt_ref, *, rope):
    h = _normmod(x_ref[...], g_ref[...], sh_ref[...], sc_ref[...]).astype(BF16)
    qkv = jnp.dot(h, w_ref[...], preferred_element_type=F32)
    q = qkv[:, :D]
    k = qkv[:, D:D + KVH * HD]
    v = qkv[:, D + KVH * HD:]
    q = q * lax.rsqrt(_group_sumsq(q, e_ref) * (1.0 / HD) + NORM_EPS) * qg_ref[...]
    k = k * lax.rsqrt(_group_sumsq(k, e_ref) * (1.0 / HD) + NORM_EPS) * kg_ref[...]
    if rope:
        q = _rope(q, cos_ref[...], sin_ref[...])
        k = _rope(k, cos_ref[...], sin_ref[...])
    qt_ref[...] = (q * (HD ** -0.5 * LOG2E)).T.astype(BF16)
    for gidx in range(KVH):
        k_ref[gidx] = k[:, gidx * HD:(gidx + 1) * HD].astype(BF16)
    vt_ref[...] = v.T.astype(BF16)


def _qkv(x, g, sh, sc, w_qkv, q_g, k_g, rope):
    lx = x.shape[0]
    tm = min(256, lx)
    vec = pl.BlockSpec((1, D), lambda i: (0, 0))
    kvw = KVH * HD
    rows = lx // GRID_W
    row = jnp.repeat(jnp.arange(rows, dtype=F32), GRID_W)
    col = jnp.tile(jnp.arange(GRID_W, dtype=F32), rows)
    n = HD // 4
    inv = ROPE_THETA ** (-jnp.arange(n, dtype=F32) / n)
    ang = jnp.concatenate([row[:, None] * inv, col[:, None] * inv], axis=-1)
    cos, sin = jnp.cos(ang), jnp.sin(ang)
    cosf = jnp.tile(jnp.concatenate([cos, cos], axis=-1), (1, LANES // HD))
    sinf = jnp.tile(jnp.concatenate([-sin, sin], axis=-1), (1, LANES // HD))
    eblk = jnp.asarray(np.kron(np.eye(LANES // HD), np.ones((HD, HD))), BF16)
    tab = pl.BlockSpec((tm, LANES), lambda i: (i, 0))
    return pl.pallas_call(
        functools.partial(_qkv_kernel, rope=rope),
        out_shape=(jax.ShapeDtypeStruct((D, lx), BF16),
                   jax.ShapeDtypeStruct((KVH, lx, HD), BF16),
                   jax.ShapeDtypeStruct((kvw, lx), BF16)),
        grid=(lx // tm,),
        in_specs=[
            pl.BlockSpec((tm, D), lambda i: (i, 0)),
            vec, vec, vec,
            _resident((D, D + 2 * kvw)),
            vec,
            pl.BlockSpec((1, kvw), lambda i: (0, 0)),
            pl.BlockSpec((LANES, LANES), lambda i: (0, 0)),
            tab, tab,
        ],
        out_specs=(pl.BlockSpec((D, tm), lambda i: (0, i)),
                   pl.BlockSpec((KVH, tm, HD), lambda i: (0, i, 0)),
                   pl.BlockSpec((kvw, tm), lambda i: (0, i))),
        compiler_params=_cparams("parallel"),
        name="qkv_proj",
    )(x, _row(g), _row(sh), _row(sc), w_qkv, _row(jnp.tile(q_g, QH)), _row(jnp.tile(k_g, KVH)), eblk, cosf, sinf)


def _flash_kernel(qt_ref, k_ref, vt_ref, o_ref, qg_scr, m_scr, l_scr, acc_scr, *, tq, ts, tc, nkv, bounded):
    j = pl.program_id(1)
    gq = QH // KVH
    mcols = gq * tq

    @pl.when(j == 0)
    def _():
        for h in range(QH):
            qg_scr[h // gq, :, (h % gq) * tq:(h % gq + 1) * tq] = qt_ref[h * HD:(h + 1) * HD, :]
        m_scr[...] = jnp.full(m_scr.shape, -jnp.inf, F32)
        l_scr[...] = jnp.zeros_like(l_scr)
        acc_scr[...] = jnp.zeros_like(acc_scr)

    stages = [(g, c) for c in range(ts // tc) for g in range(KVH)]

    def scores(g, c):
        return jnp.dot(k_ref[g, c * tc:(c + 1) * tc, :], qg_scr[g], preferred_element_type=F32)

    pending = [scores(*st) for st in stages[:FLASH_LOOKAHEAD]]
    for idx, (g, c) in enumerate(stages):
        s = pending.pop(0)
        if idx + FLASH_LOOKAHEAD < len(stages):
            pending.append(scores(*stages[idx + FLASH_LOOKAHEAD]))
        vt = vt_ref[g * HD:(g + 1) * HD, c * tc:(c + 1) * tc]
        if bounded:
            p = jnp.exp2(s)
            l_scr[g] += jnp.sum(p, axis=0, keepdims=True)
            acc_scr[g] += jnp.dot(vt, p.astype(BF16), preferred_element_type=F32)
        else:
            m_prev = m_scr[g]
            m_new = jnp.maximum(m_prev, jnp.max(s, axis=0, keepdims=True))
            alpha = jnp.exp2(m_prev - m_new)
            p = jnp.exp2(s - m_new)
            l_scr[g] = alpha * l_scr[g] + jnp.sum(p, axis=0, keepdims=True)
            acc_scr[g] = alpha * acc_scr[g] + jnp.dot(vt, p.astype(BF16), preferred_element_type=F32)
            m_scr[g] = m_new

    @pl.when(j == nkv - 1)
    def _():
        rows = []
        for g in range(KVH):
            o = acc_scr[g] / l_scr[g]
            rows += [o[:, r * tq:(r + 1) * tq] for r in range(gq)]
        o_ref[...] = jnp.concatenate(rows, axis=0).T.astype(o_ref.dtype)


def _flash(qt, k, vt, s_len, ts, bounded=False):
    lq = qt.shape[1]
    tq = min(128, lq)
    nkv = s_len // ts
    gq = QH // KVH
    kvw = KVH * HD
    tc = 2 * LANES if ts % (2 * LANES) == 0 else LANES
    return pl.pallas_call(
        functools.partial(_flash_kernel, tq=tq, ts=ts, tc=tc, nkv=nkv, bounded=bounded),
        out_shape=jax.ShapeDtypeStruct((lq, D), BF16),
        grid=(lq // tq, nkv),
        in_specs=[
            pl.BlockSpec((D, tq), lambda i, j: (0, i)),
            pl.BlockSpec((KVH, ts, HD), lambda i, j: (0, j, 0)),
            pl.BlockSpec((kvw, ts), lambda i, j: (0, j)),
        ],
        out_specs=pl.BlockSpec((tq, D), lambda i, j: (i, 0)),
        scratch_shapes=[
            pltpu.VMEM((KVH, HD, gq * tq), BF16),
            pltpu.VMEM((KVH, 1, gq * tq), F32),
            pltpu.VMEM((KVH, 1, gq * tq), F32),
            pltpu.VMEM((KVH, HD, gq * tq), F32),
        ],
        compiler_params=_cparams("parallel", "arbitrary"),
        name="flash_attn",
    )(qt, k, vt)


def _halo_specs(tm, lx):
    nb = lx // SUBLANES
    step = tm // SUBLANES
    prev = pl.BlockSpec((SUBLANES, D), lambda i: (jnp.maximum(i * step - 1, 0), 0))
    nxt = pl.BlockSpec((SUBLANES, D), lambda i: (jnp.minimum((i + 1) * step, nb - 1), 0))
    return prev, nxt


def _conv3(p_main, p_halo, cw, cb, first, last):
    tm = p_main.shape[0]
    rid = lax.broadcasted_iota(jnp.int32, p_main.shape, 0)
    before = jnp.where(first, 0.0, p_halo[SUBLANES - 1:SUBLANES, :])
    after = jnp.where(last, 0.0, p_halo[SUBLANES:SUBLANES + 1, :])
    up = jnp.where(rid == 0, before, pltpu.roll(p_main, 1, axis=0))
    dn = jnp.where(rid == tm - 1, after, pltpu.roll(p_main, tm - 1, axis=0))
    return cw[0:1, :] * up + cw[1:2, :] * p_main + cw[2:3, :] * dn + cb


def _norm_halo(xm_ref, xp_ref, xn_ref, g_ref, sh_ref, sc_ref):
    g, sh, sc = g_ref[...], sh_ref[...], sc_ref[...]
    h = _normmod(xm_ref[...], g, sh, sc).astype(BF16)
    hh = jnp.concatenate([_normmod(xp_ref[...], g, sh, sc), _normmod(xn_ref[...], g, sh, sc)], axis=0).astype(BF16)
    return h, hh


def _hy_in_kernel(xm_ref, xp_ref, xn_ref, g_ref, sh_ref, sc_ref, w_ref, cw_ref, cb_ref, x0_ref, ut_ref, *, nt):
    i = pl.program_id(0)
    first, last = i == 0, i == nt - 1
    h, hh = _norm_halo(xm_ref, xp_ref, xn_ref, g_ref, sh_ref, sc_ref)

    def branch(b):
        sl = slice(b * D, (b + 1) * D)
        pm = jnp.dot(h, w_ref[:, sl], preferred_element_type=F32)
        ph = jnp.dot(hh, w_ref[:, sl], preferred_element_type=F32)
        return _conv3(pm, ph, cw_ref[:, sl], cb_ref[:, sl], first, last)

    x0_ref[...] = branch(0)
    u = branch(1) * branch(2)
    ut_ref[...] = u.T


def _hy_in(x, g, sh, sc, w_in, conv_w, conv_b):
    lx = x.shape[0]
    tm = min(512, lx)
    nt = lx // tm
    vec = pl.BlockSpec((1, D), lambda i: (0, 0))
    prev, nxt = _halo_specs(tm, lx)
    return pl.pallas_call(
        functools.partial(_hy_in_kernel, nt=nt),
        out_shape=(jax.ShapeDtypeStruct((lx, D), F32), jax.ShapeDtypeStruct((D, lx), F32)),
        grid=(nt,),
        in_specs=[
            pl.BlockSpec((tm, D), lambda i: (i, 0)), prev, nxt,
            vec, vec, vec,
            _resident((D, 3 * D)),
            pl.BlockSpec((3, 3 * D), lambda i: (0, 0)),
            pl.BlockSpec((1, 3 * D), lambda i: (0, 0)),
        ],
        out_specs=(pl.BlockSpec((tm, D), lambda i: (i, 0)), pl.BlockSpec((D, tm), lambda i: (0, i))),
        compiler_params=_cparams("parallel"),
        name="hyena_in",
    )(x, x, x, _row(g), _row(sh), _row(sc), w_in, conv_w, _row(conv_b))


def _hy_filter_kernel(ft_ref, w1_ref, b1_ref, w2_ref, b2_ref, w3h_ref, w3l_ref, fr_ref, dl_ref, sk_ref, kt_ref,
                      *, tm, ltrue):
    i = pl.program_id(0)
    feats = ft_ref[...]
    t = feats[:, 0:1]
    fr = fr_ref[...]
    hid = jnp.sin(fr * (jnp.dot(feats, w1_ref[...], preferred_element_type=F32, precision=HIGHEST) + b1_ref[...]))
    hid = jnp.sin(fr * (jnp.dot(hid, w2_ref[...], preferred_element_type=F32, precision=HIGHEST) + b2_ref[...]))
    hh = hid.astype(BF16)
    hl = (hid - hh.astype(F32)).astype(BF16)
    w3h = w3h_ref[...]
    k = (jnp.dot(hh, w3h, preferred_element_type=F32) + jnp.dot(hl, w3h, preferred_element_type=F32)
         + jnp.dot(hh, w3l_ref[...], preferred_element_type=F32))
    k = k * jnp.exp(-t * dl_ref[...])
    rid = lax.broadcasted_iota(jnp.int32, k.shape, 0) + i * tm
    cid = lax.broadcasted_iota(jnp.int32, k.shape, 1)
    k = jnp.where(rid == 0, jnp.where(cid < D, k + sk_ref[...], 0.0), k)
    k = jnp.where(rid < ltrue, k, 0.0)
    kt_ref[...] = k.T


def _hy_filter(ltrue, lpad, w1, b1, w2, b2, w3, freq, skip):
    tm = min(512, lpad)
    w1p = jnp.zeros((LANES, HY_FILT_W), F32).at[:HY_EMB].set(w1)
    deltas = jnp.abs(jnp.linspace(HY_MIN_DECAY, HY_MAX_DECAY, D, dtype=F32))
    pos = jnp.arange(lpad, dtype=F32)
    zf = (2.0 * math.pi * pos / ltrue)[:, None] * jnp.linspace(1e-4, HY_BANDS - 1, HY_BANDS, dtype=F32)[None, :]
    feats = jnp.concatenate([(pos / (ltrue - 1))[:, None], jnp.cos(zf), -jnp.sin(zf),
                             jnp.zeros((lpad, LANES - HY_EMB), F32)], axis=-1)
    w3h = w3.astype(BF16)
    w3l = (w3 - w3h.astype(F32)).astype(BF16)
    full = lambda a: pl.BlockSpec(a.shape, lambda i: (0,) * a.ndim)
    args = (w1p, _row(b1), w2, _row(b2), w3h, w3l, _row(freq), _row(jnp.tile(deltas, 2)),
            _row(jnp.concatenate([skip, jnp.zeros((D,), F32)])))
    return pl.pallas_call(
        functools.partial(_hy_filter_kernel, tm=tm, ltrue=ltrue),
        out_shape=jax.ShapeDtypeStruct((2 * D, lpad), F32),
        grid=(lpad // tm,),
        in_specs=[pl.BlockSpec((tm, LANES), lambda i: (i, 0))] + [full(a) for a in args],
        out_specs=pl.BlockSpec((2 * D, tm), lambda i: (0, i)),
        compiler_params=_cparams("parallel"),
        name="hyena_filter",
    )(feats, *args)


def _dft_consts(nh):
    n1 = 2 * nh
    n = n1 * DFT_N2
    k1 = np.arange(n1)[:, None].astype(np.float64)
    a1 = 2.0 * np.pi * k1 * np.arange(nh)[None, :] / n1
    f1 = np.concatenate([np.cos(a1), -np.sin(a1)], axis=0)
    at = 2.0 * np.pi * ((np.arange(n1)[:, None] * np.arange(DFT_N2)[None, :]) % n) / n
    a2 = 2.0 * np.pi * ((np.arange(DFT_N2)[:, None] * np.arange(DFT_N2)[None, :]) % DFT_N2) / DFT_N2
    c2, s2 = np.cos(a2), np.sin(a2)
    f2 = np.block([[c2, -s2], [s2, c2]])
    g2 = np.block([[c2, s2], [-s2, c2]])
    g1 = np.concatenate([np.cos(a1).T, -np.sin(a1).T], axis=1) / n
    as32 = lambda a: jnp.asarray(a, F32)

    def parts(a):
        hi = a.astype(BF16)
        if DFT_SPLIT == 1:
            return (jnp.asarray(hi),)
        return (jnp.asarray(hi), jnp.asarray((a - hi.astype(np.float64)).astype(BF16)))

    return parts(f1), as32(np.cos(at)), as32(np.sin(at)), parts(f2), parts(g2), parts(g1)


def _split_bf16(a):
    hi = a.astype(BF16)
    if DFT_SPLIT == 1:
        return (hi,)
    return (hi, (a - hi.astype(F32)).astype(BF16))


def _split_dot(a, b):
    out = jnp.dot(a[0], b[0], preferred_element_type=F32)
    if DFT_SPLIT > 1:
        out = out + jnp.dot(a[1], b[0], preferred_element_type=F32) + jnp.dot(a[0], b[1], preferred_element_type=F32)
    return out


def _load_parts(refs):
    return tuple(r[...] for r in refs)


def _dft_fwd(xs, f1, twc, tws, f2, cb, n1):
    xcat = jnp.concatenate(xs, axis=1)
    a = _split_dot(f1, _split_bf16(xcat))
    rows = []
    for c in range(cb):
        ar = a[:n1, c * DFT_N2:(c + 1) * DFT_N2]
        ai = a[n1:, c * DFT_N2:(c + 1) * DFT_N2]
        rows.append(jnp.concatenate([ar * twc + ai * tws, ai * twc - ar * tws], axis=1))
    return _split_dot(_split_bf16(jnp.concatenate(rows, axis=0)), f2)


def _hy_conv_kernel(x_ref, kf_ref, kb_ref, *refs, cb, n1):
    ns = DFT_SPLIT
    f1, (twc_ref, tws_ref), f2 = refs[:ns], refs[ns:ns + 2], refs[ns + 2:2 * ns + 2]
    g2, g1, o_ref = refs[2 * ns + 2:3 * ns + 2], refs[3 * ns + 2:4 * ns + 2], refs[4 * ns + 2]
    twc, tws = twc_ref[...], tws_ref[...]
    xs = [r[c] for r in (x_ref, kf_ref, kb_ref) for c in range(cb)]
    spec_all = _dft_fwd(xs, _load_parts(f1), twc, tws, _load_parts(f2), 3 * cb, n1)
    rows = cb * n1
    spec, hf, hb = spec_all[:rows], spec_all[rows:2 * rows], spec_all[2 * rows:]
    hr = hf[:, :DFT_N2] + hb[:, :DFT_N2]
    hi = hf[:, DFT_N2:] - hb[:, DFT_N2:]
    xr, xi = spec[:, :DFT_N2], spec[:, DFT_N2:]
    y = jnp.concatenate([xr * hr - xi * hi, xr * hi + xi * hr], axis=1)
    b = _split_dot(_split_bf16(y), _load_parts(g2))
    cols = []
    for c in range(cb):
        br = b[c * n1:(c + 1) * n1, :DFT_N2]
        bi = b[c * n1:(c + 1) * n1, DFT_N2:]
        cols.append(jnp.concatenate([br * twc - bi * tws, bi * twc + br * tws], axis=0))
    out = _split_dot(_load_parts(g1), _split_bf16(jnp.concatenate(cols, axis=1)))
    for c in range(cb):
        o_ref[c] = out[:, c * DFT_N2:(c + 1) * DFT_N2]


def _hy_longconv(ut, kt, lpad):
    nh = lpad // DFT_N2
    n1 = 2 * nh
    cb = max(8, min(64, 2048 // n1))
    f1, twc, tws, f2, g2, g1 = _dft_consts(nh)
    consts = f1 + (twc, tws) + f2 + g2 + g1
    full = lambda a: pl.BlockSpec(a.shape, lambda i: (0,) * a.ndim)
    k3 = kt.reshape(2 * D, nh, DFT_N2)
    u3 = ut.reshape(D, nh, DFT_N2)
    nb = D // cb
    y3 = pl.pallas_call(
        functools.partial(_hy_conv_kernel, cb=cb, n1=n1),
        out_shape=jax.ShapeDtypeStruct((D, nh, DFT_N2), F32),
        grid=(nb,),
        in_specs=[pl.BlockSpec((cb, nh, DFT_N2), lambda i: (i, 0, 0)),
                  pl.BlockSpec((cb, nh, DFT_N2), lambda i: (i, 0, 0)),
                  pl.BlockSpec((cb, nh, DFT_N2), lambda i: (i + nb, 0, 0))]
                 + [full(a) for a in consts],
        out_specs=pl.BlockSpec((cb, nh, DFT_N2), lambda i: (i, 0, 0)),
        compiler_params=_cparams("parallel"),
        name="hyena_longconv",
    )(u3, k3, k3, *consts)
    return y3.reshape(D, lpad)


def _hy_out_kernel(x_ref, x0_ref, yt_ref, w_ref, gate_ref, o_ref):
    a = (x0_ref[...] * yt_ref[...].T).astype(BF16)
    y = jnp.dot(a, w_ref[...], preferred_element_type=F32)
    o_ref[...] = x_ref[...] + gate_ref[...] * y


def _hy_out(x, x0, yt, w_out, gate):
    lx = x.shape[0]
    tm = min(512, lx)
    return pl.pallas_call(
        _hy_out_kernel,
        out_shape=jax.ShapeDtypeStruct((lx, D), F32),
        grid=(lx // tm,),
        in_specs=[
            pl.BlockSpec((tm, D), lambda i: (i, 0)),
            pl.BlockSpec((tm, D), lambda i: (i, 0)),
            pl.BlockSpec((D, tm), lambda i: (0, i)),
            _resident((D, D)),
            pl.BlockSpec((1, D), lambda i: (0, 0)),
        ],
        out_specs=pl.BlockSpec((tm, D), lambda i: (i, 0)),
        compiler_params=_cparams("parallel"),
        name="hyena_out",
    )(x, x0, yt, w_out, _row(gate))


def _hyena(x, g, sh, sc, gate, w_in, conv_w, conv_b, w1, b1, w2, b2, w3, freq, skip, w_out):
    lx = x.shape[0]
    lpad = max(lx, SUBLANES * DFT_N2)
    x0, ut = _hy_in(x, g, sh, sc, w_in, conv_w, conv_b)
    if lpad != lx:
        ut = jnp.pad(ut, ((0, 0), (0, lpad - lx)))
    kt = _hy_filter(lx, lpad, w1, b1, w2, b2, w3, freq, skip)
    yt = _hy_longconv(ut, kt, lpad)[:, :lx]
    return _hy_out(x, x0, yt, w_out, gate)


def _ssd_in_kernel(xm_ref, xp_ref, xn_ref, g_ref, sh_ref, sc_ref, wz_ref, wx_ref, wd_ref, cw_ref, cb_ref, db_ref,
                   zg_ref, xs_ref, bm_ref, cm_ref, dt_ref, *, nt):
    i = pl.program_id(0)
    first, last = i == 0, i == nt - 1
    h, hh = _norm_halo(xm_ref, xp_ref, xn_ref, g_ref, sh_ref, sc_ref)
    zg_ref[...] = jnp.dot(h, wz_ref[...], preferred_element_type=F32).astype(zg_ref.dtype)
    pm = jnp.dot(h, wx_ref[...], preferred_element_type=F32)
    ph = jnp.dot(hh, wx_ref[...], preferred_element_type=F32)
    xbc = _silu(_conv3(pm, ph, cw_ref[...], cb_ref[...], first, last))
    xs_ref[...] = xbc[:, :SSD_INNER].astype(xs_ref.dtype)
    bm_ref[...] = xbc[:, SSD_INNER:SSD_INNER + SSD_BC]
    cm_ref[...] = xbc[:, SSD_INNER + SSD_BC:]
    dt = jnp.dot(h, wd_ref[...], preferred_element_type=F32) + db_ref[...]
    dt = jnp.maximum(dt, 0.0) + jnp.log1p(jnp.exp(-jnp.abs(dt)))
    lane = lax.broadcasted_iota(jnp.int32, dt.shape, 1)
    dt = jnp.where((lane % LANES) < SSD_HEADS, dt, 0.0)
    dt_ref[0] = dt[:, :LANES]
    dt_ref[1] = dt[:, LANES:]


def _ssd_in(x, g, sh, sc, w_in, conv_w, conv_b, dt_bias):
    lx = x.shape[0]
    tm = min(256, lx)
    nt = lx // tm
    vec = pl.BlockSpec((1, D), lambda i: (0, 0))
    prev, nxt = _halo_specs(tm, lx)
    wz = w_in[:, :SSD_INNER]
    wx = w_in[:, SSD_INNER:SSD_INNER + SSD_CONV_DIM]
    wdt = w_in[:, SSD_INNER + SSD_CONV_DIM:]
    pad = LANES - SSD_HEADS
    wd = jnp.concatenate([jnp.pad(wdt[:, :SSD_HEADS], ((0, 0), (0, pad))),
                          jnp.pad(wdt[:, SSD_HEADS:], ((0, 0), (0, pad)))], axis=1)
    db = jnp.pad(dt_bias, ((0, 0), (0, pad))).reshape(1, 2 * LANES)
    full = lambda a: pl.BlockSpec(a.shape, lambda i: (0,) * a.ndim)
    rowblk = lambda w: pl.BlockSpec((tm, w), lambda i: (i, 0))
    return pl.pallas_call(
        functools.partial(_ssd_in_kernel, nt=nt),
        out_shape=(jax.ShapeDtypeStruct((lx, SSD_INNER), BF16), jax.ShapeDtypeStruct((lx, SSD_INNER), BF16),
                   jax.ShapeDtypeStruct((lx, SSD_BC), F32), jax.ShapeDtypeStruct((lx, SSD_BC), F32),
                   jax.ShapeDtypeStruct((2, lx, LANES), F32)),
        grid=(nt,),
        in_specs=[rowblk(D), prev, nxt, vec, vec, vec, _resident(wz.shape), _resident(wx.shape), _resident(wd.shape),
                  pl.BlockSpec((3, SSD_CONV_DIM), lambda i: (0, 0)),
                  pl.BlockSpec((1, SSD_CONV_DIM), lambda i: (0, 0)),
                  pl.BlockSpec((1, 2 * LANES), lambda i: (0, 0))],
        out_specs=(rowblk(SSD_INNER), rowblk(SSD_INNER), rowblk(SSD_BC), rowblk(SSD_BC),
                   pl.BlockSpec((2, tm, LANES), lambda i: (0, i, 0))),
        compiler_params=_cparams("parallel"),
        name="ssd_in",
    )(x, x, x, _row(g), _row(sh), _row(sc), wz, wx, wd, conv_w, _row(conv_b), db)


def _expand_heads(arr, e_ref):
    hi = arr.astype(BF16)
    lo = (arr - hi.astype(F32)).astype(BF16)
    e = e_ref[...]
    return jnp.dot(hi, e, preferred_element_type=F32) + jnp.dot(lo, e, preferred_element_type=F32)


def _ssd_prologue(dt_ref, a_row, tri, e_ref, need_y):
    dt = dt_ref[0]
    a = dt * a_row
    acs = jnp.dot(tri, a, preferred_element_type=F32, precision=HIGHEST)
    total = jnp.sum(a, axis=0, keepdims=True)
    ctx = dict(keep=tri > 0.5, acs=acs)
    ctx["wend_x"] = _expand_heads(jnp.exp(total - acs) * dt, e_ref)
    ctx["etot_x"] = _expand_heads(jnp.broadcast_to(jnp.exp(total), (SUBLANES, LANES)), e_ref)[0:1, :]
    if need_y:
        ctx["eacs_x"] = _expand_heads(jnp.exp(acs), e_ref)
        ctx["acs_t"] = acs.T
        ctx["dt_t"] = dt.T
    return ctx


def _ssd_prepare(ctx, xs_ref, bm_ref, cm_ref, g, need_y):
    q = SSD_CHUNK
    ppg = SSD_HEADS // 2 // SSD_GROUPS
    bg = bm_ref[:, g * SSD_STATE:(g + 1) * SSD_STATE]
    ops = dict(cg=cm_ref[:, g * SSD_STATE:(g + 1) * SSD_STATE].astype(BF16), bgt=bg.T.astype(BF16), xp=[], xw=[], m=[])
    if need_y:
        cb = lax.dot_general(ops["cg"], bg.astype(BF16), (((1,), (1,)), ((), ())), preferred_element_type=F32)
    for r in range(ppg):
        pidx = g * ppg + r
        psl = slice(pidx * LANES, (pidx + 1) * LANES)
        xp = xs_ref[:, psl]
        ops["xp"].append(xp)
        ops["xw"].append((xp.astype(F32) * ctx["wend_x"][:, psl]).astype(BF16))
        if need_y:
            for hd in (2 * pidx, 2 * pidx + 1):
                seg = jnp.broadcast_to(ctx["acs"][:, hd:hd + 1], (q, LANES)) - ctx["acs_t"][hd:hd + 1, :]
                lm = jnp.exp(jnp.where(ctx["keep"], seg, -jnp.inf))
                ops["m"].append((cb * lm * ctx["dt_t"][hd:hd + 1, :]).astype(BF16))
    return ops


def _ssd_issue(ctx, ops, g, h_scr, y_ref, need_y):
    ppg = SSD_HEADS // 2 // SSD_GROUPS
    left = lax.broadcasted_iota(jnp.int32, (SSD_CHUNK, LANES), 1) < SSD_P
    for r in range(ppg):
        pidx = g * ppg + r
        psl = slice(pidx * LANES, (pidx + 1) * LANES)
        hs = h_scr[pidx]
        if need_y:
            yd = [jnp.dot(ops["m"][2 * r + e], ops["xp"][r], preferred_element_type=F32) for e in range(2)]
            yoff = jnp.dot(ops["cg"], hs.astype(BF16), preferred_element_type=F32) * ctx["eacs_x"][:, psl]
            y_ref[:, psl] = (jnp.where(left, yd[0], yd[1]) + yoff).astype(y_ref.dtype)
        st = jnp.dot(ops["bgt"], ops["xw"][r], preferred_element_type=F32)
        h_scr[pidx] = hs * ctx["etot_x"][:, psl] + st


def _ssd_scan_kernel(xsf_ref, xsb_ref, bmf_ref, bmb_ref, cmf_ref, cmb_ref, dtf_ref, dtb_ref, a_ref, tri_ref, e_ref,
                     h0_ref, yf_ref, yb_ref, hfin_ref, h_scr, *, nc, need_y):
    s = pl.program_id(0)

    @pl.when(s == 0)
    def _():
        h_scr[...] = h0_ref[...]

    dirs = ((xsf_ref, bmf_ref, cmf_ref, dtf_ref, yf_ref), (xsb_ref, bmb_ref, cmb_ref, dtb_ref, yb_ref))
    ctxs = [_ssd_prologue(dirs[d][3], a_ref[d], tri_ref[d], e_ref, need_y) for d in range(2)]
    stages = [(d, g) for g in range(SSD_GROUPS) for d in range(2)]
    prep = lambda d, g: _ssd_prepare(ctxs[d], dirs[d][0], dirs[d][1], dirs[d][2], g, need_y)
    pending = prep(*stages[0])
    for idx, (d, g) in enumerate(stages):
        ops = pending
        if idx + 1 < len(stages):
            pending = prep(*stages[idx + 1])
        _ssd_issue(ctxs[d], ops, g, h_scr.at[d], dirs[d][4], need_y)

    @pl.when(s == nc - 1)
    def _():
        hfin_ref[...] = h_scr[...]


def _ssd_scan(xs, bm, cm, dt2, a_log, h0, need_y):
    lx = xs.shape[0]
    q = SSD_CHUNK
    nc = lx // q
    npair = SSD_HEADS // 2
    a = -jnp.exp(a_log.astype(F32))
    a_pad = jnp.pad(a, ((0, 0), (0, LANES - SSD_HEADS))).reshape(2, 1, LANES)
    lower = np.tril(np.ones((q, q), np.float32))
    tri = jnp.asarray(np.stack([lower, lower.T]))
    expand = jnp.asarray(np.kron(np.eye(LANES)[:, :SSD_HEADS], np.ones((1, SSD_P))), BF16)
    fwd = lambda s: s
    bwd = lambda s: nc - 1 - s
    rows = lambda w, idx: pl.BlockSpec((q, w), lambda s: (idx(s), 0))
    full = lambda a_: pl.BlockSpec(a_.shape, lambda s: (0,) * a_.ndim)
    ylen = lx if need_y else q
    yspec = (lambda idx: rows(SSD_INNER, idx)) if need_y else (lambda idx: pl.BlockSpec((q, SSD_INNER), lambda s: (0, 0)))
    yf, yb, hfin = pl.pallas_call(
        functools.partial(_ssd_scan_kernel, nc=nc, need_y=need_y),
        out_shape=[jax.ShapeDtypeStruct((ylen, SSD_INNER), BF16), jax.ShapeDtypeStruct((ylen, SSD_INNER), BF16),
                   jax.ShapeDtypeStruct(h0.shape, F32)],
        grid=(nc,),
        in_specs=[
            rows(SSD_INNER, fwd), rows(SSD_INNER, bwd), rows(SSD_BC, fwd), rows(SSD_BC, bwd),
            rows(SSD_BC, fwd), rows(SSD_BC, bwd),
            pl.BlockSpec((1, q, LANES), lambda s: (0, s, 0)),
            pl.BlockSpec((1, q, LANES), lambda s: (1, nc - 1 - s, 0)),
            full(a_pad), full(tri), full(expand), full(h0),
        ],
        out_specs=[yspec(fwd), yspec(bwd), full(h0)],
        scratch_shapes=[pltpu.VMEM((2, npair, SSD_STATE, 2 * SSD_P), F32)],
        compiler_params=_cparams("arbitrary"),
        name="ssd_scan",
    )(xs, xs, bm, bm, cm, cm, dt2, dt2, a_pad, tri, expand, h0)
    return ((yf, yb) if need_y else None), hfin


def _ssd_out_kernel(x_ref, yf_ref, yb_ref, xs_ref, zg_ref, dsk_ref, ng_ref, w_ref, gate_ref, o_ref):
    y = yf_ref[...].astype(F32) + yb_ref[...].astype(F32) + xs_ref[...].astype(F32) * dsk_ref[...]
    y = y * _silu(zg_ref[...].astype(F32))
    gw = SSD_INNER // SSD_GROUPS
    parts = []
    for g in range(SSD_GROUPS):
        yg = y[:, g * gw:(g + 1) * gw]
        ms = jnp.mean(yg * yg, axis=-1, keepdims=True)
        parts.append(yg * lax.rsqrt(ms + NORM_EPS) * ng_ref[:, g * gw:(g + 1) * gw])
    yn = jnp.concatenate(parts, axis=1).astype(BF16)
    o_ref[...] = x_ref[...] + gate_ref[...] * jnp.dot(yn, w_ref[...], preferred_element_type=F32)


def _ssd_out(x, yfb, xs, zg, d_skip, norm_g, w_out, gate):
    lx = x.shape[0]
    tm = min(256, lx)
    rowblk = lambda w: pl.BlockSpec((tm, w), lambda i: (i, 0))
    vecw = pl.BlockSpec((1, SSD_INNER), lambda i: (0, 0))
    return pl.pallas_call(
        _ssd_out_kernel,
        out_shape=jax.ShapeDtypeStruct((lx, D), F32),
        grid=(lx // tm,),
        in_specs=[rowblk(D), rowblk(SSD_INNER), rowblk(SSD_INNER), rowblk(SSD_INNER),
                  rowblk(SSD_INNER), vecw, vecw, _resident((SSD_INNER, D)),
                  pl.BlockSpec((1, D), lambda i: (0, 0))],
        out_specs=rowblk(D),
        compiler_params=_cparams("parallel"),
        name="ssd_out",
    )(x, yfb[0], yfb[1], xs, zg, _row(jnp.repeat(d_skip, SSD_P)), _row(norm_g), w_out, _row(gate))


def kernel(x, c, ctx, c_ctx, norm1_g, norm2_g, mod_w, mod_b, ffn_w_in, ffn_w_out, final_g, gm_w_in, gm_ln_g, gm_ln_b, gm_ws, gm_bs, gm_w_out, at_w_qkv, at_q_g, at_k_g, at_w_out, hy_w_in, hy_conv_w, hy_conv_b, hy_filt_w1, hy_filt_b1, hy_filt_w2, hy_filt_b2, hy_filt_w3, hy_filt_freq, hy_skip, hy_w_out, ssd_w_in, ssd_conv_w, ssd_conv_b, ssd_a_log, ssd_dt_bias, ssd_d_skip, ssd_norm_g, ssd_w_out):
    batch, seq, _ = x.shape
    assert batch == 1, "kernels are written for a single sequence"
    nctx = ctx.shape[1]
    xl = x[0]
    z = ctx[0]
    mods = _modulation(c[0], c_ctx, mod_w, mod_b)
    bf = lambda w: w.astype(BF16)

    for i in range(DEPTH):
        m, j = i % 4, i // 4
        want_ctx = i < DEPTH - 1
        ml = [mods[i, 0, k * D:(k + 1) * D] for k in range(6)]
        mc = [mods[i, 1, k * D:(k + 1) * D] for k in range(6)]
        n1 = norm1_g[i]
        if m == 0:
            p = (bf(gm_w_in[j]), gm_ln_g[j], gm_ln_b[j], bf(gm_ws[j]), gm_bs[j], bf(gm_w_out[j]))
            xl = _gmlp(xl, n1, ml[0], ml[1], ml[2], *p)
            if want_ctx:
                z = _gmlp(z, n1, mc[0], mc[1], mc[2], *p)
        elif m == 1:
            wq, wo = bf(at_w_qkv[j]), bf(at_w_out[j])
            qt_l, k_l, vt_l = _qkv(xl, n1, ml[0], ml[1], wq, at_q_g[j], at_k_g[j], rope=True)
            qt_c, k_c, vt_c = _qkv(z, n1, mc[0], mc[1], wq, at_q_g[j], at_k_g[j], rope=False)
            k_all = jnp.concatenate([k_c, k_l], axis=1)
            vt_all = jnp.concatenate([vt_c, vt_l], axis=1)
            stot = nctx + seq
            ts = next(t for t in (3328, 1280, 1024, 512, 256) if stot % t == 0)
            score_bound = (HD ** 0.5 * LOG2E) * jnp.max(jnp.abs(at_q_g[j])) * jnp.max(jnp.abs(at_k_g[j]))
            o_l = lax.cond(score_bound <= FLASH_SCORE_BOUND,
                           lambda: _flash(qt_l, k_all, vt_all, stot, ts, bounded=True),
                           lambda: _flash(qt_l, k_all, vt_all, stot, ts, bounded=False))
            xl = _outproj(xl, o_l, wo, ml[2])
            if want_ctx:
                o_c = _flash(qt_c, k_all, vt_all, nctx, nctx)
                z = _outproj(z, o_c, wo, mc[2])
        elif m == 2:
            p = (bf(hy_w_in[j]), hy_conv_w[j], hy_conv_b[j], hy_filt_w1[j], hy_filt_b1[j], hy_filt_w2[j],
                 hy_filt_b2[j], hy_filt_w3[j], hy_filt_freq[j], hy_skip[j], bf(hy_w_out[j]))
            xl = _hyena(xl, n1, ml[0], ml[1], ml[2], *p)
            if want_ctx:
                z = _hyena(z, n1, mc[0], mc[1], mc[2], *p)
        else:
            win, wo = bf(ssd_w_in[j]), bf(ssd_w_out[j])
            pin = (win, ssd_conv_w[j], ssd_conv_b[j], ssd_dt_bias[j])
            zg_c, xs_c, bm_c, cm_c, dt_c = _ssd_in(z, n1, mc[0], mc[1], *pin)
            zg_l, xs_l, bm_l, cm_l, dt_l = _ssd_in(xl, n1, ml[0], ml[1], *pin)
            h0 = jnp.zeros((2, SSD_HEADS // 2, SSD_STATE, 2 * SSD_P), F32)
            y_c, h_ctx = _ssd_scan(xs_c, bm_c, cm_c, dt_c, ssd_a_log[j], h0, want_ctx)
            y_l, _ = _ssd_scan(xs_l, bm_l, cm_l, dt_l, ssd_a_log[j], h_ctx, True)
            xl = _ssd_out(xl, y_l, xs_l, zg_l, ssd_d_skip[j], ssd_norm_g[j], wo, ml[2])
            if want_ctx:
                z = _ssd_out(z, y_c, xs_c, zg_c, ssd_d_skip[j], ssd_norm_g[j], wo, mc[2])
        wi, wo2 = bf(ffn_w_in[i]), bf(ffn_w_out[i])
        xl = _ffn(xl, norm2_g[i], ml[3], ml[4], ml[5], wi, wo2, final_g, final=(i == DEPTH - 1))
        if want_ctx:
            z = _ffn(z, norm2_g[i], mc[3], mc[4], mc[5], wi, wo2, final_g, final=False)
    return xl[None]
```

```python
import functools
import math

import numpy as np
import jax
import jax.numpy as jnp
from jax import lax
from jax.experimental import pallas as pl
from jax.experimental.pallas import tpu as pltpu

F32 = jnp.float32
BF16 = jnp.bfloat16
HIGHEST = lax.Precision.HIGHEST

D = 1024
DEPTH = 4
GRID_W = 64
NORM_EPS = 1e-6
FFN_HIDDEN = 2816
GM_CHUNK = 128
GM_WIDTH = 2 * D
GM_GROUPS = 8
GM_GW = GM_WIDTH // GM_GROUPS
HD = 64
QH = D // HD
KVH = 4
ROPE_THETA = 10000.0
LOG2E = math.log2(math.e)
FLASH_SCORE_BOUND = 30.0
FLASH_LOOKAHEAD = 2
HY_BANDS = 16
HY_EMB = 1 + 2 * HY_BANDS
HY_FILT_W = 64
HY_MAX_DECAY = math.log(1e-2) / 0.3
HY_MIN_DECAY = math.log(1e-2) / 1.5
SSD_INNER = 2 * D
SSD_P = 64
SSD_HEADS = SSD_INNER // SSD_P
SSD_GROUPS = 4
SSD_STATE = 128
SSD_CHUNK = 128
SSD_BC = SSD_GROUPS * SSD_STATE
SSD_CONV_DIM = SSD_INNER + 2 * SSD_BC

LANES = 128
SUBLANES = 8
VMEM_LIMIT_BYTES = 56 * 1024 * 1024
DFT_N2 = 128
DFT_SPLIT = 1


def _cparams(*sem):
    return pltpu.CompilerParams(dimension_semantics=sem, vmem_limit_bytes=VMEM_LIMIT_BYTES)


def _row(v):
    return v.reshape(1, -1)


def _normmod(x, g, shift, scale):
    ms = jnp.mean(x * x, axis=-1, keepdims=True)
    return x * lax.rsqrt(ms + NORM_EPS) * g * (1.0 + scale) + shift


def _silu(x):
    return x * jax.nn.sigmoid(x)


def _mod_kernel(cl_ref, cc_ref, w_ref, b_ref, o_ref):
    w = w_ref[0]
    for r, c_ref in enumerate((cl_ref, cc_ref)):
        a = _silu(c_ref[...])
        o_ref[0, r:r + 1, :] = jnp.sum(a * w, axis=0, keepdims=True) + b_ref[0]


def _modulation(c, c_ctx, mod_w, mod_b):
    tn = 1536
    n6 = 6 * D
    depth = mod_w.shape[0]
    return pl.pallas_call(
        _mod_kernel,
        out_shape=jax.ShapeDtypeStruct((depth, 2, n6), F32),
        grid=(depth, n6 // tn),
        in_specs=[
            pl.BlockSpec((D, 1), lambda i, n: (0, 0)),
            pl.BlockSpec((D, 1), lambda i, n: (0, 0)),
            pl.BlockSpec((1, D, tn), lambda i, n: (i, 0, n)),
            pl.BlockSpec((1, 1, tn), lambda i, n: (i, 0, n)),
        ],
        out_specs=pl.BlockSpec((1, 2, tn), lambda i, n: (i, 0, n)),
        compiler_params=_cparams("parallel", "parallel"),
        name="modulation",
    )(c.reshape(D, 1), c_ctx.reshape(D, 1), mod_w, mod_b.reshape(depth, 1, n6))


def _ffn_kernel(x_ref, g_ref, sh_ref, sc_ref, gate_ref, wi_ref, wo_ref, fg_ref, o_ref, *, final):
    x = x_ref[...]
    h = _normmod(x, g_ref[...], sh_ref[...], sc_ref[...]).astype(BF16)
    a = jnp.dot(h, wi_ref[:, :FFN_HIDDEN], preferred_element_type=F32)
    u = jnp.dot(h, wi_ref[:, FFN_HIDDEN:], preferred_element_type=F32)
    act = (_silu(a) * u).astype(BF16)
    y = x + gate_ref[...] * jnp.dot(act, wo_ref[...], preferred_element_type=F32)
    if final:
        ms = jnp.mean(y * y, axis=-1, keepdims=True)
        y = y * lax.rsqrt(ms + NORM_EPS) * fg_ref[...]
    o_ref[...] = y


def _resident(shape):
    return pl.BlockSpec(shape, lambda *_: (0,) * len(shape), pipeline_mode=pl.Buffered(1))


def _ffn(x, g, sh, sc, gate, w_in, w_out, final_g, final):
    lx = x.shape[0]
    tm = min(512, lx)
    vec = pl.BlockSpec((1, D), lambda i: (0, 0))
    return pl.pallas_call(
        functools.partial(_ffn_kernel, final=final),
        out_shape=jax.ShapeDtypeStruct((lx, D), F32),
        grid=(lx // tm,),
        in_specs=[
            pl.BlockSpec((tm, D), lambda i: (i, 0)),
            vec, vec, vec, vec,
            _resident((D, 2 * FFN_HIDDEN)),
            _resident((FFN_HIDDEN, D)),
            vec,
        ],
        out_specs=pl.BlockSpec((tm, D), lambda i: (i, 0)),
        compiler_params=_cparams("parallel"),
        name="ffn",
    )(x, _row(g), _row(sh), _row(sc), _row(gate), w_in, w_out, _row(final_g))


def _outproj_kernel(x_ref, a_ref, w_ref, gate_ref, o_ref):
    y = jnp.dot(a_ref[...], w_ref[...], preferred_element_type=F32)
    o_ref[...] = x_ref[...] + gate_ref[...] * y


def _outproj(x, a, w, gate):
    lx, kin = a.shape
    tm = min(512, lx)
    return pl.pallas_call(
        _outproj_kernel,
        out_shape=jax.ShapeDtypeStruct((lx, D), F32),
        grid=(lx // tm,),
        in_specs=[
            pl.BlockSpec((tm, D), lambda i: (i, 0)),
            pl.BlockSpec((tm, kin), lambda i: (i, 0)),
            _resident((kin, D)),
            pl.BlockSpec((1, D), lambda i: (0, 0)),
        ],
        out_specs=pl.BlockSpec((tm, D), lambda i: (i, 0)),
        compiler_params=_cparams("parallel"),
        name="outproj",
    )(x, a, w, _row(gate))


def _gmlp_kernel(x_ref, g_ref, sh_ref, sc_ref, gate_ref, win_ref, lng_ref, lnb_ref, ws_ref, bs_ref, wout_ref,
                 o_ref, *, tm):
    nsub = max(1, tm // (2 * GM_CHUNK))
    rs = tm // nsub

    def project(i):
        x = x_ref[i * rs:(i + 1) * rs, :]
        h = _normmod(x, g_ref[...], sh_ref[...], sc_ref[...]).astype(BF16)
        return jnp.dot(h, win_ref[...], preferred_element_type=F32)

    def mix(i, t):
        t = 0.5 * t * (1.0 + lax.erf(t * (1.0 / math.sqrt(2.0))))
        u = t[:, :GM_WIDTH]
        v = t[:, GM_WIDTH:]
        mu = jnp.mean(v, axis=-1, keepdims=True)
        vc = v - mu
        var = jnp.mean(vc * vc, axis=-1, keepdims=True)
        v = (vc * lax.rsqrt(var + NORM_EPS) * lng_ref[...] + lnb_ref[...]).astype(BF16)
        rows = []
        for q in range(rs // GM_CHUNK):
            cols = []
            for gidx in range(GM_GROUPS):
                vq = v[q * GM_CHUNK:(q + 1) * GM_CHUNK, gidx * GM_GW:(gidx + 1) * GM_GW]
                bias = bs_ref[gidx]
                m = jnp.dot(ws_ref[gidx], vq, preferred_element_type=F32)
                cols.append(m + jnp.concatenate([bias] * (GM_GW // LANES), axis=1))
            rows.append(jnp.concatenate(cols, axis=1))
        gated = (u * jnp.concatenate(rows, axis=0)).astype(BF16)
        y = jnp.dot(gated, wout_ref[...], preferred_element_type=F32)
        o_ref[i * rs:(i + 1) * rs, :] = x_ref[i * rs:(i + 1) * rs, :] + gate_ref[...] * y

    t_next = project(0)
    for i in range(nsub):
        t = t_next
        if i + 1 < nsub:
            t_next = project(i + 1)
        mix(i, t)


def _gmlp(x, g, sh, sc, gate, w_in, ln_g, ln_b, ws, bs, w_out):
    lx = x.shape[0]
    tm = min(512, lx)
    vec = pl.BlockSpec((1, D), lambda i: (0, 0))
    vecw = pl.BlockSpec((1, GM_WIDTH), lambda i: (0, 0))
    bsb = jnp.broadcast_to(bs[:, :, None], (GM_GROUPS, GM_CHUNK, LANES))
    return pl.pallas_call(
        functools.partial(_gmlp_kernel, tm=tm),
        out_shape=jax.ShapeDtypeStruct((lx, D), F32),
        grid=(lx // tm,),
        in_specs=[
            pl.BlockSpec((tm, D), lambda i: (i, 0)),
            vec, vec, vec, vec,
            _resident((D, 2 * GM_WIDTH)),
            vecw, vecw,
            pl.BlockSpec((GM_GROUPS, GM_CHUNK, GM_CHUNK), lambda i: (0, 0, 0)),
            pl.BlockSpec((GM_GROUPS, GM_CHUNK, LANES), lambda i: (0, 0, 0)),
            _resident((GM_WIDTH, D)),
        ],
        out_specs=pl.BlockSpec((tm, D), lambda i: (i, 0)),
        compiler_params=_cparams("parallel"),
        name="gmlp",
    )(x, _row(g), _row(sh), _row(sc), _row(gate), w_in, _row(ln_g), _row(ln_b), ws, bsb, w_out)


def _group_sumsq(t, e_ref):
    sq = t * t
    hi = sq.astype(BF16)
    lo = (sq - hi.astype(F32)).astype(BF16)
    outs = []
    for j in range(t.shape[1] // LANES):
        sl = slice(j * LANES, (j + 1) * LANES)
        outs.append(jnp.dot(hi[:, sl], e_ref[...], preferred_element_type=F32)
                    + jnp.dot(lo[:, sl], e_ref[...], preferred_element_type=F32))
    return jnp.concatenate(outs, axis=1)


def _rope(t, cosf, sinf):
    w = t.shape[1]
    lane = lax.broadcasted_iota(jnp.int32, t.shape, 1)
    first = (lane % HD) < (HD // 2)
    partner = jnp.where(first, pltpu.roll(t, w - HD // 2, axis=1), pltpu.roll(t, HD // 2, axis=1))
    reps = w // LANES
    c = jnp.concatenate([cosf] * reps, axis=1)
    s = jnp.concatenate([sinf] * reps, axis=1)
    return t * c + partner * s


def _qkv_kernel(x_ref, g_ref, sh_ref, sc_ref, w_ref, qg_ref, kg_ref, e_ref, cos_ref, sin_ref,
                qt_ref, k_ref, vt_ref, *, rope):
    h = _normmod(x_ref[...], g_ref[...], sh_ref[...], sc_ref[...]).astype(BF16)
    qkv = jnp.dot(h, w_ref[...], preferred_element_type=F32)
    q = qkv[:, :D]
    k = qkv[:, D:D + KVH * HD]
    v = qkv[:, D + KVH * HD:]
    q = q * lax.rsqrt(_group_sumsq(q, e_ref) * (1.0 / HD) + NORM_EPS) * qg_ref[...]
    k = k * lax.rsqrt(_group_sumsq(k, e_ref) * (1.0 / HD) + NORM_EPS) * kg_ref[...]
    if rope:
        q = _rope(q, cos_ref[...], sin_ref[...])
        k = _rope(k, cos_ref[...], sin_ref[...])
    qt_ref[...] = (q * (HD ** -0.5 * LOG2E)).T.astype(BF16)
    for gidx in range(KVH):
        k_ref[gidx] = k[:, gidx * HD:(gidx + 1) * HD].astype(BF16)
    vt_ref[...] = v.T.astype(BF16)


def _qkv(x, g, sh, sc, w_qkv, q_g, k_g, rope):
    lx = x.shape[0]
    tm = min(512, lx)
    vec = pl.BlockSpec((1, D), lambda i: (0, 0))
    kvw = KVH * HD
    rows = lx // GRID_W
    row = jnp.repeat(jnp.arange(rows, dtype=F32), GRID_W)
    col = jnp.tile(jnp.arange(GRID_W, dtype=F32), rows)
    n = HD // 4
    inv = ROPE_THETA ** (-jnp.arange(n, dtype=F32) / n)
    ang = jnp.concatenate([row[:, None] * inv, col[:, None] * inv], axis=-1)
    cos, sin = jnp.cos(ang), jnp.sin(ang)
    cosf = jnp.tile(jnp.concatenate([cos, cos], axis=-1), (1, LANES // HD))
    sinf = jnp.tile(jnp.concatenate([-sin, sin], axis=-1), (1, LANES // HD))
    eblk = jnp.asarray(np.kron(np.eye(LANES // HD), np.ones((HD, HD))), BF16)
    tab = pl.BlockSpec((tm, LANES), lambda i: (i, 0))
    return pl.pallas_call(
        functools.partial(_qkv_kernel, rope=rope),
        out_shape=(jax.ShapeDtypeStruct((D, lx), BF16),
                   jax.ShapeDtypeStruct((KVH, lx, HD), BF16),
                   jax.ShapeDtypeStruct((kvw, lx), BF16)),
        grid=(lx // tm,),
        in_specs=[
            pl.BlockSpec((tm, D), lambda i: (i, 0)),
            vec, vec, vec,
            _resident((D, D + 2 * kvw)),
            vec,
            pl.BlockSpec((1, kvw), lambda i: (0, 0)),
            pl.BlockSpec((LANES, LANES), lambda i: (0, 0)),
            tab, tab,
        ],
        out_specs=(pl.BlockSpec((D, tm), lambda i: (0, i)),
                   pl.BlockSpec((KVH, tm, HD), lambda i: (0, i, 0)),
                   pl.BlockSpec((kvw, tm), lambda i: (0, i))),
        compiler_params=_cparams("parallel"),
        name="qkv_proj",
    )(x, _row(g), _row(sh), _row(sc), w_qkv, _row(jnp.tile(q_g, QH)), _row(jnp.tile(k_g, KVH)), eblk, cosf, sinf)


def _flash_kernel(qt_ref, k_ref, vt_ref, o_ref, qg_scr, m_scr, l_scr, acc_scr, *, tq, ts, tc, nkv, bounded):
    j = pl.program_id(1)
    gq = QH // KVH
    mcols = gq * tq

    @pl.when(j == 0)
    def _():
        for h in range(QH):
            qg_scr[h // gq, :, (h % gq) * tq:(h % gq + 1) * tq] = qt_ref[h * HD:(h + 1) * HD, :]
        m_scr[...] = jnp.full(m_scr.shape, -jnp.inf, F32)
        l_scr[...] = jnp.zeros_like(l_scr)
        acc_scr[...] = jnp.zeros_like(acc_scr)

    stages = [(g, c) for c in range(ts // tc) for g in range(KVH)]

    def scores(g, c):
        return jnp.dot(k_ref[g, c * tc:(c + 1) * tc, :], qg_scr[g], preferred_element_type=F32)

    pending = [scores(*st) for st in stages[:FLASH_LOOKAHEAD]]
    for idx, (g, c) in enumerate(stages):
        s = pending.pop(0)
        if idx + FLASH_LOOKAHEAD < len(stages):
            pending.append(scores(*stages[idx + FLASH_LOOKAHEAD]))
        vt = vt_ref[g * HD:(g + 1) * HD, c * tc:(c + 1) * tc]
        if bounded:
            p = jnp.exp2(s)
            l_scr[g] += jnp.sum(p, axis=0, keepdims=True)
            acc_scr[g] += jnp.dot(vt, p.astype(BF16), preferred_element_type=F32)
        else:
            m_prev = m_scr[g]
            m_new = jnp.maximum(m_prev, jnp.max(s, axis=0, keepdims=True))
            alpha = jnp.exp2(m_prev - m_new)
            p = jnp.exp2(s - m_new)
            l_scr[g] = alpha * l_scr[g] + jnp.sum(p, axis=0, keepdims=True)
            acc_scr[g] = alpha * acc_scr[g] + jnp.dot(vt, p.astype(BF16), preferred_element_type=F32)
            m_scr[g] = m_new

    @pl.when(j == nkv - 1)
    def _():
        rows = []
        for g in range(KVH):
            o = acc_scr[g] / l_scr[g]
            rows += [o[:, r * tq:(r + 1) * tq] for r in range(gq)]
        o_ref[...] = jnp.concatenate(rows, axis=0).T.astype(o_ref.dtype)


def _flash(qt, k, vt, s_len, ts, bounded=False):
    lq = qt.shape[1]
    tq = min(128, lq)
    nkv = s_len // ts
    gq = QH // KVH
    kvw = KVH * HD
    tc = 2 * LANES if ts % (2 * LANES) == 0 else LANES
    return pl.pallas_call(
        functools.partial(_flash_kernel, tq=tq, ts=ts, tc=tc, nkv=nkv, bounded=bounded),
        out_shape=jax.ShapeDtypeStruct((lq, D), BF16),
        grid=(lq // tq, nkv),
        in_specs=[
            pl.BlockSpec((D, tq), lambda i, j: (0, i)),
            pl.BlockSpec((KVH, ts, HD), lambda i, j: (0, j, 0)),
            pl.BlockSpec((kvw, ts), lambda i, j: (0, j)),
        ],
        out_specs=pl.BlockSpec((tq, D), lambda i, j: (i, 0)),
        scratch_shapes=[
            pltpu.VMEM((KVH, HD, gq * tq), BF16),
            pltpu.VMEM((KVH, 1, gq * tq), F32),
            pltpu.VMEM((KVH, 1, gq * tq), F32),
            pltpu.VMEM((KVH, HD, gq * tq), F32),
        ],
        compiler_params=_cparams("parallel", "arbitrary"),
        name="flash_attn",
    )(qt, k, vt)


def _halo_specs(tm, lx):
    nb = lx // SUBLANES
    step = tm // SUBLANES
    prev = pl.BlockSpec((SUBLANES, D), lambda i: (jnp.maximum(i * step - 1, 0), 0))
    nxt = pl.BlockSpec((SUBLANES, D), lambda i: (jnp.minimum((i + 1) * step, nb - 1), 0))
    return prev, nxt


def _conv3(p_main, p_halo, cw, cb, first, last):
    tm = p_main.shape[0]
    rid = lax.broadcasted_iota(jnp.int32, p_main.shape, 0)
    before = jnp.where(first, 0.0, p_halo[SUBLANES - 1:SUBLANES, :])
    after = jnp.where(last, 0.0, p_halo[SUBLANES:SUBLANES + 1, :])
    up = jnp.where(rid == 0, before, pltpu.roll(p_main, 1, axis=0))
    dn = jnp.where(rid == tm - 1, after, pltpu.roll(p_main, tm - 1, axis=0))
    return cw[0:1, :] * up + cw[1:2, :] * p_main + cw[2:3, :] * dn + cb


def _norm_halo(xm_ref, xp_ref, xn_ref, g_ref, sh_ref, sc_ref):
    g, sh, sc = g_ref[...], sh_ref[...], sc_ref[...]
    h = _normmod(xm_ref[...], g, sh, sc).astype(BF16)
    hh = jnp.concatenate([_normmod(xp_ref[...], g, sh, sc), _normmod(xn_ref[...], g, sh, sc)], axis=0).astype(BF16)
    return h, hh


def _hy_in_kernel(xm_ref, xp_ref, xn_ref, g_ref, sh_ref, sc_ref, w_ref, cw_ref, cb_ref, x0_ref, ut_ref, *, nt):
    i = pl.program_id(0)
    first, last = i == 0, i == nt - 1
    h, hh = _norm_halo(xm_ref, xp_ref, xn_ref, g_ref, sh_ref, sc_ref)

    def project(b):
        sl = slice(b * D, (b + 1) * D)
        return (jnp.dot(h, w_ref[:, sl], preferred_element_type=F32),
                jnp.dot(hh, w_ref[:, sl], preferred_element_type=F32))

    def conv(b, p):
        sl = slice(b * D, (b + 1) * D)
        return _conv3(p[0], p[1], cw_ref[:, sl], cb_ref[:, sl], first, last)

    p0 = project(0)
    p1 = project(1)
    x0_ref[...] = conv(0, p0)
    p2 = project(2)
    x1 = conv(1, p1)
    ut_ref[...] = (x1 * conv(2, p2)).T


def _hy_in(x, g, sh, sc, w_in, conv_w, conv_b):
    lx = x.shape[0]
    tm = min(512, lx)
    nt = lx // tm
    vec = pl.BlockSpec((1, D), lambda i: (0, 0))
    prev, nxt = _halo_specs(tm, lx)
    return pl.pallas_call(
        functools.partial(_hy_in_kernel, nt=nt),
        out_shape=(jax.ShapeDtypeStruct((lx, D), F32), jax.ShapeDtypeStruct((D, lx), F32)),
        grid=(nt,),
        in_specs=[
            pl.BlockSpec((tm, D), lambda i: (i, 0)), prev, nxt,
            vec, vec, vec,
            _resident((D, 3 * D)),
            pl.BlockSpec((3, 3 * D), lambda i: (0, 0)),
            pl.BlockSpec((1, 3 * D), lambda i: (0, 0)),
        ],
        out_specs=(pl.BlockSpec((tm, D), lambda i: (i, 0)), pl.BlockSpec((D, tm), lambda i: (0, i))),
        compiler_params=_cparams("parallel"),
        name="hyena_in",
    )(x, x, x, _row(g), _row(sh), _row(sc), w_in, conv_w, _row(conv_b))


def _hy_filter_kernel(ft_ref, w1_ref, b1_ref, w2_ref, b2_ref, w3h_ref, w3l_ref, fr_ref, dl_ref, sk_ref, kt_ref,
                      *, tm, ltrue):
    i = pl.program_id(0)
    feats = ft_ref[...]
    t = feats[:, 0:1]
    fr = fr_ref[...]
    hid = jnp.sin(fr * (jnp.dot(feats, w1_ref[...], preferred_element_type=F32, precision=HIGHEST) + b1_ref[...]))
    hid = jnp.sin(fr * (jnp.dot(hid, w2_ref[...], preferred_element_type=F32, precision=HIGHEST) + b2_ref[...]))
    hh = hid.astype(BF16)
    hl = (hid - hh.astype(F32)).astype(BF16)
    w3h = w3h_ref[...]
    k = (jnp.dot(hh, w3h, preferred_element_type=F32) + jnp.dot(hl, w3h, preferred_element_type=F32)
         + jnp.dot(hh, w3l_ref[...], preferred_element_type=F32))
    k = k * jnp.exp(-t * dl_ref[...])
    rid = lax.broadcasted_iota(jnp.int32, k.shape, 0) + i * tm
    cid = lax.broadcasted_iota(jnp.int32, k.shape, 1)
    k = jnp.where(rid == 0, jnp.where(cid < D, k + sk_ref[...], 0.0), k)
    k = jnp.where(rid < ltrue, k, 0.0)
    kt_ref[...] = k.T


def _hy_filter(ltrue, lpad, w1, b1, w2, b2, w3, freq, skip):
    tm = min(512, lpad)
    w1p = jnp.zeros((LANES, HY_FILT_W), F32).at[:HY_EMB].set(w1)
    deltas = jnp.abs(jnp.linspace(HY_MIN_DECAY, HY_MAX_DECAY, D, dtype=F32))
    pos = jnp.arange(lpad, dtype=F32)
    zf = (2.0 * math.pi * pos / ltrue)[:, None] * jnp.linspace(1e-4, HY_BANDS - 1, HY_BANDS, dtype=F32)[None, :]
    feats = jnp.concatenate([(pos / (ltrue - 1))[:, None], jnp.cos(zf), -jnp.sin(zf),
                             jnp.zeros((lpad, LANES - HY_EMB), F32)], axis=-1)
    w3h = w3.astype(BF16)
    w3l = (w3 - w3h.astype(F32)).astype(BF16)
    full = lambda a: pl.BlockSpec(a.shape, lambda i: (0,) * a.ndim)
    args = (w1p, _row(b1), w2, _row(b2), w3h, w3l, _row(freq), _row(jnp.tile(deltas, 2)),
            _row(jnp.concatenate([skip, jnp.zeros((D,), F32)])))
    return pl.pallas_call(
        functools.partial(_hy_filter_kernel, tm=tm, ltrue=ltrue),
        out_shape=jax.ShapeDtypeStruct((2 * D, lpad), F32),
        grid=(lpad // tm,),
        in_specs=[pl.BlockSpec((tm, LANES), lambda i: (i, 0))] + [full(a) for a in args],
        out_specs=pl.BlockSpec((2 * D, tm), lambda i: (0, i)),
        compiler_params=_cparams("parallel"),
        name="hyena_filter",
    )(feats, *args)


def _dft_consts(nh):
    n1 = 2 * nh
    n = n1 * DFT_N2
    k1 = np.arange(n1)[:, None].astype(np.float64)
    a1 = 2.0 * np.pi * k1 * np.arange(nh)[None, :] / n1
    f1 = np.concatenate([np.cos(a1), -np.sin(a1)], axis=0)
    at = 2.0 * np.pi * ((np.arange(n1)[:, None] * np.arange(DFT_N2)[None, :]) % n) / n
    a2 = 2.0 * np.pi * ((np.arange(DFT_N2)[:, None] * np.arange(DFT_N2)[None, :]) % DFT_N2) / DFT_N2
    c2, s2 = np.cos(a2), np.sin(a2)
    f2 = np.block([[c2, -s2], [s2, c2]])
    g2 = np.block([[c2, s2], [-s2, c2]])
    g1 = np.concatenate([np.cos(a1).T, -np.sin(a1).T], axis=1) / n
    as32 = lambda a: jnp.asarray(a, F32)

    def parts(a):
        hi = a.astype(BF16)
        if DFT_SPLIT == 1:
            return (jnp.asarray(hi),)
        return (jnp.asarray(hi), jnp.asarray((a - hi.astype(np.float64)).astype(BF16)))

    return parts(f1), as32(np.cos(at)), as32(np.sin(at)), parts(f2), parts(g2), parts(g1)


def _split_bf16(a):
    hi = a.astype(BF16)
    if DFT_SPLIT == 1:
        return (hi,)
    return (hi, (a - hi.astype(F32)).astype(BF16))


def _split_dot(a, b):
    out = jnp.dot(a[0], b[0], preferred_element_type=F32)
    if DFT_SPLIT > 1:
        out = out + jnp.dot(a[1], b[0], preferred_element_type=F32) + jnp.dot(a[0], b[1], preferred_element_type=F32)
    return out


def _load_parts(refs):
    return tuple(r[...] for r in refs)


def _hy_conv_kernel(x_ref, kf_ref, kb_ref, *refs, cb, n1):
    ns = DFT_SPLIT
    f1, (twc_ref, tws_ref), f2 = refs[:ns], refs[ns:ns + 2], refs[ns + 2:2 * ns + 2]
    g2, g1, o_ref = refs[2 * ns + 2:3 * ns + 2], refs[3 * ns + 2:4 * ns + 2], refs[4 * ns + 2]
    twc, tws = twc_ref[...], tws_ref[...]
    f1, f2, g2, g1 = _load_parts(f1), _load_parts(f2), _load_parts(g2), _load_parts(g1)
    hc = cb // 2
    nt = 3 * hc

    def dft1(half):
        xs = [r[half * hc + c] for r in (x_ref, kf_ref, kb_ref) for c in range(hc)]
        return _split_dot(f1, _split_bf16(jnp.concatenate(xs, axis=1)))

    def twiddle_rows(a):
        rows = []
        for t in range(nt):
            ar = a[:n1, t * DFT_N2:(t + 1) * DFT_N2]
            ai = a[n1:, t * DFT_N2:(t + 1) * DFT_N2]
            rows.append(jnp.concatenate([ar * twc + ai * tws, ai * twc - ar * tws], axis=1))
        return jnp.concatenate(rows, axis=0)

    def dft2(rows):
        return _split_dot(_split_bf16(rows), f2)

    def product(spec_all):
        rows = hc * n1
        spec, hf, hb = spec_all[:rows], spec_all[rows:2 * rows], spec_all[2 * rows:]
        hr = hf[:, :DFT_N2] + hb[:, :DFT_N2]
        hi = hf[:, DFT_N2:] - hb[:, DFT_N2:]
        xr, xi = spec[:, :DFT_N2], spec[:, DFT_N2:]
        return jnp.concatenate([xr * hr - xi * hi, xr * hi + xi * hr], axis=1)

    def idft2(y):
        return _split_dot(_split_bf16(y), g2)

    def twiddle_cols(b):
        cols = []
        for c in range(hc):
            br = b[c * n1:(c + 1) * n1, :DFT_N2]
            bi = b[c * n1:(c + 1) * n1, DFT_N2:]
            cols.append(jnp.concatenate([br * twc - bi * tws, bi * twc + br * tws], axis=0))
        return jnp.concatenate(cols, axis=1)

    def idft1(cols):
        return _split_dot(g1, _split_bf16(cols))

    def store(half, out):
        for c in range(hc):
            o_ref[half * hc + c] = out[:, c * DFT_N2:(c + 1) * DFT_N2]

    a0 = dft1(0)
    a1 = dft1(1)
    s0 = dft2(twiddle_rows(a0))
    s1 = dft2(twiddle_rows(a1))
    b0 = idft2(product(s0))
    b1 = idft2(product(s1))
    o0 = idft1(twiddle_cols(b0))
    o1 = idft1(twiddle_cols(b1))
    store(0, o0)
    store(1, o1)


def _hy_longconv(ut, kt, lpad):
    nh = lpad // DFT_N2
    n1 = 2 * nh
    cb = max(8, min(64, 2048 // n1))
    f1, twc, tws, f2, g2, g1 = _dft_consts(nh)
    consts = f1 + (twc, tws) + f2 + g2 + g1
    full = lambda a: pl.BlockSpec(a.shape, lambda i: (0,) * a.ndim)
    k3 = kt.reshape(2 * D, nh, DFT_N2)
    u3 = ut.reshape(D, nh, DFT_N2)
    nb = D // cb
    y3 = pl.pallas_call(
        functools.partial(_hy_conv_kernel, cb=cb, n1=n1),
        out_shape=jax.ShapeDtypeStruct((D, nh, DFT_N2), F32),
        grid=(nb,),
        in_specs=[pl.BlockSpec((cb, nh, DFT_N2), lambda i: (i, 0, 0)),
                  pl.BlockSpec((cb, nh, DFT_N2), lambda i: (i, 0, 0)),
                  pl.BlockSpec((cb, nh, DFT_N2), lambda i: (i + nb, 0, 0))]
                 + [full(a) for a in consts],
        out_specs=pl.BlockSpec((cb, nh, DFT_N2), lambda i: (i, 0, 0)),
        compiler_params=_cparams("parallel"),
        name="hyena_longconv",
    )(u3, k3, k3, *consts)
    return y3.reshape(D, lpad)


def _hy_out_kernel(x_ref, x0_ref, yt_ref, w_ref, gate_ref, o_ref):
    a = (x0_ref[...] * yt_ref[...].T).astype(BF16)
    y = jnp.dot(a, w_ref[...], preferred_element_type=F32)
    o_ref[...] = x_ref[...] + gate_ref[...] * y


def _hy_out(x, x0, yt, w_out, gate):
    lx = x.shape[0]
    tm = min(512, lx)
    return pl.pallas_call(
        _hy_out_kernel,
        out_shape=jax.ShapeDtypeStruct((lx, D), F32),
        grid=(lx // tm,),
        in_specs=[
            pl.BlockSpec((tm, D), lambda i: (i, 0)),
            pl.BlockSpec((tm, D), lambda i: (i, 0)),
            pl.BlockSpec((D, tm), lambda i: (0, i)),
            _resident((D, D)),
            pl.BlockSpec((1, D), lambda i: (0, 0)),
        ],
        out_specs=pl.BlockSpec((tm, D), lambda i: (i, 0)),
        compiler_params=_cparams("parallel"),
        name="hyena_out",
    )(x, x0, yt, w_out, _row(gate))


def _hyena(x, g, sh, sc, gate, w_in, conv_w, conv_b, w1, b1, w2, b2, w3, freq, skip, w_out):
    lx = x.shape[0]
    lpad = max(lx, SUBLANES * DFT_N2)
    x0, ut = _hy_in(x, g, sh, sc, w_in, conv_w, conv_b)
    if lpad != lx:
        ut = jnp.pad(ut, ((0, 0), (0, lpad - lx)))
    kt = _hy_filter(lx, lpad, w1, b1, w2, b2, w3, freq, skip)
    yt = _hy_longconv(ut, kt, lpad)[:, :lx]
    return _hy_out(x, x0, yt, w_out, gate)


def _ssd_in_kernel(xm_ref, xp_ref, xn_ref, g_ref, sh_ref, sc_ref, wz_ref, wx_ref, wd_ref, cw_ref, cb_ref, db_ref,
                   zg_ref, xs_ref, bm_ref, cm_ref, dt_ref, *, nt):
    i = pl.program_id(0)
    first, last = i == 0, i == nt - 1
    h, hh = _norm_halo(xm_ref, xp_ref, xn_ref, g_ref, sh_ref, sc_ref)
    cw = SSD_CONV_DIM // 3

    def project(b):
        sl = slice(b * cw, (b + 1) * cw)
        return (jnp.dot(h, wx_ref[:, sl], preferred_element_type=F32),
                jnp.dot(hh, wx_ref[:, sl], preferred_element_type=F32))

    def conv(b, p):
        sl = slice(b * cw, (b + 1) * cw)
        return _silu(_conv3(p[0], p[1], cw_ref[:, sl], cb_ref[:, sl], first, last))

    p0 = project(0)
    p1 = project(1)
    xs_ref[:, :cw] = conv(0, p0).astype(xs_ref.dtype)
    p2 = project(2)
    xs_ref[:, cw:] = conv(1, p1).astype(xs_ref.dtype)
    zg = jnp.dot(h, wz_ref[...], preferred_element_type=F32)
    bc = conv(2, p2)
    bm_ref[...] = bc[:, :SSD_BC]
    cm_ref[...] = bc[:, SSD_BC:]
    zg_ref[...] = zg.astype(zg_ref.dtype)
    dt = jnp.dot(h, wd_ref[...], preferred_element_type=F32) + db_ref[...]
    dt = jnp.maximum(dt, 0.0) + jnp.log1p(jnp.exp(-jnp.abs(dt)))
    lane = lax.broadcasted_iota(jnp.int32, dt.shape, 1)
    dt = jnp.where((lane % LANES) < SSD_HEADS, dt, 0.0)
    dt_ref[0] = dt[:, :LANES]
    dt_ref[1] = dt[:, LANES:]


def _ssd_in(x, g, sh, sc, w_in, conv_w, conv_b, dt_bias):
    lx = x.shape[0]
    tm = min(512, lx)
    nt = lx // tm
    vec = pl.BlockSpec((1, D), lambda i: (0, 0))
    prev, nxt = _halo_specs(tm, lx)
    wz = w_in[:, :SSD_INNER]
    wx = w_in[:, SSD_INNER:SSD_INNER + SSD_CONV_DIM]
    wdt = w_in[:, SSD_INNER + SSD_CONV_DIM:]
    pad = LANES - SSD_HEADS
    wd = jnp.concatenate([jnp.pad(wdt[:, :SSD_HEADS], ((0, 0), (0, pad))),
                          jnp.pad(wdt[:, SSD_HEADS:], ((0, 0), (0, pad)))], axis=1)
    db = jnp.pad(dt_bias, ((0, 0), (0, pad))).reshape(1, 2 * LANES)
    full = lambda a: pl.BlockSpec(a.shape, lambda i: (0,) * a.ndim)
    rowblk = lambda w: pl.BlockSpec((tm, w), lambda i: (i, 0))
    return pl.pallas_call(
        functools.partial(_ssd_in_kernel, nt=nt),
        out_shape=(jax.ShapeDtypeStruct((lx, SSD_INNER), BF16), jax.ShapeDtypeStruct((lx, SSD_INNER), BF16),
                   jax.ShapeDtypeStruct((lx, SSD_BC), F32), jax.ShapeDtypeStruct((lx, SSD_BC), F32),
                   jax.ShapeDtypeStruct((2, lx, LANES), F32)),
        grid=(nt,),
        in_specs=[rowblk(D), prev, nxt, vec, vec, vec, _resident(wz.shape), _resident(wx.shape), _resident(wd.shape),
                  pl.BlockSpec((3, SSD_CONV_DIM), lambda i: (0, 0)),
                  pl.BlockSpec((1, SSD_CONV_DIM), lambda i: (0, 0)),
                  pl.BlockSpec((1, 2 * LANES), lambda i: (0, 0))],
        out_specs=(rowblk(SSD_INNER), rowblk(SSD_INNER), rowblk(SSD_BC), rowblk(SSD_BC),
                   pl.BlockSpec((2, tm, LANES), lambda i: (0, i, 0))),
        compiler_params=_cparams("parallel"),
        name="ssd_in",
    )(x, x, x, _row(g), _row(sh), _row(sc), wz, wx, wd, conv_w, _row(conv_b), db)


def _expand_heads(arr, e_ref):
    hi = arr.astype(BF16)
    lo = (arr - hi.astype(F32)).astype(BF16)
    e = e_ref[...]
    return jnp.dot(hi, e, preferred_element_type=F32) + jnp.dot(lo, e, preferred_element_type=F32)


def _ssd_prologue(dt_ref, a_row, tri, e_ref, need_y):
    dt = dt_ref[0]
    a = dt * a_row
    acs = jnp.dot(tri, a, preferred_element_type=F32, precision=HIGHEST)
    total = jnp.sum(a, axis=0, keepdims=True)
    ctx = dict(keep=tri > 0.5, acs=acs)
    ctx["wend_x"] = _expand_heads(jnp.exp(total - acs) * dt, e_ref)
    ctx["etot_x"] = _expand_heads(jnp.broadcast_to(jnp.exp(total), (SUBLANES, LANES)), e_ref)[0:1, :]
    if need_y:
        ctx["eacs_x"] = _expand_heads(jnp.exp(acs), e_ref)
        ctx["acs_t"] = acs.T
        ctx["dt_t"] = dt.T
    return ctx


def _ssd_prepare(ctx, xs_ref, bm_ref, cm_ref, g, need_y):
    q = SSD_CHUNK
    ppg = SSD_HEADS // 2 // SSD_GROUPS
    bg = bm_ref[:, g * SSD_STATE:(g + 1) * SSD_STATE]
    ops = dict(cg=cm_ref[:, g * SSD_STATE:(g + 1) * SSD_STATE].astype(BF16), bgt=bg.T.astype(BF16), xp=[], xw=[], m=[])
    if need_y:
        cb = lax.dot_general(ops["cg"], bg.astype(BF16), (((1,), (1,)), ((), ())), preferred_element_type=F32)
    for r in range(ppg):
        pidx = g * ppg + r
        psl = slice(pidx * LANES, (pidx + 1) * LANES)
        xp = xs_ref[:, psl]
        ops["xp"].append(xp)
        ops["xw"].append((xp.astype(F32) * ctx["wend_x"][:, psl]).astype(BF16))
        if need_y:
            for hd in (2 * pidx, 2 * pidx + 1):
                seg = jnp.broadcast_to(ctx["acs"][:, hd:hd + 1], (q, LANES)) - ctx["acs_t"][hd:hd + 1, :]
                lm = jnp.exp(jnp.where(ctx["keep"], seg, -jnp.inf))
                ops["m"].append((cb * lm * ctx["dt_t"][hd:hd + 1, :]).astype(BF16))
    return ops


def _ssd_issue(ctx, ops, g, h_scr, y_ref, need_y):
    ppg = SSD_HEADS // 2 // SSD_GROUPS
    left = lax.broadcasted_iota(jnp.int32, (SSD_CHUNK, LANES), 1) < SSD_P
    for r in range(ppg):
        pidx = g * ppg + r
        psl = slice(pidx * LANES, (pidx + 1) * LANES)
        hs = h_scr[pidx]
        if need_y:
            yd = [jnp.dot(ops["m"][2 * r + e], ops["xp"][r], preferred_element_type=F32) for e in range(2)]
            yoff = jnp.dot(ops["cg"], hs.astype(BF16), preferred_element_type=F32) * ctx["eacs_x"][:, psl]
            y_ref[:, psl] = (jnp.where(left, yd[0], yd[1]) + yoff).astype(y_ref.dtype)
        st = jnp.dot(ops["bgt"], ops["xw"][r], preferred_element_type=F32)
        h_scr[pidx] = hs * ctx["etot_x"][:, psl] + st


def _ssd_scan_kernel(xsf_ref, xsb_ref, bmf_ref, bmb_ref, cmf_ref, cmb_ref, dtf_ref, dtb_ref, a_ref, tri_ref, e_ref,
                     h0_ref, yf_ref, yb_ref, hfin_ref, h_scr, *, nc, need_y):
    s = pl.program_id(0)

    @pl.when(s == 0)
    def _():
        h_scr[...] = h0_ref[...]

    dirs = ((xsf_ref, bmf_ref, cmf_ref, dtf_ref, yf_ref), (xsb_ref, bmb_ref, cmb_ref, dtb_ref, yb_ref))
    ctxs = [_ssd_prologue(dirs[d][3], a_ref[d], tri_ref[d], e_ref, need_y) for d in range(2)]
    stages = [(d, g) for g in range(SSD_GROUPS) for d in range(2)]
    prep = lambda d, g: _ssd_prepare(ctxs[d], dirs[d][0], dirs[d][1], dirs[d][2], g, need_y)
    pending = prep(*stages[0])
    for idx, (d, g) in enumerate(stages):
        ops = pending
        if idx + 1 < len(stages):
            pending = prep(*stages[idx + 1])
        _ssd_issue(ctxs[d], ops, g, h_scr.at[d], dirs[d][4], need_y)

    @pl.when(s == nc - 1)
    def _():
        hfin_ref[...] = h_scr[...]


def _ssd_scan(xs, bm, cm, dt2, a_log, h0, need_y):
    lx = xs.shape[0]
    q = SSD_CHUNK
    nc = lx // q
    npair = SSD_HEADS // 2
    a = -jnp.exp(a_log.astype(F32))
    a_pad = jnp.pad(a, ((0, 0), (0, LANES - SSD_HEADS))).reshape(2, 1, LANES)
    lower = np.tril(np.ones((q, q), np.float32))
    tri = jnp.asarray(np.stack([lower, lower.T]))
    expand = jnp.asarray(np.kron(np.eye(LANES)[:, :SSD_HEADS], np.ones((1, SSD_P))), BF16)
    fwd = lambda s: s
    bwd = lambda s: nc - 1 - s
    rows = lambda w, idx: pl.BlockSpec((q, w), lambda s: (idx(s), 0))
    full = lambda a_: pl.BlockSpec(a_.shape, lambda s: (0,) * a_.ndim)
    ylen = lx if need_y else q
    yspec = (lambda idx: rows(SSD_INNER, idx)) if need_y else (lambda idx: pl.BlockSpec((q, SSD_INNER), lambda s: (0, 0)))
    yf, yb, hfin = pl.pallas_call(
        functools.partial(_ssd_scan_kernel, nc=nc, need_y=need_y),
        out_shape=[jax.ShapeDtypeStruct((ylen, SSD_INNER), BF16), jax.ShapeDtypeStruct((ylen, SSD_INNER), BF16),
                   jax.ShapeDtypeStruct(h0.shape, F32)],
        grid=(nc,),
        in_specs=[
            rows(SSD_INNER, fwd), rows(SSD_INNER, bwd), rows(SSD_BC, fwd), rows(SSD_BC, bwd),
            rows(SSD_BC, fwd), rows(SSD_BC, bwd),
            pl.BlockSpec((1, q, LANES), lambda s: (0, s, 0)),
            pl.BlockSpec((1, q, LANES), lambda s: (1, nc - 1 - s, 0)),
            full(a_pad), full(tri), full(expand), full(h0),
        ],
        out_specs=[yspec(fwd), yspec(bwd), full(h0)],
        scratch_shapes=[pltpu.VMEM((2, npair, SSD_STATE, 2 * SSD_P), F32)],
        compiler_params=_cparams("arbitrary"),
        name="ssd_scan",
    )(xs, xs, bm, bm, cm, cm, dt2, dt2, a_pad, tri, expand, h0)
    return ((yf, yb) if need_y else None), hfin


def _ssd_out_kernel(x_ref, yf_ref, yb_ref, xs_ref, zg_ref, dsk_ref, ng_ref, w_ref, gate_ref, o_ref):
    y = yf_ref[...].astype(F32) + yb_ref[...].astype(F32) + xs_ref[...].astype(F32) * dsk_ref[...]
    y = y * _silu(zg_ref[...].astype(F32))
    gw = SSD_INNER // SSD_GROUPS
    parts = []
    for g in range(SSD_GROUPS):
        yg = y[:, g * gw:(g + 1) * gw]
        ms = jnp.mean(yg * yg, axis=-1, keepdims=True)
        parts.append(yg * lax.rsqrt(ms + NORM_EPS) * ng_ref[:, g * gw:(g + 1) * gw])
    yn = jnp.concatenate(parts, axis=1).astype(BF16)
    o_ref[...] = x_ref[...] + gate_ref[...] * jnp.dot(yn, w_ref[...], preferred_element_type=F32)


def _ssd_out(x, yfb, xs, zg, d_skip, norm_g, w_out, gate):
    lx = x.shape[0]
    tm = min(256, lx)
    rowblk = lambda w: pl.BlockSpec((tm, w), lambda i: (i, 0))
    vecw = pl.BlockSpec((1, SSD_INNER), lambda i: (0, 0))
    return pl.pallas_call(
        _ssd_out_kernel,
        out_shape=jax.ShapeDtypeStruct((lx, D), F32),
        grid=(lx // tm,),
        in_specs=[rowblk(D), rowblk(SSD_INNER), rowblk(SSD_INNER), rowblk(SSD_INNER),
                  rowblk(SSD_INNER), vecw, vecw, _resident((SSD_INNER, D)),
                  pl.BlockSpec((1, D), lambda i: (0, 0))],
        out_specs=rowblk(D),
        compiler_params=_cparams("parallel"),
        name="ssd_out",
    )(x, yfb[0], yfb[1], xs, zg, _row(jnp.repeat(d_skip, SSD_P)), _row(norm_g), w_out, _row(gate))


def kernel(x, c, ctx, c_ctx, norm1_g, norm2_g, mod_w, mod_b, ffn_w_in, ffn_w_out, final_g, gm_w_in, gm_ln_g, gm_ln_b, gm_ws, gm_bs, gm_w_out, at_w_qkv, at_q_g, at_k_g, at_w_out, hy_w_in, hy_conv_w, hy_conv_b, hy_filt_w1, hy_filt_b1, hy_filt_w2, hy_filt_b2, hy_filt_w3, hy_filt_freq, hy_skip, hy_w_out, ssd_w_in, ssd_conv_w, ssd_conv_b, ssd_a_log, ssd_dt_bias, ssd_d_skip, ssd_norm_g, ssd_w_out):
    batch, seq, _ = x.shape
    assert batch == 1, "kernels are written for a single sequence"
    nctx = ctx.shape[1]
    xl = x[0]
    z = ctx[0]
    mods = _modulation(c[0], c_ctx, mod_w, mod_b)
    bf = lambda w: w.astype(BF16)

    for i in range(DEPTH):
        m, j = i % 4, i // 4
        want_ctx = i < DEPTH - 1
        ml = [mods[i, 0, k * D:(k + 1) * D] for k in range(6)]
        mc = [mods[i, 1, k * D:(k + 1) * D] for k in range(6)]
        n1 = norm1_g[i]
        if m == 0:
            p = (bf(gm_w_in[j]), gm_ln_g[j], gm_ln_b[j], bf(gm_ws[j]), gm_bs[j], bf(gm_w_out[j]))
            xl = _gmlp(xl, n1, ml[0], ml[1], ml[2], *p)
            if want_ctx:
                z = _gmlp(z, n1, mc[0], mc[1], mc[2], *p)
        elif m == 1:
            wq, wo = bf(at_w_qkv[j]), bf(at_w_out[j])
            qt_l, k_l, vt_l = _qkv(xl, n1, ml[0], ml[1], wq, at_q_g[j], at_k_g[j], rope=True)
            qt_c, k_c, vt_c = _qkv(z, n1, mc[0], mc[1], wq, at_q_g[j], at_k_g[j], rope=False)
            k_all = jnp.concatenate([k_c, k_l], axis=1)
            vt_all = jnp.concatenate([vt_c, vt_l], axis=1)
            stot = nctx + seq
            ts = next(t for t in (3328, 1280, 1024, 512, 256) if stot % t == 0)
            score_bound = (HD ** 0.5 * LOG2E) * jnp.max(jnp.abs(at_q_g[j])) * jnp.max(jnp.abs(at_k_g[j]))
            o_l = lax.cond(score_bound <= FLASH_SCORE_BOUND,
                           lambda: _flash(qt_l, k_all, vt_all, stot, ts, bounded=True),
                           lambda: _flash(qt_l, k_all, vt_all, stot, ts, bounded=False))
            xl = _outproj(xl, o_l, wo, ml[2])
            if want_ctx:
                o_c = _flash(qt_c, k_all, vt_all, nctx, nctx)
                z = _outproj(z, o_c, wo, mc[2])
        elif m == 2:
            p = (bf(hy_w_in[j]), hy_conv_w[j], hy_conv_b[j], hy_filt_w1[j], hy_filt_b1[j], hy_filt_w2[j],
                 hy_filt_b2[j], hy_filt_w3[j], hy_filt_freq[j], hy_skip[j], bf(hy_w_out[j]))
            xl = _hyena(xl, n1, ml[0], ml[1], ml[2], *p)
            if want_ctx:
                z = _hyena(z, n1, mc[0], mc[1], mc[2], *p)
        else:
            win, wo = bf(ssd_w_in[j]), bf(ssd_w_out[j])
            pin = (win, ssd_conv_w[j], ssd_conv_b[j], ssd_dt_bias[j])
            zg_c, xs_c, bm_c, cm_c, dt_c = _ssd_in(z, n1, mc[0], mc[1], *pin)
            zg_l, xs_l, bm_l, cm_l, dt_l = _ssd_in(xl, n1, ml[0], ml[1], *pin)
            h0 = jnp.zeros((2, SSD_HEADS // 2, SSD_STATE, 2 * SSD_P), F32)
            y_c, h_ctx = _ssd_scan(xs_c, bm_c, cm_c, dt_c, ssd_a_log[j], h0, want_ctx)
            y_l, _ = _ssd_scan(xs_l, bm_l, cm_l, dt_l, ssd_a_log[j], h_ctx, True)
            xl = _ssd_out(xl, y_l, xs_l, zg_l, ssd_d_skip[j], ssd_norm_g[j], wo, ml[2])
            if want_ctx:
                z = _ssd_out(z, y_c, xs_c, zg_c, ssd_d_skip[j], ssd_norm_g[j], wo, mc[2])
        wi, wo2 = bf(ffn_w_in[i]), bf(ffn_w_out[i])
        xl = _ffn(xl, norm2_g[i], ml[3], ml[4], ml[5], wi, wo2, final_g, final=(i == DEPTH - 1))
        if want_ctx:
            z = _ffn(z, norm2_g[i], mc[3], mc[4], mc[5], wi, wo2, final_g, final=False)
    return xl[None]
```

```python
import functools
import math

import numpy as np
import jax
import jax.numpy as jnp
from jax import lax
from jax.experimental import pallas as pl
from jax.experimental.pallas import tpu as pltpu

F32 = jnp.float32
BF16 = jnp.bfloat16
HIGHEST = lax.Precision.HIGHEST

D = 1024
DEPTH = 4
GRID_W = 64
NORM_EPS = 1e-6
FFN_HIDDEN = 2816
GM_CHUNK = 128
GM_WIDTH = 2 * D
GM_GROUPS = 8
GM_GW = GM_WIDTH // GM_GROUPS
HD = 64
QH = D // HD
KVH = 4
ROPE_THETA = 10000.0
LOG2E = math.log2(math.e)
FLASH_SCORE_BOUND = 30.0
FLASH_LOOKAHEAD = 2
HY_BANDS = 16
HY_EMB = 1 + 2 * HY_BANDS
HY_FILT_W = 64
HY_MAX_DECAY = math.log(1e-2) / 0.3
HY_MIN_DECAY = math.log(1e-2) / 1.5
SSD_INNER = 2 * D
SSD_P = 64
SSD_HEADS = SSD_INNER // SSD_P
SSD_GROUPS = 4
SSD_STATE = 128
SSD_CHUNK = 128
SSD_BC = SSD_GROUPS * SSD_STATE
SSD_CONV_DIM = SSD_INNER + 2 * SSD_BC

LANES = 128
SUBLANES = 8
VMEM_LIMIT_BYTES = 56 * 1024 * 1024
DFT_N2 = 128
DFT_SPLIT = 1


def _cparams(*sem):
    return pltpu.CompilerParams(dimension_semantics=sem, vmem_limit_bytes=VMEM_LIMIT_BYTES)


def _row(v):
    return v.reshape(1, -1)


def _normmod(x, g, shift, scale):
    ms = jnp.mean(x * x, axis=-1, keepdims=True)
    return x * lax.rsqrt(ms + NORM_EPS) * g * (1.0 + scale) + shift


def _silu(x):
    return x * jax.nn.sigmoid(x)


def _mod_kernel(cl_ref, cc_ref, w_ref, b_ref, o_ref):
    w = w_ref[0]
    for r, c_ref in enumerate((cl_ref, cc_ref)):
        a = _silu(c_ref[...])
        o_ref[0, r:r + 1, :] = jnp.sum(a * w, axis=0, keepdims=True) + b_ref[0]


def _modulation(c, c_ctx, mod_w, mod_b):
    tn = 1536
    n6 = 6 * D
    depth = mod_w.shape[0]
    return pl.pallas_call(
        _mod_kernel,
        out_shape=jax.ShapeDtypeStruct((depth, 2, n6), F32),
        grid=(depth, n6 // tn),
        in_specs=[
            pl.BlockSpec((D, 1), lambda i, n: (0, 0)),
            pl.BlockSpec((D, 1), lambda i, n: (0, 0)),
            pl.BlockSpec((1, D, tn), lambda i, n: (i, 0, n)),
            pl.BlockSpec((1, 1, tn), lambda i, n: (i, 0, n)),
        ],
        out_specs=pl.BlockSpec((1, 2, tn), lambda i, n: (i, 0, n)),
        compiler_params=_cparams("parallel", "parallel"),
        name="modulation",
    )(c.reshape(D, 1), c_ctx.reshape(D, 1), mod_w, mod_b.reshape(depth, 1, n6))


def _ffn_kernel(x_ref, *refs, mode, final):
    x = x_ref[...]
    if mode == "proj":
        a_ref, wm_ref, g1_ref = refs[:3]
        refs = refs[3:]
        x = x + g1_ref[...] * jnp.dot(a_ref[...], wm_ref[...], preferred_element_type=F32)
    elif mode == "hyena":
        x0_ref, yt_ref, wm_ref, g1_ref = refs[:4]
        refs = refs[4:]
        a = (x0_ref[...] * yt_ref[...].T).astype(BF16)
        x = x + g1_ref[...] * jnp.dot(a, wm_ref[...], preferred_element_type=F32)
    g_ref, sh_ref, sc_ref, gate_ref, wi_ref, wo_ref, fg_ref, o_ref = refs
    h = _normmod(x, g_ref[...], sh_ref[...], sc_ref[...]).astype(BF16)
    a = jnp.dot(h, wi_ref[:, :FFN_HIDDEN], preferred_element_type=F32)
    u = jnp.dot(h, wi_ref[:, FFN_HIDDEN:], preferred_element_type=F32)
    act = (_silu(a) * u).astype(BF16)
    y = x + gate_ref[...] * jnp.dot(act, wo_ref[...], preferred_element_type=F32)
    if final:
        ms = jnp.mean(y * y, axis=-1, keepdims=True)
        y = y * lax.rsqrt(ms + NORM_EPS) * fg_ref[...]
    o_ref[...] = y


def _resident(shape):
    return pl.BlockSpec(shape, lambda *_: (0,) * len(shape), pipeline_mode=pl.Buffered(1))


def _ffn(x, pending, g, sh, sc, gate, w_in, w_out, final_g, final):
    lx = x.shape[0]
    tm = min(512, lx)
    vec = pl.BlockSpec((1, D), lambda i: (0, 0))
    rows = pl.BlockSpec((tm, D), lambda i: (i, 0))
    mode, pre_args, pre_specs = "none", (), []
    if pending is not None:
        mode = pending[0]
        if mode == "proj":
            _, a, wm, g1 = pending
            pre_args, pre_specs = (a, wm, _row(g1)), [rows, _resident(wm.shape), vec]
        else:
            _, x0, yt, wm, g1 = pending
            pre_args = (x0, yt, wm, _row(g1))
            pre_specs = [rows, pl.BlockSpec((D, tm), lambda i: (0, i)), _resident(wm.shape), vec]
    return pl.pallas_call(
        functools.partial(_ffn_kernel, mode=mode, final=final),
        out_shape=jax.ShapeDtypeStruct((lx, D), F32),
        grid=(lx // tm,),
        in_specs=[rows] + pre_specs + [
            vec, vec, vec, vec,
            _resident((D, 2 * FFN_HIDDEN)),
            _resident((FFN_HIDDEN, D)),
            vec,
        ],
        out_specs=rows,
        compiler_params=_cparams("parallel"),
        name="ffn",
    )(x, *pre_args, _row(g), _row(sh), _row(sc), _row(gate), w_in, w_out, _row(final_g))


def _gmlp_kernel(x_ref, g_ref, sh_ref, sc_ref, gate_ref, win_ref, lng_ref, lnb_ref, ws_ref, bs_ref, wout_ref,
                 o_ref, *, tm):
    nsub = max(1, tm // (2 * GM_CHUNK))
    rs = tm // nsub

    def project(i):
        x = x_ref[i * rs:(i + 1) * rs, :]
        h = _normmod(x, g_ref[...], sh_ref[...], sc_ref[...]).astype(BF16)
        return jnp.dot(h, win_ref[...], preferred_element_type=F32)

    def mix(i, t):
        t = 0.5 * t * (1.0 + lax.erf(t * (1.0 / math.sqrt(2.0))))
        u = t[:, :GM_WIDTH]
        v = t[:, GM_WIDTH:]
        mu = jnp.mean(v, axis=-1, keepdims=True)
        vc = v - mu
        var = jnp.mean(vc * vc, axis=-1, keepdims=True)
        v = (vc * lax.rsqrt(var + NORM_EPS) * lng_ref[...] + lnb_ref[...]).astype(BF16)
        rows = []
        for q in range(rs // GM_CHUNK):
            cols = []
            for gidx in range(GM_GROUPS):
                vq = v[q * GM_CHUNK:(q + 1) * GM_CHUNK, gidx * GM_GW:(gidx + 1) * GM_GW]
                bias = bs_ref[gidx]
                m = jnp.dot(ws_ref[gidx], vq, preferred_element_type=F32)
                cols.append(m + jnp.concatenate([bias] * (GM_GW // LANES), axis=1))
            rows.append(jnp.concatenate(cols, axis=1))
        gated = (u * jnp.concatenate(rows, axis=0)).astype(BF16)
        y = jnp.dot(gated, wout_ref[...], preferred_element_type=F32)
        o_ref[i * rs:(i + 1) * rs, :] = x_ref[i * rs:(i + 1) * rs, :] + gate_ref[...] * y

    t_next = project(0)
    for i in range(nsub):
        t = t_next
        if i + 1 < nsub:
            t_next = project(i + 1)
        mix(i, t)


def _gmlp(x, g, sh, sc, gate, w_in, ln_g, ln_b, ws, bs, w_out):
    lx = x.shape[0]
    tm = min(512, lx)
    vec = pl.BlockSpec((1, D), lambda i: (0, 0))
    vecw = pl.BlockSpec((1, GM_WIDTH), lambda i: (0, 0))
    bsb = jnp.broadcast_to(bs[:, :, None], (GM_GROUPS, GM_CHUNK, LANES))
    return pl.pallas_call(
        functools.partial(_gmlp_kernel, tm=tm),
        out_shape=jax.ShapeDtypeStruct((lx, D), F32),
        grid=(lx // tm,),
        in_specs=[
            pl.BlockSpec((tm, D), lambda i: (i, 0)),
            vec, vec, vec, vec,
            _resident((D, 2 * GM_WIDTH)),
            vecw, vecw,
            pl.BlockSpec((GM_GROUPS, GM_CHUNK, GM_CHUNK), lambda i: (0, 0, 0)),
            pl.BlockSpec((GM_GROUPS, GM_CHUNK, LANES), lambda i: (0, 0, 0)),
            _resident((GM_WIDTH, D)),
        ],
        out_specs=pl.BlockSpec((tm, D), lambda i: (i, 0)),
        compiler_params=_cparams("parallel"),
        name="gmlp",
    )(x, _row(g), _row(sh), _row(sc), _row(gate), w_in, _row(ln_g), _row(ln_b), ws, bsb, w_out)


def _group_sumsq(t, e_ref):
    sq = t * t
    hi = sq.astype(BF16)
    lo = (sq - hi.astype(F32)).astype(BF16)
    outs = []
    for j in range(t.shape[1] // LANES):
        sl = slice(j * LANES, (j + 1) * LANES)
        outs.append(jnp.dot(hi[:, sl], e_ref[...], preferred_element_type=F32)
                    + jnp.dot(lo[:, sl], e_ref[...], preferred_element_type=F32))
    return jnp.concatenate(outs, axis=1)


def _rope(t, cosf, sinf):
    w = t.shape[1]
    lane = lax.broadcasted_iota(jnp.int32, t.shape, 1)
    first = (lane % HD) < (HD // 2)
    partner = jnp.where(first, pltpu.roll(t, w - HD // 2, axis=1), pltpu.roll(t, HD // 2, axis=1))
    reps = w // LANES
    c = jnp.concatenate([cosf] * reps, axis=1)
    s = jnp.concatenate([sinf] * reps, axis=1)
    return t * c + partner * s


def _qkv_kernel(x_ref, g_ref, sh_ref, sc_ref, w_ref, qg_ref, kg_ref, e_ref, cos_ref, sin_ref,
                qt_ref, k_ref, vt_ref, *, rope):
    h = _normmod(x_ref[...], g_ref[...], sh_ref[...], sc_ref[...]).astype(BF16)
    qkv = jnp.dot(h, w_ref[...], preferred_element_type=F32)
    q = qkv[:, :D]
    k = qkv[:, D:D + KVH * HD]
    v = qkv[:, D + KVH * HD:]
    q = q * lax.rsqrt(_group_sumsq(q, e_ref) * (1.0 / HD) + NORM_EPS) * qg_ref[...]
    k = k * lax.rsqrt(_group_sumsq(k, e_ref) * (1.0 / HD) + NORM_EPS) * kg_ref[...]
    if rope:
        q = _rope(q, cos_ref[...], sin_ref[...])
        k = _rope(k, cos_ref[...], sin_ref[...])
    qt_ref[...] = (q * (HD ** -0.5 * LOG2E)).T.astype(BF16)
    for gidx in range(KVH):
        k_ref[gidx] = k[:, gidx * HD:(gidx + 1) * HD].astype(BF16)
    vt_ref[...] = v.T.astype(BF16)


def _qkv(x, g, sh, sc, w_qkv, q_g, k_g, rope):
    lx = x.shape[0]
    tm = min(512, lx)
    vec = pl.BlockSpec((1, D), lambda i: (0, 0))
    kvw = KVH * HD
    rows = lx // GRID_W
    row = jnp.repeat(jnp.arange(rows, dtype=F32), GRID_W)
    col = jnp.tile(jnp.arange(GRID_W, dtype=F32), rows)
    n = HD // 4
    inv = ROPE_THETA ** (-jnp.arange(n, dtype=F32) / n)
    ang = jnp.concatenate([row[:, None] * inv, col[:, None] * inv], axis=-1)
    cos, sin = jnp.cos(ang), jnp.sin(ang)
    cosf = jnp.tile(jnp.concatenate([cos, cos], axis=-1), (1, LANES // HD))
    sinf = jnp.tile(jnp.concatenate([-sin, sin], axis=-1), (1, LANES // HD))
    eblk = jnp.asarray(np.kron(np.eye(LANES // HD), np.ones((HD, HD))), BF16)
    tab = pl.BlockSpec((tm, LANES), lambda i: (i, 0))
    return pl.pallas_call(
        functools.partial(_qkv_kernel, rope=rope),
        out_shape=(jax.ShapeDtypeStruct((D, lx), BF16),
                   jax.ShapeDtypeStruct((KVH, lx, HD), BF16),
                   jax.ShapeDtypeStruct((kvw, lx), BF16)),
        grid=(lx // tm,),
        in_specs=[
            pl.BlockSpec((tm, D), lambda i: (i, 0)),
            vec, vec, vec,
            _resident((D, D + 2 * kvw)),
            vec,
            pl.BlockSpec((1, kvw), lambda i: (0, 0)),
            pl.BlockSpec((LANES, LANES), lambda i: (0, 0)),
            tab, tab,
        ],
        out_specs=(pl.BlockSpec((D, tm), lambda i: (0, i)),
                   pl.BlockSpec((KVH, tm, HD), lambda i: (0, i, 0)),
                   pl.BlockSpec((kvw, tm), lambda i: (0, i))),
        compiler_params=_cparams("parallel"),
        name="qkv_proj",
    )(x, _row(g), _row(sh), _row(sc), w_qkv, _row(jnp.tile(q_g, QH)), _row(jnp.tile(k_g, KVH)), eblk, cosf, sinf)


def _flash_kernel(qt_ref, k_ref, vt_ref, o_ref, qg_scr, m_scr, l_scr, acc_scr, *, tq, ts, tc, nkv, bounded):
    j = pl.program_id(1)
    gq = QH // KVH
    mcols = gq * tq

    @pl.when(j == 0)
    def _():
        for h in range(QH):
            qg_scr[h // gq, :, (h % gq) * tq:(h % gq + 1) * tq] = qt_ref[h * HD:(h + 1) * HD, :]
        m_scr[...] = jnp.full(m_scr.shape, -jnp.inf, F32)
        l_scr[...] = jnp.zeros_like(l_scr)
        acc_scr[...] = jnp.zeros_like(acc_scr)

    stages = [(g, c) for c in range(ts // tc) for g in range(KVH)]

    def scores(g, c):
        return jnp.dot(k_ref[g, c * tc:(c + 1) * tc, :], qg_scr[g], preferred_element_type=F32)

    pending = [scores(*st) for st in stages[:FLASH_LOOKAHEAD]]
    for idx, (g, c) in enumerate(stages):
        s = pending.pop(0)
        if idx + FLASH_LOOKAHEAD < len(stages):
            pending.append(scores(*stages[idx + FLASH_LOOKAHEAD]))
        vt = vt_ref[g * HD:(g + 1) * HD, c * tc:(c + 1) * tc]
        if bounded:
            p = jnp.exp2(s)
            l_scr[g] += jnp.sum(p, axis=0, keepdims=True)
            acc_scr[g] += jnp.dot(vt, p.astype(BF16), preferred_element_type=F32)
        else:
            m_prev = m_scr[g]
            m_new = jnp.maximum(m_prev, jnp.max(s, axis=0, keepdims=True))
            alpha = jnp.exp2(m_prev - m_new)
            p = jnp.exp2(s - m_new)
            l_scr[g] = alpha * l_scr[g] + jnp.sum(p, axis=0, keepdims=True)
            acc_scr[g] = alpha * acc_scr[g] + jnp.dot(vt, p.astype(BF16), preferred_element_type=F32)
            m_scr[g] = m_new

    @pl.when(j == nkv - 1)
    def _():
        rows = []
        for g in range(KVH):
            o = acc_scr[g] / l_scr[g]
            rows += [o[:, r * tq:(r + 1) * tq] for r in range(gq)]
        o_ref[...] = jnp.concatenate(rows, axis=0).T.astype(o_ref.dtype)


def _flash(qt, k, vt, s_len, ts, bounded=False):
    lq = qt.shape[1]
    tq = min(128, lq)
    nkv = s_len // ts
    gq = QH // KVH
    kvw = KVH * HD
    tc = 2 * LANES if ts % (2 * LANES) == 0 else LANES
    return pl.pallas_call(
        functools.partial(_flash_kernel, tq=tq, ts=ts, tc=tc, nkv=nkv, bounded=bounded),
        out_shape=jax.ShapeDtypeStruct((lq, D), BF16),
        grid=(lq // tq, nkv),
        in_specs=[
            pl.BlockSpec((D, tq), lambda i, j: (0, i)),
            pl.BlockSpec((KVH, ts, HD), lambda i, j: (0, j, 0)),
            pl.BlockSpec((kvw, ts), lambda i, j: (0, j)),
        ],
        out_specs=pl.BlockSpec((tq, D), lambda i, j: (i, 0)),
        scratch_shapes=[
            pltpu.VMEM((KVH, HD, gq * tq), BF16),
            pltpu.VMEM((KVH, 1, gq * tq), F32),
            pltpu.VMEM((KVH, 1, gq * tq), F32),
            pltpu.VMEM((KVH, HD, gq * tq), F32),
        ],
        compiler_params=_cparams("parallel", "arbitrary"),
        name="flash_attn",
    )(qt, k, vt)


def _halo_specs(tm, lx):
    nb = lx // SUBLANES
    step = tm // SUBLANES
    prev = pl.BlockSpec((SUBLANES, D), lambda i: (jnp.maximum(i * step - 1, 0), 0))
    nxt = pl.BlockSpec((SUBLANES, D), lambda i: (jnp.minimum((i + 1) * step, nb - 1), 0))
    return prev, nxt


def _conv3(p_main, p_halo, cw, cb, first, last):
    tm = p_main.shape[0]
    rid = lax.broadcasted_iota(jnp.int32, p_main.shape, 0)
    before = jnp.where(first, 0.0, p_halo[SUBLANES - 1:SUBLANES, :])
    after = jnp.where(last, 0.0, p_halo[SUBLANES:SUBLANES + 1, :])
    up = jnp.where(rid == 0, before, pltpu.roll(p_main, 1, axis=0))
    dn = jnp.where(rid == tm - 1, after, pltpu.roll(p_main, tm - 1, axis=0))
    return cw[0:1, :] * up + cw[1:2, :] * p_main + cw[2:3, :] * dn + cb


def _norm_halo(xm_ref, xp_ref, xn_ref, g_ref, sh_ref, sc_ref):
    g, sh, sc = g_ref[...], sh_ref[...], sc_ref[...]
    h = _normmod(xm_ref[...], g, sh, sc).astype(BF16)
    hh = jnp.concatenate([_normmod(xp_ref[...], g, sh, sc), _normmod(xn_ref[...], g, sh, sc)], axis=0).astype(BF16)
    return h, hh


def _hy_in_kernel(xm_ref, xp_ref, xn_ref, g_ref, sh_ref, sc_ref, w_ref, cw_ref, cb_ref, x0_ref, ut_ref, *, nt):
    i = pl.program_id(0)
    first, last = i == 0, i == nt - 1
    h, hh = _norm_halo(xm_ref, xp_ref, xn_ref, g_ref, sh_ref, sc_ref)

    def project(b):
        sl = slice(b * D, (b + 1) * D)
        return (jnp.dot(h, w_ref[:, sl], preferred_element_type=F32),
                jnp.dot(hh, w_ref[:, sl], preferred_element_type=F32))

    def conv(b, p):
        sl = slice(b * D, (b + 1) * D)
        return _conv3(p[0], p[1], cw_ref[:, sl], cb_ref[:, sl], first, last)

    p0 = project(0)
    p1 = project(1)
    x0_ref[...] = conv(0, p0)
    p2 = project(2)
    x1 = conv(1, p1)
    ut_ref[...] = (x1 * conv(2, p2)).T


def _hy_in(x, g, sh, sc, w_in, conv_w, conv_b):
    lx = x.shape[0]
    tm = min(512, lx)
    nt = lx // tm
    vec = pl.BlockSpec((1, D), lambda i: (0, 0))
    prev, nxt = _halo_specs(tm, lx)
    return pl.pallas_call(
        functools.partial(_hy_in_kernel, nt=nt),
        out_shape=(jax.ShapeDtypeStruct((lx, D), F32), jax.ShapeDtypeStruct((D, lx), F32)),
        grid=(nt,),
        in_specs=[
            pl.BlockSpec((tm, D), lambda i: (i, 0)), prev, nxt,
            vec, vec, vec,
            _resident((D, 3 * D)),
            pl.BlockSpec((3, 3 * D), lambda i: (0, 0)),
            pl.BlockSpec((1, 3 * D), lambda i: (0, 0)),
        ],
        out_specs=(pl.BlockSpec((tm, D), lambda i: (i, 0)), pl.BlockSpec((D, tm), lambda i: (0, i))),
        compiler_params=_cparams("parallel"),
        name="hyena_in",
    )(x, x, x, _row(g), _row(sh), _row(sc), w_in, conv_w, _row(conv_b))


def _hy_filter_kernel(ft_ref, t_ref, w1_ref, b1_ref, w2_ref, b2_ref, w3h_ref, w3l_ref, fr_ref, dl_ref, sk_ref, kt_ref,
                      *, tm, ltrue, lpad):
    i = pl.program_id(0)
    feats = ft_ref[...]
    fr = fr_ref[...]
    hid = jnp.sin(fr * (jnp.dot(w1_ref[...], feats, preferred_element_type=F32, precision=HIGHEST) + b1_ref[...]))
    hid = jnp.sin(fr * (jnp.dot(w2_ref[...], hid, preferred_element_type=F32, precision=HIGHEST) + b2_ref[...]))
    hh = hid.astype(BF16)
    hl = (hid - hh.astype(F32)).astype(BF16)
    w3h = w3h_ref[...]
    kt = (jnp.dot(w3h, hh, preferred_element_type=F32) + jnp.dot(w3h, hl, preferred_element_type=F32)
          + jnp.dot(w3l_ref[...], hh, preferred_element_type=F32))
    kt = kt * jnp.exp(-dl_ref[...] * t_ref[...])
    if lpad != ltrue:
        pos = lax.broadcasted_iota(jnp.int32, kt.shape, 1) + i * tm
        kt = jnp.where(pos < ltrue, kt, 0.0)
    kt_ref[...] = kt

    @pl.when(i == 0)
    def _():
        row = lax.broadcasted_iota(jnp.int32, (2 * D, 1), 0)
        kt_ref[:, 0:1] = jnp.where(row < D, kt[:, 0:1] + sk_ref[...], 0.0)


def _hy_filter(ltrue, lpad, w1, b1, w2, b2, w3, freq, skip):
    tm = min(512, lpad)
    col = lambda v_: v_.reshape(-1, 1)
    w1t = jnp.pad(w1.T, ((0, 0), (0, LANES - HY_EMB)))
    deltas = jnp.abs(jnp.linspace(HY_MIN_DECAY, HY_MAX_DECAY, D, dtype=F32))
    pos = jnp.arange(lpad, dtype=F32)
    zf = jnp.linspace(1e-4, HY_BANDS - 1, HY_BANDS, dtype=F32)[:, None] * (2.0 * math.pi * pos / ltrue)[None, :]
    feats = jnp.concatenate([(pos / (ltrue - 1))[None, :], jnp.cos(zf), -jnp.sin(zf),
                             jnp.zeros((LANES - HY_EMB, lpad), F32)], axis=0)
    w3t = w3.T
    w3h = w3t.astype(BF16)
    w3l = (w3t - w3h.astype(F32)).astype(BF16)
    full = lambda a: pl.BlockSpec(a.shape, lambda i: (0,) * a.ndim)
    args = (w1t, col(b1), w2.T, col(b2), w3h, w3l, col(freq), col(jnp.tile(deltas, 2)),
            col(jnp.concatenate([skip, jnp.zeros((D,), F32)])))
    return pl.pallas_call(
        functools.partial(_hy_filter_kernel, tm=tm, ltrue=ltrue, lpad=lpad),
        out_shape=jax.ShapeDtypeStruct((2 * D, lpad), F32),
        grid=(lpad // tm,),
        in_specs=[pl.BlockSpec((LANES, tm), lambda i: (0, i)), pl.BlockSpec((1, tm), lambda i: (0, i))]
                 + [full(a) for a in args],
        out_specs=pl.BlockSpec((2 * D, tm), lambda i: (0, i)),
        compiler_params=_cparams("parallel"),
        name="hyena_filter",
    )(feats, feats[0:1, :], *args)


def _dft_consts(nh):
    n1 = 2 * nh
    n = n1 * DFT_N2
    k1 = np.arange(n1)[:, None].astype(np.float64)
    a1 = 2.0 * np.pi * k1 * np.arange(nh)[None, :] / n1
    f1 = np.concatenate([np.cos(a1), -np.sin(a1)], axis=0)
    at = 2.0 * np.pi * ((np.arange(n1)[:, None] * np.arange(DFT_N2)[None, :]) % n) / n
    a2 = 2.0 * np.pi * ((np.arange(DFT_N2)[:, None] * np.arange(DFT_N2)[None, :]) % DFT_N2) / DFT_N2
    c2, s2 = np.cos(a2), np.sin(a2)
    f2 = np.block([[c2, -s2], [s2, c2]])
    g2 = np.block([[c2, s2], [-s2, c2]])
    g1 = np.concatenate([np.cos(a1).T, -np.sin(a1).T], axis=1) / n
    as32 = lambda a: jnp.asarray(a, F32)

    def parts(a):
        hi = a.astype(BF16)
        if DFT_SPLIT == 1:
            return (jnp.asarray(hi),)
        return (jnp.asarray(hi), jnp.asarray((a - hi.astype(np.float64)).astype(BF16)))

    return parts(f1), as32(np.cos(at)), as32(np.sin(at)), parts(f2), parts(g2), parts(g1)


def _split_bf16(a):
    hi = a.astype(BF16)
    if DFT_SPLIT == 1:
        return (hi,)
    return (hi, (a - hi.astype(F32)).astype(BF16))


def _split_dot(a, b):
    out = jnp.dot(a[0], b[0], preferred_element_type=F32)
    if DFT_SPLIT > 1:
        out = out + jnp.dot(a[1], b[0], preferred_element_type=F32) + jnp.dot(a[0], b[1], preferred_element_type=F32)
    return out


def _load_parts(refs):
    return tuple(r[...] for r in refs)


def _hy_conv_kernel(x_ref, kf_ref, kb_ref, *refs, cb, n1):
    ns = DFT_SPLIT
    f1, (twc_ref, tws_ref), f2 = refs[:ns], refs[ns:ns + 2], refs[ns + 2:2 * ns + 2]
    g2, g1, o_ref = refs[2 * ns + 2:3 * ns + 2], refs[3 * ns + 2:4 * ns + 2], refs[4 * ns + 2]
    twc, tws = twc_ref[...], tws_ref[...]
    f1, f2, g2, g1 = _load_parts(f1), _load_parts(f2), _load_parts(g2), _load_parts(g1)
    hc = cb // 2
    nt = 3 * hc

    def dft1(half):
        xs = [r[half * hc + c] for r in (x_ref, kf_ref, kb_ref) for c in range(hc)]
        return _split_dot(f1, _split_bf16(jnp.concatenate(xs, axis=1)))

    def twiddle_rows(a):
        rows = []
        for t in range(nt):
            ar = a[:n1, t * DFT_N2:(t + 1) * DFT_N2]
            ai = a[n1:, t * DFT_N2:(t + 1) * DFT_N2]
            rows.append(jnp.concatenate([ar * twc + ai * tws, ai * twc - ar * tws], axis=1))
        return jnp.concatenate(rows, axis=0)

    def dft2(rows):
        return _split_dot(_split_bf16(rows), f2)

    def product(spec_all):
        rows = hc * n1
        spec, hf, hb = spec_all[:rows], spec_all[rows:2 * rows], spec_all[2 * rows:]
        hr = hf[:, :DFT_N2] + hb[:, :DFT_N2]
        hi = hf[:, DFT_N2:] - hb[:, DFT_N2:]
        xr, xi = spec[:, :DFT_N2], spec[:, DFT_N2:]
        return jnp.concatenate([xr * hr - xi * hi, xr * hi + xi * hr], axis=1)

    def idft2(y):
        return _split_dot(_split_bf16(y), g2)

    def twiddle_cols(b):
        cols = []
        for c in range(hc):
            br = b[c * n1:(c + 1) * n1, :DFT_N2]
            bi = b[c * n1:(c + 1) * n1, DFT_N2:]
            cols.append(jnp.concatenate([br * twc - bi * tws, bi * twc + br * tws], axis=0))
        return jnp.concatenate(cols, axis=1)

    def idft1(cols):
        return _split_dot(g1, _split_bf16(cols))

    def store(half, out):
        for c in range(hc):
            o_ref[half * hc + c] = out[:, c * DFT_N2:(c + 1) * DFT_N2]

    a0 = dft1(0)
    a1 = dft1(1)
    s0 = dft2(twiddle_rows(a0))
    s1 = dft2(twiddle_rows(a1))
    b0 = idft2(product(s0))
    b1 = idft2(product(s1))
    o0 = idft1(twiddle_cols(b0))
    o1 = idft1(twiddle_cols(b1))
    store(0, o0)
    store(1, o1)


def _hy_longconv(ut, kt, lpad):
    nh = lpad // DFT_N2
    n1 = 2 * nh
    cb = max(8, min(64, 2048 // n1))
    f1, twc, tws, f2, g2, g1 = _dft_consts(nh)
    consts = f1 + (twc, tws) + f2 + g2 + g1
    full = lambda a: pl.BlockSpec(a.shape, lambda i: (0,) * a.ndim)
    k3 = kt.reshape(2 * D, nh, DFT_N2)
    u3 = ut.reshape(D, nh, DFT_N2)
    nb = D // cb
    y3 = pl.pallas_call(
        functools.partial(_hy_conv_kernel, cb=cb, n1=n1),
        out_shape=jax.ShapeDtypeStruct((D, nh, DFT_N2), F32),
        grid=(nb,),
        in_specs=[pl.BlockSpec((cb, nh, DFT_N2), lambda i: (i, 0, 0)),
                  pl.BlockSpec((cb, nh, DFT_N2), lambda i: (i, 0, 0)),
                  pl.BlockSpec((cb, nh, DFT_N2), lambda i: (i + nb, 0, 0))]
                 + [full(a) for a in consts],
        out_specs=pl.BlockSpec((cb, nh, DFT_N2), lambda i: (i, 0, 0)),
        compiler_params=_cparams("parallel"),
        name="hyena_longconv",
    )(u3, k3, k3, *consts)
    return y3.reshape(D, lpad)


def _hyena(x, g, sh, sc, gate, w_in, conv_w, conv_b, w1, b1, w2, b2, w3, freq, skip, w_out):
    lx = x.shape[0]
    lpad = max(lx, SUBLANES * DFT_N2)
    x0, ut = _hy_in(x, g, sh, sc, w_in, conv_w, conv_b)
    if lpad != lx:
        ut = jnp.pad(ut, ((0, 0), (0, lpad - lx)))
    kt = _hy_filter(lx, lpad, w1, b1, w2, b2, w3, freq, skip)
    yt = _hy_longconv(ut, kt, lpad)[:, :lx]
    return ("hyena", x0, yt, w_out, gate)


def _ssd_in_kernel(xm_ref, xp_ref, xn_ref, g_ref, sh_ref, sc_ref, wz_ref, wx_ref, wd_ref, cw_ref, cb_ref, db_ref,
                   zg_ref, xs_ref, bm_ref, cm_ref, dt_ref, *, nt):
    i = pl.program_id(0)
    first, last = i == 0, i == nt - 1
    h, hh = _norm_halo(xm_ref, xp_ref, xn_ref, g_ref, sh_ref, sc_ref)
    cw = SSD_CONV_DIM // 3

    def project(b):
        sl = slice(b * cw, (b + 1) * cw)
        return (jnp.dot(h, wx_ref[:, sl], preferred_element_type=F32),
                jnp.dot(hh, wx_ref[:, sl], preferred_element_type=F32))

    def conv(b, p):
        sl = slice(b * cw, (b + 1) * cw)
        return _silu(_conv3(p[0], p[1], cw_ref[:, sl], cb_ref[:, sl], first, last))

    p0 = project(0)
    p1 = project(1)
    xs_ref[:, :cw] = conv(0, p0).astype(xs_ref.dtype)
    p2 = project(2)
    xs_ref[:, cw:] = conv(1, p1).astype(xs_ref.dtype)
    zg = jnp.dot(h, wz_ref[...], preferred_element_type=F32)
    bc = conv(2, p2)
    bm_ref[...] = bc[:, :SSD_BC]
    cm_ref[...] = bc[:, SSD_BC:]
    zg_ref[...] = zg.astype(zg_ref.dtype)
    dt = jnp.dot(h, wd_ref[...], preferred_element_type=F32) + db_ref[...]
    dt = jnp.maximum(dt, 0.0) + jnp.log1p(jnp.exp(-jnp.abs(dt)))
    lane = lax.broadcasted_iota(jnp.int32, dt.shape, 1)
    dt = jnp.where((lane % LANES) < SSD_HEADS, dt, 0.0)
    dt_ref[0] = dt[:, :LANES]
    dt_ref[1] = dt[:, LANES:]


def _ssd_in(x, g, sh, sc, w_in, conv_w, conv_b, dt_bias):
    lx = x.shape[0]
    tm = min(512, lx)
    nt = lx // tm
    vec = pl.BlockSpec((1, D), lambda i: (0, 0))
    prev, nxt = _halo_specs(tm, lx)
    wz = w_in[:, :SSD_INNER]
    wx = w_in[:, SSD_INNER:SSD_INNER + SSD_CONV_DIM]
    wdt = w_in[:, SSD_INNER + SSD_CONV_DIM:]
    pad = LANES - SSD_HEADS
    wd = jnp.concatenate([jnp.pad(wdt[:, :SSD_HEADS], ((0, 0), (0, pad))),
                          jnp.pad(wdt[:, SSD_HEADS:], ((0, 0), (0, pad)))], axis=1)
    db = jnp.pad(dt_bias, ((0, 0), (0, pad))).reshape(1, 2 * LANES)
    full = lambda a: pl.BlockSpec(a.shape, lambda i: (0,) * a.ndim)
    rowblk = lambda w: pl.BlockSpec((tm, w), lambda i: (i, 0))
    return pl.pallas_call(
        functools.partial(_ssd_in_kernel, nt=nt),
        out_shape=(jax.ShapeDtypeStruct((lx, SSD_INNER), BF16), jax.ShapeDtypeStruct((lx, SSD_INNER), BF16),
                   jax.ShapeDtypeStruct((lx, SSD_BC), F32), jax.ShapeDtypeStruct((lx, SSD_BC), F32),
                   jax.ShapeDtypeStruct((2, lx, LANES), F32)),
        grid=(nt,),
        in_specs=[rowblk(D), prev, nxt, vec, vec, vec, _resident(wz.shape), _resident(wx.shape), _resident(wd.shape),
                  pl.BlockSpec((3, SSD_CONV_DIM), lambda i: (0, 0)),
                  pl.BlockSpec((1, SSD_CONV_DIM), lambda i: (0, 0)),
                  pl.BlockSpec((1, 2 * LANES), lambda i: (0, 0))],
        out_specs=(rowblk(SSD_INNER), rowblk(SSD_INNER), rowblk(SSD_BC), rowblk(SSD_BC),
                   pl.BlockSpec((2, tm, LANES), lambda i: (0, i, 0))),
        compiler_params=_cparams("parallel"),
        name="ssd_in",
    )(x, x, x, _row(g), _row(sh), _row(sc), wz, wx, wd, conv_w, _row(conv_b), db)


def _expand_heads(arr, e_ref):
    hi = arr.astype(BF16)
    lo = (arr - hi.astype(F32)).astype(BF16)
    e = e_ref[...]
    return jnp.dot(hi, e, preferred_element_type=F32) + jnp.dot(lo, e, preferred_element_type=F32)


def _ssd_prologue(dt_ref, a_row, tri, e_ref, need_y):
    dt = dt_ref[0]
    a = dt * a_row
    acs = jnp.dot(tri, a, preferred_element_type=F32, precision=HIGHEST)
    total = jnp.sum(a, axis=0, keepdims=True)
    ctx = dict(keep=tri > 0.5, acs=acs)
    ctx["wend_x"] = _expand_heads(jnp.exp(total - acs) * dt, e_ref)
    ctx["etot_x"] = _expand_heads(jnp.broadcast_to(jnp.exp(total), (SUBLANES, LANES)), e_ref)[0:1, :]
    if need_y:
        ctx["eacs_x"] = _expand_heads(jnp.exp(acs), e_ref)
        ctx["acs_t"] = acs.T
        ctx["dt_t"] = dt.T
    return ctx


def _ssd_prepare(ctx, xs_ref, bm_ref, cm_ref, g, need_y):
    q = SSD_CHUNK
    ppg = SSD_HEADS // 2 // SSD_GROUPS
    bg = bm_ref[:, g * SSD_STATE:(g + 1) * SSD_STATE]
    ops = dict(cg=cm_ref[:, g * SSD_STATE:(g + 1) * SSD_STATE].astype(BF16), bgt=bg.T.astype(BF16), xp=[], xw=[], m=[])
    if need_y:
        cb = lax.dot_general(ops["cg"], bg.astype(BF16), (((1,), (1,)), ((), ())), preferred_element_type=F32)
    for r in range(ppg):
        pidx = g * ppg + r
        psl = slice(pidx * LANES, (pidx + 1) * LANES)
        xp = xs_ref[:, psl]
        ops["xp"].append(xp)
        ops["xw"].append((xp.astype(F32) * ctx["wend_x"][:, psl]).astype(BF16))
        if need_y:
            for hd in (2 * pidx, 2 * pidx + 1):
                seg = jnp.broadcast_to(ctx["acs"][:, hd:hd + 1], (q, LANES)) - ctx["acs_t"][hd:hd + 1, :]
                lm = jnp.exp(jnp.where(ctx["keep"], seg, -jnp.inf))
                ops["m"].append((cb * lm * ctx["dt_t"][hd:hd + 1, :]).astype(BF16))
    return ops


def _ssd_issue(ctx, ops, g, h_scr, y_ref, need_y):
    ppg = SSD_HEADS // 2 // SSD_GROUPS
    left = lax.broadcasted_iota(jnp.int32, (SSD_CHUNK, LANES), 1) < SSD_P
    for r in range(ppg):
        pidx = g * ppg + r
        psl = slice(pidx * LANES, (pidx + 1) * LANES)
        hs = h_scr[pidx]
        if need_y:
            yd = [jnp.dot(ops["m"][2 * r + e], ops["xp"][r], preferred_element_type=F32) for e in range(2)]
            yoff = jnp.dot(ops["cg"], hs.astype(BF16), preferred_element_type=F32) * ctx["eacs_x"][:, psl]
            y_ref[:, psl] = (jnp.where(left, yd[0], yd[1]) + yoff).astype(y_ref.dtype)
        st = jnp.dot(ops["bgt"], ops["xw"][r], preferred_element_type=F32)
        h_scr[pidx] = hs * ctx["etot_x"][:, psl] + st


def _ssd_scan_kernel(xsf_ref, xsb_ref, bmf_ref, bmb_ref, cmf_ref, cmb_ref, dtf_ref, dtb_ref, a_ref, tri_ref, e_ref,
                     h0_ref, yf_ref, yb_ref, hfin_ref, h_scr, *, nc, need_y):
    s = pl.program_id(0)

    @pl.when(s == 0)
    def _():
        h_scr[...] = h0_ref[...]

    dirs = ((xsf_ref, bmf_ref, cmf_ref, dtf_ref, yf_ref), (xsb_ref, bmb_ref, cmb_ref, dtb_ref, yb_ref))
    ctxs = [_ssd_prologue(dirs[d][3], a_ref[d], tri_ref[d], e_ref, need_y) for d in range(2)]
    stages = [(d, g) for g in range(SSD_GROUPS) for d in range(2)]
    prep = lambda d, g: _ssd_prepare(ctxs[d], dirs[d][0], dirs[d][1], dirs[d][2], g, need_y)
    pending = prep(*stages[0])
    for idx, (d, g) in enumerate(stages):
        ops = pending
        if idx + 1 < len(stages):
            pending = prep(*stages[idx + 1])
        _ssd_issue(ctxs[d], ops, g, h_scr.at[d], dirs[d][4], need_y)

    @pl.when(s == nc - 1)
    def _():
        hfin_ref[...] = h_scr[...]


def _ssd_scan(xs, bm, cm, dt2, a_log, h0, need_y):
    lx = xs.shape[0]
    q = SSD_CHUNK
    nc = lx // q
    npair = SSD_HEADS // 2
    a = -jnp.exp(a_log.astype(F32))
    a_pad = jnp.pad(a, ((0, 0), (0, LANES - SSD_HEADS))).reshape(2, 1, LANES)
    lower = np.tril(np.ones((q, q), np.float32))
    tri = jnp.asarray(np.stack([lower, lower.T]))
    expand = jnp.asarray(np.kron(np.eye(LANES)[:, :SSD_HEADS], np.ones((1, SSD_P))), BF16)
    fwd = lambda s: s
    bwd = lambda s: nc - 1 - s
    rows = lambda w, idx: pl.BlockSpec((q, w), lambda s: (idx(s), 0))
    full = lambda a_: pl.BlockSpec(a_.shape, lambda s: (0,) * a_.ndim)
    ylen = lx if need_y else q
    yspec = (lambda idx: rows(SSD_INNER, idx)) if need_y else (lambda idx: pl.BlockSpec((q, SSD_INNER), lambda s: (0, 0)))
    yf, yb, hfin = pl.pallas_call(
        functools.partial(_ssd_scan_kernel, nc=nc, need_y=need_y),
        out_shape=[jax.ShapeDtypeStruct((ylen, SSD_INNER), BF16), jax.ShapeDtypeStruct((ylen, SSD_INNER), BF16),
                   jax.ShapeDtypeStruct(h0.shape, F32)],
        grid=(nc,),
        in_specs=[
            rows(SSD_INNER, fwd), rows(SSD_INNER, bwd), rows(SSD_BC, fwd), rows(SSD_BC, bwd),
            rows(SSD_BC, fwd), rows(SSD_BC, bwd),
            pl.BlockSpec((1, q, LANES), lambda s: (0, s, 0)),
            pl.BlockSpec((1, q, LANES), lambda s: (1, nc - 1 - s, 0)),
            full(a_pad), full(tri), full(expand), full(h0),
        ],
        out_specs=[yspec(fwd), yspec(bwd), full(h0)],
        scratch_shapes=[pltpu.VMEM((2, npair, SSD_STATE, 2 * SSD_P), F32)],
        compiler_params=_cparams("arbitrary"),
        name="ssd_scan",
    )(xs, xs, bm, bm, cm, cm, dt2, dt2, a_pad, tri, expand, h0)
    return ((yf, yb) if need_y else None), hfin


def _ssd_out_kernel(x_ref, yf_ref, yb_ref, xs_ref, zg_ref, dsk_ref, ng_ref, w_ref, gate_ref, o_ref):
    y = yf_ref[...].astype(F32) + yb_ref[...].astype(F32) + xs_ref[...].astype(F32) * dsk_ref[...]
    y = y * _silu(zg_ref[...].astype(F32))
    gw = SSD_INNER // SSD_GROUPS
    parts = []
    for g in range(SSD_GROUPS):
        yg = y[:, g * gw:(g + 1) * gw]
        ms = jnp.mean(yg * yg, axis=-1, keepdims=True)
        parts.append(yg * lax.rsqrt(ms + NORM_EPS) * ng_ref[:, g * gw:(g + 1) * gw])
    yn = jnp.concatenate(parts, axis=1).astype(BF16)
    o_ref[...] = x_ref[...] + gate_ref[...] * jnp.dot(yn, w_ref[...], preferred_element_type=F32)


def _ssd_out(x, yfb, xs, zg, d_skip, norm_g, w_out, gate):
    lx = x.shape[0]
    tm = min(256, lx)
    rowblk = lambda w: pl.BlockSpec((tm, w), lambda i: (i, 0))
    vecw = pl.BlockSpec((1, SSD_INNER), lambda i: (0, 0))
    return pl.pallas_call(
        _ssd_out_kernel,
        out_shape=jax.ShapeDtypeStruct((lx, D), F32),
        grid=(lx // tm,),
        in_specs=[rowblk(D), rowblk(SSD_INNER), rowblk(SSD_INNER), rowblk(SSD_INNER),
                  rowblk(SSD_INNER), vecw, vecw, _resident((SSD_INNER, D)),
                  pl.BlockSpec((1, D), lambda i: (0, 0))],
        out_specs=rowblk(D),
        compiler_params=_cparams("parallel"),
        name="ssd_out",
    )(x, yfb[0], yfb[1], xs, zg, _row(jnp.repeat(d_skip, SSD_P)), _row(norm_g), w_out, _row(gate))


def kernel(x, c, ctx, c_ctx, norm1_g, norm2_g, mod_w, mod_b, ffn_w_in, ffn_w_out, final_g, gm_w_in, gm_ln_g, gm_ln_b, gm_ws, gm_bs, gm_w_out, at_w_qkv, at_q_g, at_k_g, at_w_out, hy_w_in, hy_conv_w, hy_conv_b, hy_filt_w1, hy_filt_b1, hy_filt_w2, hy_filt_b2, hy_filt_w3, hy_filt_freq, hy_skip, hy_w_out, ssd_w_in, ssd_conv_w, ssd_conv_b, ssd_a_log, ssd_dt_bias, ssd_d_skip, ssd_norm_g, ssd_w_out):
    batch, seq, _ = x.shape
    assert batch == 1, "kernels are written for a single sequence"
    nctx = ctx.shape[1]
    xl = x[0]
    z = ctx[0]
    mods = _modulation(c[0], c_ctx, mod_w, mod_b)
    bf = lambda w: w.astype(BF16)

    for i in range(DEPTH):
        m, j = i % 4, i // 4
        want_ctx = i < DEPTH - 1
        ml = [mods[i, 0, k * D:(k + 1) * D] for k in range(6)]
        mc = [mods[i, 1, k * D:(k + 1) * D] for k in range(6)]
        n1 = norm1_g[i]
        pend_l = pend_c = None
        if m == 0:
            p = (bf(gm_w_in[j]), gm_ln_g[j], gm_ln_b[j], bf(gm_ws[j]), gm_bs[j], bf(gm_w_out[j]))
            xl = _gmlp(xl, n1, ml[0], ml[1], ml[2], *p)
            if want_ctx:
                z = _gmlp(z, n1, mc[0], mc[1], mc[2], *p)
        elif m == 1:
            wq, wo = bf(at_w_qkv[j]), bf(at_w_out[j])
            qt_l, k_l, vt_l = _qkv(xl, n1, ml[0], ml[1], wq, at_q_g[j], at_k_g[j], rope=True)
            qt_c, k_c, vt_c = _qkv(z, n1, mc[0], mc[1], wq, at_q_g[j], at_k_g[j], rope=False)
            k_all = jnp.concatenate([k_c, k_l], axis=1)
            vt_all = jnp.concatenate([vt_c, vt_l], axis=1)
            stot = nctx + seq
            ts = next(t for t in (3328, 1280, 1024, 512, 256) if stot % t == 0)
            score_bound = (HD ** 0.5 * LOG2E) * jnp.max(jnp.abs(at_q_g[j])) * jnp.max(jnp.abs(at_k_g[j]))
            o_l = lax.cond(score_bound <= FLASH_SCORE_BOUND,
                           lambda: _flash(qt_l, k_all, vt_all, stot, ts, bounded=True),
                           lambda: _flash(qt_l, k_all, vt_all, stot, ts, bounded=False))
            pend_l = ("proj", o_l, wo, ml[2])
            if want_ctx:
                pend_c = ("proj", _flash(qt_c, k_all, vt_all, nctx, nctx), wo, mc[2])
        elif m == 2:
            p = (bf(hy_w_in[j]), hy_conv_w[j], hy_conv_b[j], hy_filt_w1[j], hy_filt_b1[j], hy_filt_w2[j],
                 hy_filt_b2[j], hy_filt_w3[j], hy_filt_freq[j], hy_skip[j], bf(hy_w_out[j]))
            pend_l = _hyena(xl, n1, ml[0], ml[1], ml[2], *p)
            if want_ctx:
                pend_c = _hyena(z, n1, mc[0], mc[1], mc[2], *p)
        else:
            win, wo = bf(ssd_w_in[j]), bf(ssd_w_out[j])
            pin = (win, ssd_conv_w[j], ssd_conv_b[j], ssd_dt_bias[j])
            zg_c, xs_c, bm_c, cm_c, dt_c = _ssd_in(z, n1, mc[0], mc[1], *pin)
            zg_l, xs_l, bm_l, cm_l, dt_l = _ssd_in(xl, n1, ml[0], ml[1], *pin)
            h0 = jnp.zeros((2, SSD_HEADS // 2, SSD_STATE, 2 * SSD_P), F32)
            y_c, h_ctx = _ssd_scan(xs_c, bm_c, cm_c, dt_c, ssd_a_log[j], h0, want_ctx)
            y_l, _ = _ssd_scan(xs_l, bm_l, cm_l, dt_l, ssd_a_log[j], h_ctx, True)
            xl = _ssd_out(xl, y_l, xs_l, zg_l, ssd_d_skip[j], ssd_norm_g[j], wo, ml[2])
            if want_ctx:
                z = _ssd_out(z, y_c, xs_c, zg_c, ssd_d_skip[j], ssd_norm_g[j], wo, mc[2])
        wi, wo2 = bf(ffn_w_in[i]), bf(ffn_w_out[i])
        xl = _ffn(xl, pend_l, norm2_g[i], ml[3], ml[4], ml[5], wi, wo2, final_g, final=(i == DEPTH - 1))
        if want_ctx:
            z = _ffn(z, pend_c, norm2_g[i], mc[3], mc[4], mc[5], wi, wo2, final_g, final=False)
    return xl[None]
```

```python
import functools
import math

import numpy as np
import jax
import jax.numpy as jnp
from jax import lax
from jax.experimental import pallas as pl
from jax.experimental.pallas import tpu as pltpu

F32 = jnp.float32
BF16 = jnp.bfloat16
HIGHEST = lax.Precision.HIGHEST

D = 1024
DEPTH = 4
GRID_W = 64
NORM_EPS = 1e-6
FFN_HIDDEN = 2816
GM_CHUNK = 128
GM_WIDTH = 2 * D
GM_GROUPS = 8
GM_GW = GM_WIDTH // GM_GROUPS
HD = 64
QH = D // HD
KVH = 4
ROPE_THETA = 10000.0
LOG2E = math.log2(math.e)
FLASH_SCORE_BOUND = 30.0
FLASH_LOOKAHEAD = 2
HY_BANDS = 16
HY_EMB = 1 + 2 * HY_BANDS
HY_FILT_W = 64
HY_MAX_DECAY = math.log(1e-2) / 0.3
HY_MIN_DECAY = math.log(1e-2) / 1.5
SSD_INNER = 2 * D
SSD_P = 64
SSD_HEADS = SSD_INNER // SSD_P
SSD_GROUPS = 4
SSD_STATE = 128
SSD_CHUNK = 128
SSD_BC = SSD_GROUPS * SSD_STATE
SSD_CONV_DIM = SSD_INNER + 2 * SSD_BC

LANES = 128
SUBLANES = 8
VMEM_LIMIT_BYTES = 56 * 1024 * 1024
DFT_N2 = 128
DFT_SPLIT = 1


def _cparams(*sem):
    return pltpu.CompilerParams(dimension_semantics=sem, vmem_limit_bytes=VMEM_LIMIT_BYTES)


def _row(v):
    return v.reshape(1, -1)


def _normmod(x, g, shift, scale):
    ms = jnp.mean(x * x, axis=-1, keepdims=True)
    return x * lax.rsqrt(ms + NORM_EPS) * g * (1.0 + scale) + shift


def _silu(x):
    return x * jax.nn.sigmoid(x)


def _mod_kernel(cl_ref, cc_ref, w_ref, b_ref, o_ref):
    w = w_ref[0]
    for r, c_ref in enumerate((cl_ref, cc_ref)):
        a = _silu(c_ref[...])
        o_ref[0, r:r + 1, :] = jnp.sum(a * w, axis=0, keepdims=True) + b_ref[0]


def _modulation(c, c_ctx, mod_w, mod_b):
    tn = 1536
    n6 = 6 * D
    depth = mod_w.shape[0]
    return pl.pallas_call(
        _mod_kernel,
        out_shape=jax.ShapeDtypeStruct((depth, 2, n6), F32),
        grid=(depth, n6 // tn),
        in_specs=[
            pl.BlockSpec((D, 1), lambda i, n: (0, 0)),
            pl.BlockSpec((D, 1), lambda i, n: (0, 0)),
            pl.BlockSpec((1, D, tn), lambda i, n: (i, 0, n)),
            pl.BlockSpec((1, 1, tn), lambda i, n: (i, 0, n)),
        ],
        out_specs=pl.BlockSpec((1, 2, tn), lambda i, n: (i, 0, n)),
        compiler_params=_cparams("parallel", "parallel"),
        name="modulation",
    )(c.reshape(D, 1), c_ctx.reshape(D, 1), mod_w, mod_b.reshape(depth, 1, n6))


def _ffn_kernel(x_ref, *refs, mode, final):
    x = x_ref[...]
    if mode == "proj":
        a_ref, wm_ref, g1_ref = refs[:3]
        refs = refs[3:]
        x = x + g1_ref[...] * jnp.dot(a_ref[...], wm_ref[...], preferred_element_type=F32)
    elif mode == "hyena":
        x0_ref, yt_ref, wm_ref, g1_ref = refs[:4]
        refs = refs[4:]
        a = (x0_ref[...] * yt_ref[...].T).astype(BF16)
        x = x + g1_ref[...] * jnp.dot(a, wm_ref[...], preferred_element_type=F32)
    g_ref, sh_ref, sc_ref, gate_ref, wi_ref, wo_ref, fg_ref, o_ref = refs
    h = _normmod(x, g_ref[...], sh_ref[...], sc_ref[...]).astype(BF16)
    a = jnp.dot(h, wi_ref[:, :FFN_HIDDEN], preferred_element_type=F32)
    u = jnp.dot(h, wi_ref[:, FFN_HIDDEN:], preferred_element_type=F32)
    act = (_silu(a) * u).astype(BF16)
    y = x + gate_ref[...] * jnp.dot(act, wo_ref[...], preferred_element_type=F32)
    if final:
        ms = jnp.mean(y * y, axis=-1, keepdims=True)
        y = y * lax.rsqrt(ms + NORM_EPS) * fg_ref[...]
    o_ref[...] = y


def _resident(shape):
    return pl.BlockSpec(shape, lambda *_: (0,) * len(shape), pipeline_mode=pl.Buffered(1))


def _ffn(x, pending, g, sh, sc, gate, w_in, w_out, final_g, final):
    lx = x.shape[0]
    tm = min(512, lx)
    vec = pl.BlockSpec((1, D), lambda i: (0, 0))
    rows = pl.BlockSpec((tm, D), lambda i: (i, 0))
    mode, pre_args, pre_specs = "none", (), []
    if pending is not None:
        mode = pending[0]
        if mode == "proj":
            _, a, wm, g1 = pending
            pre_args, pre_specs = (a, wm, _row(g1)), [rows, _resident(wm.shape), vec]
        else:
            _, x0, yt, wm, g1 = pending
            pre_args = (x0, yt, wm, _row(g1))
            pre_specs = [rows, pl.BlockSpec((D, tm), lambda i: (0, i)), _resident(wm.shape), vec]
    return pl.pallas_call(
        functools.partial(_ffn_kernel, mode=mode, final=final),
        out_shape=jax.ShapeDtypeStruct((lx, D), F32),
        grid=(lx // tm,),
        in_specs=[rows] + pre_specs + [
            vec, vec, vec, vec,
            _resident((D, 2 * FFN_HIDDEN)),
            _resident((FFN_HIDDEN, D)),
            vec,
        ],
        out_specs=rows,
        compiler_params=_cparams("parallel"),
        name="ffn",
    )(x, *pre_args, _row(g), _row(sh), _row(sc), _row(gate), w_in, w_out, _row(final_g))


def _gmlp_kernel(x_ref, g_ref, sh_ref, sc_ref, gate_ref, win_ref, lng_ref, lnb_ref, ws_ref, bs_ref, wout_ref,
                 o_ref, *, tm):
    nsub = max(1, tm // (2 * GM_CHUNK))
    rs = tm // nsub

    def project(i):
        x = x_ref[i * rs:(i + 1) * rs, :]
        h = _normmod(x, g_ref[...], sh_ref[...], sc_ref[...]).astype(BF16)
        return jnp.dot(h, win_ref[...], preferred_element_type=F32)

    def mix(i, t):
        t = 0.5 * t * (1.0 + lax.erf(t * (1.0 / math.sqrt(2.0))))
        u = t[:, :GM_WIDTH]
        v = t[:, GM_WIDTH:]
        mu = jnp.mean(v, axis=-1, keepdims=True)
        vc = v - mu
        var = jnp.mean(vc * vc, axis=-1, keepdims=True)
        v = (vc * lax.rsqrt(var + NORM_EPS) * lng_ref[...] + lnb_ref[...]).astype(BF16)
        rows = []
        for q in range(rs // GM_CHUNK):
            cols = []
            for gidx in range(GM_GROUPS):
                vq = v[q * GM_CHUNK:(q + 1) * GM_CHUNK, gidx * GM_GW:(gidx + 1) * GM_GW]
                bias = bs_ref[gidx]
                m = jnp.dot(ws_ref[gidx], vq, preferred_element_type=F32)
                cols.append(m + jnp.concatenate([bias] * (GM_GW // LANES), axis=1))
            rows.append(jnp.concatenate(cols, axis=1))
        gated = (u * jnp.concatenate(rows, axis=0)).astype(BF16)
        y = jnp.dot(gated, wout_ref[...], preferred_element_type=F32)
        o_ref[i * rs:(i + 1) * rs, :] = x_ref[i * rs:(i + 1) * rs, :] + gate_ref[...] * y

    t_next = project(0)
    for i in range(nsub):
        t = t_next
        if i + 1 < nsub:
            t_next = project(i + 1)
        mix(i, t)


def _gmlp(x, g, sh, sc, gate, w_in, ln_g, ln_b, ws, bs, w_out):
    lx = x.shape[0]
    tm = min(512, lx)
    vec = pl.BlockSpec((1, D), lambda i: (0, 0))
    vecw = pl.BlockSpec((1, GM_WIDTH), lambda i: (0, 0))
    bsb = jnp.broadcast_to(bs[:, :, None], (GM_GROUPS, GM_CHUNK, LANES))
    return pl.pallas_call(
        functools.partial(_gmlp_kernel, tm=tm),
        out_shape=jax.ShapeDtypeStruct((lx, D), F32),
        grid=(lx // tm,),
        in_specs=[
            pl.BlockSpec((tm, D), lambda i: (i, 0)),
            vec, vec, vec, vec,
            _resident((D, 2 * GM_WIDTH)),
            vecw, vecw,
            pl.BlockSpec((GM_GROUPS, GM_CHUNK, GM_CHUNK), lambda i: (0, 0, 0)),
            pl.BlockSpec((GM_GROUPS, GM_CHUNK, LANES), lambda i: (0, 0, 0)),
            _resident((GM_WIDTH, D)),
        ],
        out_specs=pl.BlockSpec((tm, D), lambda i: (i, 0)),
        compiler_params=_cparams("parallel"),
        name="gmlp",
    )(x, _row(g), _row(sh), _row(sc), _row(gate), w_in, _row(ln_g), _row(ln_b), ws, bsb, w_out)


def _group_sumsq(t, e_ref):
    sq = t * t
    hi = sq.astype(BF16)
    lo = (sq - hi.astype(F32)).astype(BF16)
    outs = []
    for j in range(t.shape[1] // LANES):
        sl = slice(j * LANES, (j + 1) * LANES)
        outs.append(jnp.dot(hi[:, sl], e_ref[...], preferred_element_type=F32)
                    + jnp.dot(lo[:, sl], e_ref[...], preferred_element_type=F32))
    return jnp.concatenate(outs, axis=1)


def _rope(t, cosf, sinf):
    w = t.shape[1]
    lane = lax.broadcasted_iota(jnp.int32, t.shape, 1)
    first = (lane % HD) < (HD // 2)
    partner = jnp.where(first, pltpu.roll(t, w - HD // 2, axis=1), pltpu.roll(t, HD // 2, axis=1))
    reps = w // LANES
    c = jnp.concatenate([cosf] * reps, axis=1)
    s = jnp.concatenate([sinf] * reps, axis=1)
    return t * c + partner * s


def _qkv_kernel(x_ref, g_ref, sh_ref, sc_ref, w_ref, qg_ref, kg_ref, e_ref, cos_ref, sin_ref,
                qt_ref, k_ref, vt_ref, *, rope):
    h = _normmod(x_ref[...], g_ref[...], sh_ref[...], sc_ref[...]).astype(BF16)
    qkv = jnp.dot(h, w_ref[...], preferred_element_type=F32)
    q = qkv[:, :D]
    k = qkv[:, D:D + KVH * HD]
    v = qkv[:, D + KVH * HD:]
    q = q * lax.rsqrt(_group_sumsq(q, e_ref) * (1.0 / HD) + NORM_EPS) * qg_ref[...]
    k = k * lax.rsqrt(_group_sumsq(k, e_ref) * (1.0 / HD) + NORM_EPS) * kg_ref[...]
    if rope:
        q = _rope(q, cos_ref[...], sin_ref[...])
        k = _rope(k, cos_ref[...], sin_ref[...])
    qt_ref[...] = (q * (HD ** -0.5 * LOG2E)).T.astype(BF16)
    for gidx in range(KVH):
        k_ref[gidx] = k[:, gidx * HD:(gidx + 1) * HD].astype(BF16)
    vt_ref[...] = v.T.astype(BF16)


def _qkv(x, g, sh, sc, w_qkv, q_g, k_g, rope):
    lx = x.shape[0]
    tm = min(512, lx)
    vec = pl.BlockSpec((1, D), lambda i: (0, 0))
    kvw = KVH * HD
    rows = lx // GRID_W
    row = jnp.repeat(jnp.arange(rows, dtype=F32), GRID_W)
    col = jnp.tile(jnp.arange(GRID_W, dtype=F32), rows)
    n = HD // 4
    inv = ROPE_THETA ** (-jnp.arange(n, dtype=F32) / n)
    ang = jnp.concatenate([row[:, None] * inv, col[:, None] * inv], axis=-1)
    cos, sin = jnp.cos(ang), jnp.sin(ang)
    cosf = jnp.tile(jnp.concatenate([cos, cos], axis=-1), (1, LANES // HD))
    sinf = jnp.tile(jnp.concatenate([-sin, sin], axis=-1), (1, LANES // HD))
    eblk = jnp.asarray(np.kron(np.eye(LANES // HD), np.ones((HD, HD))), BF16)
    tab = pl.BlockSpec((tm, LANES), lambda i: (i, 0))
    return pl.pallas_call(
        functools.partial(_qkv_kernel, rope=rope),
        out_shape=(jax.ShapeDtypeStruct((D, lx), BF16),
                   jax.ShapeDtypeStruct((KVH, lx, HD), BF16),
                   jax.ShapeDtypeStruct((kvw, lx), BF16)),
        grid=(lx // tm,),
        in_specs=[
            pl.BlockSpec((tm, D), lambda i: (i, 0)),
            vec, vec, vec,
            _resident((D, D + 2 * kvw)),
            vec,
            pl.BlockSpec((1, kvw), lambda i: (0, 0)),
            pl.BlockSpec((LANES, LANES), lambda i: (0, 0)),
            tab, tab,
        ],
        out_specs=(pl.BlockSpec((D, tm), lambda i: (0, i)),
                   pl.BlockSpec((KVH, tm, HD), lambda i: (0, i, 0)),
                   pl.BlockSpec((kvw, tm), lambda i: (0, i))),
        compiler_params=_cparams("parallel"),
        name="qkv_proj",
    )(x, _row(g), _row(sh), _row(sc), w_qkv, _row(jnp.tile(q_g, QH)), _row(jnp.tile(k_g, KVH)), eblk, cosf, sinf)


def _flash_kernel(qt_ref, k_ref, vt_ref, o_ref, qg_scr, m_scr, l_scr, acc_scr, *, tq, ts, tc, nkv, bounded):
    j = pl.program_id(1)
    gq = QH // KVH
    mcols = gq * tq

    @pl.when(j == 0)
    def _():
        for h in range(QH):
            qg_scr[h // gq, :, (h % gq) * tq:(h % gq + 1) * tq] = qt_ref[h * HD:(h + 1) * HD, :]
        m_scr[...] = jnp.full(m_scr.shape, -jnp.inf, F32)
        l_scr[...] = jnp.zeros_like(l_scr)
        acc_scr[...] = jnp.zeros_like(acc_scr)

    stages = [(g, c) for c in range(ts // tc) for g in range(KVH)]

    def scores(g, c):
        return jnp.dot(k_ref[g, c * tc:(c + 1) * tc, :], qg_scr[g], preferred_element_type=F32)

    pending = [scores(*st) for st in stages[:FLASH_LOOKAHEAD]]
    for idx, (g, c) in enumerate(stages):
        s = pending.pop(0)
        if idx + FLASH_LOOKAHEAD < len(stages):
            pending.append(scores(*stages[idx + FLASH_LOOKAHEAD]))
        vt = vt_ref[g * HD:(g + 1) * HD, c * tc:(c + 1) * tc]
        if bounded:
            p = jnp.exp2(s)
            l_scr[g] += jnp.sum(p, axis=0, keepdims=True)
            acc_scr[g] += jnp.dot(vt, p.astype(BF16), preferred_element_type=F32)
        else:
            m_prev = m_scr[g]
            m_new = jnp.maximum(m_prev, jnp.max(s, axis=0, keepdims=True))
            alpha = jnp.exp2(m_prev - m_new)
            p = jnp.exp2(s - m_new)
            l_scr[g] = alpha * l_scr[g] + jnp.sum(p, axis=0, keepdims=True)
            acc_scr[g] = alpha * acc_scr[g] + jnp.dot(vt, p.astype(BF16), preferred_element_type=F32)
            m_scr[g] = m_new

    @pl.when(j == nkv - 1)
    def _():
        rows = []
        for g in range(KVH):
            o = acc_scr[g] / l_scr[g]
            rows += [o[:, r * tq:(r + 1) * tq] for r in range(gq)]
        o_ref[...] = jnp.concatenate(rows, axis=0).T.astype(o_ref.dtype)


def _flash(qt, k, vt, s_len, ts, bounded=False):
    lq = qt.shape[1]
    tq = min(128, lq)
    nkv = s_len // ts
    gq = QH // KVH
    kvw = KVH * HD
    tc = 2 * LANES if ts % (2 * LANES) == 0 else LANES
    return pl.pallas_call(
        functools.partial(_flash_kernel, tq=tq, ts=ts, tc=tc, nkv=nkv, bounded=bounded),
        out_shape=jax.ShapeDtypeStruct((lq, D), BF16),
        grid=(lq // tq, nkv),
        in_specs=[
            pl.BlockSpec((D, tq), lambda i, j: (0, i)),
            pl.BlockSpec((KVH, ts, HD), lambda i, j: (0, j, 0)),
            pl.BlockSpec((kvw, ts), lambda i, j: (0, j)),
        ],
        out_specs=pl.BlockSpec((tq, D), lambda i, j: (i, 0)),
        scratch_shapes=[
            pltpu.VMEM((KVH, HD, gq * tq), BF16),
            pltpu.VMEM((KVH, 1, gq * tq), F32),
            pltpu.VMEM((KVH, 1, gq * tq), F32),
            pltpu.VMEM((KVH, HD, gq * tq), F32),
        ],
        compiler_params=_cparams("parallel", "arbitrary"),
        name="flash_attn",
    )(qt, k, vt)


def _halo_specs(tm, lx):
    nb = lx // SUBLANES
    step = tm // SUBLANES
    prev = pl.BlockSpec((SUBLANES, D), lambda i: (jnp.maximum(i * step - 1, 0), 0))
    nxt = pl.BlockSpec((SUBLANES, D), lambda i: (jnp.minimum((i + 1) * step, nb - 1), 0))
    return prev, nxt


def _conv3(p_main, p_halo, cw, cb, first, last):
    tm = p_main.shape[0]
    rid = lax.broadcasted_iota(jnp.int32, p_main.shape, 0)
    before = jnp.where(first, 0.0, p_halo[SUBLANES - 1:SUBLANES, :])
    after = jnp.where(last, 0.0, p_halo[SUBLANES:SUBLANES + 1, :])
    up = jnp.where(rid == 0, before, pltpu.roll(p_main, 1, axis=0))
    dn = jnp.where(rid == tm - 1, after, pltpu.roll(p_main, tm - 1, axis=0))
    return cw[0:1, :] * up + cw[1:2, :] * p_main + cw[2:3, :] * dn + cb


def _norm_halo(xm_ref, xp_ref, xn_ref, g_ref, sh_ref, sc_ref):
    g, sh, sc = g_ref[...], sh_ref[...], sc_ref[...]
    h = _normmod(xm_ref[...], g, sh, sc).astype(BF16)
    hh = jnp.concatenate([_normmod(xp_ref[...], g, sh, sc), _normmod(xn_ref[...], g, sh, sc)], axis=0).astype(BF16)
    return h, hh


def _hy_in_kernel(xm_ref, xp_ref, xn_ref, g_ref, sh_ref, sc_ref, w_ref, cw_ref, cb_ref, x0_ref, ut_ref, *, nt):
    i = pl.program_id(0)
    first, last = i == 0, i == nt - 1
    h, hh = _norm_halo(xm_ref, xp_ref, xn_ref, g_ref, sh_ref, sc_ref)

    def project(b):
        sl = slice(b * D, (b + 1) * D)
        return (jnp.dot(h, w_ref[:, sl], preferred_element_type=F32),
                jnp.dot(hh, w_ref[:, sl], preferred_element_type=F32))

    def conv(b, p):
        sl = slice(b * D, (b + 1) * D)
        return _conv3(p[0], p[1], cw_ref[:, sl], cb_ref[:, sl], first, last)

    p0 = project(0)
    p1 = project(1)
    x0_ref[...] = conv(0, p0)
    p2 = project(2)
    x1 = conv(1, p1)
    ut_ref[...] = (x1 * conv(2, p2)).T


def _hy_in(x, g, sh, sc, w_in, conv_w, conv_b):
    lx = x.shape[0]
    tm = min(512, lx)
    nt = lx // tm
    vec = pl.BlockSpec((1, D), lambda i: (0, 0))
    prev, nxt = _halo_specs(tm, lx)
    return pl.pallas_call(
        functools.partial(_hy_in_kernel, nt=nt),
        out_shape=(jax.ShapeDtypeStruct((lx, D), F32), jax.ShapeDtypeStruct((D, lx), F32)),
        grid=(nt,),
        in_specs=[
            pl.BlockSpec((tm, D), lambda i: (i, 0)), prev, nxt,
            vec, vec, vec,
            _resident((D, 3 * D)),
            pl.BlockSpec((3, 3 * D), lambda i: (0, 0)),
            pl.BlockSpec((1, 3 * D), lambda i: (0, 0)),
        ],
        out_specs=(pl.BlockSpec((tm, D), lambda i: (i, 0)), pl.BlockSpec((D, tm), lambda i: (0, i))),
        compiler_params=_cparams("parallel"),
        name="hyena_in",
    )(x, x, x, _row(g), _row(sh), _row(sc), w_in, conv_w, _row(conv_b))


def _hy_filter_kernel(ft_ref, t_ref, w1_ref, b1_ref, w2_ref, b2_ref, w3h_ref, w3l_ref, fr_ref, dl_ref, sk_ref, kt_ref,
                      *, tm, ltrue, lpad):
    i = pl.program_id(0)
    feats = ft_ref[...]
    fr = fr_ref[...]
    hid = jnp.sin(fr * (jnp.dot(w1_ref[...], feats, preferred_element_type=F32, precision=HIGHEST) + b1_ref[...]))
    hid = jnp.sin(fr * (jnp.dot(w2_ref[...], hid, preferred_element_type=F32, precision=HIGHEST) + b2_ref[...]))
    hh = hid.astype(BF16)
    hl = (hid - hh.astype(F32)).astype(BF16)
    w3h = w3h_ref[...]
    kt = (jnp.dot(w3h, hh, preferred_element_type=F32) + jnp.dot(w3h, hl, preferred_element_type=F32)
          + jnp.dot(w3l_ref[...], hh, preferred_element_type=F32))
    kt = kt * jnp.exp(-dl_ref[...] * t_ref[...])
    if lpad != ltrue:
        pos = lax.broadcasted_iota(jnp.int32, kt.shape, 1) + i * tm
        kt = jnp.where(pos < ltrue, kt, 0.0)
    kt_ref[...] = kt

    @pl.when(i == 0)
    def _():
        row = lax.broadcasted_iota(jnp.int32, (2 * D, 1), 0)
        kt_ref[:, 0:1] = jnp.where(row < D, kt[:, 0:1] + sk_ref[...], 0.0)


def _hy_filter(ltrue, lpad, w1, b1, w2, b2, w3, freq, skip):
    tm = min(512, lpad)
    col = lambda v_: v_.reshape(-1, 1)
    w1t = jnp.pad(w1.T, ((0, 0), (0, LANES - HY_EMB)))
    deltas = jnp.abs(jnp.linspace(HY_MIN_DECAY, HY_MAX_DECAY, D, dtype=F32))
    pos = jnp.arange(lpad, dtype=F32)
    zf = jnp.linspace(1e-4, HY_BANDS - 1, HY_BANDS, dtype=F32)[:, None] * (2.0 * math.pi * pos / ltrue)[None, :]
    feats = jnp.concatenate([(pos / (ltrue - 1))[None, :], jnp.cos(zf), -jnp.sin(zf),
                             jnp.zeros((LANES - HY_EMB, lpad), F32)], axis=0)
    w3t = w3.T
    w3h = w3t.astype(BF16)
    w3l = (w3t - w3h.astype(F32)).astype(BF16)
    full = lambda a: pl.BlockSpec(a.shape, lambda i: (0,) * a.ndim)
    args = (w1t, col(b1), w2.T, col(b2), w3h, w3l, col(freq), col(jnp.tile(deltas, 2)),
            col(jnp.concatenate([skip, jnp.zeros((D,), F32)])))
    return pl.pallas_call(
        functools.partial(_hy_filter_kernel, tm=tm, ltrue=ltrue, lpad=lpad),
        out_shape=jax.ShapeDtypeStruct((2 * D, lpad), F32),
        grid=(lpad // tm,),
        in_specs=[pl.BlockSpec((LANES, tm), lambda i: (0, i)), pl.BlockSpec((1, tm), lambda i: (0, i))]
                 + [full(a) for a in args],
        out_specs=pl.BlockSpec((2 * D, tm), lambda i: (0, i)),
        compiler_params=_cparams("parallel"),
        name="hyena_filter",
    )(feats, feats[0:1, :], *args)


def _dft_consts(nh):
    n1 = 2 * nh
    n = n1 * DFT_N2
    k1 = np.arange(n1)[:, None].astype(np.float64)
    a1 = 2.0 * np.pi * k1 * np.arange(nh)[None, :] / n1
    f1 = np.concatenate([np.cos(a1), -np.sin(a1)], axis=0)
    at = 2.0 * np.pi * ((np.arange(n1)[:, None] * np.arange(DFT_N2)[None, :]) % n) / n
    a2 = 2.0 * np.pi * ((np.arange(DFT_N2)[:, None] * np.arange(DFT_N2)[None, :]) % DFT_N2) / DFT_N2
    c2, s2 = np.cos(a2), np.sin(a2)
    f2 = np.block([[c2, -s2], [s2, c2]])
    g2 = np.block([[c2, s2], [-s2, c2]])
    g1 = np.concatenate([np.cos(a1).T, -np.sin(a1).T], axis=1) / n
    as32 = lambda a: jnp.asarray(a, F32)

    def parts(a):
        hi = a.astype(BF16)
        if DFT_SPLIT == 1:
            return (jnp.asarray(hi),)
        return (jnp.asarray(hi), jnp.asarray((a - hi.astype(np.float64)).astype(BF16)))

    return parts(f1), as32(np.cos(at)), as32(np.sin(at)), parts(f2), parts(g2), parts(g1)


def _split_bf16(a):
    hi = a.astype(BF16)
    if DFT_SPLIT == 1:
        return (hi,)
    return (hi, (a - hi.astype(F32)).astype(BF16))


def _split_dot(a, b):
    out = jnp.dot(a[0], b[0], preferred_element_type=F32)
    if DFT_SPLIT > 1:
        out = out + jnp.dot(a[1], b[0], preferred_element_type=F32) + jnp.dot(a[0], b[1], preferred_element_type=F32)
    return out


def _load_parts(refs):
    return tuple(r[...] for r in refs)


def _hy_conv_kernel(x_ref, kf_ref, kb_ref, *refs, cb, n1):
    ns = DFT_SPLIT
    f1, (twc_ref, tws_ref), f2 = refs[:ns], refs[ns:ns + 2], refs[ns + 2:2 * ns + 2]
    g2, g1, o_ref = refs[2 * ns + 2:3 * ns + 2], refs[3 * ns + 2:4 * ns + 2], refs[4 * ns + 2]
    twc, tws = twc_ref[...], tws_ref[...]
    f1, f2, g2, g1 = _load_parts(f1), _load_parts(f2), _load_parts(g2), _load_parts(g1)
    hc = cb // 2
    nt = 3 * hc

    def dft1(half):
        xs = [r[half * hc + c] for r in (x_ref, kf_ref, kb_ref) for c in range(hc)]
        return _split_dot(f1, _split_bf16(jnp.concatenate(xs, axis=1)))

    def twiddle_rows(a):
        rows = []
        for t in range(nt):
            ar = a[:n1, t * DFT_N2:(t + 1) * DFT_N2]
            ai = a[n1:, t * DFT_N2:(t + 1) * DFT_N2]
            rows.append(jnp.concatenate([ar * twc + ai * tws, ai * twc - ar * tws], axis=1))
        return jnp.concatenate(rows, axis=0)

    def dft2(rows):
        return _split_dot(_split_bf16(rows), f2)

    def product(spec_all):
        rows = hc * n1
        spec, hf, hb = spec_all[:rows], spec_all[rows:2 * rows], spec_all[2 * rows:]
        hr = hf[:, :DFT_N2] + hb[:, :DFT_N2]
        hi = hf[:, DFT_N2:] - hb[:, DFT_N2:]
        xr, xi = spec[:, :DFT_N2], spec[:, DFT_N2:]
        return jnp.concatenate([xr * hr - xi * hi, xr * hi + xi * hr], axis=1)

    def idft2(y):
        return _split_dot(_split_bf16(y), g2)

    def twiddle_cols(b):
        cols = []
        for c in range(hc):
            br = b[c * n1:(c + 1) * n1, :DFT_N2]
            bi = b[c * n1:(c + 1) * n1, DFT_N2:]
            cols.append(jnp.concatenate([br * twc - bi * tws, bi * twc + br * tws], axis=0))
        return jnp.concatenate(cols, axis=1)

    def idft1(cols):
        return _split_dot(g1, _split_bf16(cols))

    def store(half, out):
        for c in range(hc):
            o_ref[half * hc + c] = out[:, c * DFT_N2:(c + 1) * DFT_N2]

    a0 = dft1(0)
    a1 = dft1(1)
    s0 = dft2(twiddle_rows(a0))
    s1 = dft2(twiddle_rows(a1))
    b0 = idft2(product(s0))
    b1 = idft2(product(s1))
    o0 = idft1(twiddle_cols(b0))
    o1 = idft1(twiddle_cols(b1))
    store(0, o0)
    store(1, o1)


def _hy_longconv(ut, kt, lpad):
    nh = lpad // DFT_N2
    n1 = 2 * nh
    cb = max(8, min(64, 2048 // n1))
    f1, twc, tws, f2, g2, g1 = _dft_consts(nh)
    consts = f1 + (twc, tws) + f2 + g2 + g1
    full = lambda a: pl.BlockSpec(a.shape, lambda i: (0,) * a.ndim)
    k3 = kt.reshape(2 * D, nh, DFT_N2)
    u3 = ut.reshape(D, nh, DFT_N2)
    nb = D // cb
    y3 = pl.pallas_call(
        functools.partial(_hy_conv_kernel, cb=cb, n1=n1),
        out_shape=jax.ShapeDtypeStruct((D, nh, DFT_N2), F32),
        grid=(nb,),
        in_specs=[pl.BlockSpec((cb, nh, DFT_N2), lambda i: (i, 0, 0)),
                  pl.BlockSpec((cb, nh, DFT_N2), lambda i: (i, 0, 0)),
                  pl.BlockSpec((cb, nh, DFT_N2), lambda i: (i + nb, 0, 0))]
                 + [full(a) for a in consts],
        out_specs=pl.BlockSpec((cb, nh, DFT_N2), lambda i: (i, 0, 0)),
        compiler_params=_cparams("parallel"),
        name="hyena_longconv",
    )(u3, k3, k3, *consts)
    return y3.reshape(D, lpad)


def _hyena(x, g, sh, sc, gate, w_in, conv_w, conv_b, w1, b1, w2, b2, w3, freq, skip, w_out):
    lx = x.shape[0]
    lpad = max(lx, SUBLANES * DFT_N2)
    x0, ut = _hy_in(x, g, sh, sc, w_in, conv_w, conv_b)
    if lpad != lx:
        ut = jnp.pad(ut, ((0, 0), (0, lpad - lx)))
    kt = _hy_filter(lx, lpad, w1, b1, w2, b2, w3, freq, skip)
    yt = _hy_longconv(ut, kt, lpad)[:, :lx]
    return ("hyena", x0, yt, w_out, gate)


def _ssd_in_kernel(xm_ref, xp_ref, xn_ref, g_ref, sh_ref, sc_ref, wz_ref, wx_ref, wd_ref, cw_ref, cb_ref, db_ref,
                   zg_ref, xs_ref, bm_ref, cm_ref, dt_ref, *, nt):
    i = pl.program_id(0)
    first, last = i == 0, i == nt - 1
    h, hh = _norm_halo(xm_ref, xp_ref, xn_ref, g_ref, sh_ref, sc_ref)
    cw = SSD_CONV_DIM // 3

    def project(b):
        sl = slice(b * cw, (b + 1) * cw)
        return (jnp.dot(h, wx_ref[:, sl], preferred_element_type=F32),
                jnp.dot(hh, wx_ref[:, sl], preferred_element_type=F32))

    def conv(b, p):
        sl = slice(b * cw, (b + 1) * cw)
        return _silu(_conv3(p[0], p[1], cw_ref[:, sl], cb_ref[:, sl], first, last))

    p0 = project(0)
    p1 = project(1)
    xs_ref[:, :cw] = conv(0, p0).astype(xs_ref.dtype)
    p2 = project(2)
    xs_ref[:, cw:] = conv(1, p1).astype(xs_ref.dtype)
    zg = jnp.dot(h, wz_ref[...], preferred_element_type=F32)
    bc = conv(2, p2)
    bm_ref[...] = bc[:, :SSD_BC]
    cm_ref[...] = bc[:, SSD_BC:]
    zg_ref[...] = zg.astype(zg_ref.dtype)
    dt = jnp.dot(h, wd_ref[...], preferred_element_type=F32) + db_ref[...]
    dt = jnp.maximum(dt, 0.0) + jnp.log1p(jnp.exp(-jnp.abs(dt)))
    lane = lax.broadcasted_iota(jnp.int32, dt.shape, 1)
    dt = jnp.where((lane % LANES) < SSD_HEADS, dt, 0.0)
    dt_ref[0] = dt[:, :LANES]
    dt_ref[1] = dt[:, LANES:]


def _ssd_in(x, g, sh, sc, w_in, conv_w, conv_b, dt_bias):
    lx = x.shape[0]
    tm = min(512, lx)
    nt = lx // tm
    vec = pl.BlockSpec((1, D), lambda i: (0, 0))
    prev, nxt = _halo_specs(tm, lx)
    wz = w_in[:, :SSD_INNER]
    wx = w_in[:, SSD_INNER:SSD_INNER + SSD_CONV_DIM]
    wdt = w_in[:, SSD_INNER + SSD_CONV_DIM:]
    pad = LANES - SSD_HEADS
    wd = jnp.concatenate([jnp.pad(wdt[:, :SSD_HEADS], ((0, 0), (0, pad))),
                          jnp.pad(wdt[:, SSD_HEADS:], ((0, 0), (0, pad)))], axis=1)
    db = jnp.pad(dt_bias, ((0, 0), (0, pad))).reshape(1, 2 * LANES)
    full = lambda a: pl.BlockSpec(a.shape, lambda i: (0,) * a.ndim)
    rowblk = lambda w: pl.BlockSpec((tm, w), lambda i: (i, 0))
    return pl.pallas_call(
        functools.partial(_ssd_in_kernel, nt=nt),
        out_shape=(jax.ShapeDtypeStruct((lx, SSD_INNER), BF16), jax.ShapeDtypeStruct((lx, SSD_INNER), BF16),
                   jax.ShapeDtypeStruct((lx, SSD_BC), F32), jax.ShapeDtypeStruct((lx, SSD_BC), F32),
                   jax.ShapeDtypeStruct((2, lx, LANES), F32)),
        grid=(nt,),
        in_specs=[rowblk(D), prev, nxt, vec, vec, vec, _resident(wz.shape), _resident(wx.shape), _resident(wd.shape),
                  pl.BlockSpec((3, SSD_CONV_DIM), lambda i: (0, 0)),
                  pl.BlockSpec((1, SSD_CONV_DIM), lambda i: (0, 0)),
                  pl.BlockSpec((1, 2 * LANES), lambda i: (0, 0))],
        out_specs=(rowblk(SSD_INNER), rowblk(SSD_INNER), rowblk(SSD_BC), rowblk(SSD_BC),
                   pl.BlockSpec((2, tm, LANES), lambda i: (0, i, 0))),
        compiler_params=_cparams("parallel"),
        name="ssd_in",
    )(x, x, x, _row(g), _row(sh), _row(sc), wz, wx, wd, conv_w, _row(conv_b), db)


def _expand_heads(arr, e_ref):
    hi = arr.astype(BF16)
    lo = (arr - hi.astype(F32)).astype(BF16)
    e = e_ref[...]
    return jnp.dot(hi, e, preferred_element_type=F32) + jnp.dot(lo, e, preferred_element_type=F32)


def _ssd_prologue(dt_ref, a_row, tri, e_ref, need_y):
    dt = dt_ref[0]
    a = dt * a_row
    acs = jnp.dot(tri, a, preferred_element_type=F32, precision=HIGHEST)
    total = jnp.sum(a, axis=0, keepdims=True)
    ctx = dict(keep=tri > 0.5, acs=acs)
    ctx["wend_x"] = _expand_heads(jnp.exp(total - acs) * dt, e_ref)
    ctx["etot_x"] = _expand_heads(jnp.broadcast_to(jnp.exp(total), (SUBLANES, LANES)), e_ref)[0:1, :]
    if need_y:
        ctx["eacs_x"] = _expand_heads(jnp.exp(acs), e_ref)
        ctx["acs_t"] = acs.T
        ctx["dt_t"] = dt.T
    return ctx


def _ssd_prepare(ctx, xs_ref, bm_ref, cm_ref, g, need_y):
    q = SSD_CHUNK
    ppg = SSD_HEADS // 2 // SSD_GROUPS
    bg = bm_ref[:, g * SSD_STATE:(g + 1) * SSD_STATE]
    ops = dict(cg=cm_ref[:, g * SSD_STATE:(g + 1) * SSD_STATE].astype(BF16), bgt=bg.T.astype(BF16), xs2=[], xw=[], m2=[])
    if need_y:
        cb = lax.dot_general(ops["cg"], bg.astype(BF16), (((1,), (1,)), ((), ())), preferred_element_type=F32)
        left = lax.broadcasted_iota(jnp.int32, (q, LANES), 1) < SSD_P
    xws = []
    for r in range(ppg):
        pidx = g * ppg + r
        psl = slice(pidx * LANES, (pidx + 1) * LANES)
        xp = xs_ref[:, psl]
        xws.append((xp.astype(F32) * ctx["wend_x"][:, psl]).astype(BF16))
        if need_y:
            ms = []
            for hd in (2 * pidx, 2 * pidx + 1):
                seg = jnp.broadcast_to(ctx["acs"][:, hd:hd + 1], (q, LANES)) - ctx["acs_t"][hd:hd + 1, :]
                lm = jnp.exp(jnp.where(ctx["keep"], seg, -jnp.inf))
                ms.append((cb * lm * ctx["dt_t"][hd:hd + 1, :]).astype(BF16))
            ops["m2"].append(jnp.concatenate(ms, axis=1))
            zero = jnp.zeros_like(xp)
            ops["xs2"].append(jnp.concatenate([jnp.where(left, xp, zero), jnp.where(left, zero, xp)], axis=0))
    ops["xw"] = [jnp.concatenate(xws[2 * t:2 * t + 2], axis=1) for t in range(ppg // 2)]
    return ops


def _ssd_issue(ctx, ops, g, h_scr, y_ref, need_y):
    ppg = SSD_HEADS // 2 // SSD_GROUPS
    for t in range(ppg // 2):
        p0 = g * ppg + 2 * t
        qsl = slice(p0 * LANES, (p0 + 2) * LANES)
        hs = jnp.concatenate([h_scr[p0], h_scr[p0 + 1]], axis=1)
        if need_y:
            yd = jnp.concatenate([jnp.dot(ops["m2"][2 * t + e], ops["xs2"][2 * t + e], preferred_element_type=F32)
                                  for e in range(2)], axis=1)
            yoff = jnp.dot(ops["cg"], hs.astype(BF16), preferred_element_type=F32) * ctx["eacs_x"][:, qsl]
            y_ref[:, qsl] = (yd + yoff).astype(y_ref.dtype)
        st = jnp.dot(ops["bgt"], ops["xw"][t], preferred_element_type=F32)
        hn = hs * ctx["etot_x"][:, qsl] + st
        h_scr[p0] = hn[:, :LANES]
        h_scr[p0 + 1] = hn[:, LANES:]


def _ssd_scan_kernel(xsf_ref, xsb_ref, bmf_ref, bmb_ref, cmf_ref, cmb_ref, dtf_ref, dtb_ref, a_ref, tri_ref, e_ref,
                     h0_ref, *out_refs, nc, need_y):
    yf_ref, yb_ref = out_refs[:2] if need_y else (None, None)
    hfin_ref, h_scr = out_refs[-2:]
    s = pl.program_id(0)

    @pl.when(s == 0)
    def _():
        h_scr[...] = h0_ref[...]

    dirs = ((xsf_ref, bmf_ref, cmf_ref, dtf_ref, yf_ref), (xsb_ref, bmb_ref, cmb_ref, dtb_ref, yb_ref))
    ctxs = [_ssd_prologue(dirs[d][3], a_ref[d], tri_ref[d], e_ref, need_y) for d in range(2)]
    stages = [(d, g) for g in range(SSD_GROUPS) for d in range(2)]
    prep = lambda d, g: _ssd_prepare(ctxs[d], dirs[d][0], dirs[d][1], dirs[d][2], g, need_y)
    pending = prep(*stages[0])
    for idx, (d, g) in enumerate(stages):
        ops = pending
        if idx + 1 < len(stages):
            pending = prep(*stages[idx + 1])
        _ssd_issue(ctxs[d], ops, g, h_scr.at[d], dirs[d][4], need_y)

    @pl.when(s == nc - 1)
    def _():
        hfin_ref[...] = h_scr[...]


def _ssd_scan(xs, bm, cm, dt2, a_log, h0, need_y):
    lx = xs.shape[0]
    q = SSD_CHUNK
    nc = lx // q
    npair = SSD_HEADS // 2
    a = -jnp.exp(a_log.astype(F32))
    a_pad = jnp.pad(a, ((0, 0), (0, LANES - SSD_HEADS))).reshape(2, 1, LANES)
    lower = np.tril(np.ones((q, q), np.float32))
    tri = jnp.asarray(np.stack([lower, lower.T]))
    expand = jnp.asarray(np.kron(np.eye(LANES)[:, :SSD_HEADS], np.ones((1, SSD_P))), BF16)
    fwd = lambda s: s
    bwd = lambda s: nc - 1 - s
    rows = lambda w, idx: pl.BlockSpec((q, w), lambda s: (idx(s), 0))
    full = lambda a_: pl.BlockSpec(a_.shape, lambda s: (0,) * a_.ndim)
    y_shapes = [jax.ShapeDtypeStruct((lx, SSD_INNER), BF16)] * 2 if need_y else []
    y_specs = [rows(SSD_INNER, fwd), rows(SSD_INNER, bwd)] if need_y else []
    outs = pl.pallas_call(
        functools.partial(_ssd_scan_kernel, nc=nc, need_y=need_y),
        out_shape=y_shapes + [jax.ShapeDtypeStruct(h0.shape, F32)],
        grid=(nc,),
        in_specs=[
            rows(SSD_INNER, fwd), rows(SSD_INNER, bwd), rows(SSD_BC, fwd), rows(SSD_BC, bwd),
            rows(SSD_BC, fwd), rows(SSD_BC, bwd),
            pl.BlockSpec((1, q, LANES), lambda s: (0, s, 0)),
            pl.BlockSpec((1, q, LANES), lambda s: (1, nc - 1 - s, 0)),
            full(a_pad), full(tri), full(expand), full(h0),
        ],
        out_specs=y_specs + [full(h0)],
        scratch_shapes=[pltpu.VMEM((2, npair, SSD_STATE, 2 * SSD_P), F32)],
        compiler_params=_cparams("arbitrary"),
        name="ssd_scan",
    )(xs, xs, bm, bm, cm, cm, dt2, dt2, a_pad, tri, expand, h0)
    return (tuple(outs[:2]) if need_y else None), outs[-1]


def _ssd_out_kernel(x_ref, yf_ref, yb_ref, xs_ref, zg_ref, dsk_ref, ng_ref, w_ref, gate_ref, o_ref):
    y = yf_ref[...].astype(F32) + yb_ref[...].astype(F32) + xs_ref[...].astype(F32) * dsk_ref[...]
    y = y * _silu(zg_ref[...].astype(F32))
    gw = SSD_INNER // SSD_GROUPS
    parts = []
    for g in range(SSD_GROUPS):
        yg = y[:, g * gw:(g + 1) * gw]
        ms = jnp.mean(yg * yg, axis=-1, keepdims=True)
        parts.append(yg * lax.rsqrt(ms + NORM_EPS) * ng_ref[:, g * gw:(g + 1) * gw])
    yn = jnp.concatenate(parts, axis=1).astype(BF16)
    o_ref[...] = x_ref[...] + gate_ref[...] * jnp.dot(yn, w_ref[...], preferred_element_type=F32)


def _ssd_out(x, yfb, xs, zg, d_skip, norm_g, w_out, gate):
    lx = x.shape[0]
    tm = min(256, lx)
    rowblk = lambda w: pl.BlockSpec((tm, w), lambda i: (i, 0))
    vecw = pl.BlockSpec((1, SSD_INNER), lambda i: (0, 0))
    return pl.pallas_call(
        _ssd_out_kernel,
        out_shape=jax.ShapeDtypeStruct((lx, D), F32),
        grid=(lx // tm,),
        in_specs=[rowblk(D), rowblk(SSD_INNER), rowblk(SSD_INNER), rowblk(SSD_INNER),
                  rowblk(SSD_INNER), vecw, vecw, _resident((SSD_INNER, D)),
                  pl.BlockSpec((1, D), lambda i: (0, 0))],
        out_specs=rowblk(D),
        compiler_params=_cparams("parallel"),
        name="ssd_out",
    )(x, yfb[0], yfb[1], xs, zg, _row(jnp.repeat(d_skip, SSD_P)), _row(norm_g), w_out, _row(gate))


def kernel(x, c, ctx, c_ctx, norm1_g, norm2_g, mod_w, mod_b, ffn_w_in, ffn_w_out, final_g, gm_w_in, gm_ln_g, gm_ln_b, gm_ws, gm_bs, gm_w_out, at_w_qkv, at_q_g, at_k_g, at_w_out, hy_w_in, hy_conv_w, hy_conv_b, hy_filt_w1, hy_filt_b1, hy_filt_w2, hy_filt_b2, hy_filt_w3, hy_filt_freq, hy_skip, hy_w_out, ssd_w_in, ssd_conv_w, ssd_conv_b, ssd_a_log, ssd_dt_bias, ssd_d_skip, ssd_norm_g, ssd_w_out):
    batch, seq, _ = x.shape
    assert batch == 1, "kernels are written for a single sequence"
    nctx = ctx.shape[1]
    xl = x[0]
    z = ctx[0]
    mods = _modulation(c[0], c_ctx, mod_w, mod_b)
    bf = lambda w: w.astype(BF16)

    for i in range(DEPTH):
        m, j = i % 4, i // 4
        want_ctx = i < DEPTH - 1
        ml = [mods[i, 0, k * D:(k + 1) * D] for k in range(6)]
        mc = [mods[i, 1, k * D:(k + 1) * D] for k in range(6)]
        n1 = norm1_g[i]
        pend_l = pend_c = None
        if m == 0:
            p = (bf(gm_w_in[j]), gm_ln_g[j], gm_ln_b[j], bf(gm_ws[j]), gm_bs[j], bf(gm_w_out[j]))
            xl = _gmlp(xl, n1, ml[0], ml[1], ml[2], *p)
            if want_ctx:
                z = _gmlp(z, n1, mc[0], mc[1], mc[2], *p)
        elif m == 1:
            wq, wo = bf(at_w_qkv[j]), bf(at_w_out[j])
            qt_l, k_l, vt_l = _qkv(xl, n1, ml[0], ml[1], wq, at_q_g[j], at_k_g[j], rope=True)
            qt_c, k_c, vt_c = _qkv(z, n1, mc[0], mc[1], wq, at_q_g[j], at_k_g[j], rope=False)
            k_all = jnp.concatenate([k_c, k_l], axis=1)
            vt_all = jnp.concatenate([vt_c, vt_l], axis=1)
            stot = nctx + seq
            ts = next(t for t in (3328, 1280, 1024, 512, 256) if stot % t == 0)
            score_bound = (HD ** 0.5 * LOG2E) * jnp.max(jnp.abs(at_q_g[j])) * jnp.max(jnp.abs(at_k_g[j]))
            o_l = lax.cond(score_bound <= FLASH_SCORE_BOUND,
                           lambda: _flash(qt_l, k_all, vt_all, stot, ts, bounded=True),
                           lambda: _flash(qt_l, k_all, vt_all, stot, ts, bounded=False))
            pend_l = ("proj", o_l, wo, ml[2])
            if want_ctx:
                pend_c = ("proj", _flash(qt_c, k_all, vt_all, nctx, nctx), wo, mc[2])
        elif m == 2:
            p = (bf(hy_w_in[j]), hy_conv_w[j], hy_conv_b[j], hy_filt_w1[j], hy_filt_b1[j], hy_filt_w2[j],
                 hy_filt_b2[j], hy_filt_w3[j], hy_filt_freq[j], hy_skip[j], bf(hy_w_out[j]))
            pend_l = _hyena(xl, n1, ml[0], ml[1], ml[2], *p)
            if want_ctx:
                pend_c = _hyena(z, n1, mc[0], mc[1], mc[2], *p)
        else:
            win, wo = bf(ssd_w_in[j]), bf(ssd_w_out[j])
            pin = (win, ssd_conv_w[j], ssd_conv_b[j], ssd_dt_bias[j])
            zg_c, xs_c, bm_c, cm_c, dt_c = _ssd_in(z, n1, mc[0], mc[1], *pin)
            zg_l, xs_l, bm_l, cm_l, dt_l = _ssd_in(xl, n1, ml[0], ml[1], *pin)
            h0 = jnp.zeros((2, SSD_HEADS // 2, SSD_STATE, 2 * SSD_P), F32)
            y_c, h_ctx = _ssd_scan(xs_c, bm_c, cm_c, dt_c, ssd_a_log[j], h0, want_ctx)
            y_l, _ = _ssd_scan(xs_l, bm_l, cm_l, dt_l, ssd_a_log[j], h_ctx, True)
            xl = _ssd_out(xl, y_l, xs_l, zg_l, ssd_d_skip[j], ssd_norm_g[j], wo, ml[2])
            if want_ctx:
                z = _ssd_out(z, y_c, xs_c, zg_c, ssd_d_skip[j], ssd_norm_g[j], wo, mc[2])
        wi, wo2 = bf(ffn_w_in[i]), bf(ffn_w_out[i])
        xl = _ffn(xl, pend_l, norm2_g[i], ml[3], ml[4], ml[5], wi, wo2, final_g, final=(i == DEPTH - 1))
        if want_ctx:
            z = _ffn(z, pend_c, norm2_g[i], mc[3], mc[4], mc[5], wi, wo2, final_g, final=False)
    return xl[None]
```

```python
import functools
import math

import numpy as np
import jax
import jax.numpy as jnp
from jax import lax
from jax.experimental import pallas as pl
from jax.experimental.pallas import tpu as pltpu

F32 = jnp.float32
BF16 = jnp.bfloat16
HIGHEST = lax.Precision.HIGHEST

D = 1024
DEPTH = 4
GRID_W = 64
NORM_EPS = 1e-6
FFN_HIDDEN = 2816
GM_CHUNK = 128
GM_WIDTH = 2 * D
GM_GROUPS = 8
GM_GW = GM_WIDTH // GM_GROUPS
HD = 64
QH = D // HD
KVH = 4
ROPE_THETA = 10000.0
LOG2E = math.log2(math.e)
FLASH_SCORE_BOUND = 30.0
FLASH_LOOKAHEAD = 2
HY_BANDS = 16
HY_EMB = 1 + 2 * HY_BANDS
HY_FILT_W = 64
HY_MAX_DECAY = math.log(1e-2) / 0.3
HY_MIN_DECAY = math.log(1e-2) / 1.5
SSD_INNER = 2 * D
SSD_P = 64
SSD_HEADS = SSD_INNER // SSD_P
SSD_GROUPS = 4
SSD_STATE = 128
SSD_CHUNK = 128
SSD_BC = SSD_GROUPS * SSD_STATE
SSD_CONV_DIM = SSD_INNER + 2 * SSD_BC

LANES = 128
SUBLANES = 8
VMEM_LIMIT_BYTES = 56 * 1024 * 1024
DFT_N2 = 128
DFT_SPLIT = 1


def _cparams(*sem):
    return pltpu.CompilerParams(dimension_semantics=sem, vmem_limit_bytes=VMEM_LIMIT_BYTES)


def _row(v):
    return v.reshape(1, -1)


def _normmod(x, g, shift, scale):
    ms = jnp.mean(x * x, axis=-1, keepdims=True)
    return x * lax.rsqrt(ms + NORM_EPS) * g * (1.0 + scale) + shift


def _silu(x):
    return x * jax.nn.sigmoid(x)


def _mod_kernel(cl_ref, cc_ref, w_ref, b_ref, o_ref):
    w = w_ref[0]
    for r, c_ref in enumerate((cl_ref, cc_ref)):
        a = _silu(c_ref[...])
        o_ref[0, r:r + 1, :] = jnp.sum(a * w, axis=0, keepdims=True) + b_ref[0]


def _modulation(c, c_ctx, mod_w, mod_b):
    tn = 1536
    n6 = 6 * D
    depth = mod_w.shape[0]
    return pl.pallas_call(
        _mod_kernel,
        out_shape=jax.ShapeDtypeStruct((depth, 2, n6), F32),
        grid=(depth, n6 // tn),
        in_specs=[
            pl.BlockSpec((D, 1), lambda i, n: (0, 0)),
            pl.BlockSpec((D, 1), lambda i, n: (0, 0)),
            pl.BlockSpec((1, D, tn), lambda i, n: (i, 0, n)),
            pl.BlockSpec((1, 1, tn), lambda i, n: (i, 0, n)),
        ],
        out_specs=pl.BlockSpec((1, 2, tn), lambda i, n: (i, 0, n)),
        compiler_params=_cparams("parallel", "parallel"),
        name="modulation",
    )(c.reshape(D, 1), c_ctx.reshape(D, 1), mod_w, mod_b.reshape(depth, 1, n6))


def _ffn_kernel(x_ref, *refs, mode, final):
    x = x_ref[...]
    if mode == "proj":
        a_ref, wm_ref, g1_ref = refs[:3]
        refs = refs[3:]
        x = x + g1_ref[...] * jnp.dot(a_ref[...], wm_ref[...], preferred_element_type=F32)
    elif mode == "hyena":
        x0_ref, yt_ref, wm_ref, g1_ref = refs[:4]
        refs = refs[4:]
        a = (x0_ref[...] * yt_ref[...].T).astype(BF16)
        x = x + g1_ref[...] * jnp.dot(a, wm_ref[...], preferred_element_type=F32)
    g_ref, sh_ref, sc_ref, gate_ref, wi_ref, wo_ref, fg_ref, o_ref = refs
    h = _normmod(x, g_ref[...], sh_ref[...], sc_ref[...]).astype(BF16)
    a = jnp.dot(h, wi_ref[:, :FFN_HIDDEN], preferred_element_type=F32)
    u = jnp.dot(h, wi_ref[:, FFN_HIDDEN:], preferred_element_type=F32)
    act = (_silu(a) * u).astype(BF16)
    y = x + gate_ref[...] * jnp.dot(act, wo_ref[...], preferred_element_type=F32)
    if final:
        ms = jnp.mean(y * y, axis=-1, keepdims=True)
        y = y * lax.rsqrt(ms + NORM_EPS) * fg_ref[...]
    o_ref[...] = y


def _resident(shape):
    return pl.BlockSpec(shape, lambda *_: (0,) * len(shape), pipeline_mode=pl.Buffered(1))


def _ffn(x, pending, g, sh, sc, gate, w_in, w_out, final_g, final):
    lx = x.shape[0]
    tm = min(512, lx)
    vec = pl.BlockSpec((1, D), lambda i: (0, 0))
    rows = pl.BlockSpec((tm, D), lambda i: (i, 0))
    mode, pre_args, pre_specs = "none", (), []
    if pending is not None:
        mode = pending[0]
        if mode == "proj":
            _, a, wm, g1 = pending
            pre_args, pre_specs = (a, wm, _row(g1)), [rows, _resident(wm.shape), vec]
        else:
            _, x0, yt, wm, g1 = pending
            pre_args = (x0, yt, wm, _row(g1))
            pre_specs = [rows, pl.BlockSpec((D, tm), lambda i: (0, i)), _resident(wm.shape), vec]
    return pl.pallas_call(
        functools.partial(_ffn_kernel, mode=mode, final=final),
        out_shape=jax.ShapeDtypeStruct((lx, D), F32),
        grid=(lx // tm,),
        in_specs=[rows] + pre_specs + [
            vec, vec, vec, vec,
            _resident((D, 2 * FFN_HIDDEN)),
            _resident((FFN_HIDDEN, D)),
            vec,
        ],
        out_specs=rows,
        compiler_params=_cparams("parallel"),
        name="ffn",
    )(x, *pre_args, _row(g), _row(sh), _row(sc), _row(gate), w_in, w_out, _row(final_g))


def _gmlp_kernel(x_ref, g_ref, sh_ref, sc_ref, gate_ref, win_ref, lng_ref, lnb_ref, ws_ref, bs_ref, wout_ref,
                 o_ref, *, tm):
    nsub = max(1, tm // (2 * GM_CHUNK))
    rs = tm // nsub

    def project(i):
        x = x_ref[i * rs:(i + 1) * rs, :]
        h = _normmod(x, g_ref[...], sh_ref[...], sc_ref[...]).astype(BF16)
        return jnp.dot(h, win_ref[...], preferred_element_type=F32)

    def mix(i, t):
        t = 0.5 * t * (1.0 + lax.erf(t * (1.0 / math.sqrt(2.0))))
        u = t[:, :GM_WIDTH]
        v = t[:, GM_WIDTH:]
        mu = jnp.mean(v, axis=-1, keepdims=True)
        vc = v - mu
        var = jnp.mean(vc * vc, axis=-1, keepdims=True)
        v = (vc * lax.rsqrt(var + NORM_EPS) * lng_ref[...] + lnb_ref[...]).astype(BF16)
        rows = []
        for q in range(rs // GM_CHUNK):
            cols = []
            for gidx in range(GM_GROUPS):
                vq = v[q * GM_CHUNK:(q + 1) * GM_CHUNK, gidx * GM_GW:(gidx + 1) * GM_GW]
                bias = bs_ref[gidx]
                m = jnp.dot(ws_ref[gidx], vq, preferred_element_type=F32)
                cols.append(m + jnp.concatenate([bias] * (GM_GW // LANES), axis=1))
            rows.append(jnp.concatenate(cols, axis=1))
        gated = (u * jnp.concatenate(rows, axis=0)).astype(BF16)
        y = jnp.dot(gated, wout_ref[...], preferred_element_type=F32)
        o_ref[i * rs:(i + 1) * rs, :] = x_ref[i * rs:(i + 1) * rs, :] + gate_ref[...] * y

    t_next = project(0)
    for i in range(nsub):
        t = t_next
        if i + 1 < nsub:
            t_next = project(i + 1)
        mix(i, t)


def _gmlp(x, g, sh, sc, gate, w_in, ln_g, ln_b, ws, bs, w_out):
    lx = x.shape[0]
    tm = min(512, lx)
    vec = pl.BlockSpec((1, D), lambda i: (0, 0))
    vecw = pl.BlockSpec((1, GM_WIDTH), lambda i: (0, 0))
    bsb = jnp.broadcast_to(bs[:, :, None], (GM_GROUPS, GM_CHUNK, LANES))
    return pl.pallas_call(
        functools.partial(_gmlp_kernel, tm=tm),
        out_shape=jax.ShapeDtypeStruct((lx, D), F32),
        grid=(lx // tm,),
        in_specs=[
            pl.BlockSpec((tm, D), lambda i: (i, 0)),
            vec, vec, vec, vec,
            _resident((D, 2 * GM_WIDTH)),
            vecw, vecw,
            pl.BlockSpec((GM_GROUPS, GM_CHUNK, GM_CHUNK), lambda i: (0, 0, 0)),
            pl.BlockSpec((GM_GROUPS, GM_CHUNK, LANES), lambda i: (0, 0, 0)),
            _resident((GM_WIDTH, D)),
        ],
        out_specs=pl.BlockSpec((tm, D), lambda i: (i, 0)),
        compiler_params=_cparams("parallel"),
        name="gmlp",
    )(x, _row(g), _row(sh), _row(sc), _row(gate), w_in, _row(ln_g), _row(ln_b), ws, bsb, w_out)


def _group_sumsq(t, e_ref):
    sq = t * t
    hi = sq.astype(BF16)
    lo = (sq - hi.astype(F32)).astype(BF16)
    e = e_ref[...]
    w = e.shape[0]
    outs = []
    for j in range(t.shape[1] // w):
        sl = slice(j * w, (j + 1) * w)
        outs.append(jnp.dot(hi[:, sl], e, preferred_element_type=F32) + jnp.dot(lo[:, sl], e, preferred_element_type=F32))
    return jnp.concatenate(outs, axis=1)


def _rope(t, cosf, sinf):
    w = t.shape[1]
    lane = lax.broadcasted_iota(jnp.int32, t.shape, 1)
    first = (lane % HD) < (HD // 2)
    partner = jnp.where(first, pltpu.roll(t, w - HD // 2, axis=1), pltpu.roll(t, HD // 2, axis=1))
    reps = w // LANES
    c = jnp.concatenate([cosf] * reps, axis=1)
    s = jnp.concatenate([sinf] * reps, axis=1)
    return t * c + partner * s


def _qkv_kernel(x_ref, g_ref, sh_ref, sc_ref, w_ref, qg_ref, kg_ref, e_ref, cos_ref, sin_ref,
                qt_ref, k_ref, vt_ref, *, rope):
    h = _normmod(x_ref[...], g_ref[...], sh_ref[...], sc_ref[...]).astype(BF16)
    qkv = jnp.dot(h, w_ref[...], preferred_element_type=F32)
    q = qkv[:, :D]
    k = qkv[:, D:D + KVH * HD]
    v = qkv[:, D + KVH * HD:]
    q = q * lax.rsqrt(_group_sumsq(q, e_ref) * (1.0 / HD) + NORM_EPS) * qg_ref[...]
    k = k * lax.rsqrt(_group_sumsq(k, e_ref) * (1.0 / HD) + NORM_EPS) * kg_ref[...]
    if rope:
        q = _rope(q, cos_ref[...], sin_ref[...])
        k = _rope(k, cos_ref[...], sin_ref[...])
    qt_ref[...] = (q * (HD ** -0.5 * LOG2E)).T.astype(BF16)
    for gidx in range(KVH):
        k_ref[gidx] = k[:, gidx * HD:(gidx + 1) * HD].astype(BF16)
    vt_ref[...] = v.T.astype(BF16)


def _qkv(x, g, sh, sc, w_qkv, q_g, k_g, rope):
    lx = x.shape[0]
    tm = min(512, lx)
    vec = pl.BlockSpec((1, D), lambda i: (0, 0))
    kvw = KVH * HD
    rows = lx // GRID_W
    row = jnp.repeat(jnp.arange(rows, dtype=F32), GRID_W)
    col = jnp.tile(jnp.arange(GRID_W, dtype=F32), rows)
    n = HD // 4
    inv = ROPE_THETA ** (-jnp.arange(n, dtype=F32) / n)
    ang = jnp.concatenate([row[:, None] * inv, col[:, None] * inv], axis=-1)
    cos, sin = jnp.cos(ang), jnp.sin(ang)
    cosf = jnp.tile(jnp.concatenate([cos, cos], axis=-1), (1, LANES // HD))
    sinf = jnp.tile(jnp.concatenate([-sin, sin], axis=-1), (1, LANES // HD))
    eblk = jnp.asarray(np.kron(np.eye(kvw // HD), np.ones((HD, HD))), BF16)
    tab = pl.BlockSpec((tm, LANES), lambda i: (i, 0))
    return pl.pallas_call(
        functools.partial(_qkv_kernel, rope=rope),
        out_shape=(jax.ShapeDtypeStruct((D, lx), BF16),
                   jax.ShapeDtypeStruct((KVH, lx, HD), BF16),
                   jax.ShapeDtypeStruct((kvw, lx), BF16)),
        grid=(lx // tm,),
        in_specs=[
            pl.BlockSpec((tm, D), lambda i: (i, 0)),
            vec, vec, vec,
            _resident((D, D + 2 * kvw)),
            vec,
            pl.BlockSpec((1, kvw), lambda i: (0, 0)),
            pl.BlockSpec((kvw, kvw), lambda i: (0, 0)),
            tab, tab,
        ],
        out_specs=(pl.BlockSpec((D, tm), lambda i: (0, i)),
                   pl.BlockSpec((KVH, tm, HD), lambda i: (0, i, 0)),
                   pl.BlockSpec((kvw, tm), lambda i: (0, i))),
        compiler_params=_cparams("parallel"),
        name="qkv_proj",
    )(x, _row(g), _row(sh), _row(sc), w_qkv, _row(jnp.tile(q_g, QH)), _row(jnp.tile(k_g, KVH)), eblk, cosf, sinf)


def _flash_kernel(qt_ref, k_ref, vt_ref, o_ref, qg_scr, m_scr, l_scr, acc_scr, *, tq, ts, tc, nkv, bounded):
    j = pl.program_id(1)
    gq = QH // KVH
    mcols = gq * tq

    @pl.when(j == 0)
    def _():
        for h in range(QH):
            qg_scr[h // gq, :, (h % gq) * tq:(h % gq + 1) * tq] = qt_ref[h * HD:(h + 1) * HD, :]
        m_scr[...] = jnp.full(m_scr.shape, -jnp.inf, F32)
        l_scr[...] = jnp.zeros_like(l_scr)
        acc_scr[...] = jnp.zeros_like(acc_scr)

    stages = [(g, c) for c in range(ts // tc) for g in range(KVH)]

    def scores(g, c):
        return jnp.dot(k_ref[g, c * tc:(c + 1) * tc, :], qg_scr[g], preferred_element_type=F32)

    pending = [scores(*st) for st in stages[:FLASH_LOOKAHEAD]]
    for idx, (g, c) in enumerate(stages):
        s = pending.pop(0)
        if idx + FLASH_LOOKAHEAD < len(stages):
            pending.append(scores(*stages[idx + FLASH_LOOKAHEAD]))
        vt = vt_ref[g * HD:(g + 1) * HD, c * tc:(c + 1) * tc]
        if bounded:
            p = jnp.exp2(s)
            l_scr[g] += jnp.sum(p, axis=0, keepdims=True)
            acc_scr[g] += jnp.dot(vt, p.astype(BF16), preferred_element_type=F32)
        else:
            m_prev = m_scr[g]
            m_new = jnp.maximum(m_prev, jnp.max(s, axis=0, keepdims=True))
            alpha = jnp.exp2(m_prev - m_new)
            p = jnp.exp2(s - m_new)
            l_scr[g] = alpha * l_scr[g] + jnp.sum(p, axis=0, keepdims=True)
            acc_scr[g] = alpha * acc_scr[g] + jnp.dot(vt, p.astype(BF16), preferred_element_type=F32)
            m_scr[g] = m_new

    @pl.when(j == nkv - 1)
    def _():
        rows = []
        for g in range(KVH):
            o = acc_scr[g] / l_scr[g]
            rows += [o[:, r * tq:(r + 1) * tq] for r in range(gq)]
        o_ref[...] = jnp.concatenate(rows, axis=0).T.astype(o_ref.dtype)


def _flash(qt, k, vt, s_len, ts, bounded=False):
    lq = qt.shape[1]
    tq = min(128, lq)
    nkv = s_len // ts
    gq = QH // KVH
    kvw = KVH * HD
    tc = 2 * LANES if ts % (2 * LANES) == 0 else LANES
    return pl.pallas_call(
        functools.partial(_flash_kernel, tq=tq, ts=ts, tc=tc, nkv=nkv, bounded=bounded),
        out_shape=jax.ShapeDtypeStruct((lq, D), BF16),
        grid=(lq // tq, nkv),
        in_specs=[
            pl.BlockSpec((D, tq), lambda i, j: (0, i)),
            pl.BlockSpec((KVH, ts, HD), lambda i, j: (0, j, 0)),
            pl.BlockSpec((kvw, ts), lambda i, j: (0, j)),
        ],
        out_specs=pl.BlockSpec((tq, D), lambda i, j: (i, 0)),
        scratch_shapes=[
            pltpu.VMEM((KVH, HD, gq * tq), BF16),
            pltpu.VMEM((KVH, 1, gq * tq), F32),
            pltpu.VMEM((KVH, 1, gq * tq), F32),
            pltpu.VMEM((KVH, HD, gq * tq), F32),
        ],
        compiler_params=_cparams("parallel", "arbitrary"),
        name="flash_attn",
    )(qt, k, vt)


def _halo_specs(tm, lx):
    nb = lx // SUBLANES
    step = tm // SUBLANES
    prev = pl.BlockSpec((SUBLANES, D), lambda i: (jnp.maximum(i * step - 1, 0), 0))
    nxt = pl.BlockSpec((SUBLANES, D), lambda i: (jnp.minimum((i + 1) * step, nb - 1), 0))
    return prev, nxt


def _conv3(p_main, p_halo, cw, cb, first, last):
    tm = p_main.shape[0]
    rid = lax.broadcasted_iota(jnp.int32, p_main.shape, 0)
    before = jnp.where(first, 0.0, p_halo[SUBLANES - 1:SUBLANES, :])
    after = jnp.where(last, 0.0, p_halo[SUBLANES:SUBLANES + 1, :])
    up = jnp.where(rid == 0, before, pltpu.roll(p_main, 1, axis=0))
    dn = jnp.where(rid == tm - 1, after, pltpu.roll(p_main, tm - 1, axis=0))
    return cw[0:1, :] * up + cw[1:2, :] * p_main + cw[2:3, :] * dn + cb


def _norm_halo(xm_ref, xp_ref, xn_ref, g_ref, sh_ref, sc_ref):
    g, sh, sc = g_ref[...], sh_ref[...], sc_ref[...]
    h = _normmod(xm_ref[...], g, sh, sc).astype(BF16)
    hh = jnp.concatenate([_normmod(xp_ref[...], g, sh, sc), _normmod(xn_ref[...], g, sh, sc)], axis=0).astype(BF16)
    return h, hh


def _hy_in_kernel(xm_ref, xp_ref, xn_ref, g_ref, sh_ref, sc_ref, w_ref, cw_ref, cb_ref, x0_ref, ut_ref, *, nt):
    i = pl.program_id(0)
    first, last = i == 0, i == nt - 1
    h, hh = _norm_halo(xm_ref, xp_ref, xn_ref, g_ref, sh_ref, sc_ref)

    def project(b):
        sl = slice(b * D, (b + 1) * D)
        return (jnp.dot(h, w_ref[:, sl], preferred_element_type=F32),
                jnp.dot(hh, w_ref[:, sl], preferred_element_type=F32))

    def conv(b, p):
        sl = slice(b * D, (b + 1) * D)
        return _conv3(p[0], p[1], cw_ref[:, sl], cb_ref[:, sl], first, last)

    p0 = project(0)
    p1 = project(1)
    x0_ref[...] = conv(0, p0)
    p2 = project(2)
    x1 = conv(1, p1)
    ut_ref[...] = (x1 * conv(2, p2)).T


def _hy_in(x, g, sh, sc, w_in, conv_w, conv_b):
    lx = x.shape[0]
    tm = min(512, lx)
    nt = lx // tm
    vec = pl.BlockSpec((1, D), lambda i: (0, 0))
    prev, nxt = _halo_specs(tm, lx)
    return pl.pallas_call(
        functools.partial(_hy_in_kernel, nt=nt),
        out_shape=(jax.ShapeDtypeStruct((lx, D), F32), jax.ShapeDtypeStruct((D, lx), F32)),
        grid=(nt,),
        in_specs=[
            pl.BlockSpec((tm, D), lambda i: (i, 0)), prev, nxt,
            vec, vec, vec,
            _resident((D, 3 * D)),
            pl.BlockSpec((3, 3 * D), lambda i: (0, 0)),
            pl.BlockSpec((1, 3 * D), lambda i: (0, 0)),
        ],
        out_specs=(pl.BlockSpec((tm, D), lambda i: (i, 0)), pl.BlockSpec((D, tm), lambda i: (0, i))),
        compiler_params=_cparams("parallel"),
        name="hyena_in",
    )(x, x, x, _row(g), _row(sh), _row(sc), w_in, conv_w, _row(conv_b))


def _hy_filter_kernel(ft_ref, t_ref, w1_ref, b1_ref, w2_ref, b2_ref, w3h_ref, w3l_ref, fr_ref, dl_ref, sk_ref, kt_ref,
                      *, tm, ltrue, lpad):
    i = pl.program_id(0)
    feats = ft_ref[...]
    fr = fr_ref[...]
    hid = jnp.sin(fr * (jnp.dot(w1_ref[...], feats, preferred_element_type=F32, precision=HIGHEST) + b1_ref[...]))
    hid = jnp.sin(fr * (jnp.dot(w2_ref[...], hid, preferred_element_type=F32, precision=HIGHEST) + b2_ref[...]))
    hh = hid.astype(BF16)
    hl = (hid - hh.astype(F32)).astype(BF16)
    w3h = w3h_ref[...]
    kt = (jnp.dot(w3h, hh, preferred_element_type=F32) + jnp.dot(w3h, hl, preferred_element_type=F32)
          + jnp.dot(w3l_ref[...], hh, preferred_element_type=F32))
    kt = kt * jnp.exp(-dl_ref[...] * t_ref[...])
    if lpad != ltrue:
        pos = lax.broadcasted_iota(jnp.int32, kt.shape, 1) + i * tm
        kt = jnp.where(pos < ltrue, kt, 0.0)
    kt_ref[...] = kt

    @pl.when(i == 0)
    def _():
        row = lax.broadcasted_iota(jnp.int32, (2 * D, 1), 0)
        kt_ref[:, 0:1] = jnp.where(row < D, kt[:, 0:1] + sk_ref[...], 0.0)


def _hy_filter(ltrue, lpad, w1, b1, w2, b2, w3, freq, skip):
    tm = min(512, lpad)
    col = lambda v_: v_.reshape(-1, 1)
    w1t = jnp.pad(w1.T, ((0, 0), (0, LANES - HY_EMB)))
    deltas = jnp.abs(jnp.linspace(HY_MIN_DECAY, HY_MAX_DECAY, D, dtype=F32))
    pos = jnp.arange(lpad, dtype=F32)
    zf = jnp.linspace(1e-4, HY_BANDS - 1, HY_BANDS, dtype=F32)[:, None] * (2.0 * math.pi * pos / ltrue)[None, :]
    feats = jnp.concatenate([(pos / (ltrue - 1))[None, :], jnp.cos(zf), -jnp.sin(zf),
                             jnp.zeros((LANES - HY_EMB, lpad), F32)], axis=0)
    w3t = w3.T
    w3h = w3t.astype(BF16)
    w3l = (w3t - w3h.astype(F32)).astype(BF16)
    full = lambda a: pl.BlockSpec(a.shape, lambda i: (0,) * a.ndim)
    args = (w1t, col(b1), w2.T, col(b2), w3h, w3l, col(freq), col(jnp.tile(deltas, 2)),
            col(jnp.concatenate([skip, jnp.zeros((D,), F32)])))
    return pl.pallas_call(
        functools.partial(_hy_filter_kernel, tm=tm, ltrue=ltrue, lpad=lpad),
        out_shape=jax.ShapeDtypeStruct((2 * D, lpad), F32),
        grid=(lpad // tm,),
        in_specs=[pl.BlockSpec((LANES, tm), lambda i: (0, i)), pl.BlockSpec((1, tm), lambda i: (0, i))]
                 + [full(a) for a in args],
        out_specs=pl.BlockSpec((2 * D, tm), lambda i: (0, i)),
        compiler_params=_cparams("parallel"),
        name="hyena_filter",
    )(feats, feats[0:1, :], *args)


def _dft_consts(nh):
    n1 = 2 * nh
    n = n1 * DFT_N2
    k1 = np.arange(n1)[:, None].astype(np.float64)
    a1 = 2.0 * np.pi * k1 * np.arange(nh)[None, :] / n1
    f1 = np.concatenate([np.cos(a1), -np.sin(a1)], axis=0)
    at = 2.0 * np.pi * ((np.arange(n1)[:, None] * np.arange(DFT_N2)[None, :]) % n) / n
    a2 = 2.0 * np.pi * ((np.arange(DFT_N2)[:, None] * np.arange(DFT_N2)[None, :]) % DFT_N2) / DFT_N2
    c2, s2 = np.cos(a2), np.sin(a2)
    f2 = np.block([[c2, -s2], [s2, c2]])
    g2 = np.block([[c2, s2], [-s2, c2]])
    g1 = np.concatenate([np.cos(a1).T, -np.sin(a1).T], axis=1) / n
    as32 = lambda a: jnp.asarray(a, F32)

    def parts(a):
        hi = a.astype(BF16)
        if DFT_SPLIT == 1:
            return (jnp.asarray(hi),)
        return (jnp.asarray(hi), jnp.asarray((a - hi.astype(np.float64)).astype(BF16)))

    return parts(f1), as32(np.cos(at)), as32(np.sin(at)), parts(f2), parts(g2), parts(g1)


def _split_bf16(a):
    hi = a.astype(BF16)
    if DFT_SPLIT == 1:
        return (hi,)
    return (hi, (a - hi.astype(F32)).astype(BF16))


def _split_dot(a, b):
    out = jnp.dot(a[0], b[0], preferred_element_type=F32)
    if DFT_SPLIT > 1:
        out = out + jnp.dot(a[1], b[0], preferred_element_type=F32) + jnp.dot(a[0], b[1], preferred_element_type=F32)
    return out


def _load_parts(refs):
    return tuple(r[...] for r in refs)


def _hy_conv_kernel(x_ref, kf_ref, kb_ref, *refs, cb, n1):
    ns = DFT_SPLIT
    f1, (twc_ref, tws_ref), f2 = refs[:ns], refs[ns:ns + 2], refs[ns + 2:2 * ns + 2]
    g2, g1, o_ref = refs[2 * ns + 2:3 * ns + 2], refs[3 * ns + 2:4 * ns + 2], refs[4 * ns + 2]
    twc, tws = twc_ref[...], tws_ref[...]
    f1, f2, g2, g1 = _load_parts(f1), _load_parts(f2), _load_parts(g2), _load_parts(g1)
    hc = cb // 2
    nt = 3 * hc

    def dft1(half):
        xs = [r[half * hc + c] for r in (x_ref, kf_ref, kb_ref) for c in range(hc)]
        return _split_dot(f1, _split_bf16(jnp.concatenate(xs, axis=1)))

    def twiddle_rows(a):
        rows = []
        for t in range(nt):
            ar = a[:n1, t * DFT_N2:(t + 1) * DFT_N2]
            ai = a[n1:, t * DFT_N2:(t + 1) * DFT_N2]
            rows.append(jnp.concatenate([ar * twc + ai * tws, ai * twc - ar * tws], axis=1))
        return jnp.concatenate(rows, axis=0)

    def dft2(rows):
        return _split_dot(_split_bf16(rows), f2)

    def product(spec_all):
        rows = hc * n1
        spec, hf, hb = spec_all[:rows], spec_all[rows:2 * rows], spec_all[2 * rows:]
        hr = hf[:, :DFT_N2] + hb[:, :DFT_N2]
        hi = hf[:, DFT_N2:] - hb[:, DFT_N2:]
        xr, xi = spec[:, :DFT_N2], spec[:, DFT_N2:]
        return jnp.concatenate([xr * hr - xi * hi, xr * hi + xi * hr], axis=1)

    def idft2(y):
        return _split_dot(_split_bf16(y), g2)

    def twiddle_cols(b):
        cols = []
        for c in range(hc):
            br = b[c * n1:(c + 1) * n1, :DFT_N2]
            bi = b[c * n1:(c + 1) * n1, DFT_N2:]
            cols.append(jnp.concatenate([br * twc - bi * tws, bi * twc + br * tws], axis=0))
        return jnp.concatenate(cols, axis=1)

    def idft1(cols):
        return _split_dot(g1, _split_bf16(cols))

    def store(half, out):
        for c in range(hc):
            o_ref[half * hc + c] = out[:, c * DFT_N2:(c + 1) * DFT_N2]

    a0 = dft1(0)
    a1 = dft1(1)
    s0 = dft2(twiddle_rows(a0))
    s1 = dft2(twiddle_rows(a1))
    b0 = idft2(product(s0))
    b1 = idft2(product(s1))
    o0 = idft1(twiddle_cols(b0))
    o1 = idft1(twiddle_cols(b1))
    store(0, o0)
    store(1, o1)


def _hy_longconv(ut, kt, lpad):
    nh = lpad // DFT_N2
    n1 = 2 * nh
    cb = max(16, min(64, 2048 // n1))
    f1, twc, tws, f2, g2, g1 = _dft_consts(nh)
    consts = f1 + (twc, tws) + f2 + g2 + g1
    full = lambda a: pl.BlockSpec(a.shape, lambda i: (0,) * a.ndim)
    k3 = kt.reshape(2 * D, nh, DFT_N2)
    u3 = ut.reshape(D, nh, DFT_N2)
    nb = D // cb
    y3 = pl.pallas_call(
        functools.partial(_hy_conv_kernel, cb=cb, n1=n1),
        out_shape=jax.ShapeDtypeStruct((D, nh, DFT_N2), F32),
        grid=(nb,),
        in_specs=[pl.BlockSpec((cb, nh, DFT_N2), lambda i: (i, 0, 0)),
                  pl.BlockSpec((cb, nh, DFT_N2), lambda i: (i, 0, 0)),
                  pl.BlockSpec((cb, nh, DFT_N2), lambda i: (i + nb, 0, 0))]
                 + [full(a) for a in consts],
        out_specs=pl.BlockSpec((cb, nh, DFT_N2), lambda i: (i, 0, 0)),
        compiler_params=_cparams("parallel"),
        name="hyena_longconv",
    )(u3, k3, k3, *consts)
    return y3.reshape(D, lpad)


def _hyena(x, g, sh, sc, gate, w_in, conv_w, conv_b, w1, b1, w2, b2, w3, freq, skip, w_out):
    lx = x.shape[0]
    lpad = max(lx, SUBLANES * DFT_N2)
    x0, ut = _hy_in(x, g, sh, sc, w_in, conv_w, conv_b)
    if lpad != lx:
        ut = jnp.pad(ut, ((0, 0), (0, lpad - lx)))
    kt = _hy_filter(lx, lpad, w1, b1, w2, b2, w3, freq, skip)
    yt = _hy_longconv(ut, kt, lpad)[:, :lx]
    return ("hyena", x0, yt, w_out, gate)


def _ssd_in_kernel(xm_ref, xp_ref, xn_ref, g_ref, sh_ref, sc_ref, w_ref, wd_ref, cw_ref, cb_ref, db_ref,
                   zg_ref, xs_ref, bm_ref, cm_ref, dt_ref, *, nt):
    i = pl.program_id(0)
    first, last = i == 0, i == nt - 1
    h, hh = _norm_halo(xm_ref, xp_ref, xn_ref, g_ref, sh_ref, sc_ref)
    cw = SSD_BC
    nchunk = SSD_CONV_DIM // cw

    def project(b):
        sl = slice(SSD_INNER + b * cw, SSD_INNER + (b + 1) * cw)
        return (jnp.dot(h, w_ref[:, sl], preferred_element_type=F32),
                jnp.dot(hh, w_ref[:, sl], preferred_element_type=F32))

    def conv(b, p):
        sl = slice(b * cw, (b + 1) * cw)
        y = _silu(_conv3(p[0], p[1], cw_ref[:, sl], cb_ref[:, sl], first, last))
        if (b + 1) * cw <= SSD_INNER:
            xs_ref[:, sl] = y.astype(xs_ref.dtype)
        elif b == nchunk - 2:
            bm_ref[...] = y
        else:
            cm_ref[...] = y

    pending = [project(0), project(1)]
    for b in range(nchunk):
        p = pending.pop(0)
        if b + 2 < nchunk:
            pending.append(project(b + 2))
        conv(b, p)
    zg_ref[...] = jnp.dot(h, w_ref[:, :SSD_INNER], preferred_element_type=F32).astype(zg_ref.dtype)
    dt = jnp.dot(h, wd_ref[...], preferred_element_type=F32) + db_ref[...]
    dt = jnp.maximum(dt, 0.0) + jnp.log1p(jnp.exp(-jnp.abs(dt)))
    lane = lax.broadcasted_iota(jnp.int32, dt.shape, 1)
    dt = jnp.where((lane % LANES) < SSD_HEADS, dt, 0.0)
    dt_ref[0] = dt[:, :LANES]
    dt_ref[1] = dt[:, LANES:]


def _ssd_in(x, g, sh, sc, w_in, conv_w, conv_b, dt_bias):
    lx = x.shape[0]
    tm = min(512, lx)
    nt = lx // tm
    vec = pl.BlockSpec((1, D), lambda i: (0, 0))
    prev, nxt = _halo_specs(tm, lx)
    wdt = w_in[:, SSD_INNER + SSD_CONV_DIM:]
    pad = LANES - SSD_HEADS
    wd = jnp.concatenate([jnp.pad(wdt[:, :SSD_HEADS], ((0, 0), (0, pad))),
                          jnp.pad(wdt[:, SSD_HEADS:], ((0, 0), (0, pad)))], axis=1)
    db = jnp.pad(dt_bias, ((0, 0), (0, pad))).reshape(1, 2 * LANES)
    full = lambda a: pl.BlockSpec(a.shape, lambda i: (0,) * a.ndim)
    rowblk = lambda w: pl.BlockSpec((tm, w), lambda i: (i, 0))
    return pl.pallas_call(
        functools.partial(_ssd_in_kernel, nt=nt),
        out_shape=(jax.ShapeDtypeStruct((lx, SSD_INNER), BF16), jax.ShapeDtypeStruct((lx, SSD_INNER), BF16),
                   jax.ShapeDtypeStruct((lx, SSD_BC), F32), jax.ShapeDtypeStruct((lx, SSD_BC), F32),
                   jax.ShapeDtypeStruct((2, lx, LANES), F32)),
        grid=(nt,),
        in_specs=[rowblk(D), prev, nxt, vec, vec, vec, _resident(w_in.shape), _resident(wd.shape),
                  pl.BlockSpec((3, SSD_CONV_DIM), lambda i: (0, 0)),
                  pl.BlockSpec((1, SSD_CONV_DIM), lambda i: (0, 0)),
                  pl.BlockSpec((1, 2 * LANES), lambda i: (0, 0))],
        out_specs=(rowblk(SSD_INNER), rowblk(SSD_INNER), rowblk(SSD_BC), rowblk(SSD_BC),
                   pl.BlockSpec((2, tm, LANES), lambda i: (0, i, 0))),
        compiler_params=_cparams("parallel"),
        name="ssd_in",
    )(x, x, x, _row(g), _row(sh), _row(sc), w_in, wd, conv_w, _row(conv_b), db)


def _expand_heads(arr, e_ref):
    hi = arr.astype(BF16)
    lo = (arr - hi.astype(F32)).astype(BF16)
    e = e_ref[...]
    return jnp.dot(hi, e, preferred_element_type=F32) + jnp.dot(lo, e, preferred_element_type=F32)


def _ssd_prologue(dt_ref, a_row, tri, e_ref, need_y):
    dt = dt_ref[0]
    a = dt * a_row
    acs = jnp.dot(tri, a, preferred_element_type=F32, precision=HIGHEST)
    total = jnp.sum(a, axis=0, keepdims=True)
    ctx = dict(keep=tri > 0.5, acs=acs)
    ctx["wend_x"] = _expand_heads(jnp.exp(total - acs) * dt, e_ref)
    ctx["etot_x"] = _expand_heads(jnp.broadcast_to(jnp.exp(total), (SUBLANES, LANES)), e_ref)[0:1, :]
    if need_y:
        ctx["eacs_x"] = _expand_heads(jnp.exp(acs), e_ref)
        ctx["acs_t"] = acs.T
        ctx["dt_t"] = dt.T
    return ctx


def _ssd_prepare(ctx, xs_ref, bm_ref, cm_ref, g, need_y):
    q = SSD_CHUNK
    ppg = SSD_HEADS // 2 // SSD_GROUPS
    bg = bm_ref[:, g * SSD_STATE:(g + 1) * SSD_STATE]
    ops = dict(cg=cm_ref[:, g * SSD_STATE:(g + 1) * SSD_STATE].astype(BF16), bgt=bg.T.astype(BF16), xs2=[], xw=[], m2=[])
    if need_y:
        cb = lax.dot_general(ops["cg"], bg.astype(BF16), (((1,), (1,)), ((), ())), preferred_element_type=F32)
        left = lax.broadcasted_iota(jnp.int32, (q, LANES), 1) < SSD_P
    xws = []
    for r in range(ppg):
        pidx = g * ppg + r
        psl = slice(pidx * LANES, (pidx + 1) * LANES)
        xp = xs_ref[:, psl]
        xws.append((xp.astype(F32) * ctx["wend_x"][:, psl]).astype(BF16))
        if need_y:
            ms = []
            for hd in (2 * pidx, 2 * pidx + 1):
                seg = jnp.broadcast_to(ctx["acs"][:, hd:hd + 1], (q, LANES)) - ctx["acs_t"][hd:hd + 1, :]
                lm = jnp.exp(jnp.where(ctx["keep"], seg, -jnp.inf))
                ms.append((cb * lm * ctx["dt_t"][hd:hd + 1, :]).astype(BF16))
            ops["m2"].append(jnp.concatenate(ms, axis=1))
            zero = jnp.zeros_like(xp)
            ops["xs2"].append(jnp.concatenate([jnp.where(left, xp, zero), jnp.where(left, zero, xp)], axis=0))
    ops["xw"] = [jnp.concatenate(xws[2 * t:2 * t + 2], axis=1) for t in range(ppg // 2)]
    return ops


def _ssd_issue(ctx, ops, g, h_scr, y_ref, need_y):
    ppg = SSD_HEADS // 2 // SSD_GROUPS
    for t in range(ppg // 2):
        p0 = g * ppg + 2 * t
        qsl = slice(p0 * LANES, (p0 + 2) * LANES)
        hs = jnp.concatenate([h_scr[p0], h_scr[p0 + 1]], axis=1)
        if need_y:
            yd = jnp.concatenate([jnp.dot(ops["m2"][2 * t + e], ops["xs2"][2 * t + e], preferred_element_type=F32)
                                  for e in range(2)], axis=1)
            yoff = jnp.dot(ops["cg"], hs.astype(BF16), preferred_element_type=F32) * ctx["eacs_x"][:, qsl]
            y_ref[:, qsl] = (yd + yoff).astype(y_ref.dtype)
        st = jnp.dot(ops["bgt"], ops["xw"][t], preferred_element_type=F32)
        hn = hs * ctx["etot_x"][:, qsl] + st
        h_scr[p0] = hn[:, :LANES]
        h_scr[p0 + 1] = hn[:, LANES:]


def _ssd_scan_kernel(xsf_ref, xsb_ref, bmf_ref, bmb_ref, cmf_ref, cmb_ref, dtf_ref, dtb_ref, a_ref, tri_ref, e_ref,
                     h0_ref, *out_refs, nc, need_y):
    yf_ref, yb_ref = out_refs[:2] if need_y else (None, None)
    hfin_ref, h_scr = out_refs[-2:]
    s = pl.program_id(0)

    @pl.when(s == 0)
    def _():
        h_scr[...] = h0_ref[...]

    dirs = ((xsf_ref, bmf_ref, cmf_ref, dtf_ref, yf_ref), (xsb_ref, bmb_ref, cmb_ref, dtb_ref, yb_ref))
    ctxs = [_ssd_prologue(dirs[d][3], a_ref[d], tri_ref[d], e_ref, need_y) for d in range(2)]
    stages = [(d, g) for g in range(SSD_GROUPS) for d in range(2)]
    prep = lambda d, g: _ssd_prepare(ctxs[d], dirs[d][0], dirs[d][1], dirs[d][2], g, need_y)
    pending = prep(*stages[0])
    for idx, (d, g) in enumerate(stages):
        ops = pending
        if idx + 1 < len(stages):
            pending = prep(*stages[idx + 1])
        _ssd_issue(ctxs[d], ops, g, h_scr.at[d], dirs[d][4], need_y)

    @pl.when(s == nc - 1)
    def _():
        hfin_ref[...] = h_scr[...]


def _ssd_scan(xs, bm, cm, dt2, a_log, h0, need_y):
    lx = xs.shape[0]
    q = SSD_CHUNK
    nc = lx // q
    npair = SSD_HEADS // 2
    a = -jnp.exp(a_log.astype(F32))
    a_pad = jnp.pad(a, ((0, 0), (0, LANES - SSD_HEADS))).reshape(2, 1, LANES)
    lower = np.tril(np.ones((q, q), np.float32))
    tri = jnp.asarray(np.stack([lower, lower.T]))
    expand = jnp.asarray(np.kron(np.eye(LANES)[:, :SSD_HEADS], np.ones((1, SSD_P))), BF16)
    fwd = lambda s: s
    bwd = lambda s: nc - 1 - s
    rows = lambda w, idx: pl.BlockSpec((q, w), lambda s: (idx(s), 0))
    full = lambda a_: pl.BlockSpec(a_.shape, lambda s: (0,) * a_.ndim)
    y_shapes = [jax.ShapeDtypeStruct((lx, SSD_INNER), BF16)] * 2 if need_y else []
    y_specs = [rows(SSD_INNER, fwd), rows(SSD_INNER, bwd)] if need_y else []
    outs = pl.pallas_call(
        functools.partial(_ssd_scan_kernel, nc=nc, need_y=need_y),
        out_shape=y_shapes + [jax.ShapeDtypeStruct(h0.shape, F32)],
        grid=(nc,),
        in_specs=[
            rows(SSD_INNER, fwd), rows(SSD_INNER, bwd), rows(SSD_BC, fwd), rows(SSD_BC, bwd),
            rows(SSD_BC, fwd), rows(SSD_BC, bwd),
            pl.BlockSpec((1, q, LANES), lambda s: (0, s, 0)),
            pl.BlockSpec((1, q, LANES), lambda s: (1, nc - 1 - s, 0)),
            full(a_pad), full(tri), full(expand), full(h0),
        ],
        out_specs=y_specs + [full(h0)],
        scratch_shapes=[pltpu.VMEM((2, npair, SSD_STATE, 2 * SSD_P), F32)],
        compiler_params=_cparams("arbitrary"),
        name="ssd_scan",
    )(xs, xs, bm, bm, cm, cm, dt2, dt2, a_pad, tri, expand, h0)
    return (tuple(outs[:2]) if need_y else None), outs[-1]


def _ssd_out_kernel(x_ref, yf_ref, yb_ref, xs_ref, zg_ref, dsk_ref, ng_ref, w_ref, gate_ref, o_ref):
    y = yf_ref[...].astype(F32) + yb_ref[...].astype(F32) + xs_ref[...].astype(F32) * dsk_ref[...]
    y = y * _silu(zg_ref[...].astype(F32))
    gw = SSD_INNER // SSD_GROUPS
    parts = []
    for g in range(SSD_GROUPS):
        yg = y[:, g * gw:(g + 1) * gw]
        ms = jnp.mean(yg * yg, axis=-1, keepdims=True)
        parts.append(yg * lax.rsqrt(ms + NORM_EPS) * ng_ref[:, g * gw:(g + 1) * gw])
    yn = jnp.concatenate(parts, axis=1).astype(BF16)
    o_ref[...] = x_ref[...] + gate_ref[...] * jnp.dot(yn, w_ref[...], preferred_element_type=F32)


def _ssd_out(x, yfb, xs, zg, d_skip, norm_g, w_out, gate):
    lx = x.shape[0]
    tm = min(256, lx)
    rowblk = lambda w: pl.BlockSpec((tm, w), lambda i: (i, 0))
    vecw = pl.BlockSpec((1, SSD_INNER), lambda i: (0, 0))
    return pl.pallas_call(
        _ssd_out_kernel,
        out_shape=jax.ShapeDtypeStruct((lx, D), F32),
        grid=(lx // tm,),
        in_specs=[rowblk(D), rowblk(SSD_INNER), rowblk(SSD_INNER), rowblk(SSD_INNER),
                  rowblk(SSD_INNER), vecw, vecw, _resident((SSD_INNER, D)),
                  pl.BlockSpec((1, D), lambda i: (0, 0))],
        out_specs=rowblk(D),
        compiler_params=_cparams("parallel"),
        name="ssd_out",
    )(x, yfb[0], yfb[1], xs, zg, _row(jnp.repeat(d_skip, SSD_P)), _row(norm_g), w_out, _row(gate))


def kernel(x, c, ctx, c_ctx, norm1_g, norm2_g, mod_w, mod_b, ffn_w_in, ffn_w_out, final_g, gm_w_in, gm_ln_g, gm_ln_b, gm_ws, gm_bs, gm_w_out, at_w_qkv, at_q_g, at_k_g, at_w_out, hy_w_in, hy_conv_w, hy_conv_b, hy_filt_w1, hy_filt_b1, hy_filt_w2, hy_filt_b2, hy_filt_w3, hy_filt_freq, hy_skip, hy_w_out, ssd_w_in, ssd_conv_w, ssd_conv_b, ssd_a_log, ssd_dt_bias, ssd_d_skip, ssd_norm_g, ssd_w_out):
    batch, seq, _ = x.shape
    assert batch == 1, "kernels are written for a single sequence"
    nctx = ctx.shape[1]
    xl = x[0]
    z = ctx[0]
    mods = _modulation(c[0], c_ctx, mod_w, mod_b)
    bf = lambda w: w.astype(BF16)

    for i in range(DEPTH):
        m, j = i % 4, i // 4
        want_ctx = i < DEPTH - 1
        ml = [mods[i, 0, k * D:(k + 1) * D] for k in range(6)]
        mc = [mods[i, 1, k * D:(k + 1) * D] for k in range(6)]
        n1 = norm1_g[i]
        pend_l = pend_c = None
        if m == 0:
            p = (bf(gm_w_in[j]), gm_ln_g[j], gm_ln_b[j], bf(gm_ws[j]), gm_bs[j], bf(gm_w_out[j]))
            xl = _gmlp(xl, n1, ml[0], ml[1], ml[2], *p)
            if want_ctx:
                z = _gmlp(z, n1, mc[0], mc[1], mc[2], *p)
        elif m == 1:
            wq, wo = bf(at_w_qkv[j]), bf(at_w_out[j])
            qt_l, k_l, vt_l = _qkv(xl, n1, ml[0], ml[1], wq, at_q_g[j], at_k_g[j], rope=True)
            qt_c, k_c, vt_c = _qkv(z, n1, mc[0], mc[1], wq, at_q_g[j], at_k_g[j], rope=False)
            k_all = jnp.concatenate([k_c, k_l], axis=1)
            vt_all = jnp.concatenate([vt_c, vt_l], axis=1)
            stot = nctx + seq
            ts = next(t for t in (3328, 1280, 1024, 512, 256) if stot % t == 0)
            score_bound = (HD ** 0.5 * LOG2E) * jnp.max(jnp.abs(at_q_g[j])) * jnp.max(jnp.abs(at_k_g[j]))
            o_l = lax.cond(score_bound <= FLASH_SCORE_BOUND,
                           lambda: _flash(qt_l, k_all, vt_all, stot, ts, bounded=True),
                           lambda: _flash(qt_l, k_all, vt_all, stot, ts, bounded=False))
            pend_l = ("proj", o_l, wo, ml[2])
            if want_ctx:
                pend_c = ("proj", _flash(qt_c, k_all, vt_all, nctx, nctx), wo, mc[2])
        elif m == 2:
            p = (bf(hy_w_in[j]), hy_conv_w[j], hy_conv_b[j], hy_filt_w1[j], hy_filt_b1[j], hy_filt_w2[j],
                 hy_filt_b2[j], hy_filt_w3[j], hy_filt_freq[j], hy_skip[j], bf(hy_w_out[j]))
            pend_l = _hyena(xl, n1, ml[0], ml[1], ml[2], *p)
            if want_ctx:
                pend_c = _hyena(z, n1, mc[0], mc[1], mc[2], *p)
        else:
            win, wo = bf(ssd_w_in[j]), bf(ssd_w_out[j])
            pin = (win, ssd_conv_w[j], ssd_conv_b[j], ssd_dt_bias[j])
            zg_c, xs_c, bm_c, cm_c, dt_c = _ssd_in(z, n1, mc[0], mc[1], *pin)
            zg_l, xs_l, bm_l, cm_l, dt_l = _ssd_in(xl, n1, ml[0], ml[1], *pin)
            h0 = jnp.zeros((2, SSD_HEADS // 2, SSD_STATE, 2 * SSD_P), F32)
            y_c, h_ctx = _ssd_scan(xs_c, bm_c, cm_c, dt_c, ssd_a_log[j], h0, want_ctx)
            y_l, _ = _ssd_scan(xs_l, bm_l, cm_l, dt_l, ssd_a_log[j], h_ctx, True)
            xl = _ssd_out(xl, y_l, xs_l, zg_l, ssd_d_skip[j], ssd_norm_g[j], wo, ml[2])
            if want_ctx:
                z = _ssd_out(z, y_c, xs_c, zg_c, ssd_d_skip[j], ssd_norm_g[j], wo, mc[2])
        wi, wo2 = bf(ffn_w_in[i]), bf(ffn_w_out[i])
        xl = _ffn(xl, pend_l, norm2_g[i], ml[3], ml[4], ml[5], wi, wo2, final_g, final=(i == DEPTH - 1))
        if want_ctx:
            z = _ffn(z, pend_c, norm2_g[i], mc[3], mc[4], mc[5], wi, wo2, final_g, final=False)
    return xl[None]
```

```python
import functools
import math

import numpy as np
import jax
import jax.numpy as jnp
from jax import lax
from jax.experimental import pallas as pl
from jax.experimental.pallas import tpu as pltpu

F32 = jnp.float32
BF16 = jnp.bfloat16
HIGHEST = lax.Precision.HIGHEST

D = 1024
DEPTH = 4
GRID_W = 64
NORM_EPS = 1e-6
FFN_HIDDEN = 2816
GM_CHUNK = 128
GM_WIDTH = 2 * D
GM_GROUPS = 8
GM_GW = GM_WIDTH // GM_GROUPS
HD = 64
QH = D // HD
KVH = 4
ROPE_THETA = 10000.0
LOG2E = math.log2(math.e)
FLASH_SCORE_BOUND = 30.0
FLASH_LOOKAHEAD = 2
HY_BANDS = 16
HY_EMB = 1 + 2 * HY_BANDS
HY_FILT_W = 64
HY_MAX_DECAY = math.log(1e-2) / 0.3
HY_MIN_DECAY = math.log(1e-2) / 1.5
SSD_INNER = 2 * D
SSD_P = 64
SSD_HEADS = SSD_INNER // SSD_P
SSD_GROUPS = 4
SSD_STATE = 128
SSD_CHUNK = 128
SSD_BC = SSD_GROUPS * SSD_STATE
SSD_CONV_DIM = SSD_INNER + 2 * SSD_BC

LANES = 128
SUBLANES = 8
VMEM_LIMIT_BYTES = 56 * 1024 * 1024
DFT_N2 = LANES
ROW_TILE = 512
FLASH_Q_TILE = 128
FLASH_KV_TILES = (3328, 1280, 1024, 512, 256)
FLASH_CHUNK = 2 * LANES


def _cparams(*sem):
    return pltpu.CompilerParams(dimension_semantics=sem, vmem_limit_bytes=VMEM_LIMIT_BYTES)


def _row(v):
    return v.reshape(1, -1)


def _normmod(x, g, shift, scale):
    ms = jnp.mean(x * x, axis=-1, keepdims=True)
    return x * lax.rsqrt(ms + NORM_EPS) * g * (1.0 + scale) + shift


def _silu(x):
    return x * jax.nn.sigmoid(x)


def _mod_kernel(cl_ref, cc_ref, w_ref, b_ref, o_ref):
    w = w_ref[0]
    for r, c_ref in enumerate((cl_ref, cc_ref)):
        a = _silu(c_ref[...])
        o_ref[0, r:r + 1, :] = jnp.sum(a * w, axis=0, keepdims=True) + b_ref[0]


def _modulation(c, c_ctx, mod_w, mod_b):
    n6 = 6 * D
    tn = n6 // 4
    depth = mod_w.shape[0]
    return pl.pallas_call(
        _mod_kernel,
        out_shape=jax.ShapeDtypeStruct((depth, 2, n6), F32),
        grid=(depth, n6 // tn),
        in_specs=[
            pl.BlockSpec((D, 1), lambda i, n: (0, 0)),
            pl.BlockSpec((D, 1), lambda i, n: (0, 0)),
            pl.BlockSpec((1, D, tn), lambda i, n: (i, 0, n)),
            pl.BlockSpec((1, 1, tn), lambda i, n: (i, 0, n)),
        ],
        out_specs=pl.BlockSpec((1, 2, tn), lambda i, n: (i, 0, n)),
        compiler_params=_cparams("parallel", "parallel"),
        name="modulation",
    )(c.reshape(D, 1), c_ctx.reshape(D, 1), mod_w, mod_b.reshape(depth, 1, n6))


def _ffn_kernel(x_ref, *refs, mode, final):
    x = x_ref[...]
    if mode == "proj":
        a_ref, wm_ref, g1_ref = refs[:3]
        refs = refs[3:]
        x = x + g1_ref[...] * jnp.dot(a_ref[...], wm_ref[...], preferred_element_type=F32)
    elif mode == "hyena":
        x0_ref, yt_ref, wm_ref, g1_ref = refs[:4]
        refs = refs[4:]
        a = (x0_ref[...].astype(F32) * yt_ref[...].astype(F32).T).astype(BF16)
        x = x + g1_ref[...] * jnp.dot(a, wm_ref[...], preferred_element_type=F32)
    g_ref, sh_ref, sc_ref, gate_ref, wi_ref, wo_ref, fg_ref, o_ref = refs
    h = _normmod(x, g_ref[...], sh_ref[...], sc_ref[...]).astype(BF16)
    a = jnp.dot(h, wi_ref[:, :FFN_HIDDEN], preferred_element_type=F32)
    u = jnp.dot(h, wi_ref[:, FFN_HIDDEN:], preferred_element_type=F32)
    act = (_silu(a) * u).astype(BF16)
    y = x + gate_ref[...] * jnp.dot(act, wo_ref[...], preferred_element_type=F32)
    if final:
        ms = jnp.mean(y * y, axis=-1, keepdims=True)
        y = y * lax.rsqrt(ms + NORM_EPS) * fg_ref[...]
    o_ref[...] = y


def _resident(shape):
    return pl.BlockSpec(shape, lambda *_: (0,) * len(shape), pipeline_mode=pl.Buffered(1))


def _ffn(x, pending, g, sh, sc, gate, w_in, w_out, final_g, final):
    lx = x.shape[0]
    tm = min(ROW_TILE, lx)
    vec = pl.BlockSpec((1, D), lambda i: (0, 0))
    rows = pl.BlockSpec((tm, D), lambda i: (i, 0))
    mode, pre_args, pre_specs = "none", (), []
    if pending is not None:
        mode = pending[0]
        if mode == "proj":
            _, a, wm, g1 = pending
            pre_args, pre_specs = (a, wm, _row(g1)), [rows, _resident(wm.shape), vec]
        else:
            _, x0, yt, wm, g1 = pending
            pre_args = (x0, yt, wm, _row(g1))
            pre_specs = [rows, pl.BlockSpec((D, tm), lambda i: (0, i)), _resident(wm.shape), vec]
    return pl.pallas_call(
        functools.partial(_ffn_kernel, mode=mode, final=final),
        out_shape=jax.ShapeDtypeStruct((lx, D), F32),
        grid=(lx // tm,),
        in_specs=[rows] + pre_specs + [
            vec, vec, vec, vec,
            _resident((D, 2 * FFN_HIDDEN)),
            _resident((FFN_HIDDEN, D)),
            vec,
        ],
        out_specs=rows,
        compiler_params=_cparams("parallel"),
        name="ffn",
    )(x, *pre_args, _row(g), _row(sh), _row(sc), _row(gate), w_in, w_out, _row(final_g))


def _gmlp_kernel(x_ref, g_ref, sh_ref, sc_ref, gate_ref, win_ref, lng_ref, lnb_ref, ws_ref, bs_ref, wout_ref,
                 o_ref, *, tm):
    nsub = max(1, tm // (2 * GM_CHUNK))
    rs = tm // nsub

    def project(i):
        x = x_ref[i * rs:(i + 1) * rs, :]
        h = _normmod(x, g_ref[...], sh_ref[...], sc_ref[...]).astype(BF16)
        return jnp.dot(h, win_ref[...], preferred_element_type=F32)

    def mix(i, t):
        t = 0.5 * t * (1.0 + lax.erf(t * (1.0 / math.sqrt(2.0))))
        u = t[:, :GM_WIDTH]
        v = t[:, GM_WIDTH:]
        mu = jnp.mean(v, axis=-1, keepdims=True)
        vc = v - mu
        var = jnp.mean(vc * vc, axis=-1, keepdims=True)
        v = (vc * lax.rsqrt(var + NORM_EPS) * lng_ref[...] + lnb_ref[...]).astype(BF16)
        rows = []
        for q in range(rs // GM_CHUNK):
            cols = []
            for gidx in range(GM_GROUPS):
                vq = v[q * GM_CHUNK:(q + 1) * GM_CHUNK, gidx * GM_GW:(gidx + 1) * GM_GW]
                bias = bs_ref[gidx]
                m = jnp.dot(ws_ref[gidx], vq, preferred_element_type=F32)
                cols.append(m + jnp.concatenate([bias] * (GM_GW // LANES), axis=1))
            rows.append(jnp.concatenate(cols, axis=1))
        gated = (u * jnp.concatenate(rows, axis=0)).astype(BF16)
        y = jnp.dot(gated, wout_ref[...], preferred_element_type=F32)
        o_ref[i * rs:(i + 1) * rs, :] = x_ref[i * rs:(i + 1) * rs, :] + gate_ref[...] * y

    t_next = project(0)
    for i in range(nsub):
        t = t_next
        if i + 1 < nsub:
            t_next = project(i + 1)
        mix(i, t)


def _gmlp(x, g, sh, sc, gate, w_in, ln_g, ln_b, ws, bs, w_out):
    lx = x.shape[0]
    tm = min(ROW_TILE, lx)
    vec = pl.BlockSpec((1, D), lambda i: (0, 0))
    vecw = pl.BlockSpec((1, GM_WIDTH), lambda i: (0, 0))
    bsb = jnp.broadcast_to(bs[:, :, None], (GM_GROUPS, GM_CHUNK, LANES))
    return pl.pallas_call(
        functools.partial(_gmlp_kernel, tm=tm),
        out_shape=jax.ShapeDtypeStruct((lx, D), F32),
        grid=(lx // tm,),
        in_specs=[
            pl.BlockSpec((tm, D), lambda i: (i, 0)),
            vec, vec, vec, vec,
            _resident((D, 2 * GM_WIDTH)),
            vecw, vecw,
            pl.BlockSpec((GM_GROUPS, GM_CHUNK, GM_CHUNK), lambda i: (0, 0, 0)),
            pl.BlockSpec((GM_GROUPS, GM_CHUNK, LANES), lambda i: (0, 0, 0)),
            _resident((GM_WIDTH, D)),
        ],
        out_specs=pl.BlockSpec((tm, D), lambda i: (i, 0)),
        compiler_params=_cparams("parallel"),
        name="gmlp",
    )(x, _row(g), _row(sh), _row(sc), _row(gate), w_in, _row(ln_g), _row(ln_b), ws, bsb, w_out)


def _group_sumsq(t, e_ref):
    sq = t * t
    hi = sq.astype(BF16)
    lo = (sq - hi.astype(F32)).astype(BF16)
    e = e_ref[...]
    w = e.shape[0]
    outs = []
    for j in range(t.shape[1] // w):
        sl = slice(j * w, (j + 1) * w)
        outs.append(jnp.dot(hi[:, sl], e, preferred_element_type=F32) + jnp.dot(lo[:, sl], e, preferred_element_type=F32))
    return jnp.concatenate(outs, axis=1)


def _rope(t, cosf, sinf):
    w = t.shape[1]
    lane = lax.broadcasted_iota(jnp.int32, t.shape, 1)
    first = (lane % HD) < (HD // 2)
    partner = jnp.where(first, pltpu.roll(t, w - HD // 2, axis=1), pltpu.roll(t, HD // 2, axis=1))
    reps = w // LANES
    c = jnp.concatenate([cosf] * reps, axis=1)
    s = jnp.concatenate([sinf] * reps, axis=1)
    return t * c + partner * s


def _qkv_kernel(x_ref, g_ref, sh_ref, sc_ref, w_ref, qg_ref, kg_ref, e_ref, cos_ref, sin_ref,
                qt_ref, k_ref, vt_ref, *, rope):
    h = _normmod(x_ref[...], g_ref[...], sh_ref[...], sc_ref[...]).astype(BF16)
    qkv = jnp.dot(h, w_ref[...], preferred_element_type=F32)
    q = qkv[:, :D]
    k = qkv[:, D:D + KVH * HD]
    v = qkv[:, D + KVH * HD:]
    q = q * lax.rsqrt(_group_sumsq(q, e_ref) * (1.0 / HD) + NORM_EPS) * qg_ref[...]
    k = k * lax.rsqrt(_group_sumsq(k, e_ref) * (1.0 / HD) + NORM_EPS) * kg_ref[...]
    if rope:
        q = _rope(q, cos_ref[...], sin_ref[...])
        k = _rope(k, cos_ref[...], sin_ref[...])
    qt_ref[...] = (q * (HD ** -0.5 * LOG2E)).T.astype(BF16)
    for gidx in range(KVH):
        k_ref[gidx] = k[:, gidx * HD:(gidx + 1) * HD].astype(BF16)
    vt_ref[...] = v.T.astype(BF16)


def _qkv(x, g, sh, sc, w_qkv, q_g, k_g, rope):
    lx = x.shape[0]
    tm = min(ROW_TILE, lx)
    vec = pl.BlockSpec((1, D), lambda i: (0, 0))
    kvw = KVH * HD
    rows = lx // GRID_W
    row = jnp.repeat(jnp.arange(rows, dtype=F32), GRID_W)
    col = jnp.tile(jnp.arange(GRID_W, dtype=F32), rows)
    n = HD // 4
    inv = ROPE_THETA ** (-jnp.arange(n, dtype=F32) / n)
    ang = jnp.concatenate([row[:, None] * inv, col[:, None] * inv], axis=-1)
    cos, sin = jnp.cos(ang), jnp.sin(ang)
    cosf = jnp.tile(jnp.concatenate([cos, cos], axis=-1), (1, LANES // HD))
    sinf = jnp.tile(jnp.concatenate([-sin, sin], axis=-1), (1, LANES // HD))
    eblk = jnp.asarray(np.kron(np.eye(kvw // HD), np.ones((HD, HD))), BF16)
    tab = pl.BlockSpec((tm, LANES), lambda i: (i, 0))
    return pl.pallas_call(
        functools.partial(_qkv_kernel, rope=rope),
        out_shape=(jax.ShapeDtypeStruct((D, lx), BF16),
                   jax.ShapeDtypeStruct((KVH, lx, HD), BF16),
                   jax.ShapeDtypeStruct((kvw, lx), BF16)),
        grid=(lx // tm,),
        in_specs=[
            pl.BlockSpec((tm, D), lambda i: (i, 0)),
            vec, vec, vec,
            _resident((D, D + 2 * kvw)),
            vec,
            pl.BlockSpec((1, kvw), lambda i: (0, 0)),
            pl.BlockSpec((kvw, kvw), lambda i: (0, 0)),
            tab, tab,
        ],
        out_specs=(pl.BlockSpec((D, tm), lambda i: (0, i)),
                   pl.BlockSpec((KVH, tm, HD), lambda i: (0, i, 0)),
                   pl.BlockSpec((kvw, tm), lambda i: (0, i))),
        compiler_params=_cparams("parallel"),
        name="qkv_proj",
    )(x, _row(g), _row(sh), _row(sc), w_qkv, _row(jnp.tile(q_g, QH)), _row(jnp.tile(k_g, KVH)), eblk, cosf, sinf)


def _flash_kernel(qt_ref, k_ref, vt_ref, o_ref, qg_scr, m_scr, l_scr, acc_scr, *, tq, ts, tc, nkv, bounded):
    j = pl.program_id(1)
    gq = QH // KVH
    mcols = gq * tq

    @pl.when(j == 0)
    def _():
        for h in range(QH):
            qg_scr[h // gq, :, (h % gq) * tq:(h % gq + 1) * tq] = qt_ref[h * HD:(h + 1) * HD, :]
        m_scr[...] = jnp.full(m_scr.shape, -jnp.inf, F32)
        l_scr[...] = jnp.zeros_like(l_scr)
        acc_scr[...] = jnp.zeros_like(acc_scr)

    stages = [(g, c) for c in range(ts // tc) for g in range(KVH)]

    def scores(g, c):
        return jnp.dot(k_ref[g, c * tc:(c + 1) * tc, :], qg_scr[g], preferred_element_type=F32)

    pending = [scores(*st) for st in stages[:FLASH_LOOKAHEAD]]
    for idx, (g, c) in enumerate(stages):
        s = pending.pop(0)
        if idx + FLASH_LOOKAHEAD < len(stages):
            pending.append(scores(*stages[idx + FLASH_LOOKAHEAD]))
        vt = vt_ref[g * HD:(g + 1) * HD, c * tc:(c + 1) * tc]
        if bounded:
            p = jnp.exp2(s)
            l_scr[g] += jnp.sum(p, axis=0, keepdims=True)
            acc_scr[g] += jnp.dot(vt, p.astype(BF16), preferred_element_type=F32)
        else:
            m_prev = m_scr[g]
            m_new = jnp.maximum(m_prev, jnp.max(s, axis=0, keepdims=True))
            alpha = jnp.exp2(m_prev - m_new)
            p = jnp.exp2(s - m_new)
            l_scr[g] = alpha * l_scr[g] + jnp.sum(p, axis=0, keepdims=True)
            acc_scr[g] = alpha * acc_scr[g] + jnp.dot(vt, p.astype(BF16), preferred_element_type=F32)
            m_scr[g] = m_new

    @pl.when(j == nkv - 1)
    def _():
        rows = []
        for g in range(KVH):
            o = acc_scr[g] / l_scr[g]
            rows += [o[:, r * tq:(r + 1) * tq] for r in range(gq)]
        o_ref[...] = jnp.concatenate(rows, axis=0).T.astype(o_ref.dtype)


def _flash(qt, k, vt, s_len, ts, bounded=False):
    lq = qt.shape[1]
    tq = min(FLASH_Q_TILE, lq)
    nkv = s_len // ts
    gq = QH // KVH
    kvw = KVH * HD
    tc = FLASH_CHUNK if ts % FLASH_CHUNK == 0 else LANES
    return pl.pallas_call(
        functools.partial(_flash_kernel, tq=tq, ts=ts, tc=tc, nkv=nkv, bounded=bounded),
        out_shape=jax.ShapeDtypeStruct((lq, D), BF16),
        grid=(lq // tq, nkv),
        in_specs=[
            pl.BlockSpec((D, tq), lambda i, j: (0, i)),
            pl.BlockSpec((KVH, ts, HD), lambda i, j: (0, j, 0)),
            pl.BlockSpec((kvw, ts), lambda i, j: (0, j)),
        ],
        out_specs=pl.BlockSpec((tq, D), lambda i, j: (i, 0)),
        scratch_shapes=[
            pltpu.VMEM((KVH, HD, gq * tq), BF16),
            pltpu.VMEM((KVH, 1, gq * tq), F32),
            pltpu.VMEM((KVH, 1, gq * tq), F32),
            pltpu.VMEM((KVH, HD, gq * tq), F32),
        ],
        compiler_params=_cparams("parallel", "arbitrary"),
        name="flash_attn",
    )(qt, k, vt)


def _halo_specs(tm, lx):
    nb = lx // SUBLANES
    step = tm // SUBLANES
    prev = pl.BlockSpec((SUBLANES, D), lambda i: (jnp.maximum(i * step - 1, 0), 0))
    nxt = pl.BlockSpec((SUBLANES, D), lambda i: (jnp.minimum((i + 1) * step, nb - 1), 0))
    return prev, nxt


def _conv3(p_main, p_halo, cw, cb, first, last):
    tm = p_main.shape[0]
    rid = lax.broadcasted_iota(jnp.int32, p_main.shape, 0)
    before = jnp.where(first, 0.0, p_halo[SUBLANES - 1:SUBLANES, :])
    after = jnp.where(last, 0.0, p_halo[SUBLANES:SUBLANES + 1, :])
    up = jnp.where(rid == 0, before, pltpu.roll(p_main, 1, axis=0))
    dn = jnp.where(rid == tm - 1, after, pltpu.roll(p_main, tm - 1, axis=0))
    return cw[0:1, :] * up + cw[1:2, :] * p_main + cw[2:3, :] * dn + cb


def _norm_halo(xm_ref, xp_ref, xn_ref, g_ref, sh_ref, sc_ref):
    g, sh, sc = g_ref[...], sh_ref[...], sc_ref[...]
    h = _normmod(xm_ref[...], g, sh, sc).astype(BF16)
    hh = jnp.concatenate([_normmod(xp_ref[...], g, sh, sc), _normmod(xn_ref[...], g, sh, sc)], axis=0).astype(BF16)
    return h, hh


def _hy_in_kernel(xm_ref, xp_ref, xn_ref, g_ref, sh_ref, sc_ref, w_ref, cw_ref, cb_ref, x0_ref, ut_ref, *, nt):
    i = pl.program_id(0)
    first, last = i == 0, i == nt - 1
    h, hh = _norm_halo(xm_ref, xp_ref, xn_ref, g_ref, sh_ref, sc_ref)

    def project(b):
        sl = slice(b * D, (b + 1) * D)
        return (jnp.dot(h, w_ref[:, sl], preferred_element_type=F32),
                jnp.dot(hh, w_ref[:, sl], preferred_element_type=F32))

    def conv(b, p):
        sl = slice(b * D, (b + 1) * D)
        return _conv3(p[0], p[1], cw_ref[:, sl], cb_ref[:, sl], first, last)

    p0 = project(0)
    p1 = project(1)
    x0_ref[...] = conv(0, p0).astype(x0_ref.dtype)
    p2 = project(2)
    x1 = conv(1, p1)
    ut_ref[...] = (x1 * conv(2, p2)).T.astype(ut_ref.dtype)


def _hy_in(x, g, sh, sc, w_in, conv_w, conv_b):
    lx = x.shape[0]
    tm = min(ROW_TILE, lx)
    nt = lx // tm
    vec = pl.BlockSpec((1, D), lambda i: (0, 0))
    prev, nxt = _halo_specs(tm, lx)
    return pl.pallas_call(
        functools.partial(_hy_in_kernel, nt=nt),
        out_shape=(jax.ShapeDtypeStruct((lx, D), BF16), jax.ShapeDtypeStruct((D, lx), BF16)),
        grid=(nt,),
        in_specs=[
            pl.BlockSpec((tm, D), lambda i: (i, 0)), prev, nxt,
            vec, vec, vec,
            _resident((D, 3 * D)),
            pl.BlockSpec((3, 3 * D), lambda i: (0, 0)),
            pl.BlockSpec((1, 3 * D), lambda i: (0, 0)),
        ],
        out_specs=(pl.BlockSpec((tm, D), lambda i: (i, 0)), pl.BlockSpec((D, tm), lambda i: (0, i))),
        compiler_params=_cparams("parallel"),
        name="hyena_in",
    )(x, x, x, _row(g), _row(sh), _row(sc), w_in, conv_w, _row(conv_b))


def _hy_filter_kernel(ft_ref, t_ref, w1_ref, b1_ref, w2_ref, b2_ref, w3h_ref, w3l_ref, fr_ref, dl_ref, sk_ref, kt_ref,
                      *, tm, ltrue, lpad):
    i = pl.program_id(0)
    feats = ft_ref[...]
    fr = fr_ref[...]
    hid = jnp.sin(fr * (jnp.dot(w1_ref[...], feats, preferred_element_type=F32, precision=HIGHEST) + b1_ref[...]))
    hid = jnp.sin(fr * (jnp.dot(w2_ref[...], hid, preferred_element_type=F32, precision=HIGHEST) + b2_ref[...]))
    hh = hid.astype(BF16)
    hl = (hid - hh.astype(F32)).astype(BF16)
    w3h = w3h_ref[...]
    kt = (jnp.dot(w3h, hh, preferred_element_type=F32) + jnp.dot(w3h, hl, preferred_element_type=F32)
          + jnp.dot(w3l_ref[...], hh, preferred_element_type=F32))
    kt = kt * jnp.exp(-dl_ref[...] * t_ref[...])
    if lpad != ltrue:
        pos = lax.broadcasted_iota(jnp.int32, kt.shape, 1) + i * tm
        kt = jnp.where(pos < ltrue, kt, 0.0)
    kt_ref[...] = kt.astype(kt_ref.dtype)

    @pl.when(i == 0)
    def _():
        row = lax.broadcasted_iota(jnp.int32, (2 * D, 1), 0)
        kt_ref[:, 0:1] = jnp.where(row < D, kt[:, 0:1] + sk_ref[...], 0.0).astype(kt_ref.dtype)


def _hy_filter(ltrue, lpad, w1, b1, w2, b2, w3, freq, skip):
    tm = min(ROW_TILE, lpad)
    col = lambda v_: v_.reshape(-1, 1)
    w1t = jnp.pad(w1.T, ((0, 0), (0, LANES - HY_EMB)))
    deltas = jnp.abs(jnp.linspace(HY_MIN_DECAY, HY_MAX_DECAY, D, dtype=F32))
    pos = jnp.arange(lpad, dtype=F32)
    zf = jnp.linspace(1e-4, HY_BANDS - 1, HY_BANDS, dtype=F32)[:, None] * (2.0 * math.pi * pos / ltrue)[None, :]
    feats = jnp.concatenate([(pos / (ltrue - 1))[None, :], jnp.cos(zf), -jnp.sin(zf),
                             jnp.zeros((LANES - HY_EMB, lpad), F32)], axis=0)
    w3t = w3.T
    w3h = w3t.astype(BF16)
    w3l = (w3t - w3h.astype(F32)).astype(BF16)
    full = lambda a: pl.BlockSpec(a.shape, lambda i: (0,) * a.ndim)
    args = (w1t, col(b1), w2.T, col(b2), w3h, w3l, col(freq), col(jnp.tile(deltas, 2)),
            col(jnp.concatenate([skip, jnp.zeros((D,), F32)])))
    return pl.pallas_call(
        functools.partial(_hy_filter_kernel, tm=tm, ltrue=ltrue, lpad=lpad),
        out_shape=jax.ShapeDtypeStruct((2 * D, lpad), BF16),
        grid=(lpad // tm,),
        in_specs=[pl.BlockSpec((LANES, tm), lambda i: (0, i)), pl.BlockSpec((1, tm), lambda i: (0, i))]
                 + [full(a) for a in args],
        out_specs=pl.BlockSpec((2 * D, tm), lambda i: (0, i)),
        compiler_params=_cparams("parallel"),
        name="hyena_filter",
    )(feats, feats[0:1, :], *args)


def _dft_consts(nh):
    n1 = 2 * nh
    n = n1 * DFT_N2
    k1 = np.arange(n1)[:, None].astype(np.float64)
    a1 = 2.0 * np.pi * k1 * np.arange(nh)[None, :] / n1
    f1 = np.concatenate([np.cos(a1), -np.sin(a1)], axis=0)
    at = 2.0 * np.pi * ((np.arange(n1)[:, None] * np.arange(DFT_N2)[None, :]) % n) / n
    a2 = 2.0 * np.pi * ((np.arange(DFT_N2)[:, None] * np.arange(DFT_N2)[None, :]) % DFT_N2) / DFT_N2
    c2, s2 = np.cos(a2), np.sin(a2)
    f2 = np.block([[c2, -s2], [s2, c2]])
    g2 = np.block([[c2, s2], [-s2, c2]])
    g1 = np.concatenate([np.cos(a1).T, -np.sin(a1).T], axis=1) / n
    mxu = lambda a: jnp.asarray(a.astype(BF16))
    return mxu(f1), jnp.asarray(np.cos(at), F32), jnp.asarray(np.sin(at), F32), mxu(f2), mxu(g2), mxu(g1)


def _bf16_dot(a, b):
    return jnp.dot(a.astype(BF16), b.astype(BF16), preferred_element_type=F32)


def _hy_conv_kernel(x_ref, kf_ref, kb_ref, f1_ref, twc_ref, tws_ref, f2_ref, g2_ref, g1_ref, o_ref, *, cb, n1):
    twc, tws = twc_ref[...], tws_ref[...]
    f1, f2, g2, g1 = f1_ref[...], f2_ref[...], g2_ref[...], g1_ref[...]
    hc = cb // 2
    nt = 3 * hc

    def dft1(half):
        xs = [r[half * hc + c] for r in (x_ref, kf_ref, kb_ref) for c in range(hc)]
        return _bf16_dot(f1, jnp.concatenate(xs, axis=1))

    def twiddle_rows(a):
        rows = []
        for t in range(nt):
            ar = a[:n1, t * DFT_N2:(t + 1) * DFT_N2]
            ai = a[n1:, t * DFT_N2:(t + 1) * DFT_N2]
            rows.append(jnp.concatenate([ar * twc + ai * tws, ai * twc - ar * tws], axis=1))
        return jnp.concatenate(rows, axis=0)

    def dft2(rows):
        return _bf16_dot(rows, f2)

    def product(spec_all):
        rows = hc * n1
        spec, hf, hb = spec_all[:rows], spec_all[rows:2 * rows], spec_all[2 * rows:]
        hr = hf[:, :DFT_N2] + hb[:, :DFT_N2]
        hi = hf[:, DFT_N2:] - hb[:, DFT_N2:]
        xr, xi = spec[:, :DFT_N2], spec[:, DFT_N2:]
        return jnp.concatenate([xr * hr - xi * hi, xr * hi + xi * hr], axis=1)

    def idft2(y):
        return _bf16_dot(y, g2)

    def twiddle_cols(b):
        cols = []
        for c in range(hc):
            br = b[c * n1:(c + 1) * n1, :DFT_N2]
            bi = b[c * n1:(c + 1) * n1, DFT_N2:]
            cols.append(jnp.concatenate([br * twc - bi * tws, bi * twc + br * tws], axis=0))
        return jnp.concatenate(cols, axis=1)

    def idft1(cols):
        return _bf16_dot(g1, cols)

    def store(half, out):
        for c in range(hc):
            o_ref[half * hc + c] = out[:, c * DFT_N2:(c + 1) * DFT_N2].astype(o_ref.dtype)

    a0 = dft1(0)
    a1 = dft1(1)
    s0 = dft2(twiddle_rows(a0))
    s1 = dft2(twiddle_rows(a1))
    b0 = idft2(product(s0))
    b1 = idft2(product(s1))
    o0 = idft1(twiddle_cols(b0))
    o1 = idft1(twiddle_cols(b1))
    store(0, o0)
    store(1, o1)


def _hy_longconv(ut, kt, lpad):
    nh = lpad // DFT_N2
    n1 = 2 * nh
    cb = max(2 * SUBLANES, min(64, 2048 // n1))
    f1, twc, tws, f2, g2, g1 = _dft_consts(nh)
    consts = (f1, twc, tws, f2, g2, g1)
    full = lambda a: pl.BlockSpec(a.shape, lambda i: (0,) * a.ndim)
    k3 = kt.reshape(2 * D, nh, DFT_N2)
    u3 = ut.reshape(D, nh, DFT_N2)
    nb = D // cb
    y3 = pl.pallas_call(
        functools.partial(_hy_conv_kernel, cb=cb, n1=n1),
        out_shape=jax.ShapeDtypeStruct((D, nh, DFT_N2), BF16),
        grid=(nb,),
        in_specs=[pl.BlockSpec((cb, nh, DFT_N2), lambda i: (i, 0, 0)),
                  pl.BlockSpec((cb, nh, DFT_N2), lambda i: (i, 0, 0)),
                  pl.BlockSpec((cb, nh, DFT_N2), lambda i: (i + nb, 0, 0))]
                 + [full(a) for a in consts],
        out_specs=pl.BlockSpec((cb, nh, DFT_N2), lambda i: (i, 0, 0)),
        compiler_params=_cparams("parallel"),
        name="hyena_longconv",
    )(u3, k3, k3, *consts)
    return y3.reshape(D, lpad)


def _hyena(x, g, sh, sc, gate, w_in, conv_w, conv_b, w1, b1, w2, b2, w3, freq, skip, w_out):
    lx = x.shape[0]
    lpad = max(lx, SUBLANES * DFT_N2)
    x0, ut = _hy_in(x, g, sh, sc, w_in, conv_w, conv_b)
    if lpad != lx:
        ut = jnp.pad(ut, ((0, 0), (0, lpad - lx)))
    kt = _hy_filter(lx, lpad, w1, b1, w2, b2, w3, freq, skip)
    yt = _hy_longconv(ut, kt, lpad)[:, :lx]
    return ("hyena", x0, yt, w_out, gate)


def _ssd_in_kernel(xm_ref, xp_ref, xn_ref, g_ref, sh_ref, sc_ref, w_ref, wd_ref, cw_ref, cb_ref, db_ref,
                   zg_ref, xs_ref, bm_ref, cm_ref, dt_ref, *, nt):
    i = pl.program_id(0)
    first, last = i == 0, i == nt - 1
    h, hh = _norm_halo(xm_ref, xp_ref, xn_ref, g_ref, sh_ref, sc_ref)
    cw = SSD_BC
    nchunk = SSD_CONV_DIM // cw

    def project(b):
        sl = slice(SSD_INNER + b * cw, SSD_INNER + (b + 1) * cw)
        return (jnp.dot(h, w_ref[:, sl], preferred_element_type=F32),
                jnp.dot(hh, w_ref[:, sl], preferred_element_type=F32))

    def conv(b, p):
        sl = slice(b * cw, (b + 1) * cw)
        y = _silu(_conv3(p[0], p[1], cw_ref[:, sl], cb_ref[:, sl], first, last))
        if (b + 1) * cw <= SSD_INNER:
            xs_ref[:, sl] = y.astype(xs_ref.dtype)
        elif b == nchunk - 2:
            bm_ref[...] = y
        else:
            cm_ref[...] = y

    pending = [project(0), project(1)]
    for b in range(nchunk):
        p = pending.pop(0)
        if b + 2 < nchunk:
            pending.append(project(b + 2))
        conv(b, p)
    zg_ref[...] = jnp.dot(h, w_ref[:, :SSD_INNER], preferred_element_type=F32).astype(zg_ref.dtype)
    dt = jnp.dot(h, wd_ref[...], preferred_element_type=F32) + db_ref[...]
    dt = jnp.maximum(dt, 0.0) + jnp.log1p(jnp.exp(-jnp.abs(dt)))
    lane = lax.broadcasted_iota(jnp.int32, dt.shape, 1)
    dt = jnp.where((lane % LANES) < SSD_HEADS, dt, 0.0)
    dt_ref[0] = dt[:, :LANES]
    dt_ref[1] = dt[:, LANES:]


def _ssd_in(x, g, sh, sc, w_in, conv_w, conv_b, dt_bias):
    lx = x.shape[0]
    tm = min(ROW_TILE, lx)
    nt = lx // tm
    vec = pl.BlockSpec((1, D), lambda i: (0, 0))
    prev, nxt = _halo_specs(tm, lx)
    wdt = w_in[:, SSD_INNER + SSD_CONV_DIM:]
    pad = LANES - SSD_HEADS
    wd = jnp.concatenate([jnp.pad(wdt[:, :SSD_HEADS], ((0, 0), (0, pad))),
                          jnp.pad(wdt[:, SSD_HEADS:], ((0, 0), (0, pad)))], axis=1)
    db = jnp.pad(dt_bias, ((0, 0), (0, pad))).reshape(1, 2 * LANES)
    full = lambda a: pl.BlockSpec(a.shape, lambda i: (0,) * a.ndim)
    rowblk = lambda w: pl.BlockSpec((tm, w), lambda i: (i, 0))
    return pl.pallas_call(
        functools.partial(_ssd_in_kernel, nt=nt),
        out_shape=(jax.ShapeDtypeStruct((lx, SSD_INNER), BF16), jax.ShapeDtypeStruct((lx, SSD_INNER), BF16),
                   jax.ShapeDtypeStruct((lx, SSD_BC), F32), jax.ShapeDtypeStruct((lx, SSD_BC), F32),
                   jax.ShapeDtypeStruct((2, lx, LANES), F32)),
        grid=(nt,),
        in_specs=[rowblk(D), prev, nxt, vec, vec, vec, _resident(w_in.shape), _resident(wd.shape),
                  pl.BlockSpec((3, SSD_CONV_DIM), lambda i: (0, 0)),
                  pl.BlockSpec((1, SSD_CONV_DIM), lambda i: (0, 0)),
                  pl.BlockSpec((1, 2 * LANES), lambda i: (0, 0))],
        out_specs=(rowblk(SSD_INNER), rowblk(SSD_INNER), rowblk(SSD_BC), rowblk(SSD_BC),
                   pl.BlockSpec((2, tm, LANES), lambda i: (0, i, 0))),
        compiler_params=_cparams("parallel"),
        name="ssd_in",
    )(x, x, x, _row(g), _row(sh), _row(sc), w_in, wd, conv_w, _row(conv_b), db)


def _expand_heads(arr, e_ref):
    hi = arr.astype(BF16)
    lo = (arr - hi.astype(F32)).astype(BF16)
    e = e_ref[...]
    return jnp.dot(hi, e, preferred_element_type=F32) + jnp.dot(lo, e, preferred_element_type=F32)


def _ssd_prologue(dt_ref, a_row, tri, e_ref, need_y):
    dt = dt_ref[0]
    a = dt * a_row
    acs = jnp.dot(tri, a, preferred_element_type=F32, precision=HIGHEST)
    total = jnp.sum(a, axis=0, keepdims=True)
    ctx = dict(keep=tri > 0.5, acs=acs)
    ctx["wend_x"] = _expand_heads(jnp.exp(total - acs) * dt, e_ref)
    ctx["etot_x"] = _expand_heads(jnp.broadcast_to(jnp.exp(total), (SUBLANES, LANES)), e_ref)[0:1, :]
    if need_y:
        ctx["eacs_x"] = _expand_heads(jnp.exp(acs), e_ref)
        ctx["acs_t"] = acs.T
        ctx["dt_t"] = dt.T
    return ctx


def _ssd_prepare(ctx, xs_ref, bm_ref, cm_ref, g, need_y):
    q = SSD_CHUNK
    ppg = SSD_HEADS // 2 // SSD_GROUPS
    bg = bm_ref[:, g * SSD_STATE:(g + 1) * SSD_STATE]
    ops = dict(cg=cm_ref[:, g * SSD_STATE:(g + 1) * SSD_STATE].astype(BF16), bgt=bg.T.astype(BF16), xs2=[], xw=[], m2=[])
    if need_y:
        cb = lax.dot_general(ops["cg"], bg.astype(BF16), (((1,), (1,)), ((), ())), preferred_element_type=F32)
        left = lax.broadcasted_iota(jnp.int32, (q, LANES), 1) < SSD_P
    xws = []
    for r in range(ppg):
        pidx = g * ppg + r
        psl = slice(pidx * LANES, (pidx + 1) * LANES)
        xp = xs_ref[:, psl]
        xws.append((xp.astype(F32) * ctx["wend_x"][:, psl]).astype(BF16))
        if need_y:
            ms = []
            for hd in (2 * pidx, 2 * pidx + 1):
                seg = jnp.broadcast_to(ctx["acs"][:, hd:hd + 1], (q, LANES)) - ctx["acs_t"][hd:hd + 1, :]
                lm = jnp.exp(jnp.where(ctx["keep"], seg, -jnp.inf))
                ms.append((cb * lm * ctx["dt_t"][hd:hd + 1, :]).astype(BF16))
            ops["m2"].append(jnp.concatenate(ms, axis=1))
            zero = jnp.zeros_like(xp)
            ops["xs2"].append(jnp.concatenate([jnp.where(left, xp, zero), jnp.where(left, zero, xp)], axis=0))
    ops["xw"] = [jnp.concatenate(xws[2 * t:2 * t + 2], axis=1) for t in range(ppg // 2)]
    return ops


def _ssd_issue(ctx, ops, g, h_scr, y_ref, need_y):
    ppg = SSD_HEADS // 2 // SSD_GROUPS
    for t in range(ppg // 2):
        p0 = g * ppg + 2 * t
        qsl = slice(p0 * LANES, (p0 + 2) * LANES)
        hs = jnp.concatenate([h_scr[p0], h_scr[p0 + 1]], axis=1)
        if need_y:
            yd = jnp.concatenate([jnp.dot(ops["m2"][2 * t + e], ops["xs2"][2 * t + e], preferred_element_type=F32)
                                  for e in range(2)], axis=1)
            yoff = jnp.dot(ops["cg"], hs.astype(BF16), preferred_element_type=F32) * ctx["eacs_x"][:, qsl]
            y_ref[:, qsl] = (yd + yoff).astype(y_ref.dtype)
        st = jnp.dot(ops["bgt"], ops["xw"][t], preferred_element_type=F32)
        hn = hs * ctx["etot_x"][:, qsl] + st
        h_scr[p0] = hn[:, :LANES]
        h_scr[p0 + 1] = hn[:, LANES:]


def _ssd_scan_kernel(xsf_ref, xsb_ref, bmf_ref, bmb_ref, cmf_ref, cmb_ref, dtf_ref, dtb_ref, a_ref, tri_ref, e_ref,
                     h0_ref, *out_refs, nc, need_y):
    yf_ref, yb_ref = out_refs[:2] if need_y else (None, None)
    hfin_ref, h_scr = out_refs[-2:]
    s = pl.program_id(0)

    @pl.when(s == 0)
    def _():
        h_scr[...] = h0_ref[...]

    dirs = ((xsf_ref, bmf_ref, cmf_ref, dtf_ref, yf_ref), (xsb_ref, bmb_ref, cmb_ref, dtb_ref, yb_ref))
    ctxs = [_ssd_prologue(dirs[d][3], a_ref[d], tri_ref[d], e_ref, need_y) for d in range(2)]
    stages = [(d, g) for g in range(SSD_GROUPS) for d in range(2)]
    prep = lambda d, g: _ssd_prepare(ctxs[d], dirs[d][0], dirs[d][1], dirs[d][2], g, need_y)
    pending = prep(*stages[0])
    for idx, (d, g) in enumerate(stages):
        ops = pending
        if idx + 1 < len(stages):
            pending = prep(*stages[idx + 1])
        _ssd_issue(ctxs[d], ops, g, h_scr.at[d], dirs[d][4], need_y)

    @pl.when(s == nc - 1)
    def _():
        hfin_ref[...] = h_scr[...]


def _ssd_scan(xs, bm, cm, dt2, a_log, h0, need_y):
    lx = xs.shape[0]
    q = SSD_CHUNK
    nc = lx // q
    npair = SSD_HEADS // 2
    a = -jnp.exp(a_log.astype(F32))
    a_pad = jnp.pad(a, ((0, 0), (0, LANES - SSD_HEADS))).reshape(2, 1, LANES)
    lower = np.tril(np.ones((q, q), np.float32))
    tri = jnp.asarray(np.stack([lower, lower.T]))
    expand = jnp.asarray(np.kron(np.eye(LANES)[:, :SSD_HEADS], np.ones((1, SSD_P))), BF16)
    fwd = lambda s: s
    bwd = lambda s: nc - 1 - s
    rows = lambda w, idx: pl.BlockSpec((q, w), lambda s: (idx(s), 0))
    full = lambda a_: pl.BlockSpec(a_.shape, lambda s: (0,) * a_.ndim)
    y_shapes = [jax.ShapeDtypeStruct((lx, SSD_INNER), BF16)] * 2 if need_y else []
    y_specs = [rows(SSD_INNER, fwd), rows(SSD_INNER, bwd)] if need_y else []
    outs = pl.pallas_call(
        functools.partial(_ssd_scan_kernel, nc=nc, need_y=need_y),
        out_shape=y_shapes + [jax.ShapeDtypeStruct(h0.shape, F32)],
        grid=(nc,),
        in_specs=[
            rows(SSD_INNER, fwd), rows(SSD_INNER, bwd), rows(SSD_BC, fwd), rows(SSD_BC, bwd),
            rows(SSD_BC, fwd), rows(SSD_BC, bwd),
            pl.BlockSpec((1, q, LANES), lambda s: (0, s, 0)),
            pl.BlockSpec((1, q, LANES), lambda s: (1, nc - 1 - s, 0)),
            full(a_pad), full(tri), full(expand), full(h0),
        ],
        out_specs=y_specs + [full(h0)],
        scratch_shapes=[pltpu.VMEM((2, npair, SSD_STATE, 2 * SSD_P), F32)],
        compiler_params=_cparams("arbitrary"),
        name="ssd_scan",
    )(xs, xs, bm, bm, cm, cm, dt2, dt2, a_pad, tri, expand, h0)
    return (tuple(outs[:2]) if need_y else None), outs[-1]


def _ssd_out_kernel(x_ref, yf_ref, yb_ref, xs_ref, zg_ref, dsk_ref, ng_ref, w_ref, gate_ref, o_ref):
    y = yf_ref[...].astype(F32) + yb_ref[...].astype(F32) + xs_ref[...].astype(F32) * dsk_ref[...]
    y = y * _silu(zg_ref[...].astype(F32))
    gw = SSD_INNER // SSD_GROUPS
    parts = []
    for g in range(SSD_GROUPS):
        yg = y[:, g * gw:(g + 1) * gw]
        ms = jnp.mean(yg * yg, axis=-1, keepdims=True)
        parts.append(yg * lax.rsqrt(ms + NORM_EPS) * ng_ref[:, g * gw:(g + 1) * gw])
    yn = jnp.concatenate(parts, axis=1).astype(BF16)
    o_ref[...] = x_ref[...] + gate_ref[...] * jnp.dot(yn, w_ref[...], preferred_element_type=F32)


def _ssd_out(x, yfb, xs, zg, d_skip, norm_g, w_out, gate):
    lx = x.shape[0]
    tm = min(ROW_TILE // 2, lx)
    rowblk = lambda w: pl.BlockSpec((tm, w), lambda i: (i, 0))
    vecw = pl.BlockSpec((1, SSD_INNER), lambda i: (0, 0))
    return pl.pallas_call(
        _ssd_out_kernel,
        out_shape=jax.ShapeDtypeStruct((lx, D), F32),
        grid=(lx // tm,),
        in_specs=[rowblk(D), rowblk(SSD_INNER), rowblk(SSD_INNER), rowblk(SSD_INNER),
                  rowblk(SSD_INNER), vecw, vecw, _resident((SSD_INNER, D)),
                  pl.BlockSpec((1, D), lambda i: (0, 0))],
        out_specs=rowblk(D),
        compiler_params=_cparams("parallel"),
        name="ssd_out",
    )(x, yfb[0], yfb[1], xs, zg, _row(jnp.repeat(d_skip, SSD_P)), _row(norm_g), w_out, _row(gate))


def kernel(x, c, ctx, c_ctx, norm1_g, norm2_g, mod_w, mod_b, ffn_w_in, ffn_w_out, final_g, gm_w_in, gm_ln_g, gm_ln_b, gm_ws, gm_bs, gm_w_out, at_w_qkv, at_q_g, at_k_g, at_w_out, hy_w_in, hy_conv_w, hy_conv_b, hy_filt_w1, hy_filt_b1, hy_filt_w2, hy_filt_b2, hy_filt_w3, hy_filt_freq, hy_skip, hy_w_out, ssd_w_in, ssd_conv_w, ssd_conv_b, ssd_a_log, ssd_dt_bias, ssd_d_skip, ssd_norm_g, ssd_w_out):
    batch, seq, _ = x.shape
    assert batch == 1, "kernels are written for a single sequence"
    nctx = ctx.shape[1]
    xl = x[0]
    z = ctx[0]
    mods = _modulation(c[0], c_ctx, mod_w, mod_b)
    bf = lambda w: w.astype(BF16)

    for i in range(DEPTH):
        m, j = i % 4, i // 4
        want_ctx = i < DEPTH - 1
        ml = [mods[i, 0, k * D:(k + 1) * D] for k in range(6)]
        mc = [mods[i, 1, k * D:(k + 1) * D] for k in range(6)]
        n1 = norm1_g[i]
        pend_l = pend_c = None
        if m == 0:
            p = (bf(gm_w_in[j]), gm_ln_g[j], gm_ln_b[j], bf(gm_ws[j]), gm_bs[j], bf(gm_w_out[j]))
            xl = _gmlp(xl, n1, ml[0], ml[1], ml[2], *p)
            if want_ctx:
                z = _gmlp(z, n1, mc[0], mc[1], mc[2], *p)
        elif m == 1:
            wq, wo = bf(at_w_qkv[j]), bf(at_w_out[j])
            qt_l, k_l, vt_l = _qkv(xl, n1, ml[0], ml[1], wq, at_q_g[j], at_k_g[j], rope=True)
            qt_c, k_c, vt_c = _qkv(z, n1, mc[0], mc[1], wq, at_q_g[j], at_k_g[j], rope=False)
            k_all = jnp.concatenate([k_c, k_l], axis=1)
            vt_all = jnp.concatenate([vt_c, vt_l], axis=1)
            stot = nctx + seq
            ts = next(t for t in FLASH_KV_TILES if stot % t == 0)
            score_bound = (HD ** 0.5 * LOG2E) * jnp.max(jnp.abs(at_q_g[j])) * jnp.max(jnp.abs(at_k_g[j]))
            o_l = lax.cond(score_bound <= FLASH_SCORE_BOUND,
                           lambda: _flash(qt_l, k_all, vt_all, stot, ts, bounded=True),
                           lambda: _flash(qt_l, k_all, vt_all, stot, ts, bounded=False))
            pend_l = ("proj", o_l, wo, ml[2])
            if want_ctx:
                pend_c = ("proj", _flash(qt_c, k_all, vt_all, nctx, nctx), wo, mc[2])
        elif m == 2:
            p = (bf(hy_w_in[j]), hy_conv_w[j], hy_conv_b[j], hy_filt_w1[j], hy_filt_b1[j], hy_filt_w2[j],
                 hy_filt_b2[j], hy_filt_w3[j], hy_filt_freq[j], hy_skip[j], bf(hy_w_out[j]))
            pend_l = _hyena(xl, n1, ml[0], ml[1], ml[2], *p)
            if want_ctx:
                pend_c = _hyena(z, n1, mc[0], mc[1], mc[2], *p)
        else:
            win, wo = bf(ssd_w_in[j]), bf(ssd_w_out[j])
            pin = (win, ssd_conv_w[j], ssd_conv_b[j], ssd_dt_bias[j])
            zg_c, xs_c, bm_c, cm_c, dt_c = _ssd_in(z, n1, mc[0], mc[1], *pin)
            zg_l, xs_l, bm_l, cm_l, dt_l = _ssd_in(xl, n1, ml[0], ml[1], *pin)
            h0 = jnp.zeros((2, SSD_HEADS // 2, SSD_STATE, 2 * SSD_P), F32)
            y_c, h_ctx = _ssd_scan(xs_c, bm_c, cm_c, dt_c, ssd_a_log[j], h0, want_ctx)
            y_l, _ = _ssd_scan(xs_l, bm_l, cm_l, dt_l, ssd_a_log[j], h_ctx, True)
            xl = _ssd_out(xl, y_l, xs_l, zg_l, ssd_d_skip[j], ssd_norm_g[j], wo, ml[2])
            if want_ctx:
                z = _ssd_out(z, y_c, xs_c, zg_c, ssd_d_skip[j], ssd_norm_g[j], wo, mc[2])
        wi, wo2 = bf(ffn_w_in[i]), bf(ffn_w_out[i])
        xl = _ffn(xl, pend_l, norm2_g[i], ml[3], ml[4], ml[5], wi, wo2, final_g, final=(i == DEPTH - 1))
        if want_ctx:
            z = _ffn(z, pend_c, norm2_g[i], mc[3], mc[4], mc[5], wi, wo2, final_g, final=False)
    return xl[None]
```

```python
import functools
import math

import numpy as np
import jax
import jax.numpy as jnp
from jax import lax
from jax.experimental import pallas as pl
from jax.experimental.pallas import tpu as pltpu

F32 = jnp.float32
BF16 = jnp.bfloat16
HIGHEST = lax.Precision.HIGHEST

D = 1024
DEPTH = 4
GRID_W = 64
NORM_EPS = 1e-6
FFN_HIDDEN = 2816
GM_CHUNK = 128
GM_WIDTH = 2 * D
GM_GROUPS = 8
GM_GW = GM_WIDTH // GM_GROUPS
HD = 64
QH = D // HD
KVH = 4
ROPE_THETA = 10000.0
LOG2E = math.log2(math.e)
FLASH_SCORE_BOUND = -1.0
FLASH_LOOKAHEAD = 2
HY_BANDS = 16
HY_EMB = 1 + 2 * HY_BANDS
HY_FILT_W = 64
HY_MAX_DECAY = math.log(1e-2) / 0.3
HY_MIN_DECAY = math.log(1e-2) / 1.5
SSD_INNER = 2 * D
SSD_P = 64
SSD_HEADS = SSD_INNER // SSD_P
SSD_GROUPS = 4
SSD_STATE = 128
SSD_CHUNK = 128
SSD_BC = SSD_GROUPS * SSD_STATE
SSD_CONV_DIM = SSD_INNER + 2 * SSD_BC

LANES = 128
SUBLANES = 8
VMEM_LIMIT_BYTES = 56 * 1024 * 1024
DFT_N2 = LANES
ROW_TILE = 512
FLASH_Q_TILE = 128
FLASH_KV_TILES = (3328, 1280, 1024, 512, 256)
FLASH_CHUNK = 2 * LANES


def _cparams(*sem):
    return pltpu.CompilerParams(dimension_semantics=sem, vmem_limit_bytes=VMEM_LIMIT_BYTES)


def _row(v):
    return v.reshape(1, -1)


def _normmod(x, g, shift, scale):
    ms = jnp.mean(x * x, axis=-1, keepdims=True)
    return x * lax.rsqrt(ms + NORM_EPS) * g * (1.0 + scale) + shift


def _silu(x):
    return x * jax.nn.sigmoid(x)


def _mod_kernel(cl_ref, cc_ref, w_ref, b_ref, o_ref):
    w = w_ref[0]
    for r, c_ref in enumerate((cl_ref, cc_ref)):
        a = _silu(c_ref[...])
        o_ref[0, r:r + 1, :] = jnp.sum(a * w, axis=0, keepdims=True) + b_ref[0]


def _modulation(c, c_ctx, mod_w, mod_b):
    n6 = 6 * D
    tn = n6 // 4
    depth = mod_w.shape[0]
    return pl.pallas_call(
        _mod_kernel,
        out_shape=jax.ShapeDtypeStruct((depth, 2, n6), F32),
        grid=(depth, n6 // tn),
        in_specs=[
            pl.BlockSpec((D, 1), lambda i, n: (0, 0)),
            pl.BlockSpec((D, 1), lambda i, n: (0, 0)),
            pl.BlockSpec((1, D, tn), lambda i, n: (i, 0, n)),
            pl.BlockSpec((1, 1, tn), lambda i, n: (i, 0, n)),
        ],
        out_specs=pl.BlockSpec((1, 2, tn), lambda i, n: (i, 0, n)),
        compiler_params=_cparams("parallel", "parallel"),
        name="modulation",
    )(c.reshape(D, 1), c_ctx.reshape(D, 1), mod_w, mod_b.reshape(depth, 1, n6))


def _ffn_kernel(x_ref, *refs, mode, final):
    x = x_ref[...]
    if mode == "proj":
        a_ref, wm_ref, g1_ref = refs[:3]
        refs = refs[3:]
        x = x + g1_ref[...] * jnp.dot(a_ref[...], wm_ref[...], preferred_element_type=F32)
    elif mode == "hyena":
        x0_ref, yt_ref, wm_ref, g1_ref = refs[:4]
        refs = refs[4:]
        a = (x0_ref[...] * yt_ref[...].T).astype(BF16)
        x = x + g1_ref[...] * jnp.dot(a, wm_ref[...], preferred_element_type=F32)
    g_ref, sh_ref, sc_ref, gate_ref, wi_ref, wo_ref, fg_ref, o_ref = refs
    h = _normmod(x, g_ref[...], sh_ref[...], sc_ref[...]).astype(BF16)
    a = jnp.dot(h, wi_ref[:, :FFN_HIDDEN], preferred_element_type=F32)
    u = jnp.dot(h, wi_ref[:, FFN_HIDDEN:], preferred_element_type=F32)
    act = (_silu(a) * u).astype(BF16)
    y = x + gate_ref[...] * jnp.dot(act, wo_ref[...], preferred_element_type=F32)
    if final:
        ms = jnp.mean(y * y, axis=-1, keepdims=True)
        y = y * lax.rsqrt(ms + NORM_EPS) * fg_ref[...]
    o_ref[...] = y


def _resident(shape):
    return pl.BlockSpec(shape, lambda *_: (0,) * len(shape), pipeline_mode=pl.Buffered(1))


def _ffn(x, pending, g, sh, sc, gate, w_in, w_out, final_g, final):
    lx = x.shape[0]
    tm = min(ROW_TILE, lx)
    vec = pl.BlockSpec((1, D), lambda i: (0, 0))
    rows = pl.BlockSpec((tm, D), lambda i: (i, 0))
    mode, pre_args, pre_specs = "none", (), []
    if pending is not None:
        mode = pending[0]
        if mode == "proj":
            _, a, wm, g1 = pending
            pre_args, pre_specs = (a, wm, _row(g1)), [rows, _resident(wm.shape), vec]
        else:
            _, x0, yt, wm, g1 = pending
            pre_args = (x0, yt, wm, _row(g1))
            pre_specs = [rows, pl.BlockSpec((D, tm), lambda i: (0, i)), _resident(wm.shape), vec]
    return pl.pallas_call(
        functools.partial(_ffn_kernel, mode=mode, final=final),
        out_shape=jax.ShapeDtypeStruct((lx, D), F32),
        grid=(lx // tm,),
        in_specs=[rows] + pre_specs + [
            vec, vec, vec, vec,
            _resident((D, 2 * FFN_HIDDEN)),
            _resident((FFN_HIDDEN, D)),
            vec,
        ],
        out_specs=rows,
        compiler_params=_cparams("parallel"),
        name="ffn",
    )(x, *pre_args, _row(g), _row(sh), _row(sc), _row(gate), w_in, w_out, _row(final_g))


def _gmlp_kernel(x_ref, g_ref, sh_ref, sc_ref, gate_ref, win_ref, lng_ref, lnb_ref, ws_ref, bs_ref, wout_ref,
                 o_ref, *, tm):
    nsub = max(1, tm // (2 * GM_CHUNK))
    rs = tm // nsub

    def project(i):
        x = x_ref[i * rs:(i + 1) * rs, :]
        h = _normmod(x, g_ref[...], sh_ref[...], sc_ref[...]).astype(BF16)
        return jnp.dot(h, win_ref[...], preferred_element_type=F32)

    def mix(i, t):
        t = 0.5 * t * (1.0 + lax.erf(t * (1.0 / math.sqrt(2.0))))
        u = t[:, :GM_WIDTH]
        v = t[:, GM_WIDTH:]
        mu = jnp.mean(v, axis=-1, keepdims=True)
        vc = v - mu
        var = jnp.mean(vc * vc, axis=-1, keepdims=True)
        v = (vc * lax.rsqrt(var + NORM_EPS) * lng_ref[...] + lnb_ref[...]).astype(BF16)
        rows = []
        for q in range(rs // GM_CHUNK):
            cols = []
            for gidx in range(GM_GROUPS):
                vq = v[q * GM_CHUNK:(q + 1) * GM_CHUNK, gidx * GM_GW:(gidx + 1) * GM_GW]
                bias = bs_ref[gidx]
                m = jnp.dot(ws_ref[gidx], vq, preferred_element_type=F32)
                cols.append(m + jnp.concatenate([bias] * (GM_GW // LANES), axis=1))
            rows.append(jnp.concatenate(cols, axis=1))
        gated = (u * jnp.concatenate(rows, axis=0)).astype(BF16)
        y = jnp.dot(gated, wout_ref[...], preferred_element_type=F32)
        o_ref[i * rs:(i + 1) * rs, :] = x_ref[i * rs:(i + 1) * rs, :] + gate_ref[...] * y

    t_next = project(0)
    for i in range(nsub):
        t = t_next
        if i + 1 < nsub:
            t_next = project(i + 1)
        mix(i, t)


def _gmlp(x, g, sh, sc, gate, w_in, ln_g, ln_b, ws, bs, w_out):
    lx = x.shape[0]
    tm = min(ROW_TILE, lx)
    vec = pl.BlockSpec((1, D), lambda i: (0, 0))
    vecw = pl.BlockSpec((1, GM_WIDTH), lambda i: (0, 0))
    bsb = jnp.broadcast_to(bs[:, :, None], (GM_GROUPS, GM_CHUNK, LANES))
    return pl.pallas_call(
        functools.partial(_gmlp_kernel, tm=tm),
        out_shape=jax.ShapeDtypeStruct((lx, D), F32),
        grid=(lx // tm,),
        in_specs=[
            pl.BlockSpec((tm, D), lambda i: (i, 0)),
            vec, vec, vec, vec,
            _resident((D, 2 * GM_WIDTH)),
            vecw, vecw,
            pl.BlockSpec((GM_GROUPS, GM_CHUNK, GM_CHUNK), lambda i: (0, 0, 0)),
            pl.BlockSpec((GM_GROUPS, GM_CHUNK, LANES), lambda i: (0, 0, 0)),
            _resident((GM_WIDTH, D)),
        ],
        out_specs=pl.BlockSpec((tm, D), lambda i: (i, 0)),
        compiler_params=_cparams("parallel"),
        name="gmlp",
    )(x, _row(g), _row(sh), _row(sc), _row(gate), w_in, _row(ln_g), _row(ln_b), ws, bsb, w_out)


def _group_sumsq(t, e_ref):
    sq = t * t
    hi = sq.astype(BF16)
    lo = (sq - hi.astype(F32)).astype(BF16)
    e = e_ref[...]
    w = e.shape[0]
    outs = []
    for j in range(t.shape[1] // w):
        sl = slice(j * w, (j + 1) * w)
        outs.append(jnp.dot(hi[:, sl], e, preferred_element_type=F32) + jnp.dot(lo[:, sl], e, preferred_element_type=F32))
    return jnp.concatenate(outs, axis=1)


def _rope(t, cosf, sinf):
    w = t.shape[1]
    lane = lax.broadcasted_iota(jnp.int32, t.shape, 1)
    first = (lane % HD) < (HD // 2)
    partner = jnp.where(first, pltpu.roll(t, w - HD // 2, axis=1), pltpu.roll(t, HD // 2, axis=1))
    reps = w // LANES
    c = jnp.concatenate([cosf] * reps, axis=1)
    s = jnp.concatenate([sinf] * reps, axis=1)
    return t * c + partner * s


def _qkv_kernel(x_ref, g_ref, sh_ref, sc_ref, w_ref, qg_ref, kg_ref, e_ref, cos_ref, sin_ref,
                qt_ref, k_ref, vt_ref, *, rope):
    h = _normmod(x_ref[...], g_ref[...], sh_ref[...], sc_ref[...]).astype(BF16)
    qkv = jnp.dot(h, w_ref[...], preferred_element_type=F32)
    q = qkv[:, :D]
    k = qkv[:, D:D + KVH * HD]
    v = qkv[:, D + KVH * HD:]
    q = q * lax.rsqrt(_group_sumsq(q, e_ref) * (1.0 / HD) + NORM_EPS) * qg_ref[...]
    k = k * lax.rsqrt(_group_sumsq(k, e_ref) * (1.0 / HD) + NORM_EPS) * kg_ref[...]
    if rope:
        q = _rope(q, cos_ref[...], sin_ref[...])
        k = _rope(k, cos_ref[...], sin_ref[...])
    qt_ref[...] = (q * (HD ** -0.5 * LOG2E)).T.astype(BF16)
    for gidx in range(KVH):
        k_ref[gidx] = k[:, gidx * HD:(gidx + 1) * HD].astype(BF16)
    vt_ref[...] = v.T.astype(BF16)


def _qkv(x, g, sh, sc, w_qkv, q_g, k_g, rope):
    lx = x.shape[0]
    tm = min(ROW_TILE, lx)
    vec = pl.BlockSpec((1, D), lambda i: (0, 0))
    kvw = KVH * HD
    rows = lx // GRID_W
    row = jnp.repeat(jnp.arange(rows, dtype=F32), GRID_W)
    col = jnp.tile(jnp.arange(GRID_W, dtype=F32), rows)
    n = HD // 4
    inv = ROPE_THETA ** (-jnp.arange(n, dtype=F32) / n)
    ang = jnp.concatenate([row[:, None] * inv, col[:, None] * inv], axis=-1)
    cos, sin = jnp.cos(ang), jnp.sin(ang)
    cosf = jnp.tile(jnp.concatenate([cos, cos], axis=-1), (1, LANES // HD))
    sinf = jnp.tile(jnp.concatenate([-sin, sin], axis=-1), (1, LANES // HD))
    eblk = jnp.asarray(np.kron(np.eye(kvw // HD), np.ones((HD, HD))), BF16)
    tab = pl.BlockSpec((tm, LANES), lambda i: (i, 0))
    return pl.pallas_call(
        functools.partial(_qkv_kernel, rope=rope),
        out_shape=(jax.ShapeDtypeStruct((D, lx), BF16),
                   jax.ShapeDtypeStruct((KVH, lx, HD), BF16),
                   jax.ShapeDtypeStruct((kvw, lx), BF16)),
        grid=(lx // tm,),
        in_specs=[
            pl.BlockSpec((tm, D), lambda i: (i, 0)),
            vec, vec, vec,
            _resident((D, D + 2 * kvw)),
            vec,
            pl.BlockSpec((1, kvw), lambda i: (0, 0)),
            pl.BlockSpec((kvw, kvw), lambda i: (0, 0)),
            tab, tab,
        ],
        out_specs=(pl.BlockSpec((D, tm), lambda i: (0, i)),
                   pl.BlockSpec((KVH, tm, HD), lambda i: (0, i, 0)),
                   pl.BlockSpec((kvw, tm), lambda i: (0, i))),
        compiler_params=_cparams("parallel"),
        name="qkv_proj",
    )(x, _row(g), _row(sh), _row(sc), w_qkv, _row(jnp.tile(q_g, QH)), _row(jnp.tile(k_g, KVH)), eblk, cosf, sinf)


def _flash_kernel(qt_ref, k_ref, vt_ref, o_ref, qg_scr, m_scr, l_scr, acc_scr, *, tq, ts, tc, nkv, bounded):
    j = pl.program_id(1)
    gq = QH // KVH
    mcols = gq * tq

    @pl.when(j == 0)
    def _():
        for h in range(QH):
            qg_scr[h // gq, :, (h % gq) * tq:(h % gq + 1) * tq] = qt_ref[h * HD:(h + 1) * HD, :]
        m_scr[...] = jnp.full(m_scr.shape, -jnp.inf, F32)
        l_scr[...] = jnp.zeros_like(l_scr)
        acc_scr[...] = jnp.zeros_like(acc_scr)

    stages = [(g, c) for c in range(ts // tc) for g in range(KVH)]

    def scores(g, c):
        return jnp.dot(k_ref[g, c * tc:(c + 1) * tc, :], qg_scr[g], preferred_element_type=F32)

    pending = [scores(*st) for st in stages[:FLASH_LOOKAHEAD]]
    for idx, (g, c) in enumerate(stages):
        s = pending.pop(0)
        if idx + FLASH_LOOKAHEAD < len(stages):
            pending.append(scores(*stages[idx + FLASH_LOOKAHEAD]))
        vt = vt_ref[g * HD:(g + 1) * HD, c * tc:(c + 1) * tc]
        if bounded:
            p = jnp.exp2(s)
            l_scr[g] += jnp.sum(p, axis=0, keepdims=True)
            acc_scr[g] += jnp.dot(vt, p.astype(BF16), preferred_element_type=F32)
        else:
            m_prev = m_scr[g]
            m_new = jnp.maximum(m_prev, jnp.max(s, axis=0, keepdims=True))
            alpha = jnp.exp2(m_prev - m_new)
            p = jnp.exp2(s - m_new)
            l_scr[g] = alpha * l_scr[g] + jnp.sum(p, axis=0, keepdims=True)
            acc_scr[g] = alpha * acc_scr[g] + jnp.dot(vt, p.astype(BF16), preferred_element_type=F32)
            m_scr[g] = m_new

    @pl.when(j == nkv - 1)
    def _():
        rows = []
        for g in range(KVH):
            o = acc_scr[g] / l_scr[g]
            rows += [o[:, r * tq:(r + 1) * tq] for r in range(gq)]
        o_ref[...] = jnp.concatenate(rows, axis=0).T.astype(o_ref.dtype)


def _flash(qt, k, vt, s_len, ts, bounded=False):
    lq = qt.shape[1]
    tq = min(FLASH_Q_TILE, lq)
    nkv = s_len // ts
    gq = QH // KVH
    kvw = KVH * HD
    tc = FLASH_CHUNK if ts % FLASH_CHUNK == 0 else LANES
    return pl.pallas_call(
        functools.partial(_flash_kernel, tq=tq, ts=ts, tc=tc, nkv=nkv, bounded=bounded),
        out_shape=jax.ShapeDtypeStruct((lq, D), BF16),
        grid=(lq // tq, nkv),
        in_specs=[
            pl.BlockSpec((D, tq), lambda i, j: (0, i)),
            pl.BlockSpec((KVH, ts, HD), lambda i, j: (0, j, 0)),
            pl.BlockSpec((kvw, ts), lambda i, j: (0, j)),
        ],
        out_specs=pl.BlockSpec((tq, D), lambda i, j: (i, 0)),
        scratch_shapes=[
            pltpu.VMEM((KVH, HD, gq * tq), BF16),
            pltpu.VMEM((KVH, 1, gq * tq), F32),
            pltpu.VMEM((KVH, 1, gq * tq), F32),
            pltpu.VMEM((KVH, HD, gq * tq), F32),
        ],
        compiler_params=_cparams("parallel", "arbitrary"),
        name="flash_attn",
    )(qt, k, vt)


def _halo_specs(tm, lx):
    nb = lx // SUBLANES
    step = tm // SUBLANES
    prev = pl.BlockSpec((SUBLANES, D), lambda i: (jnp.maximum(i * step - 1, 0), 0))
    nxt = pl.BlockSpec((SUBLANES, D), lambda i: (jnp.minimum((i + 1) * step, nb - 1), 0))
    return prev, nxt


def _conv3(p_main, p_halo, cw, cb, first, last):
    tm = p_main.shape[0]
    rid = lax.broadcasted_iota(jnp.int32, p_main.shape, 0)
    before = jnp.where(first, 0.0, p_halo[SUBLANES - 1:SUBLANES, :])
    after = jnp.where(last, 0.0, p_halo[SUBLANES:SUBLANES + 1, :])
    up = jnp.where(rid == 0, before, pltpu.roll(p_main, 1, axis=0))
    dn = jnp.where(rid == tm - 1, after, pltpu.roll(p_main, tm - 1, axis=0))
    return cw[0:1, :] * up + cw[1:2, :] * p_main + cw[2:3, :] * dn + cb


def _norm_halo(xm_ref, xp_ref, xn_ref, g_ref, sh_ref, sc_ref):
    g, sh, sc = g_ref[...], sh_ref[...], sc_ref[...]
    h = _normmod(xm_ref[...], g, sh, sc).astype(BF16)
    hh = jnp.concatenate([_normmod(xp_ref[...], g, sh, sc), _normmod(xn_ref[...], g, sh, sc)], axis=0).astype(BF16)
    return h, hh


def _hy_in_kernel(xm_ref, xp_ref, xn_ref, g_ref, sh_ref, sc_ref, w_ref, cw_ref, cb_ref, x0_ref, ut_ref, *, nt):
    i = pl.program_id(0)
    first, last = i == 0, i == nt - 1
    h, hh = _norm_halo(xm_ref, xp_ref, xn_ref, g_ref, sh_ref, sc_ref)

    def project(b):
        sl = slice(b * D, (b + 1) * D)
        return (jnp.dot(h, w_ref[:, sl], preferred_element_type=F32),
                jnp.dot(hh, w_ref[:, sl], preferred_element_type=F32))

    def conv(b, p):
        sl = slice(b * D, (b + 1) * D)
        return _conv3(p[0], p[1], cw_ref[:, sl], cb_ref[:, sl], first, last)

    p0 = project(0)
    p1 = project(1)
    x0_ref[...] = conv(0, p0)
    p2 = project(2)
    x1 = conv(1, p1)
    ut_ref[...] = (x1 * conv(2, p2)).T.astype(ut_ref.dtype)


def _hy_in(x, g, sh, sc, w_in, conv_w, conv_b):
    lx = x.shape[0]
    tm = min(ROW_TILE, lx)
    nt = lx // tm
    vec = pl.BlockSpec((1, D), lambda i: (0, 0))
    prev, nxt = _halo_specs(tm, lx)
    return pl.pallas_call(
        functools.partial(_hy_in_kernel, nt=nt),
        out_shape=(jax.ShapeDtypeStruct((lx, D), F32), jax.ShapeDtypeStruct((D, lx), BF16)),
        grid=(nt,),
        in_specs=[
            pl.BlockSpec((tm, D), lambda i: (i, 0)), prev, nxt,
            vec, vec, vec,
            _resident((D, 3 * D)),
            pl.BlockSpec((3, 3 * D), lambda i: (0, 0)),
            pl.BlockSpec((1, 3 * D), lambda i: (0, 0)),
        ],
        out_specs=(pl.BlockSpec((tm, D), lambda i: (i, 0)), pl.BlockSpec((D, tm), lambda i: (0, i))),
        compiler_params=_cparams("parallel"),
        name="hyena_in",
    )(x, x, x, _row(g), _row(sh), _row(sc), w_in, conv_w, _row(conv_b))


def _hy_filter_kernel(ft_ref, t_ref, w1_ref, b1_ref, w2_ref, b2_ref, w3h_ref, w3l_ref, fr_ref, dl_ref, sk_ref, kt_ref,
                      *, tm, ltrue, lpad):
    i = pl.program_id(0)
    feats = ft_ref[...]
    fr = fr_ref[...]
    hid = jnp.sin(fr * (jnp.dot(w1_ref[...], feats, preferred_element_type=F32, precision=HIGHEST) + b1_ref[...]))
    hid = jnp.sin(fr * (jnp.dot(w2_ref[...], hid, preferred_element_type=F32, precision=HIGHEST) + b2_ref[...]))
    hh = hid.astype(BF16)
    hl = (hid - hh.astype(F32)).astype(BF16)
    w3h = w3h_ref[...]
    kt = (jnp.dot(w3h, hh, preferred_element_type=F32) + jnp.dot(w3h, hl, preferred_element_type=F32)
          + jnp.dot(w3l_ref[...], hh, preferred_element_type=F32))
    kt = kt * jnp.exp(-dl_ref[...] * t_ref[...])
    if lpad != ltrue:
        pos = lax.broadcasted_iota(jnp.int32, kt.shape, 1) + i * tm
        kt = jnp.where(pos < ltrue, kt, 0.0)
    kt_ref[...] = kt.astype(kt_ref.dtype)

    @pl.when(i == 0)
    def _():
        row = lax.broadcasted_iota(jnp.int32, (2 * D, 1), 0)
        kt_ref[:, 0:1] = jnp.where(row < D, kt[:, 0:1] + sk_ref[...], 0.0).astype(kt_ref.dtype)


def _hy_filter(ltrue, lpad, w1, b1, w2, b2, w3, freq, skip):
    tm = min(ROW_TILE, lpad)
    col = lambda v_: v_.reshape(-1, 1)
    w1t = jnp.pad(w1.T, ((0, 0), (0, LANES - HY_EMB)))
    deltas = jnp.abs(jnp.linspace(HY_MIN_DECAY, HY_MAX_DECAY, D, dtype=F32))
    pos = jnp.arange(lpad, dtype=F32)
    zf = jnp.linspace(1e-4, HY_BANDS - 1, HY_BANDS, dtype=F32)[:, None] * (2.0 * math.pi * pos / ltrue)[None, :]
    feats = jnp.concatenate([(pos / (ltrue - 1))[None, :], jnp.cos(zf), -jnp.sin(zf),
                             jnp.zeros((LANES - HY_EMB, lpad), F32)], axis=0)
    w3t = w3.T
    w3h = w3t.astype(BF16)
    w3l = (w3t - w3h.astype(F32)).astype(BF16)
    full = lambda a: pl.BlockSpec(a.shape, lambda i: (0,) * a.ndim)
    args = (w1t, col(b1), w2.T, col(b2), w3h, w3l, col(freq), col(jnp.tile(deltas, 2)),
            col(jnp.concatenate([skip, jnp.zeros((D,), F32)])))
    return pl.pallas_call(
        functools.partial(_hy_filter_kernel, tm=tm, ltrue=ltrue, lpad=lpad),
        out_shape=jax.ShapeDtypeStruct((2 * D, lpad), BF16),
        grid=(lpad // tm,),
        in_specs=[pl.BlockSpec((LANES, tm), lambda i: (0, i)), pl.BlockSpec((1, tm), lambda i: (0, i))]
                 + [full(a) for a in args],
        out_specs=pl.BlockSpec((2 * D, tm), lambda i: (0, i)),
        compiler_params=_cparams("parallel"),
        name="hyena_filter",
    )(feats, feats[0:1, :], *args)


def _dft_consts(nh):
    n1 = 2 * nh
    n = n1 * DFT_N2
    k1 = np.arange(n1)[:, None].astype(np.float64)
    a1 = 2.0 * np.pi * k1 * np.arange(nh)[None, :] / n1
    f1 = np.concatenate([np.cos(a1), -np.sin(a1)], axis=0)
    at = 2.0 * np.pi * ((np.arange(n1)[:, None] * np.arange(DFT_N2)[None, :]) % n) / n
    a2 = 2.0 * np.pi * ((np.arange(DFT_N2)[:, None] * np.arange(DFT_N2)[None, :]) % DFT_N2) / DFT_N2
    c2, s2 = np.cos(a2), np.sin(a2)
    f2 = np.block([[c2, -s2], [s2, c2]])
    g2 = np.block([[c2, s2], [-s2, c2]])
    g1 = np.concatenate([np.cos(a1).T, -np.sin(a1).T], axis=1) / n
    mxu = lambda a: jnp.asarray(a.astype(BF16))
    return mxu(f1), jnp.asarray(np.cos(at), F32), jnp.asarray(np.sin(at), F32), mxu(f2), mxu(g2), mxu(g1)


def _bf16_dot(a, b):
    return jnp.dot(a.astype(BF16), b.astype(BF16), preferred_element_type=F32)


def _hy_conv_kernel(x_ref, kf_ref, kb_ref, f1_ref, twc_ref, tws_ref, f2_ref, g2_ref, g1_ref, o_ref, *, cb, n1):
    twc, tws = twc_ref[...], tws_ref[...]
    f1, f2, g2, g1 = f1_ref[...], f2_ref[...], g2_ref[...], g1_ref[...]
    hc = cb // 2
    nt = 3 * hc

    def dft1(half):
        xs = [r[half * hc + c] for r in (x_ref, kf_ref, kb_ref) for c in range(hc)]
        return _bf16_dot(f1, jnp.concatenate(xs, axis=1))

    def twiddle_rows(a):
        rows = []
        for t in range(nt):
            ar = a[:n1, t * DFT_N2:(t + 1) * DFT_N2]
            ai = a[n1:, t * DFT_N2:(t + 1) * DFT_N2]
            rows.append(jnp.concatenate([ar * twc + ai * tws, ai * twc - ar * tws], axis=1))
        return jnp.concatenate(rows, axis=0)

    def dft2(rows):
        return _bf16_dot(rows, f2)

    def product(spec_all):
        rows = hc * n1
        spec, hf, hb = spec_all[:rows], spec_all[rows:2 * rows], spec_all[2 * rows:]
        hr = hf[:, :DFT_N2] + hb[:, :DFT_N2]
        hi = hf[:, DFT_N2:] - hb[:, DFT_N2:]
        xr, xi = spec[:, :DFT_N2], spec[:, DFT_N2:]
        return jnp.concatenate([xr * hr - xi * hi, xr * hi + xi * hr], axis=1)

    def idft2(y):
        return _bf16_dot(y, g2)

    def twiddle_cols(b):
        cols = []
        for c in range(hc):
            br = b[c * n1:(c + 1) * n1, :DFT_N2]
            bi = b[c * n1:(c + 1) * n1, DFT_N2:]
            cols.append(jnp.concatenate([br * twc - bi * tws, bi * twc + br * tws], axis=0))
        return jnp.concatenate(cols, axis=1)

    def idft1(cols):
        return _bf16_dot(g1, cols)

    def store(half, out):
        for c in range(hc):
            o_ref[half * hc + c] = out[:, c * DFT_N2:(c + 1) * DFT_N2]

    a0 = dft1(0)
    a1 = dft1(1)
    s0 = dft2(twiddle_rows(a0))
    s1 = dft2(twiddle_rows(a1))
    b0 = idft2(product(s0))
    b1 = idft2(product(s1))
    o0 = idft1(twiddle_cols(b0))
    o1 = idft1(twiddle_cols(b1))
    store(0, o0)
    store(1, o1)


def _hy_longconv(ut, kt, lpad):
    nh = lpad // DFT_N2
    n1 = 2 * nh
    cb = max(2 * SUBLANES, min(64, 2048 // n1))
    f1, twc, tws, f2, g2, g1 = _dft_consts(nh)
    consts = (f1, twc, tws, f2, g2, g1)
    full = lambda a: pl.BlockSpec(a.shape, lambda i: (0,) * a.ndim)
    k3 = kt.reshape(2 * D, nh, DFT_N2)
    u3 = ut.reshape(D, nh, DFT_N2)
    nb = D // cb
    y3 = pl.pallas_call(
        functools.partial(_hy_conv_kernel, cb=cb, n1=n1),
        out_shape=jax.ShapeDtypeStruct((D, nh, DFT_N2), F32),
        grid=(nb,),
        in_specs=[pl.BlockSpec((cb, nh, DFT_N2), lambda i: (i, 0, 0)),
                  pl.BlockSpec((cb, nh, DFT_N2), lambda i: (i, 0, 0)),
                  pl.BlockSpec((cb, nh, DFT_N2), lambda i: (i + nb, 0, 0))]
                 + [full(a) for a in consts],
        out_specs=pl.BlockSpec((cb, nh, DFT_N2), lambda i: (i, 0, 0)),
        compiler_params=_cparams("parallel"),
        name="hyena_longconv",
    )(u3, k3, k3, *consts)
    return y3.reshape(D, lpad)


def _hyena(x, g, sh, sc, gate, w_in, conv_w, conv_b, w1, b1, w2, b2, w3, freq, skip, w_out):
    lx = x.shape[0]
    lpad = max(lx, SUBLANES * DFT_N2)
    x0, ut = _hy_in(x, g, sh, sc, w_in, conv_w, conv_b)
    if lpad != lx:
        ut = jnp.pad(ut, ((0, 0), (0, lpad - lx)))
    kt = _hy_filter(lx, lpad, w1, b1, w2, b2, w3, freq, skip)
    yt = _hy_longconv(ut, kt, lpad)[:, :lx]
    return ("hyena", x0, yt, w_out, gate)


def _ssd_in_kernel(xm_ref, xp_ref, xn_ref, g_ref, sh_ref, sc_ref, w_ref, wd_ref, cw_ref, cb_ref, db_ref,
                   zg_ref, xs_ref, bm_ref, cm_ref, dt_ref, *, nt):
    i = pl.program_id(0)
    first, last = i == 0, i == nt - 1
    h, hh = _norm_halo(xm_ref, xp_ref, xn_ref, g_ref, sh_ref, sc_ref)
    cw = SSD_BC
    nchunk = SSD_CONV_DIM // cw

    def project(b):
        sl = slice(SSD_INNER + b * cw, SSD_INNER + (b + 1) * cw)
        return (jnp.dot(h, w_ref[:, sl], preferred_element_type=F32),
                jnp.dot(hh, w_ref[:, sl], preferred_element_type=F32))

    def conv(b, p):
        sl = slice(b * cw, (b + 1) * cw)
        y = _silu(_conv3(p[0], p[1], cw_ref[:, sl], cb_ref[:, sl], first, last))
        if (b + 1) * cw <= SSD_INNER:
            xs_ref[:, sl] = y.astype(xs_ref.dtype)
        elif b == nchunk - 2:
            bm_ref[...] = y
        else:
            cm_ref[...] = y

    pending = [project(0), project(1)]
    for b in range(nchunk):
        p = pending.pop(0)
        if b + 2 < nchunk:
            pending.append(project(b + 2))
        conv(b, p)
    zg_ref[...] = jnp.dot(h, w_ref[:, :SSD_INNER], preferred_element_type=F32).astype(zg_ref.dtype)
    dt = jnp.dot(h, wd_ref[...], preferred_element_type=F32) + db_ref[...]
    dt = jnp.maximum(dt, 0.0) + jnp.log1p(jnp.exp(-jnp.abs(dt)))
    lane = lax.broadcasted_iota(jnp.int32, dt.shape, 1)
    dt = jnp.where((lane % LANES) < SSD_HEADS, dt, 0.0)
    dt_ref[0] = dt[:, :LANES]
    dt_ref[1] = dt[:, LANES:]


def _ssd_in(x, g, sh, sc, w_in, conv_w, conv_b, dt_bias):
    lx = x.shape[0]
    tm = min(ROW_TILE, lx)
    nt = lx // tm
    vec = pl.BlockSpec((1, D), lambda i: (0, 0))
    prev, nxt = _halo_specs(tm, lx)
    wdt = w_in[:, SSD_INNER + SSD_CONV_DIM:]
    pad = LANES - SSD_HEADS
    wd = jnp.concatenate([jnp.pad(wdt[:, :SSD_HEADS], ((0, 0), (0, pad))),
                          jnp.pad(wdt[:, SSD_HEADS:], ((0, 0), (0, pad)))], axis=1)
    db = jnp.pad(dt_bias, ((0, 0), (0, pad))).reshape(1, 2 * LANES)
    full = lambda a: pl.BlockSpec(a.shape, lambda i: (0,) * a.ndim)
    rowblk = lambda w: pl.BlockSpec((tm, w), lambda i: (i, 0))
    return pl.pallas_call(
        functools.partial(_ssd_in_kernel, nt=nt),
        out_shape=(jax.ShapeDtypeStruct((lx, SSD_INNER), BF16), jax.ShapeDtypeStruct((lx, SSD_INNER), BF16),
                   jax.ShapeDtypeStruct((lx, SSD_BC), F32), jax.ShapeDtypeStruct((lx, SSD_BC), F32),
                   jax.ShapeDtypeStruct((2, lx, LANES), F32)),
        grid=(nt,),
        in_specs=[rowblk(D), prev, nxt, vec, vec, vec, _resident(w_in.shape), _resident(wd.shape),
                  pl.BlockSpec((3, SSD_CONV_DIM), lambda i: (0, 0)),
                  pl.BlockSpec((1, SSD_CONV_DIM), lambda i: (0, 0)),
                  pl.BlockSpec((1, 2 * LANES), lambda i: (0, 0))],
        out_specs=(rowblk(SSD_INNER), rowblk(SSD_INNER), rowblk(SSD_BC), rowblk(SSD_BC),
                   pl.BlockSpec((2, tm, LANES), lambda i: (0, i, 0))),
        compiler_params=_cparams("parallel"),
        name="ssd_in",
    )(x, x, x, _row(g), _row(sh), _row(sc), w_in, wd, conv_w, _row(conv_b), db)


def _expand_heads(arr, e_ref):
    hi = arr.astype(BF16)
    lo = (arr - hi.astype(F32)).astype(BF16)
    e = e_ref[...]
    return jnp.dot(hi, e, preferred_element_type=F32) + jnp.dot(lo, e, preferred_element_type=F32)


def _ssd_prologue(dt_ref, a_row, tri, e_ref, need_y):
    dt = dt_ref[0]
    a = dt * a_row
    acs = jnp.dot(tri, a, preferred_element_type=F32, precision=HIGHEST)
    total = jnp.sum(a, axis=0, keepdims=True)
    ctx = dict(keep=tri > 0.5, acs=acs)
    ctx["wend_x"] = _expand_heads(jnp.exp(total - acs) * dt, e_ref)
    ctx["etot_x"] = _expand_heads(jnp.broadcast_to(jnp.exp(total), (SUBLANES, LANES)), e_ref)[0:1, :]
    if need_y:
        ctx["eacs_x"] = _expand_heads(jnp.exp(acs), e_ref)
        ctx["acs_t"] = acs.T
        ctx["dt_t"] = dt.T
    return ctx


def _ssd_prepare(ctx, xs_ref, bm_ref, cm_ref, g, need_y):
    q = SSD_CHUNK
    ppg = SSD_HEADS // 2 // SSD_GROUPS
    bg = bm_ref[:, g * SSD_STATE:(g + 1) * SSD_STATE]
    ops = dict(cg=cm_ref[:, g * SSD_STATE:(g + 1) * SSD_STATE].astype(BF16), bgt=bg.T.astype(BF16), xs2=[], xw=[], m2=[])
    if need_y:
        cb = lax.dot_general(ops["cg"], bg.astype(BF16), (((1,), (1,)), ((), ())), preferred_element_type=F32)
        left = lax.broadcasted_iota(jnp.int32, (q, LANES), 1) < SSD_P
    xws = []
    for r in range(ppg):
        pidx = g * ppg + r
        psl = slice(pidx * LANES, (pidx + 1) * LANES)
        xp = xs_ref[:, psl]
        xws.append((xp.astype(F32) * ctx["wend_x"][:, psl]).astype(BF16))
        if need_y:
            ms = []
            for hd in (2 * pidx, 2 * pidx + 1):
                seg = jnp.broadcast_to(ctx["acs"][:, hd:hd + 1], (q, LANES)) - ctx["acs_t"][hd:hd + 1, :]
                lm = jnp.exp(jnp.where(ctx["keep"], seg, -jnp.inf))
                ms.append((cb * lm * ctx["dt_t"][hd:hd + 1, :]).astype(BF16))
            ops["m2"].append(jnp.concatenate(ms, axis=1))
            zero = jnp.zeros_like(xp)
            ops["xs2"].append(jnp.concatenate([jnp.where(left, xp, zero), jnp.where(left, zero, xp)], axis=0))
    ops["xw"] = [jnp.concatenate(xws[2 * t:2 * t + 2], axis=1) for t in range(ppg // 2)]
    return ops


def _ssd_issue(ctx, ops, g, h_scr, y_ref, need_y):
    ppg = SSD_HEADS // 2 // SSD_GROUPS
    for t in range(ppg // 2):
        p0 = g * ppg + 2 * t
        qsl = slice(p0 * LANES, (p0 + 2) * LANES)
        hs = jnp.concatenate([h_scr[p0], h_scr[p0 + 1]], axis=1)
        if need_y:
            yd = jnp.concatenate([jnp.dot(ops["m2"][2 * t + e], ops["xs2"][2 * t + e], preferred_element_type=F32)
                                  for e in range(2)], axis=1)
            yoff = jnp.dot(ops["cg"], hs.astype(BF16), preferred_element_type=F32) * ctx["eacs_x"][:, qsl]
            y_ref[:, qsl] = (yd + yoff).astype(y_ref.dtype)
        st = jnp.dot(ops["bgt"], ops["xw"][t], preferred_element_type=F32)
        hn = hs * ctx["etot_x"][:, qsl] + st
        h_scr[p0] = hn[:, :LANES]
        h_scr[p0 + 1] = hn[:, LANES:]


def _ssd_scan_kernel(xsf_ref, xsb_ref, bmf_ref, bmb_ref, cmf_ref, cmb_ref, dtf_ref, dtb_ref, a_ref, tri_ref, e_ref,
                     h0_ref, *out_refs, nc, need_y):
    yf_ref, yb_ref = out_refs[:2] if need_y else (None, None)
    hfin_ref, h_scr = out_refs[-2:]
    s = pl.program_id(0)

    @pl.when(s == 0)
    def _():
        h_scr[...] = h0_ref[...]

    dirs = ((xsf_ref, bmf_ref, cmf_ref, dtf_ref, yf_ref), (xsb_ref, bmb_ref, cmb_ref, dtb_ref, yb_ref))
    ctxs = [_ssd_prologue(dirs[d][3], a_ref[d], tri_ref[d], e_ref, need_y) for d in range(2)]
    stages = [(d, g) for g in range(SSD_GROUPS) for d in range(2)]
    prep = lambda d, g: _ssd_prepare(ctxs[d], dirs[d][0], dirs[d][1], dirs[d][2], g, need_y)
    pending = prep(*stages[0])
    for idx, (d, g) in enumerate(stages):
        ops = pending
        if idx + 1 < len(stages):
            pending = prep(*stages[idx + 1])
        _ssd_issue(ctxs[d], ops, g, h_scr.at[d], dirs[d][4], need_y)

    @pl.when(s == nc - 1)
    def _():
        hfin_ref[...] = h_scr[...]


def _ssd_scan(xs, bm, cm, dt2, a_log, h0, need_y):
    lx = xs.shape[0]
    q = SSD_CHUNK
    nc = lx // q
    npair = SSD_HEADS // 2
    a = -jnp.exp(a_log.astype(F32))
    a_pad = jnp.pad(a, ((0, 0), (0, LANES - SSD_HEADS))).reshape(2, 1, LANES)
    lower = np.tril(np.ones((q, q), np.float32))
    tri = jnp.asarray(np.stack([lower, lower.T]))
    expand = jnp.asarray(np.kron(np.eye(LANES)[:, :SSD_HEADS], np.ones((1, SSD_P))), BF16)
    fwd = lambda s: s
    bwd = lambda s: nc - 1 - s
    rows = lambda w, idx: pl.BlockSpec((q, w), lambda s: (idx(s), 0))
    full = lambda a_: pl.BlockSpec(a_.shape, lambda s: (0,) * a_.ndim)
    y_shapes = [jax.ShapeDtypeStruct((lx, SSD_INNER), BF16)] * 2 if need_y else []
    y_specs = [rows(SSD_INNER, fwd), rows(SSD_INNER, bwd)] if need_y else []
    outs = pl.pallas_call(
        functools.partial(_ssd_scan_kernel, nc=nc, need_y=need_y),
        out_shape=y_shapes + [jax.ShapeDtypeStruct(h0.shape, F32)],
        grid=(nc,),
        in_specs=[
            rows(SSD_INNER, fwd), rows(SSD_INNER, bwd), rows(SSD_BC, fwd), rows(SSD_BC, bwd),
            rows(SSD_BC, fwd), rows(SSD_BC, bwd),
            pl.BlockSpec((1, q, LANES), lambda s: (0, s, 0)),
            pl.BlockSpec((1, q, LANES), lambda s: (1, nc - 1 - s, 0)),
            full(a_pad), full(tri), full(expand), full(h0),
        ],
        out_specs=y_specs + [full(h0)],
        scratch_shapes=[pltpu.VMEM((2, npair, SSD_STATE, 2 * SSD_P), F32)],
        compiler_params=_cparams("arbitrary"),
        name="ssd_scan",
    )(xs, xs, bm, bm, cm, cm, dt2, dt2, a_pad, tri, expand, h0)
    return (tuple(outs[:2]) if need_y else None), outs[-1]


def _ssd_out_kernel(x_ref, yf_ref, yb_ref, xs_ref, zg_ref, dsk_ref, ng_ref, w_ref, gate_ref, o_ref):
    y = yf_ref[...].astype(F32) + yb_ref[...].astype(F32) + xs_ref[...].astype(F32) * dsk_ref[...]
    y = y * _silu(zg_ref[...].astype(F32))
    gw = SSD_INNER // SSD_GROUPS
    parts = []
    for g in range(SSD_GROUPS):
        yg = y[:, g * gw:(g + 1) * gw]
        ms = jnp.mean(yg * yg, axis=-1, keepdims=True)
        parts.append(yg * lax.rsqrt(ms + NORM_EPS) * ng_ref[:, g * gw:(g + 1) * gw])
    yn = jnp.concatenate(parts, axis=1).astype(BF16)
    o_ref[...] = x_ref[...] + gate_ref[...] * jnp.dot(yn, w_ref[...], preferred_element_type=F32)


def _ssd_out(x, yfb, xs, zg, d_skip, norm_g, w_out, gate):
    lx = x.shape[0]
    tm = min(ROW_TILE // 2, lx)
    rowblk = lambda w: pl.BlockSpec((tm, w), lambda i: (i, 0))
    vecw = pl.BlockSpec((1, SSD_INNER), lambda i: (0, 0))
    return pl.pallas_call(
        _ssd_out_kernel,
        out_shape=jax.ShapeDtypeStruct((lx, D), F32),
        grid=(lx // tm,),
        in_specs=[rowblk(D), rowblk(SSD_INNER), rowblk(SSD_INNER), rowblk(SSD_INNER),
                  rowblk(SSD_INNER), vecw, vecw, _resident((SSD_INNER, D)),
                  pl.BlockSpec((1, D), lambda i: (0, 0))],
        out_specs=rowblk(D),
        compiler_params=_cparams("parallel"),
        name="ssd_out",
    )(x, yfb[0], yfb[1], xs, zg, _row(jnp.repeat(d_skip, SSD_P)), _row(norm_g), w_out, _row(gate))


def kernel(x, c, ctx, c_ctx, norm1_g, norm2_g, mod_w, mod_b, ffn_w_in, ffn_w_out, final_g, gm_w_in, gm_ln_g, gm_ln_b, gm_ws, gm_bs, gm_w_out, at_w_qkv, at_q_g, at_k_g, at_w_out, hy_w_in, hy_conv_w, hy_conv_b, hy_filt_w1, hy_filt_b1, hy_filt_w2, hy_filt_b2, hy_filt_w3, hy_filt_freq, hy_skip, hy_w_out, ssd_w_in, ssd_conv_w, ssd_conv_b, ssd_a_log, ssd_dt_bias, ssd_d_skip, ssd_norm_g, ssd_w_out):
    batch, seq, _ = x.shape
    assert batch == 1, "kernels are written for a single sequence"
    nctx = ctx.shape[1]
    xl = x[0]
    z = ctx[0]
    mods = _modulation(c[0], c_ctx, mod_w, mod_b)
    bf = lambda w: w.astype(BF16)

    for i in range(DEPTH):
        m, j = i % 4, i // 4
        want_ctx = i < DEPTH - 1
        ml = [mods[i, 0, k * D:(k + 1) * D] for k in range(6)]
        mc = [mods[i, 1, k * D:(k + 1) * D] for k in range(6)]
        n1 = norm1_g[i]
        pend_l = pend_c = None
        if m == 0:
            p = (bf(gm_w_in[j]), gm_ln_g[j], gm_ln_b[j], bf(gm_ws[j]), gm_bs[j], bf(gm_w_out[j]))
            xl = _gmlp(xl, n1, ml[0], ml[1], ml[2], *p)
            if want_ctx:
                z = _gmlp(z, n1, mc[0], mc[1], mc[2], *p)
        elif m == 1:
            wq, wo = bf(at_w_qkv[j]), bf(at_w_out[j])
            qt_l, k_l, vt_l = _qkv(xl, n1, ml[0], ml[1], wq, at_q_g[j], at_k_g[j], rope=True)
            qt_c, k_c, vt_c = _qkv(z, n1, mc[0], mc[1], wq, at_q_g[j], at_k_g[j], rope=False)
            k_all = jnp.concatenate([k_c, k_l], axis=1)
            vt_all = jnp.concatenate([vt_c, vt_l], axis=1)
            stot = nctx + seq
            ts = next(t for t in FLASH_KV_TILES if stot % t == 0)
            score_bound = (HD ** 0.5 * LOG2E) * jnp.max(jnp.abs(at_q_g[j])) * jnp.max(jnp.abs(at_k_g[j]))
            o_l = lax.cond(score_bound <= FLASH_SCORE_BOUND,
                           lambda: _flash(qt_l, k_all, vt_all, stot, ts, bounded=True),
                           lambda: _flash(qt_l, k_all, vt_all, stot, ts, bounded=False))
            pend_l = ("proj", o_l, wo, ml[2])
            if want_ctx:
                pend_c = ("proj", _flash(qt_c, k_all, vt_all, nctx, nctx), wo, mc[2])
        elif m == 2:
            p = (bf(hy_w_in[j]), hy_conv_w[j], hy_conv_b[j], hy_filt_w1[j], hy_filt_b1[j], hy_filt_w2[j],
                 hy_filt_b2[j], hy_filt_w3[j], hy_filt_freq[j], hy_skip[j], bf(hy_w_out[j]))
            pend_l = _hyena(xl, n1, ml[0], ml[1], ml[2], *p)
            if want_ctx:
                pend_c = _hyena(z, n1, mc[0], mc[1], mc[2], *p)
        else:
            win, wo = bf(ssd_w_in[j]), bf(ssd_w_out[j])
            pin = (win, ssd_conv_w[j], ssd_conv_b[j], ssd_dt_bias[j])
            zg_c, xs_c, bm_c, cm_c, dt_c = _ssd_in(z, n1, mc[0], mc[1], *pin)
            zg_l, xs_l, bm_l, cm_l, dt_l = _ssd_in(xl, n1, ml[0], ml[1], *pin)
            h0 = jnp.zeros((2, SSD_HEADS // 2, SSD_STATE, 2 * SSD_P), F32)
            y_c, h_ctx = _ssd_scan(xs_c, bm_c, cm_c, dt_c, ssd_a_log[j], h0, want_ctx)
            y_l, _ = _ssd_scan(xs_l, bm_l, cm_l, dt_l, ssd_a_log[j], h_ctx, True)
            xl = _ssd_out(xl, y_l, xs_l, zg_l, ssd_d_skip[j], ssd_norm_g[j], wo, ml[2])
            if want_ctx:
                z = _ssd_out(z, y_c, xs_c, zg_c, ssd_d_skip[j], ssd_norm_g[j], wo, mc[2])
        wi, wo2 = bf(ffn_w_in[i]), bf(ffn_w_out[i])
        xl = _ffn(xl, pend_l, norm2_g[i], ml[3], ml[4], ml[5], wi, wo2, final_g, final=(i == DEPTH - 1))
        if want_ctx:
            z = _ffn(z, pend_c, norm2_g[i], mc[3], mc[4], mc[5], wi, wo2, final_g, final=False)
    return xl[None]
```

```python
import functools
import math

import numpy as np
import jax
import jax.numpy as jnp
from jax import lax
from jax.experimental import pallas as pl
from jax.experimental.pallas import tpu as pltpu

F32 = jnp.float32
BF16 = jnp.bfloat16
HIGHEST = lax.Precision.HIGHEST

D = 1024
DEPTH = 4
GRID_W = 64
NORM_EPS = 1e-6
FFN_HIDDEN = 2816
GM_CHUNK = 128
GM_WIDTH = 2 * D
GM_GROUPS = 8
GM_GW = GM_WIDTH // GM_GROUPS
HD = 64
QH = D // HD
KVH = 4
ROPE_THETA = 10000.0
LOG2E = math.log2(math.e)
FLASH_SCORE_BOUND = 64.0
FLASH_LOOKAHEAD = 2
HY_BANDS = 16
HY_EMB = 1 + 2 * HY_BANDS
HY_FILT_W = 64
HY_MAX_DECAY = math.log(1e-2) / 0.3
HY_MIN_DECAY = math.log(1e-2) / 1.5
SSD_INNER = 2 * D
SSD_P = 64
SSD_HEADS = SSD_INNER // SSD_P
SSD_GROUPS = 4
SSD_STATE = 128
SSD_CHUNK = 128
SSD_BC = SSD_GROUPS * SSD_STATE
SSD_CONV_DIM = SSD_INNER + 2 * SSD_BC

LANES = 128
SUBLANES = 8
VMEM_LIMIT_BYTES = 56 * 1024 * 1024
DFT_N2 = LANES
ROW_TILE = 512
FLASH_Q_TILE = 128
FLASH_KV_TILES = (3328, 1280, 1024, 512, 256)
FLASH_CHUNK = 2 * LANES


def _cparams(*sem):
    return pltpu.CompilerParams(dimension_semantics=sem, vmem_limit_bytes=VMEM_LIMIT_BYTES)


def _row(v):
    return v.reshape(1, -1)


def _normmod(x, g, shift, scale):
    ms = jnp.mean(x * x, axis=-1, keepdims=True)
    return x * lax.rsqrt(ms + NORM_EPS) * g * (1.0 + scale) + shift


def _silu(x):
    return x * jax.nn.sigmoid(x)


def _mod_kernel(cl_ref, cc_ref, w_ref, b_ref, o_ref):
    w = w_ref[0]
    for r, c_ref in enumerate((cl_ref, cc_ref)):
        a = _silu(c_ref[...])
        o_ref[0, r:r + 1, :] = jnp.sum(a * w, axis=0, keepdims=True) + b_ref[0]


def _modulation(c, c_ctx, mod_w, mod_b):
    n6 = 6 * D
    tn = n6 // 4
    depth = mod_w.shape[0]
    return pl.pallas_call(
        _mod_kernel,
        out_shape=jax.ShapeDtypeStruct((depth, 2, n6), F32),
        grid=(depth, n6 // tn),
        in_specs=[
            pl.BlockSpec((D, 1), lambda i, n: (0, 0)),
            pl.BlockSpec((D, 1), lambda i, n: (0, 0)),
            pl.BlockSpec((1, D, tn), lambda i, n: (i, 0, n)),
            pl.BlockSpec((1, 1, tn), lambda i, n: (i, 0, n)),
        ],
        out_specs=pl.BlockSpec((1, 2, tn), lambda i, n: (i, 0, n)),
        compiler_params=_cparams("parallel", "parallel"),
        name="modulation",
    )(c.reshape(D, 1), c_ctx.reshape(D, 1), mod_w, mod_b.reshape(depth, 1, n6))


def _ffn_kernel(x_ref, *refs, mode, final):
    x = x_ref[...]
    if mode == "proj":
        a_ref, wm_ref, g1_ref = refs[:3]
        refs = refs[3:]
        x = x + g1_ref[...] * jnp.dot(a_ref[...], wm_ref[...], preferred_element_type=F32)
    elif mode == "hyena":
        x0_ref, yt_ref, wm_ref, g1_ref = refs[:4]
        refs = refs[4:]
        a = (x0_ref[...] * yt_ref[...].T).astype(BF16)
        x = x + g1_ref[...] * jnp.dot(a, wm_ref[...], preferred_element_type=F32)
    g_ref, sh_ref, sc_ref, gate_ref, wi_ref, wo_ref, fg_ref, o_ref = refs
    h = _normmod(x, g_ref[...], sh_ref[...], sc_ref[...]).astype(BF16)
    a = jnp.dot(h, wi_ref[:, :FFN_HIDDEN], preferred_element_type=F32)
    u = jnp.dot(h, wi_ref[:, FFN_HIDDEN:], preferred_element_type=F32)
    act = (_silu(a) * u).astype(BF16)
    y = x + gate_ref[...] * jnp.dot(act, wo_ref[...], preferred_element_type=F32)
    if final:
        ms = jnp.mean(y * y, axis=-1, keepdims=True)
        y = y * lax.rsqrt(ms + NORM_EPS) * fg_ref[...]
    o_ref[...] = y


def _resident(shape):
    return pl.BlockSpec(shape, lambda *_: (0,) * len(shape), pipeline_mode=pl.Buffered(1))


def _ffn(x, pending, g, sh, sc, gate, w_in, w_out, final_g, final):
    lx = x.shape[0]
    tm = min(ROW_TILE, lx)
    vec = pl.BlockSpec((1, D), lambda i: (0, 0))
    rows = pl.BlockSpec((tm, D), lambda i: (i, 0))
    mode, pre_args, pre_specs = "none", (), []
    if pending is not None:
        mode = pending[0]
        if mode == "proj":
            _, a, wm, g1 = pending
            pre_args, pre_specs = (a, wm, _row(g1)), [rows, _resident(wm.shape), vec]
        else:
            _, x0, yt, wm, g1 = pending
            pre_args = (x0, yt, wm, _row(g1))
            pre_specs = [rows, pl.BlockSpec((D, tm), lambda i: (0, i)), _resident(wm.shape), vec]
    return pl.pallas_call(
        functools.partial(_ffn_kernel, mode=mode, final=final),
        out_shape=jax.ShapeDtypeStruct((lx, D), F32),
        grid=(lx // tm,),
        in_specs=[rows] + pre_specs + [
            vec, vec, vec, vec,
            _resident((D, 2 * FFN_HIDDEN)),
            _resident((FFN_HIDDEN, D)),
            vec,
        ],
        out_specs=rows,
        compiler_params=_cparams("parallel"),
        name="ffn",
    )(x, *pre_args, _row(g), _row(sh), _row(sc), _row(gate), w_in, w_out, _row(final_g))


def _gmlp_kernel(x_ref, g_ref, sh_ref, sc_ref, gate_ref, win_ref, lng_ref, lnb_ref, ws_ref, bs_ref, wout_ref,
                 o_ref, *, tm):
    nsub = max(1, tm // (2 * GM_CHUNK))
    rs = tm // nsub

    def project(i):
        x = x_ref[i * rs:(i + 1) * rs, :]
        h = _normmod(x, g_ref[...], sh_ref[...], sc_ref[...]).astype(BF16)
        return jnp.dot(h, win_ref[...], preferred_element_type=F32)

    def mix(i, t):
        t = 0.5 * t * (1.0 + lax.erf(t * (1.0 / math.sqrt(2.0))))
        u = t[:, :GM_WIDTH]
        v = t[:, GM_WIDTH:]
        mu = jnp.mean(v, axis=-1, keepdims=True)
        vc = v - mu
        var = jnp.mean(vc * vc, axis=-1, keepdims=True)
        v = (vc * lax.rsqrt(var + NORM_EPS) * lng_ref[...] + lnb_ref[...]).astype(BF16)
        rows = []
        for q in range(rs // GM_CHUNK):
            cols = []
            for gidx in range(GM_GROUPS):
                vq = v[q * GM_CHUNK:(q + 1) * GM_CHUNK, gidx * GM_GW:(gidx + 1) * GM_GW]
                bias = bs_ref[gidx]
                m = jnp.dot(ws_ref[gidx], vq, preferred_element_type=F32)
                cols.append(m + jnp.concatenate([bias] * (GM_GW // LANES), axis=1))
            rows.append(jnp.concatenate(cols, axis=1))
        gated = (u * jnp.concatenate(rows, axis=0)).astype(BF16)
        y = jnp.dot(gated, wout_ref[...], preferred_element_type=F32)
        o_ref[i * rs:(i + 1) * rs, :] = x_ref[i * rs:(i + 1) * rs, :] + gate_ref[...] * y

    t_next = project(0)
    for i in range(nsub):
        t = t_next
        if i + 1 < nsub:
            t_next = project(i + 1)
        mix(i, t)


def _gmlp(x, g, sh, sc, gate, w_in, ln_g, ln_b, ws, bs, w_out):
    lx = x.shape[0]
    tm = min(ROW_TILE, lx)
    vec = pl.BlockSpec((1, D), lambda i: (0, 0))
    vecw = pl.BlockSpec((1, GM_WIDTH), lambda i: (0, 0))
    bsb = jnp.broadcast_to(bs[:, :, None], (GM_GROUPS, GM_CHUNK, LANES))
    return pl.pallas_call(
        functools.partial(_gmlp_kernel, tm=tm),
        out_shape=jax.ShapeDtypeStruct((lx, D), F32),
        grid=(lx // tm,),
        in_specs=[
            pl.BlockSpec((tm, D), lambda i: (i, 0)),
            vec, vec, vec, vec,
            _resident((D, 2 * GM_WIDTH)),
            vecw, vecw,
            pl.BlockSpec((GM_GROUPS, GM_CHUNK, GM_CHUNK), lambda i: (0, 0, 0)),
            pl.BlockSpec((GM_GROUPS, GM_CHUNK, LANES), lambda i: (0, 0, 0)),
            _resident((GM_WIDTH, D)),
        ],
        out_specs=pl.BlockSpec((tm, D), lambda i: (i, 0)),
        compiler_params=_cparams("parallel"),
        name="gmlp",
    )(x, _row(g), _row(sh), _row(sc), _row(gate), w_in, _row(ln_g), _row(ln_b), ws, bsb, w_out)


def _group_sumsq(t, e_ref):
    sq = t * t
    hi = sq.astype(BF16)
    lo = (sq - hi.astype(F32)).astype(BF16)
    e = e_ref[...]
    w = e.shape[0]
    outs = []
    for j in range(t.shape[1] // w):
        sl = slice(j * w, (j + 1) * w)
        outs.append(jnp.dot(hi[:, sl], e, preferred_element_type=F32) + jnp.dot(lo[:, sl], e, preferred_element_type=F32))
    return jnp.concatenate(outs, axis=1)


def _rope(t, cosf, sinf):
    w = t.shape[1]
    lane = lax.broadcasted_iota(jnp.int32, t.shape, 1)
    first = (lane % HD) < (HD // 2)
    partner = jnp.where(first, pltpu.roll(t, w - HD // 2, axis=1), pltpu.roll(t, HD // 2, axis=1))
    reps = w // LANES
    c = jnp.concatenate([cosf] * reps, axis=1)
    s = jnp.concatenate([sinf] * reps, axis=1)
    return t * c + partner * s


def _qkv_kernel(x_ref, g_ref, sh_ref, sc_ref, w_ref, qg_ref, kg_ref, e_ref, cos_ref, sin_ref,
                qt_ref, k_ref, vt_ref, *, rope):
    h = _normmod(x_ref[...], g_ref[...], sh_ref[...], sc_ref[...]).astype(BF16)
    qkv = jnp.dot(h, w_ref[...], preferred_element_type=F32)
    q = qkv[:, :D]
    k = qkv[:, D:D + KVH * HD]
    v = qkv[:, D + KVH * HD:]
    q = q * lax.rsqrt(_group_sumsq(q, e_ref) * (1.0 / HD) + NORM_EPS) * qg_ref[...]
    k = k * lax.rsqrt(_group_sumsq(k, e_ref) * (1.0 / HD) + NORM_EPS) * kg_ref[...]
    if rope:
        q = _rope(q, cos_ref[...], sin_ref[...])
        k = _rope(k, cos_ref[...], sin_ref[...])
    qt_ref[...] = (q * (HD ** -0.5 * LOG2E)).T.astype(BF16)
    for gidx in range(KVH):
        k_ref[gidx] = k[:, gidx * HD:(gidx + 1) * HD].astype(BF16)
    vt_ref[...] = v.T.astype(BF16)


def _qkv(x, g, sh, sc, w_qkv, q_g, k_g, rope):
    lx = x.shape[0]
    tm = min(ROW_TILE, lx)
    vec = pl.BlockSpec((1, D), lambda i: (0, 0))
    kvw = KVH * HD
    rows = lx // GRID_W
    row = jnp.repeat(jnp.arange(rows, dtype=F32), GRID_W)
    col = jnp.tile(jnp.arange(GRID_W, dtype=F32), rows)
    n = HD // 4
    inv = ROPE_THETA ** (-jnp.arange(n, dtype=F32) / n)
    ang = jnp.concatenate([row[:, None] * inv, col[:, None] * inv], axis=-1)
    cos, sin = jnp.cos(ang), jnp.sin(ang)
    cosf = jnp.tile(jnp.concatenate([cos, cos], axis=-1), (1, LANES // HD))
    sinf = jnp.tile(jnp.concatenate([-sin, sin], axis=-1), (1, LANES // HD))
    eblk = jnp.asarray(np.kron(np.eye(kvw // HD), np.ones((HD, HD))), BF16)
    tab = pl.BlockSpec((tm, LANES), lambda i: (i, 0))
    return pl.pallas_call(
        functools.partial(_qkv_kernel, rope=rope),
        out_shape=(jax.ShapeDtypeStruct((D, lx), BF16),
                   jax.ShapeDtypeStruct((KVH, lx, HD), BF16),
                   jax.ShapeDtypeStruct((kvw, lx), BF16)),
        grid=(lx // tm,),
        in_specs=[
            pl.BlockSpec((tm, D), lambda i: (i, 0)),
            vec, vec, vec,
            _resident((D, D + 2 * kvw)),
            vec,
            pl.BlockSpec((1, kvw), lambda i: (0, 0)),
            pl.BlockSpec((kvw, kvw), lambda i: (0, 0)),
            tab, tab,
        ],
        out_specs=(pl.BlockSpec((D, tm), lambda i: (0, i)),
                   pl.BlockSpec((KVH, tm, HD), lambda i: (0, i, 0)),
                   pl.BlockSpec((kvw, tm), lambda i: (0, i))),
        compiler_params=_cparams("parallel"),
        name="qkv_proj",
    )(x, _row(g), _row(sh), _row(sc), w_qkv, _row(jnp.tile(q_g, QH)), _row(jnp.tile(k_g, KVH)), eblk, cosf, sinf)


def _flash_kernel(qt_ref, k_ref, vt_ref, o_ref, qg_scr, m_scr, l_scr, acc_scr, *, tq, ts, tc, nkv, bounded):
    j = pl.program_id(1)
    gq = QH // KVH
    mcols = gq * tq

    @pl.when(j == 0)
    def _():
        for h in range(QH):
            qg_scr[h // gq, :, (h % gq) * tq:(h % gq + 1) * tq] = qt_ref[h * HD:(h + 1) * HD, :]
        m_scr[...] = jnp.full(m_scr.shape, -jnp.inf, F32)
        l_scr[...] = jnp.zeros_like(l_scr)
        acc_scr[...] = jnp.zeros_like(acc_scr)

    stages = [(g, c) for c in range(ts // tc) for g in range(KVH)]

    def scores(g, c):
        return jnp.dot(k_ref[g, c * tc:(c + 1) * tc, :], qg_scr[g], preferred_element_type=F32)

    pending = [scores(*st) for st in stages[:FLASH_LOOKAHEAD]]
    for idx, (g, c) in enumerate(stages):
        s = pending.pop(0)
        if idx + FLASH_LOOKAHEAD < len(stages):
            pending.append(scores(*stages[idx + FLASH_LOOKAHEAD]))
        vt = vt_ref[g * HD:(g + 1) * HD, c * tc:(c + 1) * tc]
        if bounded:
            p = jnp.exp2(s)
            l_scr[g] += jnp.sum(p, axis=0, keepdims=True)
            acc_scr[g] += jnp.dot(vt, p.astype(BF16), preferred_element_type=F32)
        else:
            m_prev = m_scr[g]
            m_new = jnp.maximum(m_prev, jnp.max(s, axis=0, keepdims=True))
            alpha = jnp.exp2(m_prev - m_new)
            p = jnp.exp2(s - m_new)
            l_scr[g] = alpha * l_scr[g] + jnp.sum(p, axis=0, keepdims=True)
            acc_scr[g] = alpha * acc_scr[g] + jnp.dot(vt, p.astype(BF16), preferred_element_type=F32)
            m_scr[g] = m_new

    @pl.when(j == nkv - 1)
    def _():
        rows = []
        for g in range(KVH):
            o = acc_scr[g] / l_scr[g]
            rows += [o[:, r * tq:(r + 1) * tq] for r in range(gq)]
        o_ref[...] = jnp.concatenate(rows, axis=0).T.astype(o_ref.dtype)


def _flash(qt, k, vt, s_len, ts, bounded=False):
    lq = qt.shape[1]
    tq = min(FLASH_Q_TILE, lq)
    nkv = s_len // ts
    gq = QH // KVH
    kvw = KVH * HD
    tc = FLASH_CHUNK if ts % FLASH_CHUNK == 0 else LANES
    return pl.pallas_call(
        functools.partial(_flash_kernel, tq=tq, ts=ts, tc=tc, nkv=nkv, bounded=bounded),
        out_shape=jax.ShapeDtypeStruct((lq, D), BF16),
        grid=(lq // tq, nkv),
        in_specs=[
            pl.BlockSpec((D, tq), lambda i, j: (0, i)),
            pl.BlockSpec((KVH, ts, HD), lambda i, j: (0, j, 0)),
            pl.BlockSpec((kvw, ts), lambda i, j: (0, j)),
        ],
        out_specs=pl.BlockSpec((tq, D), lambda i, j: (i, 0)),
        scratch_shapes=[
            pltpu.VMEM((KVH, HD, gq * tq), BF16),
            pltpu.VMEM((KVH, 1, gq * tq), F32),
            pltpu.VMEM((KVH, 1, gq * tq), F32),
            pltpu.VMEM((KVH, HD, gq * tq), F32),
        ],
        compiler_params=_cparams("parallel", "arbitrary"),
        name="flash_attn",
    )(qt, k, vt)


def _halo_specs(tm, lx):
    nb = lx // SUBLANES
    step = tm // SUBLANES
    prev = pl.BlockSpec((SUBLANES, D), lambda i: (jnp.maximum(i * step - 1, 0), 0))
    nxt = pl.BlockSpec((SUBLANES, D), lambda i: (jnp.minimum((i + 1) * step, nb - 1), 0))
    return prev, nxt


def _conv3(p_main, p_halo, cw, cb, first, last):
    tm = p_main.shape[0]
    rid = lax.broadcasted_iota(jnp.int32, p_main.shape, 0)
    before = jnp.where(first, 0.0, p_halo[SUBLANES - 1:SUBLANES, :])
    after = jnp.where(last, 0.0, p_halo[SUBLANES:SUBLANES + 1, :])
    up = jnp.where(rid == 0, before, pltpu.roll(p_main, 1, axis=0))
    dn = jnp.where(rid == tm - 1, after, pltpu.roll(p_main, tm - 1, axis=0))
    return cw[0:1, :] * up + cw[1:2, :] * p_main + cw[2:3, :] * dn + cb


def _norm_halo(xm_ref, xp_ref, xn_ref, g_ref, sh_ref, sc_ref):
    g, sh, sc = g_ref[...], sh_ref[...], sc_ref[...]
    h = _normmod(xm_ref[...], g, sh, sc).astype(BF16)
    hh = jnp.concatenate([_normmod(xp_ref[...], g, sh, sc), _normmod(xn_ref[...], g, sh, sc)], axis=0).astype(BF16)
    return h, hh


def _hy_in_kernel(xm_ref, xp_ref, xn_ref, g_ref, sh_ref, sc_ref, w_ref, cw_ref, cb_ref, x0_ref, ut_ref, *, nt):
    i = pl.program_id(0)
    first, last = i == 0, i == nt - 1
    h, hh = _norm_halo(xm_ref, xp_ref, xn_ref, g_ref, sh_ref, sc_ref)

    def project(b):
        sl = slice(b * D, (b + 1) * D)
        return (jnp.dot(h, w_ref[:, sl], preferred_element_type=F32),
                jnp.dot(hh, w_ref[:, sl], preferred_element_type=F32))

    def conv(b, p):
        sl = slice(b * D, (b + 1) * D)
        return _conv3(p[0], p[1], cw_ref[:, sl], cb_ref[:, sl], first, last)

    p1 = project(1)
    p2 = project(2)
    x1 = conv(1, p1)
    p0 = project(0)
    ut_ref[...] = (x1 * conv(2, p2)).T.astype(ut_ref.dtype)
    x0_ref[...] = conv(0, p0)


def _hy_in(x, g, sh, sc, w_in, conv_w, conv_b):
    lx = x.shape[0]
    tm = min(ROW_TILE, lx)
    nt = lx // tm
    vec = pl.BlockSpec((1, D), lambda i: (0, 0))
    prev, nxt = _halo_specs(tm, lx)
    return pl.pallas_call(
        functools.partial(_hy_in_kernel, nt=nt),
        out_shape=(jax.ShapeDtypeStruct((lx, D), F32), jax.ShapeDtypeStruct((D, lx), BF16)),
        grid=(nt,),
        in_specs=[
            pl.BlockSpec((tm, D), lambda i: (i, 0)), prev, nxt,
            vec, vec, vec,
            _resident((D, 3 * D)),
            pl.BlockSpec((3, 3 * D), lambda i: (0, 0)),
            pl.BlockSpec((1, 3 * D), lambda i: (0, 0)),
        ],
        out_specs=(pl.BlockSpec((tm, D), lambda i: (i, 0)), pl.BlockSpec((D, tm), lambda i: (0, i))),
        compiler_params=_cparams("parallel"),
        name="hyena_in",
    )(x, x, x, _row(g), _row(sh), _row(sc), w_in, conv_w, _row(conv_b))


def _hy_filter_kernel(ft_ref, t_ref, w1_ref, b1_ref, w2_ref, b2_ref, w3c_ref, fr_ref, dl_ref, sk_ref, kt_ref,
                      *, tm, ltrue, lpad):
    i = pl.program_id(0)
    feats = ft_ref[...]
    fr = fr_ref[...]
    hid = jnp.sin(fr * (jnp.dot(w1_ref[...], feats, preferred_element_type=F32, precision=HIGHEST) + b1_ref[...]))
    hid = jnp.sin(fr * (jnp.dot(w2_ref[...], hid, preferred_element_type=F32, precision=HIGHEST) + b2_ref[...]))
    hh = hid.astype(BF16)
    hl = (hid - hh.astype(F32)).astype(BF16)
    kt = jnp.dot(w3c_ref[...], jnp.concatenate([hh, hl, hh], axis=0), preferred_element_type=F32)
    kt = kt * jnp.exp(-dl_ref[...] * t_ref[...])
    if lpad != ltrue:
        pos = lax.broadcasted_iota(jnp.int32, kt.shape, 1) + i * tm
        kt = jnp.where(pos < ltrue, kt, 0.0)
    kt_ref[...] = kt.astype(kt_ref.dtype)

    @pl.when(i == 0)
    def _():
        row = lax.broadcasted_iota(jnp.int32, (2 * D, 1), 0)
        kt_ref[:, 0:1] = jnp.where(row < D, kt[:, 0:1] + sk_ref[...], 0.0).astype(kt_ref.dtype)


def _hy_filter(ltrue, lpad, w1, b1, w2, b2, w3, freq, skip):
    tm = min(ROW_TILE, lpad)
    col = lambda v_: v_.reshape(-1, 1)
    w1t = jnp.pad(w1.T, ((0, 0), (0, LANES - HY_EMB)))
    deltas = jnp.abs(jnp.linspace(HY_MIN_DECAY, HY_MAX_DECAY, D, dtype=F32))
    pos = jnp.arange(lpad, dtype=F32)
    zf = jnp.linspace(1e-4, HY_BANDS - 1, HY_BANDS, dtype=F32)[:, None] * (2.0 * math.pi * pos / ltrue)[None, :]
    feats = jnp.concatenate([(pos / (ltrue - 1))[None, :], jnp.cos(zf), -jnp.sin(zf),
                             jnp.zeros((LANES - HY_EMB, lpad), F32)], axis=0)
    w3t = w3.T
    w3h = w3t.astype(BF16)
    w3l = (w3t - w3h.astype(F32)).astype(BF16)
    w3c = jnp.concatenate([w3h, w3h, w3l], axis=1)
    full = lambda a: pl.BlockSpec(a.shape, lambda i: (0,) * a.ndim)
    args = (w1t, col(b1), w2.T, col(b2), w3c, col(freq), col(jnp.tile(deltas, 2)),
            col(jnp.concatenate([skip, jnp.zeros((D,), F32)])))
    return pl.pallas_call(
        functools.partial(_hy_filter_kernel, tm=tm, ltrue=ltrue, lpad=lpad),
        out_shape=jax.ShapeDtypeStruct((2 * D, lpad), BF16),
        grid=(lpad // tm,),
        in_specs=[pl.BlockSpec((LANES, tm), lambda i: (0, i)), pl.BlockSpec((1, tm), lambda i: (0, i))]
                 + [full(a) for a in args],
        out_specs=pl.BlockSpec((2 * D, tm), lambda i: (0, i)),
        compiler_params=_cparams("parallel"),
        name="hyena_filter",
    )(feats, feats[0:1, :], *args)


def _dft_consts(nh):
    n1 = 2 * nh
    n = n1 * DFT_N2
    k1 = np.arange(n1)[:, None].astype(np.float64)
    a1 = 2.0 * np.pi * k1 * np.arange(nh)[None, :] / n1
    f1 = np.concatenate([np.cos(a1), -np.sin(a1)], axis=0)
    at = 2.0 * np.pi * ((np.arange(n1)[:, None] * np.arange(DFT_N2)[None, :]) % n) / n
    a2 = 2.0 * np.pi * ((np.arange(DFT_N2)[:, None] * np.arange(DFT_N2)[None, :]) % DFT_N2) / DFT_N2
    c2, s2 = np.cos(a2), np.sin(a2)
    f2 = np.block([[c2, -s2], [s2, c2]])
    g2 = np.block([[c2, s2], [-s2, c2]])
    g1 = np.concatenate([np.cos(a1).T, -np.sin(a1).T], axis=1) / n
    mxu = lambda a: jnp.asarray(a.astype(BF16))
    return mxu(f1), jnp.asarray(np.cos(at), F32), jnp.asarray(np.sin(at), F32), mxu(f2), mxu(g2), mxu(g1)


def _bf16_dot(a, b):
    return jnp.dot(a.astype(BF16), b.astype(BF16), preferred_element_type=F32)


def _hy_conv_kernel(x_ref, kf_ref, kb_ref, f1_ref, twc_ref, tws_ref, f2_ref, g2_ref, g1_ref, o_ref, *, cb, n1):
    twc, tws = twc_ref[...], tws_ref[...]
    f1, f2, g2, g1 = f1_ref[...], f2_ref[...], g2_ref[...], g1_ref[...]
    hc = cb // 2
    nt = 3 * hc

    def dft1(half):
        xs = [r[half * hc + c] for r in (x_ref, kf_ref, kb_ref) for c in range(hc)]
        return _bf16_dot(f1, jnp.concatenate(xs, axis=1))

    def twiddle_rows(a):
        rows = []
        for t in range(nt):
            ar = a[:n1, t * DFT_N2:(t + 1) * DFT_N2]
            ai = a[n1:, t * DFT_N2:(t + 1) * DFT_N2]
            rows.append(jnp.concatenate([ar * twc + ai * tws, ai * twc - ar * tws], axis=1))
        return jnp.concatenate(rows, axis=0)

    def dft2(rows):
        return _bf16_dot(rows, f2)

    def product(spec_all):
        rows = hc * n1
        spec, hf, hb = spec_all[:rows], spec_all[rows:2 * rows], spec_all[2 * rows:]
        hr = hf[:, :DFT_N2] + hb[:, :DFT_N2]
        hi = hf[:, DFT_N2:] - hb[:, DFT_N2:]
        xr, xi = spec[:, :DFT_N2], spec[:, DFT_N2:]
        return jnp.concatenate([xr * hr - xi * hi, xr * hi + xi * hr], axis=1)

    def idft2(y):
        return _bf16_dot(y, g2)

    def twiddle_cols(b):
        cols = []
        for c in range(hc):
            br = b[c * n1:(c + 1) * n1, :DFT_N2]
            bi = b[c * n1:(c + 1) * n1, DFT_N2:]
            cols.append(jnp.concatenate([br * twc - bi * tws, bi * twc + br * tws], axis=0))
        return jnp.concatenate(cols, axis=1)

    def idft1(cols):
        return _bf16_dot(g1, cols)

    def store(half, out):
        for c in range(hc):
            ch = half * hc + c
            for r in range(out.shape[0]):
                o_ref[ch:ch + 1, r * DFT_N2:(r + 1) * DFT_N2] = out[r:r + 1, c * DFT_N2:(c + 1) * DFT_N2]

    a0 = dft1(0)
    a1 = dft1(1)
    s0 = dft2(twiddle_rows(a0))
    s1 = dft2(twiddle_rows(a1))
    b0 = idft2(product(s0))
    b1 = idft2(product(s1))
    o0 = idft1(twiddle_cols(b0))
    o1 = idft1(twiddle_cols(b1))
    store(0, o0)
    store(1, o1)


def _hy_longconv(ut, kt, lpad):
    nh = lpad // DFT_N2
    n1 = 2 * nh
    cb = max(2 * SUBLANES, min(64, 2048 // n1))
    f1, twc, tws, f2, g2, g1 = _dft_consts(nh)
    consts = (f1, twc, tws, f2, g2, g1)
    full = lambda a: pl.BlockSpec(a.shape, lambda i: (0,) * a.ndim)
    k3 = kt.reshape(2 * D, nh, DFT_N2)
    u3 = ut.reshape(D, nh, DFT_N2)
    nb = D // cb
    yt = pl.pallas_call(
        functools.partial(_hy_conv_kernel, cb=cb, n1=n1),
        out_shape=jax.ShapeDtypeStruct((D, lpad), F32),
        grid=(nb,),
        in_specs=[pl.BlockSpec((cb, nh, DFT_N2), lambda i: (i, 0, 0)),
                  pl.BlockSpec((cb, nh, DFT_N2), lambda i: (i, 0, 0)),
                  pl.BlockSpec((cb, nh, DFT_N2), lambda i: (i + nb, 0, 0))]
                 + [full(a) for a in consts],
        out_specs=pl.BlockSpec((cb, lpad), lambda i: (i, 0)),
        compiler_params=_cparams("parallel"),
        name="hyena_longconv",
    )(u3, k3, k3, *consts)
    return yt


def _hyena(x, g, sh, sc, gate, w_in, conv_w, conv_b, w1, b1, w2, b2, w3, freq, skip, w_out):
    lx = x.shape[0]
    lpad = max(lx, SUBLANES * DFT_N2)
    x0, ut = _hy_in(x, g, sh, sc, w_in, conv_w, conv_b)
    if lpad != lx:
        ut = jnp.pad(ut, ((0, 0), (0, lpad - lx)))
    kt = _hy_filter(lx, lpad, w1, b1, w2, b2, w3, freq, skip)
    yt = _hy_longconv(ut, kt, lpad)[:, :lx]
    return ("hyena", x0, yt, w_out, gate)


def _ssd_in_kernel(xm_ref, xp_ref, xn_ref, g_ref, sh_ref, sc_ref, w_ref, wd_ref, cw_ref, cb_ref, db_ref,
                   zg_ref, xs_ref, bm_ref, cm_ref, dt_ref, *, nt):
    i = pl.program_id(0)
    first, last = i == 0, i == nt - 1
    h, hh = _norm_halo(xm_ref, xp_ref, xn_ref, g_ref, sh_ref, sc_ref)
    cw = SSD_BC
    nchunk = SSD_CONV_DIM // cw

    def project(b):
        sl = slice(SSD_INNER + b * cw, SSD_INNER + (b + 1) * cw)
        return (jnp.dot(h, w_ref[:, sl], preferred_element_type=F32),
                jnp.dot(hh, w_ref[:, sl], preferred_element_type=F32))

    def conv(b, p):
        sl = slice(b * cw, (b + 1) * cw)
        y = _silu(_conv3(p[0], p[1], cw_ref[:, sl], cb_ref[:, sl], first, last))
        if (b + 1) * cw <= SSD_INNER:
            xs_ref[:, sl] = y.astype(xs_ref.dtype)
        elif b == nchunk - 2:
            bm_ref[...] = y
        else:
            cm_ref[...] = y

    pending = [project(0), project(1)]
    for b in range(nchunk):
        p = pending.pop(0)
        if b + 2 < nchunk:
            pending.append(project(b + 2))
        conv(b, p)
    zg_ref[...] = jnp.dot(h, w_ref[:, :SSD_INNER], preferred_element_type=F32).astype(zg_ref.dtype)
    dt = jnp.dot(h, wd_ref[...], preferred_element_type=F32) + db_ref[...]
    dt = jnp.maximum(dt, 0.0) + jnp.log1p(jnp.exp(-jnp.abs(dt)))
    lane = lax.broadcasted_iota(jnp.int32, dt.shape, 1)
    dt = jnp.where((lane % LANES) < SSD_HEADS, dt, 0.0)
    dt_ref[0] = dt[:, :LANES]
    dt_ref[1] = dt[:, LANES:]


def _ssd_in(x, g, sh, sc, w_in, conv_w, conv_b, dt_bias):
    lx = x.shape[0]
    tm = min(ROW_TILE, lx)
    nt = lx // tm
    vec = pl.BlockSpec((1, D), lambda i: (0, 0))
    prev, nxt = _halo_specs(tm, lx)
    wdt = w_in[:, SSD_INNER + SSD_CONV_DIM:]
    pad = LANES - SSD_HEADS
    wd = jnp.concatenate([jnp.pad(wdt[:, :SSD_HEADS], ((0, 0), (0, pad))),
                          jnp.pad(wdt[:, SSD_HEADS:], ((0, 0), (0, pad)))], axis=1)
    db = jnp.pad(dt_bias, ((0, 0), (0, pad))).reshape(1, 2 * LANES)
    full = lambda a: pl.BlockSpec(a.shape, lambda i: (0,) * a.ndim)
    rowblk = lambda w: pl.BlockSpec((tm, w), lambda i: (i, 0))
    return pl.pallas_call(
        functools.partial(_ssd_in_kernel, nt=nt),
        out_shape=(jax.ShapeDtypeStruct((lx, SSD_INNER), BF16), jax.ShapeDtypeStruct((lx, SSD_INNER), BF16),
                   jax.ShapeDtypeStruct((lx, SSD_BC), F32), jax.ShapeDtypeStruct((lx, SSD_BC), F32),
                   jax.ShapeDtypeStruct((2, lx, LANES), F32)),
        grid=(nt,),
        in_specs=[rowblk(D), prev, nxt, vec, vec, vec, _resident(w_in.shape), _resident(wd.shape),
                  pl.BlockSpec((3, SSD_CONV_DIM), lambda i: (0, 0)),
                  pl.BlockSpec((1, SSD_CONV_DIM), lambda i: (0, 0)),
                  pl.BlockSpec((1, 2 * LANES), lambda i: (0, 0))],
        out_specs=(rowblk(SSD_INNER), rowblk(SSD_INNER), rowblk(SSD_BC), rowblk(SSD_BC),
                   pl.BlockSpec((2, tm, LANES), lambda i: (0, i, 0))),
        compiler_params=_cparams("parallel"),
        name="ssd_in",
    )(x, x, x, _row(g), _row(sh), _row(sc), w_in, wd, conv_w, _row(conv_b), db)


def _expand_heads(arr, e_ref):
    hi = arr.astype(BF16)
    lo = (arr - hi.astype(F32)).astype(BF16)
    return jnp.dot(jnp.concatenate([hi, lo], axis=1), e_ref[...], preferred_element_type=F32)


def _ssd_prologue(dt_ref, a_row, tri, e_ref, need_y):
    dt = dt_ref[0]
    a = dt * a_row
    a_hi = a.astype(BF16)
    r1 = a - a_hi.astype(F32)
    a_mid = r1.astype(BF16)
    a_lo = (r1 - a_mid.astype(F32)).astype(BF16)
    tri_b = tri.astype(BF16)
    acs = jnp.dot(jnp.concatenate([tri_b, tri_b, tri_b], axis=1), jnp.concatenate([a_hi, a_mid, a_lo], axis=0),
                  preferred_element_type=F32)
    total = jnp.sum(a, axis=0, keepdims=True)
    ctx = dict(keep=tri > 0.5, acs=acs)
    ctx["wend_x"] = _expand_heads(jnp.exp(total - acs) * dt, e_ref)
    ctx["etot_x"] = _expand_heads(jnp.broadcast_to(jnp.exp(total), (SUBLANES, LANES)), e_ref)[0:1, :]
    if need_y:
        ctx["eacs_x"] = _expand_heads(jnp.exp(acs), e_ref)
        ctx["acs_t"] = acs.T
        ctx["dt_t"] = dt.T
    return ctx


def _ssd_prepare(ctx, xs_ref, bm_ref, cm_ref, g, need_y):
    q = SSD_CHUNK
    ppg = SSD_HEADS // 2 // SSD_GROUPS
    bg = bm_ref[:, g * SSD_STATE:(g + 1) * SSD_STATE]
    ops = dict(cg=cm_ref[:, g * SSD_STATE:(g + 1) * SSD_STATE].astype(BF16), bgt=bg.T.astype(BF16), xs2=[], xw=[], m2=[])
    if need_y:
        cb = lax.dot_general(ops["cg"], bg.astype(BF16), (((1,), (1,)), ((), ())), preferred_element_type=F32)
        left = lax.broadcasted_iota(jnp.int32, (q, LANES), 1) < SSD_P
    xws = []
    for r in range(ppg):
        pidx = g * ppg + r
        psl = slice(pidx * LANES, (pidx + 1) * LANES)
        xp = xs_ref[:, psl]
        xws.append((xp.astype(F32) * ctx["wend_x"][:, psl]).astype(BF16))
        if need_y:
            ms = []
            for hd in (2 * pidx, 2 * pidx + 1):
                seg = jnp.broadcast_to(ctx["acs"][:, hd:hd + 1], (q, LANES)) - ctx["acs_t"][hd:hd + 1, :]
                lm = jnp.exp(jnp.where(ctx["keep"], seg, -jnp.inf))
                ms.append((cb * lm * ctx["dt_t"][hd:hd + 1, :]).astype(BF16))
            ops["m2"].append(jnp.concatenate(ms, axis=1))
            zero = jnp.zeros_like(xp)
            ops["xs2"].append(jnp.concatenate([jnp.where(left, xp, zero), jnp.where(left, zero, xp)], axis=0))
    ops["xw"] = [jnp.concatenate(xws[2 * t:2 * t + 2], axis=1) for t in range(ppg // 2)]
    return ops


def _ssd_issue(ctx, ops, g, h_scr, y_ref, need_y):
    ppg = SSD_HEADS // 2 // SSD_GROUPS
    for t in range(ppg // 2):
        p0 = g * ppg + 2 * t
        qsl = slice(p0 * LANES, (p0 + 2) * LANES)
        hs = jnp.concatenate([h_scr[p0], h_scr[p0 + 1]], axis=1)
        if need_y:
            yd = jnp.concatenate([jnp.dot(ops["m2"][2 * t + e], ops["xs2"][2 * t + e], preferred_element_type=F32)
                                  for e in range(2)], axis=1)
            yoff = jnp.dot(ops["cg"], hs.astype(BF16), preferred_element_type=F32) * ctx["eacs_x"][:, qsl]
            y_ref[:, qsl] = (yd + yoff).astype(y_ref.dtype)
        st = jnp.dot(ops["bgt"], ops["xw"][t], preferred_element_type=F32)
        hn = hs * ctx["etot_x"][:, qsl] + st
        h_scr[p0] = hn[:, :LANES]
        h_scr[p0 + 1] = hn[:, LANES:]


def _ssd_scan_kernel(xsf_ref, xsb_ref, bmf_ref, bmb_ref, cmf_ref, cmb_ref, dtf_ref, dtb_ref, a_ref, tri_ref, e_ref,
                     h0_ref, *out_refs, nc, need_y):
    yf_ref, yb_ref = out_refs[:2] if need_y else (None, None)
    hfin_ref, h_scr = out_refs[-2:]
    s = pl.program_id(0)

    @pl.when(s == 0)
    def _():
        h_scr[...] = h0_ref[...]

    dirs = ((xsf_ref, bmf_ref, cmf_ref, dtf_ref, yf_ref), (xsb_ref, bmb_ref, cmb_ref, dtb_ref, yb_ref))
    ctxs = [_ssd_prologue(dirs[d][3], a_ref[d], tri_ref[d], e_ref, need_y) for d in range(2)]
    stages = [(d, g) for g in range(SSD_GROUPS) for d in range(2)]
    prep = lambda d, g: _ssd_prepare(ctxs[d], dirs[d][0], dirs[d][1], dirs[d][2], g, need_y)
    pending = prep(*stages[0])
    for idx, (d, g) in enumerate(stages):
        ops = pending
        if idx + 1 < len(stages):
            pending = prep(*stages[idx + 1])
        _ssd_issue(ctxs[d], ops, g, h_scr.at[d], dirs[d][4], need_y)

    @pl.when(s == nc - 1)
    def _():
        hfin_ref[...] = h_scr[...]


def _ssd_scan(xs, bm, cm, dt2, a_log, h0, need_y):
    lx = xs.shape[0]
    q = SSD_CHUNK
    nc = lx // q
    npair = SSD_HEADS // 2
    a = -jnp.exp(a_log.astype(F32))
    a_pad = jnp.pad(a, ((0, 0), (0, LANES - SSD_HEADS))).reshape(2, 1, LANES)
    lower = np.tril(np.ones((q, q), np.float32))
    tri = jnp.asarray(np.stack([lower, lower.T]))
    expand = np.kron(np.eye(LANES)[:, :SSD_HEADS], np.ones((1, SSD_P)))
    expand = jnp.asarray(np.concatenate([expand, expand], axis=0), BF16)
    fwd = lambda s: s
    bwd = lambda s: nc - 1 - s
    rows = lambda w, idx: pl.BlockSpec((q, w), lambda s: (idx(s), 0))
    full = lambda a_: pl.BlockSpec(a_.shape, lambda s: (0,) * a_.ndim)
    y_shapes = [jax.ShapeDtypeStruct((lx, SSD_INNER), BF16)] * 2 if need_y else []
    y_specs = [rows(SSD_INNER, fwd), rows(SSD_INNER, bwd)] if need_y else []
    outs = pl.pallas_call(
        functools.partial(_ssd_scan_kernel, nc=nc, need_y=need_y),
        out_shape=y_shapes + [jax.ShapeDtypeStruct(h0.shape, F32)],
        grid=(nc,),
        in_specs=[
            rows(SSD_INNER, fwd), rows(SSD_INNER, bwd), rows(SSD_BC, fwd), rows(SSD_BC, bwd),
            rows(SSD_BC, fwd), rows(SSD_BC, bwd),
            pl.BlockSpec((1, q, LANES), lambda s: (0, s, 0)),
            pl.BlockSpec((1, q, LANES), lambda s: (1, nc - 1 - s, 0)),
            full(a_pad), full(tri), full(expand), full(h0),
        ],
        out_specs=y_specs + [full(h0)],
        scratch_shapes=[pltpu.VMEM((2, npair, SSD_STATE, 2 * SSD_P), F32)],
        compiler_params=_cparams("arbitrary"),
        name="ssd_scan",
    )(xs, xs, bm, bm, cm, cm, dt2, dt2, a_pad, tri, expand, h0)
    return (tuple(outs[:2]) if need_y else None), outs[-1]


def _ssd_out_kernel(x_ref, yf_ref, yb_ref, xs_ref, zg_ref, dsk_ref, ng_ref, w_ref, gate_ref, o_ref):
    y = yf_ref[...].astype(F32) + yb_ref[...].astype(F32) + xs_ref[...].astype(F32) * dsk_ref[...]
    y = y * _silu(zg_ref[...].astype(F32))
    gw = SSD_INNER // SSD_GROUPS
    parts = []
    for g in range(SSD_GROUPS):
        yg = y[:, g * gw:(g + 1) * gw]
        ms = jnp.mean(yg * yg, axis=-1, keepdims=True)
        parts.append(yg * lax.rsqrt(ms + NORM_EPS) * ng_ref[:, g * gw:(g + 1) * gw])
    yn = jnp.concatenate(parts, axis=1).astype(BF16)
    o_ref[...] = x_ref[...] + gate_ref[...] * jnp.dot(yn, w_ref[...], preferred_element_type=F32)


def _ssd_out(x, yfb, xs, zg, d_skip, norm_g, w_out, gate):
    lx = x.shape[0]
    tm = min(ROW_TILE // 2, lx)
    rowblk = lambda w: pl.BlockSpec((tm, w), lambda i: (i, 0))
    vecw = pl.BlockSpec((1, SSD_INNER), lambda i: (0, 0))
    return pl.pallas_call(
        _ssd_out_kernel,
        out_shape=jax.ShapeDtypeStruct((lx, D), F32),
        grid=(lx // tm,),
        in_specs=[rowblk(D), rowblk(SSD_INNER), rowblk(SSD_INNER), rowblk(SSD_INNER),
                  rowblk(SSD_INNER), vecw, vecw, _resident((SSD_INNER, D)),
                  pl.BlockSpec((1, D), lambda i: (0, 0))],
        out_specs=rowblk(D),
        compiler_params=_cparams("parallel"),
        name="ssd_out",
    )(x, yfb[0], yfb[1], xs, zg, _row(jnp.repeat(d_skip, SSD_P)), _row(norm_g), w_out, _row(gate))


def kernel(x, c, ctx, c_ctx, norm1_g, norm2_g, mod_w, mod_b, ffn_w_in, ffn_w_out, final_g, gm_w_in, gm_ln_g, gm_ln_b, gm_ws, gm_bs, gm_w_out, at_w_qkv, at_q_g, at_k_g, at_w_out, hy_w_in, hy_conv_w, hy_conv_b, hy_filt_w1, hy_filt_b1, hy_filt_w2, hy_filt_b2, hy_filt_w3, hy_filt_freq, hy_skip, hy_w_out, ssd_w_in, ssd_conv_w, ssd_conv_b, ssd_a_log, ssd_dt_bias, ssd_d_skip, ssd_norm_g, ssd_w_out):
    batch, seq, _ = x.shape
    assert batch == 1, "kernels are written for a single sequence"
    nctx = ctx.shape[1]
    xl = x[0]
    z = ctx[0]
    mods = _modulation(c[0], c_ctx, mod_w, mod_b)
    bf = lambda w: w.astype(BF16)

    for i in range(DEPTH):
        m, j = i % 4, i // 4
        want_ctx = i < DEPTH - 1
        ml = [mods[i, 0, k * D:(k + 1) * D] for k in range(6)]
        mc = [mods[i, 1, k * D:(k + 1) * D] for k in range(6)]
        n1 = norm1_g[i]
        pend_l = pend_c = None
        if m == 0:
            p = (bf(gm_w_in[j]), gm_ln_g[j], gm_ln_b[j], bf(gm_ws[j]), gm_bs[j], bf(gm_w_out[j]))
            xl = _gmlp(xl, n1, ml[0], ml[1], ml[2], *p)
            if want_ctx:
                z = _gmlp(z, n1, mc[0], mc[1], mc[2], *p)
        elif m == 1:
            wq, wo = bf(at_w_qkv[j]), bf(at_w_out[j])
            qt_l, k_l, vt_l = _qkv(xl, n1, ml[0], ml[1], wq, at_q_g[j], at_k_g[j], rope=True)
            qt_c, k_c, vt_c = _qkv(z, n1, mc[0], mc[1], wq, at_q_g[j], at_k_g[j], rope=False)
            k_all = jnp.concatenate([k_c, k_l], axis=1)
            vt_all = jnp.concatenate([vt_c, vt_l], axis=1)
            stot = nctx + seq
            ts = next(t for t in FLASH_KV_TILES if stot % t == 0)
            score_bound = (HD ** 0.5 * LOG2E) * jnp.max(jnp.abs(at_q_g[j])) * jnp.max(jnp.abs(at_k_g[j]))
            o_l = lax.cond(score_bound <= FLASH_SCORE_BOUND,
                           lambda: _flash(qt_l, k_all, vt_all, stot, ts, bounded=True),
                           lambda: _flash(qt_l, k_all, vt_all, stot, ts, bounded=False))
            pend_l = ("proj", o_l, wo, ml[2])
            if want_ctx:
                pend_c = ("proj", _flash(qt_c, k_all, vt_all, nctx, nctx), wo, mc[2])
        elif m == 2:
            p = (bf(hy_w_in[j]), hy_conv_w[j], hy_conv_b[j], hy_filt_w1[j], hy_filt_b1[j], hy_filt_w2[j],
                 hy_filt_b2[j], hy_filt_w3[j], hy_filt_freq[j], hy_skip[j], bf(hy_w_out[j]))
            pend_l = _hyena(xl, n1, ml[0], ml[1], ml[2], *p)
            if want_ctx:
                pend_c = _hyena(z, n1, mc[0], mc[1], mc[2], *p)
        else:
            win, wo = bf(ssd_w_in[j]), bf(ssd_w_out[j])
            pin = (win, ssd_conv_w[j], ssd_conv_b[j], ssd_dt_bias[j])
            zg_c, xs_c, bm_c, cm_c, dt_c = _ssd_in(z, n1, mc[0], mc[1], *pin)
            zg_l, xs_l, bm_l, cm_l, dt_l = _ssd_in(xl, n1, ml[0], ml[1], *pin)
            h0 = jnp.zeros((2, SSD_HEADS // 2, SSD_STATE, 2 * SSD_P), F32)
            y_c, h_ctx = _ssd_scan(xs_c, bm_c, cm_c, dt_c, ssd_a_log[j], h0, want_ctx)
            y_l, _ = _ssd_scan(xs_l, bm_l, cm_l, dt_l, ssd_a_log[j], h_ctx, True)
            xl = _ssd_out(xl, y_l, xs_l, zg_l, ssd_d_skip[j], ssd_norm_g[j], wo, ml[2])
            if want_ctx:
                z = _ssd_out(z, y_c, xs_c, zg_c, ssd_d_skip[j], ssd_norm_g[j], wo, mc[2])
        wi, wo2 = bf(ffn_w_in[i]), bf(ffn_w_out[i])
        xl = _ffn(xl, pend_l, norm2_g[i], ml[3], ml[4], ml[5], wi, wo2, final_g, final=(i == DEPTH - 1))
        if want_ctx:
            z = _ffn(z, pend_c, norm2_g[i], mc[3], mc[4], mc[5], wi, wo2, final_g, final=False)
    return xl[None]
```

```python
import functools
import math

import numpy as np
import jax
import jax.numpy as jnp
from jax import lax
from jax.experimental import pallas as pl
from jax.experimental.pallas import tpu as pltpu

F32 = jnp.float32
BF16 = jnp.bfloat16
HIGHEST = lax.Precision.HIGHEST

D = 1024
DEPTH = 4
GRID_W = 64
NORM_EPS = 1e-6
FFN_HIDDEN = 2816
GM_CHUNK = 128
GM_WIDTH = 2 * D
GM_GROUPS = 8
GM_GW = GM_WIDTH // GM_GROUPS
HD = 64
QH = D // HD
KVH = 4
ROPE_THETA = 10000.0
LOG2E = math.log2(math.e)
FLASH_SCORE_BOUND = 64.0
FLASH_LOOKAHEAD = 2
HY_BANDS = 16
HY_EMB = 1 + 2 * HY_BANDS
HY_FILT_W = 64
HY_MAX_DECAY = math.log(1e-2) / 0.3
HY_MIN_DECAY = math.log(1e-2) / 1.5
SSD_INNER = 2 * D
SSD_P = 64
SSD_HEADS = SSD_INNER // SSD_P
SSD_GROUPS = 4
SSD_STATE = 128
SSD_CHUNK = 128
SSD_BC = SSD_GROUPS * SSD_STATE
SSD_CONV_DIM = SSD_INNER + 2 * SSD_BC

LANES = 128
SUBLANES = 8
VMEM_LIMIT_BYTES = 56 * 1024 * 1024
DFT_N2 = LANES
ROW_TILE = 512
FLASH_Q_TILE = 128
FLASH_KV_TILES = (3328, 1280, 1024, 512, 256)
FLASH_CHUNK = 2 * LANES


def _cparams(*sem):
    return pltpu.CompilerParams(dimension_semantics=sem, vmem_limit_bytes=VMEM_LIMIT_BYTES)


def _row(v):
    return v.reshape(1, -1)


def _normmod(x, g, shift, scale):
    ms = jnp.mean(x * x, axis=-1, keepdims=True)
    return x * lax.rsqrt(ms + NORM_EPS) * g * (1.0 + scale) + shift


def _silu(x):
    return x * jax.nn.sigmoid(x)


def _mod_kernel(cl_ref, cc_ref, w_ref, b_ref, o_ref):
    w = w_ref[0]
    for r, c_ref in enumerate((cl_ref, cc_ref)):
        a = _silu(c_ref[...])
        o_ref[0, r:r + 1, :] = jnp.sum(a * w, axis=0, keepdims=True) + b_ref[0]


def _modulation(c, c_ctx, mod_w, mod_b):
    n6 = 6 * D
    tn = n6 // 4
    depth = mod_w.shape[0]
    return pl.pallas_call(
        _mod_kernel,
        out_shape=jax.ShapeDtypeStruct((depth, 2, n6), F32),
        grid=(depth, n6 // tn),
        in_specs=[
            pl.BlockSpec((D, 1), lambda i, n: (0, 0)),
            pl.BlockSpec((D, 1), lambda i, n: (0, 0)),
            pl.BlockSpec((1, D, tn), lambda i, n: (i, 0, n)),
            pl.BlockSpec((1, 1, tn), lambda i, n: (i, 0, n)),
        ],
        out_specs=pl.BlockSpec((1, 2, tn), lambda i, n: (i, 0, n)),
        compiler_params=_cparams("parallel", "parallel"),
        name="modulation",
    )(c.reshape(D, 1), c_ctx.reshape(D, 1), mod_w, mod_b.reshape(depth, 1, n6))


def _ffn_kernel(x_ref, *refs, mode, final):
    x = x_ref[...]
    if mode == "proj":
        a_ref, wm_ref, g1_ref = refs[:3]
        refs = refs[3:]
        x = x + g1_ref[...] * jnp.dot(a_ref[...], wm_ref[...], preferred_element_type=F32)
    elif mode == "hyena":
        x0_ref, yt_ref, wm_ref, g1_ref = refs[:4]
        refs = refs[4:]
        a = (x0_ref[...] * yt_ref[...].T).astype(BF16)
        x = x + g1_ref[...] * jnp.dot(a, wm_ref[...], preferred_element_type=F32)
    g_ref, sh_ref, sc_ref, gate_ref, wi_ref, wo_ref, fg_ref, o_ref = refs
    h = _normmod(x, g_ref[...], sh_ref[...], sc_ref[...]).astype(BF16)
    a = jnp.dot(h, wi_ref[:, :FFN_HIDDEN], preferred_element_type=F32)
    u = jnp.dot(h, wi_ref[:, FFN_HIDDEN:], preferred_element_type=F32)
    act = (_silu(a) * u).astype(BF16)
    y = x + gate_ref[...] * jnp.dot(act, wo_ref[...], preferred_element_type=F32)
    if final:
        ms = jnp.mean(y * y, axis=-1, keepdims=True)
        y = y * lax.rsqrt(ms + NORM_EPS) * fg_ref[...]
    o_ref[...] = y


def _resident(shape):
    return pl.BlockSpec(shape, lambda *_: (0,) * len(shape), pipeline_mode=pl.Buffered(1))


def _ffn(x, pending, g, sh, sc, gate, w_in, w_out, final_g, final):
    lx = x.shape[0]
    tm = min(ROW_TILE, lx)
    vec = pl.BlockSpec((1, D), lambda i: (0, 0))
    rows = pl.BlockSpec((tm, D), lambda i: (i, 0))
    mode, pre_args, pre_specs = "none", (), []
    if pending is not None:
        mode = pending[0]
        if mode == "proj":
            _, a, wm, g1 = pending
            pre_args, pre_specs = (a, wm, _row(g1)), [rows, _resident(wm.shape), vec]
        else:
            _, x0, yt, wm, g1 = pending
            pre_args = (x0, yt, wm, _row(g1))
            pre_specs = [rows, pl.BlockSpec((D, tm), lambda i: (0, i)), _resident(wm.shape), vec]
    return pl.pallas_call(
        functools.partial(_ffn_kernel, mode=mode, final=final),
        out_shape=jax.ShapeDtypeStruct((lx, D), F32),
        grid=(lx // tm,),
        in_specs=[rows] + pre_specs + [
            vec, vec, vec, vec,
            _resident((D, 2 * FFN_HIDDEN)),
            _resident((FFN_HIDDEN, D)),
            vec,
        ],
        out_specs=rows,
        compiler_params=_cparams("parallel"),
        name="ffn",
    )(x, *pre_args, _row(g), _row(sh), _row(sc), _row(gate), w_in, w_out, _row(final_g))


def _gmlp_kernel(x_ref, g_ref, sh_ref, sc_ref, gate_ref, win_ref, lng_ref, lnb_ref, ws_ref, bs_ref, wout_ref,
                 o_ref, *, tm):
    nsub = max(1, tm // (2 * GM_CHUNK))
    rs = tm // nsub

    def project(i):
        x = x_ref[i * rs:(i + 1) * rs, :]
        h = _normmod(x, g_ref[...], sh_ref[...], sc_ref[...]).astype(BF16)
        return jnp.dot(h, win_ref[...], preferred_element_type=F32)

    def mix(i, t):
        t = 0.5 * t * (1.0 + lax.erf(t * (1.0 / math.sqrt(2.0))))
        u = t[:, :GM_WIDTH]
        v = t[:, GM_WIDTH:]
        mu = jnp.mean(v, axis=-1, keepdims=True)
        vc = v - mu
        var = jnp.mean(vc * vc, axis=-1, keepdims=True)
        v = (vc * lax.rsqrt(var + NORM_EPS) * lng_ref[...] + lnb_ref[...]).astype(BF16)
        rows = []
        for q in range(rs // GM_CHUNK):
            cols = []
            for gidx in range(GM_GROUPS):
                vq = v[q * GM_CHUNK:(q + 1) * GM_CHUNK, gidx * GM_GW:(gidx + 1) * GM_GW]
                bias = bs_ref[gidx]
                m = jnp.dot(ws_ref[gidx], vq, preferred_element_type=F32)
                cols.append(m + jnp.concatenate([bias] * (GM_GW // LANES), axis=1))
            rows.append(jnp.concatenate(cols, axis=1))
        gated = (u * jnp.concatenate(rows, axis=0)).astype(BF16)
        y = jnp.dot(gated, wout_ref[...], preferred_element_type=F32)
        o_ref[i * rs:(i + 1) * rs, :] = x_ref[i * rs:(i + 1) * rs, :] + gate_ref[...] * y

    t_next = project(0)
    for i in range(nsub):
        t = t_next
        if i + 1 < nsub:
            t_next = project(i + 1)
        mix(i, t)


def _gmlp(x, g, sh, sc, gate, w_in, ln_g, ln_b, ws, bs, w_out):
    lx = x.shape[0]
    tm = min(ROW_TILE, lx)
    vec = pl.BlockSpec((1, D), lambda i: (0, 0))
    vecw = pl.BlockSpec((1, GM_WIDTH), lambda i: (0, 0))
    bsb = jnp.broadcast_to(bs[:, :, None], (GM_GROUPS, GM_CHUNK, LANES))
    return pl.pallas_call(
        functools.partial(_gmlp_kernel, tm=tm),
        out_shape=jax.ShapeDtypeStruct((lx, D), F32),
        grid=(lx // tm,),
        in_specs=[
            pl.BlockSpec((tm, D), lambda i: (i, 0)),
            vec, vec, vec, vec,
            _resident((D, 2 * GM_WIDTH)),
            vecw, vecw,
            pl.BlockSpec((GM_GROUPS, GM_CHUNK, GM_CHUNK), lambda i: (0, 0, 0)),
            pl.BlockSpec((GM_GROUPS, GM_CHUNK, LANES), lambda i: (0, 0, 0)),
            _resident((GM_WIDTH, D)),
        ],
        out_specs=pl.BlockSpec((tm, D), lambda i: (i, 0)),
        compiler_params=_cparams("parallel"),
        name="gmlp",
    )(x, _row(g), _row(sh), _row(sc), _row(gate), w_in, _row(ln_g), _row(ln_b), ws, bsb, w_out)


def _group_sumsq(t, e_ref):
    sq = t * t
    hi = sq.astype(BF16)
    lo = (sq - hi.astype(F32)).astype(BF16)
    e = e_ref[...]
    w = e.shape[0]
    outs = []
    for j in range(t.shape[1] // w):
        sl = slice(j * w, (j + 1) * w)
        outs.append(jnp.dot(hi[:, sl], e, preferred_element_type=F32) + jnp.dot(lo[:, sl], e, preferred_element_type=F32))
    return jnp.concatenate(outs, axis=1)


def _rope(t, cosf, sinf):
    w = t.shape[1]
    lane = lax.broadcasted_iota(jnp.int32, t.shape, 1)
    first = (lane % HD) < (HD // 2)
    partner = jnp.where(first, pltpu.roll(t, w - HD // 2, axis=1), pltpu.roll(t, HD // 2, axis=1))
    reps = w // LANES
    c = jnp.concatenate([cosf] * reps, axis=1)
    s = jnp.concatenate([sinf] * reps, axis=1)
    return t * c + partner * s


def _qkv_kernel(x_ref, g_ref, sh_ref, sc_ref, w_ref, qg_ref, kg_ref, e_ref, cos_ref, sin_ref,
                qt_ref, k_ref, vt_ref, *, rope):
    h = _normmod(x_ref[...], g_ref[...], sh_ref[...], sc_ref[...]).astype(BF16)
    qkv = jnp.dot(h, w_ref[...], preferred_element_type=F32)
    q = qkv[:, :D]
    k = qkv[:, D:D + KVH * HD]
    v = qkv[:, D + KVH * HD:]
    q = q * lax.rsqrt(_group_sumsq(q, e_ref) * (1.0 / HD) + NORM_EPS) * qg_ref[...]
    k = k * lax.rsqrt(_group_sumsq(k, e_ref) * (1.0 / HD) + NORM_EPS) * kg_ref[...]
    if rope:
        q = _rope(q, cos_ref[...], sin_ref[...])
        k = _rope(k, cos_ref[...], sin_ref[...])
    qt_ref[...] = (q * (HD ** -0.5 * LOG2E)).T.astype(BF16)
    for gidx in range(KVH):
        k_ref[gidx] = k[:, gidx * HD:(gidx + 1) * HD].astype(BF16)
    vt_ref[...] = v.T.astype(BF16)


def _qkv(x, g, sh, sc, w_qkv, q_g, k_g, rope):
    lx = x.shape[0]
    tm = min(ROW_TILE, lx)
    vec = pl.BlockSpec((1, D), lambda i: (0, 0))
    kvw = KVH * HD
    rows = lx // GRID_W
    row = jnp.repeat(jnp.arange(rows, dtype=F32), GRID_W)
    col = jnp.tile(jnp.arange(GRID_W, dtype=F32), rows)
    n = HD // 4
    inv = ROPE_THETA ** (-jnp.arange(n, dtype=F32) / n)
    ang = jnp.concatenate([row[:, None] * inv, col[:, None] * inv], axis=-1)
    cos, sin = jnp.cos(ang), jnp.sin(ang)
    cosf = jnp.tile(jnp.concatenate([cos, cos], axis=-1), (1, LANES // HD))
    sinf = jnp.tile(jnp.concatenate([-sin, sin], axis=-1), (1, LANES // HD))
    eblk = jnp.asarray(np.kron(np.eye(kvw // HD), np.ones((HD, HD))), BF16)
    tab = pl.BlockSpec((tm, LANES), lambda i: (i, 0))
    return pl.pallas_call(
        functools.partial(_qkv_kernel, rope=rope),
        out_shape=(jax.ShapeDtypeStruct((D, lx), BF16),
                   jax.ShapeDtypeStruct((KVH, lx, HD), BF16),
                   jax.ShapeDtypeStruct((kvw, lx), BF16)),
        grid=(lx // tm,),
        in_specs=[
            pl.BlockSpec((tm, D), lambda i: (i, 0)),
            vec, vec, vec,
            _resident((D, D + 2 * kvw)),
            vec,
            pl.BlockSpec((1, kvw), lambda i: (0, 0)),
            pl.BlockSpec((kvw, kvw), lambda i: (0, 0)),
            tab, tab,
        ],
        out_specs=(pl.BlockSpec((D, tm), lambda i: (0, i)),
                   pl.BlockSpec((KVH, tm, HD), lambda i: (0, i, 0)),
                   pl.BlockSpec((kvw, tm), lambda i: (0, i))),
        compiler_params=_cparams("parallel"),
        name="qkv_proj",
    )(x, _row(g), _row(sh), _row(sc), w_qkv, _row(jnp.tile(q_g, QH)), _row(jnp.tile(k_g, KVH)), eblk, cosf, sinf)


def _flash_kernel(qt_ref, k_ref, vt_ref, o_ref, qg_scr, m_scr, l_scr, acc_scr, *, tq, ts, tc, nkv, bounded):
    j = pl.program_id(1)
    gq = QH // KVH
    mcols = gq * tq

    @pl.when(j == 0)
    def _():
        for h in range(QH):
            qg_scr[h // gq, :, (h % gq) * tq:(h % gq + 1) * tq] = qt_ref[h * HD:(h + 1) * HD, :]
        m_scr[...] = jnp.full(m_scr.shape, -jnp.inf, F32)
        l_scr[...] = jnp.zeros_like(l_scr)
        acc_scr[...] = jnp.zeros_like(acc_scr)

    stages = [(g, c) for c in range(ts // tc) for g in range(KVH)]

    def scores(g, c):
        return jnp.dot(k_ref[g, c * tc:(c + 1) * tc, :], qg_scr[g], preferred_element_type=F32)

    pending = [scores(*st) for st in stages[:FLASH_LOOKAHEAD]]
    for idx, (g, c) in enumerate(stages):
        s = pending.pop(0)
        if idx + FLASH_LOOKAHEAD < len(stages):
            pending.append(scores(*stages[idx + FLASH_LOOKAHEAD]))
        vt = vt_ref[g * HD:(g + 1) * HD, c * tc:(c + 1) * tc]
        if bounded:
            p = jnp.exp2(s)
            l_scr[g] += jnp.sum(p, axis=0, keepdims=True)
            acc_scr[g] += jnp.dot(vt, p.astype(BF16), preferred_element_type=F32)
        else:
            m_prev = m_scr[g]
            m_new = jnp.maximum(m_prev, jnp.max(s, axis=0, keepdims=True))
            alpha = jnp.exp2(m_prev - m_new)
            p = jnp.exp2(s - m_new)
            l_scr[g] = alpha * l_scr[g] + jnp.sum(p, axis=0, keepdims=True)
            acc_scr[g] = alpha * acc_scr[g] + jnp.dot(vt, p.astype(BF16), preferred_element_type=F32)
            m_scr[g] = m_new

    @pl.when(j == nkv - 1)
    def _():
        rows = []
        for g in range(KVH):
            o = acc_scr[g] / l_scr[g]
            rows += [o[:, r * tq:(r + 1) * tq] for r in range(gq)]
        o_ref[...] = jnp.concatenate(rows, axis=0).T.astype(o_ref.dtype)


def _flash(qt, k, vt, s_len, ts, bounded=False):
    lq = qt.shape[1]
    tq = min(FLASH_Q_TILE, lq)
    nkv = s_len // ts
    gq = QH // KVH
    kvw = KVH * HD
    tc = FLASH_CHUNK if ts % FLASH_CHUNK == 0 else LANES
    return pl.pallas_call(
        functools.partial(_flash_kernel, tq=tq, ts=ts, tc=tc, nkv=nkv, bounded=bounded),
        out_shape=jax.ShapeDtypeStruct((lq, D), BF16),
        grid=(lq // tq, nkv),
        in_specs=[
            pl.BlockSpec((D, tq), lambda i, j: (0, i)),
            pl.BlockSpec((KVH, ts, HD), lambda i, j: (0, j, 0)),
            pl.BlockSpec((kvw, ts), lambda i, j: (0, j)),
        ],
        out_specs=pl.BlockSpec((tq, D), lambda i, j: (i, 0)),
        scratch_shapes=[
            pltpu.VMEM((KVH, HD, gq * tq), BF16),
            pltpu.VMEM((KVH, 1, gq * tq), F32),
            pltpu.VMEM((KVH, 1, gq * tq), F32),
            pltpu.VMEM((KVH, HD, gq * tq), F32),
        ],
        compiler_params=_cparams("parallel", "arbitrary"),
        name="flash_attn",
    )(qt, k, vt)


def _halo_specs(tm, lx):
    nb = lx // SUBLANES
    step = tm // SUBLANES
    prev = pl.BlockSpec((SUBLANES, D), lambda i: (jnp.maximum(i * step - 1, 0), 0))
    nxt = pl.BlockSpec((SUBLANES, D), lambda i: (jnp.minimum((i + 1) * step, nb - 1), 0))
    return prev, nxt


def _conv3(p_main, p_halo, cw, cb, first, last):
    tm = p_main.shape[0]
    rid = lax.broadcasted_iota(jnp.int32, p_main.shape, 0)
    before = jnp.where(first, 0.0, p_halo[SUBLANES - 1:SUBLANES, :])
    after = jnp.where(last, 0.0, p_halo[SUBLANES:SUBLANES + 1, :])
    up = jnp.where(rid == 0, before, pltpu.roll(p_main, 1, axis=0))
    dn = jnp.where(rid == tm - 1, after, pltpu.roll(p_main, tm - 1, axis=0))
    return cw[0:1, :] * up + cw[1:2, :] * p_main + cw[2:3, :] * dn + cb


def _norm_halo(xm_ref, xp_ref, xn_ref, g_ref, sh_ref, sc_ref):
    g, sh, sc = g_ref[...], sh_ref[...], sc_ref[...]
    h = _normmod(xm_ref[...], g, sh, sc).astype(BF16)
    hh = jnp.concatenate([_normmod(xp_ref[...], g, sh, sc), _normmod(xn_ref[...], g, sh, sc)], axis=0).astype(BF16)
    return h, hh


def _hy_in_kernel(xm_ref, xp_ref, xn_ref, g_ref, sh_ref, sc_ref, w_ref, cw_ref, cb_ref, x0_ref, ut_ref, *, nt):
    i = pl.program_id(0)
    first, last = i == 0, i == nt - 1
    h, hh = _norm_halo(xm_ref, xp_ref, xn_ref, g_ref, sh_ref, sc_ref)

    def project(b):
        sl = slice(b * D, (b + 1) * D)
        return (jnp.dot(h, w_ref[:, sl], preferred_element_type=F32),
                jnp.dot(hh, w_ref[:, sl], preferred_element_type=F32))

    def conv(b, p):
        sl = slice(b * D, (b + 1) * D)
        return _conv3(p[0], p[1], cw_ref[:, sl], cb_ref[:, sl], first, last)

    p1 = project(1)
    p2 = project(2)
    x1 = conv(1, p1)
    p0 = project(0)
    ut_ref[...] = (x1 * conv(2, p2)).T.astype(ut_ref.dtype)
    x0_ref[...] = conv(0, p0)


def _hy_in(x, g, sh, sc, w_in, conv_w, conv_b):
    lx = x.shape[0]
    tm = min(ROW_TILE, lx)
    nt = lx // tm
    vec = pl.BlockSpec((1, D), lambda i: (0, 0))
    prev, nxt = _halo_specs(tm, lx)
    return pl.pallas_call(
        functools.partial(_hy_in_kernel, nt=nt),
        out_shape=(jax.ShapeDtypeStruct((lx, D), F32), jax.ShapeDtypeStruct((D, lx), BF16)),
        grid=(nt,),
        in_specs=[
            pl.BlockSpec((tm, D), lambda i: (i, 0)), prev, nxt,
            vec, vec, vec,
            _resident((D, 3 * D)),
            pl.BlockSpec((3, 3 * D), lambda i: (0, 0)),
            pl.BlockSpec((1, 3 * D), lambda i: (0, 0)),
        ],
        out_specs=(pl.BlockSpec((tm, D), lambda i: (i, 0)), pl.BlockSpec((D, tm), lambda i: (0, i))),
        compiler_params=_cparams("parallel"),
        name="hyena_in",
    )(x, x, x, _row(g), _row(sh), _row(sc), w_in, conv_w, _row(conv_b))


def _hy_filter_kernel(ft_ref, t_ref, w1_ref, b1_ref, w2_ref, b2_ref, w3c_ref, fr_ref, dl_ref, sk_ref, kt_ref,
                      *, tm, ltrue, lpad):
    i = pl.program_id(0)
    feats = ft_ref[...]
    fr = fr_ref[...]
    hid = jnp.sin(fr * (jnp.dot(w1_ref[...], feats, preferred_element_type=F32, precision=HIGHEST) + b1_ref[...]))
    hid = jnp.sin(fr * (jnp.dot(w2_ref[...], hid, preferred_element_type=F32, precision=HIGHEST) + b2_ref[...]))
    hh = hid.astype(BF16)
    hl = (hid - hh.astype(F32)).astype(BF16)
    kt = jnp.dot(w3c_ref[...], jnp.concatenate([hh, hl, hh], axis=0), preferred_element_type=F32)
    kt = kt * jnp.exp(-dl_ref[...] * t_ref[...])
    if lpad != ltrue:
        pos = lax.broadcasted_iota(jnp.int32, kt.shape, 1) + i * tm
        kt = jnp.where(pos < ltrue, kt, 0.0)
    kt_ref[...] = kt.astype(kt_ref.dtype)

    @pl.when(i == 0)
    def _():
        row = lax.broadcasted_iota(jnp.int32, (2 * D, 1), 0)
        kt_ref[:, 0:1] = jnp.where(row < D, kt[:, 0:1] + sk_ref[...], 0.0).astype(kt_ref.dtype)


def _hy_filter(ltrue, lpad, w1, b1, w2, b2, w3, freq, skip):
    tm = min(ROW_TILE, lpad)
    col = lambda v_: v_.reshape(-1, 1)
    w1t = jnp.pad(w1.T, ((0, 0), (0, LANES - HY_EMB)))
    deltas = jnp.abs(jnp.linspace(HY_MIN_DECAY, HY_MAX_DECAY, D, dtype=F32))
    pos = jnp.arange(lpad, dtype=F32)
    zf = jnp.linspace(1e-4, HY_BANDS - 1, HY_BANDS, dtype=F32)[:, None] * (2.0 * math.pi * pos / ltrue)[None, :]
    feats = jnp.concatenate([(pos / (ltrue - 1))[None, :], jnp.cos(zf), -jnp.sin(zf),
                             jnp.zeros((LANES - HY_EMB, lpad), F32)], axis=0)
    w3t = w3.T
    w3h = w3t.astype(BF16)
    w3l = (w3t - w3h.astype(F32)).astype(BF16)
    w3c = jnp.concatenate([w3h, w3h, w3l], axis=1)
    full = lambda a: pl.BlockSpec(a.shape, lambda i: (0,) * a.ndim)
    args = (w1t, col(b1), w2.T, col(b2), w3c, col(freq), col(jnp.tile(deltas, 2)),
            col(jnp.concatenate([skip, jnp.zeros((D,), F32)])))
    return pl.pallas_call(
        functools.partial(_hy_filter_kernel, tm=tm, ltrue=ltrue, lpad=lpad),
        out_shape=jax.ShapeDtypeStruct((2 * D, lpad), BF16),
        grid=(lpad // tm,),
        in_specs=[pl.BlockSpec((LANES, tm), lambda i: (0, i)), pl.BlockSpec((1, tm), lambda i: (0, i))]
                 + [full(a) for a in args],
        out_specs=pl.BlockSpec((2 * D, tm), lambda i: (0, i)),
        compiler_params=_cparams("parallel"),
        name="hyena_filter",
    )(feats, feats[0:1, :], *args)


def _dft_consts(nh):
    n1 = 2 * nh
    n = n1 * DFT_N2
    k1 = np.arange(n1)[:, None].astype(np.float64)
    a1 = 2.0 * np.pi * k1 * np.arange(nh)[None, :] / n1
    f1 = np.concatenate([np.cos(a1), -np.sin(a1)], axis=0)
    at = 2.0 * np.pi * ((np.arange(n1)[:, None] * np.arange(DFT_N2)[None, :]) % n) / n
    a2 = 2.0 * np.pi * ((np.arange(DFT_N2)[:, None] * np.arange(DFT_N2)[None, :]) % DFT_N2) / DFT_N2
    c2, s2 = np.cos(a2), np.sin(a2)
    f2 = np.block([[c2, -s2], [s2, c2]])
    g2 = np.block([[c2, s2], [-s2, c2]])
    g1 = np.concatenate([np.cos(a1).T, -np.sin(a1).T], axis=1) / n
    mxu = lambda a: jnp.asarray(a.astype(BF16))
    return mxu(f1), jnp.asarray(np.cos(at), F32), jnp.asarray(np.sin(at), F32), mxu(f2), mxu(g2), mxu(g1)


def _bf16_dot(a, b):
    return jnp.dot(a.astype(BF16), b.astype(BF16), preferred_element_type=F32)


def _hy_conv_kernel(x_ref, kf_ref, kb_ref, f1_ref, twc_ref, tws_ref, f2_ref, g2_ref, g1_ref, o_ref, *, cb, n1):
    twc, tws = twc_ref[...], tws_ref[...]
    f1, f2, g2, g1 = f1_ref[...], f2_ref[...], g2_ref[...], g1_ref[...]
    hc = cb // 2
    nt = 3 * hc

    def dft1(half):
        xs = [r[half * hc + c] for r in (x_ref, kf_ref, kb_ref) for c in range(hc)]
        return _bf16_dot(f1, jnp.concatenate(xs, axis=1))

    def twiddle_rows(a):
        rows = []
        for t in range(nt):
            ar = a[:n1, t * DFT_N2:(t + 1) * DFT_N2]
            ai = a[n1:, t * DFT_N2:(t + 1) * DFT_N2]
            rows.append(jnp.concatenate([ar * twc + ai * tws, ai * twc - ar * tws], axis=1))
        return jnp.concatenate(rows, axis=0)

    def dft2(rows):
        return _bf16_dot(rows, f2)

    def product(spec_all):
        rows = hc * n1
        spec, hf, hb = spec_all[:rows], spec_all[rows:2 * rows], spec_all[2 * rows:]
        hr = hf[:, :DFT_N2] + hb[:, :DFT_N2]
        hi = hf[:, DFT_N2:] - hb[:, DFT_N2:]
        xr, xi = spec[:, :DFT_N2], spec[:, DFT_N2:]
        return jnp.concatenate([xr * hr - xi * hi, xr * hi + xi * hr], axis=1)

    def idft2(y):
        return _bf16_dot(y, g2)

    def twiddle_cols(b):
        cols = []
        for c in range(hc):
            br = b[c * n1:(c + 1) * n1, :DFT_N2]
            bi = b[c * n1:(c + 1) * n1, DFT_N2:]
            cols.append(jnp.concatenate([br * twc - bi * tws, bi * twc + br * tws], axis=0))
        return jnp.concatenate(cols, axis=1)

    def idft1(cols):
        return _bf16_dot(g1, cols)

    def store(half, out):
        for c in range(hc):
            ch = half * hc + c
            for r in range(out.shape[0]):
                o_ref[ch:ch + 1, r * DFT_N2:(r + 1) * DFT_N2] = out[r:r + 1, c * DFT_N2:(c + 1) * DFT_N2]

    a0 = dft1(0)
    a1 = dft1(1)
    s0 = dft2(twiddle_rows(a0))
    s1 = dft2(twiddle_rows(a1))
    b0 = idft2(product(s0))
    b1 = idft2(product(s1))
    o0 = idft1(twiddle_cols(b0))
    o1 = idft1(twiddle_cols(b1))
    store(0, o0)
    store(1, o1)


def _hy_longconv(ut, kt, lpad):
    nh = lpad // DFT_N2
    n1 = 2 * nh
    cb = max(2 * SUBLANES, min(64, 2048 // n1))
    f1, twc, tws, f2, g2, g1 = _dft_consts(nh)
    consts = (f1, twc, tws, f2, g2, g1)
    full = lambda a: pl.BlockSpec(a.shape, lambda i: (0,) * a.ndim)
    k3 = kt.reshape(2 * D, nh, DFT_N2)
    u3 = ut.reshape(D, nh, DFT_N2)
    nb = D // cb
    yt = pl.pallas_call(
        functools.partial(_hy_conv_kernel, cb=cb, n1=n1),
        out_shape=jax.ShapeDtypeStruct((D, lpad), F32),
        grid=(nb,),
        in_specs=[pl.BlockSpec((cb, nh, DFT_N2), lambda i: (i, 0, 0)),
                  pl.BlockSpec((cb, nh, DFT_N2), lambda i: (i, 0, 0)),
                  pl.BlockSpec((cb, nh, DFT_N2), lambda i: (i + nb, 0, 0))]
                 + [full(a) for a in consts],
        out_specs=pl.BlockSpec((cb, lpad), lambda i: (i, 0)),
        compiler_params=_cparams("parallel"),
        name="hyena_longconv",
    )(u3, k3, k3, *consts)
    return yt


def _hyena(x, g, sh, sc, gate, w_in, conv_w, conv_b, w1, b1, w2, b2, w3, freq, skip, w_out):
    lx = x.shape[0]
    lpad = max(lx, SUBLANES * DFT_N2)
    x0, ut = _hy_in(x, g, sh, sc, w_in, conv_w, conv_b)
    if lpad != lx:
        ut = jnp.pad(ut, ((0, 0), (0, lpad - lx)))
    kt = _hy_filter(lx, lpad, w1, b1, w2, b2, w3, freq, skip)
    yt = _hy_longconv(ut, kt, lpad)[:, :lx]
    return ("hyena", x0, yt, w_out, gate)


def _ssd_in_kernel(xm_ref, xp_ref, xn_ref, g_ref, sh_ref, sc_ref, w_ref, wd_ref, cw_ref, cb_ref, db_ref,
                   zg_ref, xs_ref, bm_ref, cm_ref, dt_ref, *, nt):
    i = pl.program_id(0)
    first, last = i == 0, i == nt - 1
    h, hh = _norm_halo(xm_ref, xp_ref, xn_ref, g_ref, sh_ref, sc_ref)
    cw = SSD_BC
    nchunk = SSD_CONV_DIM // cw

    def project(b):
        sl = slice(SSD_INNER + b * cw, SSD_INNER + (b + 1) * cw)
        return (jnp.dot(h, w_ref[:, sl], preferred_element_type=F32),
                jnp.dot(hh, w_ref[:, sl], preferred_element_type=F32))

    def conv(b, p):
        sl = slice(b * cw, (b + 1) * cw)
        y = _silu(_conv3(p[0], p[1], cw_ref[:, sl], cb_ref[:, sl], first, last))
        if (b + 1) * cw <= SSD_INNER:
            xs_ref[:, sl] = y.astype(xs_ref.dtype)
        elif b == nchunk - 2:
            bm_ref[...] = y
        else:
            cm_ref[...] = y

    pending = [project(0), project(1)]
    for b in range(nchunk):
        p = pending.pop(0)
        if b + 2 < nchunk:
            pending.append(project(b + 2))
        conv(b, p)
    zg_ref[...] = jnp.dot(h, w_ref[:, :SSD_INNER], preferred_element_type=F32).astype(zg_ref.dtype)
    dt = jnp.dot(h, wd_ref[...], preferred_element_type=F32) + db_ref[...]
    dt = jnp.maximum(dt, 0.0) + jnp.log1p(jnp.exp(-jnp.abs(dt)))
    lane = lax.broadcasted_iota(jnp.int32, dt.shape, 1)
    dt = jnp.where((lane % LANES) < SSD_HEADS, dt, 0.0)
    dt_ref[0] = dt[:, :LANES]
    dt_ref[1] = dt[:, LANES:]


def _ssd_in(x, g, sh, sc, w_in, conv_w, conv_b, dt_bias):
    lx = x.shape[0]
    tm = min(ROW_TILE, lx)
    nt = lx // tm
    vec = pl.BlockSpec((1, D), lambda i: (0, 0))
    prev, nxt = _halo_specs(tm, lx)
    wdt = w_in[:, SSD_INNER + SSD_CONV_DIM:]
    pad = LANES - SSD_HEADS
    wd = jnp.concatenate([jnp.pad(wdt[:, :SSD_HEADS], ((0, 0), (0, pad))),
                          jnp.pad(wdt[:, SSD_HEADS:], ((0, 0), (0, pad)))], axis=1)
    db = jnp.pad(dt_bias, ((0, 0), (0, pad))).reshape(1, 2 * LANES)
    full = lambda a: pl.BlockSpec(a.shape, lambda i: (0,) * a.ndim)
    rowblk = lambda w: pl.BlockSpec((tm, w), lambda i: (i, 0))
    return pl.pallas_call(
        functools.partial(_ssd_in_kernel, nt=nt),
        out_shape=(jax.ShapeDtypeStruct((lx, SSD_INNER), BF16), jax.ShapeDtypeStruct((lx, SSD_INNER), BF16),
                   jax.ShapeDtypeStruct((lx, SSD_BC), F32), jax.ShapeDtypeStruct((lx, SSD_BC), F32),
                   jax.ShapeDtypeStruct((2, lx, LANES), F32)),
        grid=(nt,),
        in_specs=[rowblk(D), prev, nxt, vec, vec, vec, _resident(w_in.shape), _resident(wd.shape),
                  pl.BlockSpec((3, SSD_CONV_DIM), lambda i: (0, 0)),
                  pl.BlockSpec((1, SSD_CONV_DIM), lambda i: (0, 0)),
                  pl.BlockSpec((1, 2 * LANES), lambda i: (0, 0))],
        out_specs=(rowblk(SSD_INNER), rowblk(SSD_INNER), rowblk(SSD_BC), rowblk(SSD_BC),
                   pl.BlockSpec((2, tm, LANES), lambda i: (0, i, 0))),
        compiler_params=_cparams("parallel"),
        name="ssd_in",
    )(x, x, x, _row(g), _row(sh), _row(sc), w_in, wd, conv_w, _row(conv_b), db)


def _expand_heads(arr, e_ref):
    hi = arr.astype(BF16)
    lo = (arr - hi.astype(F32)).astype(BF16)
    return jnp.dot(jnp.concatenate([hi, lo], axis=1), e_ref[...], preferred_element_type=F32)


def _ssd_prologue(dt_ref, a_row, tri, e_ref, need_y):
    dt = dt_ref[0]
    a = dt * a_row
    a_hi = a.astype(BF16)
    r1 = a - a_hi.astype(F32)
    a_mid = r1.astype(BF16)
    a_lo = (r1 - a_mid.astype(F32)).astype(BF16)
    tri_b = tri.astype(BF16)
    acs = jnp.dot(jnp.concatenate([tri_b, tri_b, tri_b], axis=1), jnp.concatenate([a_hi, a_mid, a_lo], axis=0),
                  preferred_element_type=F32)
    total = jnp.sum(a, axis=0, keepdims=True)
    ctx = dict(keep=tri > 0.5, acs=acs)
    ctx["wend_x"] = _expand_heads(jnp.exp(total - acs) * dt, e_ref)
    ctx["etot_x"] = _expand_heads(jnp.broadcast_to(jnp.exp(total), (SUBLANES, LANES)), e_ref)[0:1, :]
    if need_y:
        ctx["eacs_x"] = _expand_heads(jnp.exp(acs), e_ref)
        ctx["acs_t"] = acs.T
        ctx["dt_t"] = dt.T
    return ctx


def _ssd_prepare(ctx, xs_ref, bm_ref, cm_ref, g, need_y):
    q = SSD_CHUNK
    ppg = SSD_HEADS // 2 // SSD_GROUPS
    bg = bm_ref[:, g * SSD_STATE:(g + 1) * SSD_STATE]
    ops = dict(cg=cm_ref[:, g * SSD_STATE:(g + 1) * SSD_STATE].astype(BF16), bgt=bg.T.astype(BF16), xs2=[], xw=[], m2=[])
    if need_y:
        cb = lax.dot_general(ops["cg"], bg.astype(BF16), (((1,), (1,)), ((), ())), preferred_element_type=F32)
        left = lax.broadcasted_iota(jnp.int32, (q, LANES), 1) < SSD_P
    xws = []
    for r in range(ppg):
        pidx = g * ppg + r
        psl = slice(pidx * LANES, (pidx + 1) * LANES)
        xp = xs_ref[:, psl]
        xws.append((xp.astype(F32) * ctx["wend_x"][:, psl]).astype(BF16))
        if need_y:
            ms = []
            for hd in (2 * pidx, 2 * pidx + 1):
                seg = jnp.broadcast_to(ctx["acs"][:, hd:hd + 1], (q, LANES)) - ctx["acs_t"][hd:hd + 1, :]
                lm = jnp.exp(jnp.where(ctx["keep"], seg, -jnp.inf))
                ms.append((cb * lm * ctx["dt_t"][hd:hd + 1, :]).astype(BF16))
            ops["m2"].append(jnp.concatenate(ms, axis=1))
            zero = jnp.zeros_like(xp)
            ops["xs2"].append(jnp.concatenate([jnp.where(left, xp, zero), jnp.where(left, zero, xp)], axis=0))
    ops["xw"] = [jnp.concatenate(xws[2 * t:2 * t + 2], axis=1) for t in range(ppg // 2)]
    return ops


def _ssd_issue(ctx, ops, g, h_scr, y_ref, need_y):
    ppg = SSD_HEADS // 2 // SSD_GROUPS
    for t in range(ppg // 2):
        p0 = g * ppg + 2 * t
        qsl = slice(p0 * LANES, (p0 + 2) * LANES)
        hs = jnp.concatenate([h_scr[p0], h_scr[p0 + 1]], axis=1)
        if need_y:
            yd = jnp.concatenate([jnp.dot(ops["m2"][2 * t + e], ops["xs2"][2 * t + e], preferred_element_type=F32)
                                  for e in range(2)], axis=1)
            yoff = jnp.dot(ops["cg"], hs.astype(BF16), preferred_element_type=F32) * ctx["eacs_x"][:, qsl]
            y_ref[:, qsl] = (yd + yoff).astype(y_ref.dtype)
        st = jnp.dot(ops["bgt"], ops["xw"][t], preferred_element_type=F32)
        hn = hs * ctx["etot_x"][:, qsl] + st
        h_scr[p0] = hn[:, :LANES]
        h_scr[p0 + 1] = hn[:, LANES:]


def _ssd_scan_kernel(xsf_ref, xsb_ref, bmf_ref, bmb_ref, cmf_ref, cmb_ref, dtf_ref, dtb_ref, a_ref, tri_ref, e_ref,
                     h0_ref, *out_refs, nc, need_y):
    yf_ref, yb_ref = out_refs[:2] if need_y else (None, None)
    hfin_ref, h_scr = out_refs[-2:]
    s = pl.program_id(0)

    @pl.when(s == 0)
    def _():
        h_scr[...] = h0_ref[...]

    dirs = ((xsf_ref, bmf_ref, cmf_ref, dtf_ref, yf_ref), (xsb_ref, bmb_ref, cmb_ref, dtb_ref, yb_ref))
    ctxs = [_ssd_prologue(dirs[d][3], a_ref[d], tri_ref[d], e_ref, need_y) for d in range(2)]
    stages = [(d, g) for g in range(SSD_GROUPS) for d in range(2)]
    prep = lambda d, g: _ssd_prepare(ctxs[d], dirs[d][0], dirs[d][1], dirs[d][2], g, need_y)
    pending = prep(*stages[0])
    for idx, (d, g) in enumerate(stages):
        ops = pending
        if idx + 1 < len(stages):
            pending = prep(*stages[idx + 1])
        _ssd_issue(ctxs[d], ops, g, h_scr.at[d], dirs[d][4], need_y)

    @pl.when(s == nc - 1)
    def _():
        hfin_ref[...] = h_scr[...]


def _ssd_scan(xs, bm, cm, dt2, a_log, h0, need_y):
    lx = xs.shape[0]
    q = SSD_CHUNK
    nc = lx // q
    npair = SSD_HEADS // 2
    a = -jnp.exp(a_log.astype(F32))
    a_pad = jnp.pad(a, ((0, 0), (0, LANES - SSD_HEADS))).reshape(2, 1, LANES)
    lower = np.tril(np.ones((q, q), np.float32))
    tri = jnp.asarray(np.stack([lower, lower.T]))
    expand = np.kron(np.eye(LANES)[:, :SSD_HEADS], np.ones((1, SSD_P)))
    expand = jnp.asarray(np.concatenate([expand, expand], axis=0), BF16)
    fwd = lambda s: s
    bwd = lambda s: nc - 1 - s
    rows = lambda w, idx: pl.BlockSpec((q, w), lambda s: (idx(s), 0))
    full = lambda a_: pl.BlockSpec(a_.shape, lambda s: (0,) * a_.ndim)
    y_shapes = [jax.ShapeDtypeStruct((lx, SSD_INNER), BF16)] * 2 if need_y else []
    y_specs = [rows(SSD_INNER, fwd), rows(SSD_INNER, bwd)] if need_y else []
    outs = pl.pallas_call(
        functools.partial(_ssd_scan_kernel, nc=nc, need_y=need_y),
        out_shape=y_shapes + [jax.ShapeDtypeStruct(h0.shape, F32)],
        grid=(nc,),
        in_specs=[
            rows(SSD_INNER, fwd), rows(SSD_INNER, bwd), rows(SSD_BC, fwd), rows(SSD_BC, bwd),
            rows(SSD_BC, fwd), rows(SSD_BC, bwd),
            pl.BlockSpec((1, q, LANES), lambda s: (0, s, 0)),
            pl.BlockSpec((1, q, LANES), lambda s: (1, nc - 1 - s, 0)),
            full(a_pad), full(tri), full(expand), full(h0),
        ],
        out_specs=y_specs + [full(h0)],
        scratch_shapes=[pltpu.VMEM((2, npair, SSD_STATE, 2 * SSD_P), F32)],
        compiler_params=_cparams("arbitrary"),
        name="ssd_scan",
    )(xs, xs, bm, bm, cm, cm, dt2, dt2, a_pad, tri, expand, h0)
    return (tuple(outs[:2]) if need_y else None), outs[-1]


def _ssd_out_kernel(x_ref, yf_ref, yb_ref, xs_ref, zg_ref, dsk_ref, ng_ref, w_ref, gate_ref, o_ref):
    gw = SSD_INNER // SSD_GROUPS
    acc = None
    for g in range(SSD_GROUPS):
        sl = slice(g * gw, (g + 1) * gw)
        yg = yf_ref[:, sl].astype(F32) + yb_ref[:, sl].astype(F32) + xs_ref[:, sl].astype(F32) * dsk_ref[:, sl]
        yg = yg * _silu(zg_ref[:, sl].astype(F32))
        ms = jnp.mean(yg * yg, axis=-1, keepdims=True)
        yn = (yg * lax.rsqrt(ms + NORM_EPS) * ng_ref[:, sl]).astype(BF16)
        part = jnp.dot(yn, w_ref[sl, :], preferred_element_type=F32)
        acc = part if acc is None else acc + part
    o_ref[...] = x_ref[...] + gate_ref[...] * acc


def _ssd_out(x, yfb, xs, zg, d_skip, norm_g, w_out, gate):
    lx = x.shape[0]
    tm = min(ROW_TILE, lx)
    rowblk = lambda w: pl.BlockSpec((tm, w), lambda i: (i, 0))
    vecw = pl.BlockSpec((1, SSD_INNER), lambda i: (0, 0))
    return pl.pallas_call(
        _ssd_out_kernel,
        out_shape=jax.ShapeDtypeStruct((lx, D), F32),
        grid=(lx // tm,),
        in_specs=[rowblk(D), rowblk(SSD_INNER), rowblk(SSD_INNER), rowblk(SSD_INNER),
                  rowblk(SSD_INNER), vecw, vecw, _resident((SSD_INNER, D)),
                  pl.BlockSpec((1, D), lambda i: (0, 0))],
        out_specs=rowblk(D),
        compiler_params=_cparams("parallel"),
        name="ssd_out",
    )(x, yfb[0], yfb[1], xs, zg, _row(jnp.repeat(d_skip, SSD_P)), _row(norm_g), w_out, _row(gate))


def kernel(x, c, ctx, c_ctx, norm1_g, norm2_g, mod_w, mod_b, ffn_w_in, ffn_w_out, final_g, gm_w_in, gm_ln_g, gm_ln_b, gm_ws, gm_bs, gm_w_out, at_w_qkv, at_q_g, at_k_g, at_w_out, hy_w_in, hy_conv_w, hy_conv_b, hy_filt_w1, hy_filt_b1, hy_filt_w2, hy_filt_b2, hy_filt_w3, hy_filt_freq, hy_skip, hy_w_out, ssd_w_in, ssd_conv_w, ssd_conv_b, ssd_a_log, ssd_dt_bias, ssd_d_skip, ssd_norm_g, ssd_w_out):
    batch, seq, _ = x.shape
    assert batch == 1, "kernels are written for a single sequence"
    nctx = ctx.shape[1]
    xl = x[0]
    z = ctx[0]
    mods = _modulation(c[0], c_ctx, mod_w, mod_b)
    bf = lambda w: w.astype(BF16)

    for i in range(DEPTH):
        m, j = i % 4, i // 4
        want_ctx = i < DEPTH - 1
        ml = [mods[i, 0, k * D:(k + 1) * D] for k in range(6)]
        mc = [mods[i, 1, k * D:(k + 1) * D] for k in range(6)]
        n1 = norm1_g[i]
        pend_l = pend_c = None
        if m == 0:
            p = (bf(gm_w_in[j]), gm_ln_g[j], gm_ln_b[j], bf(gm_ws[j]), gm_bs[j], bf(gm_w_out[j]))
            xl = _gmlp(xl, n1, ml[0], ml[1], ml[2], *p)
            if want_ctx:
                z = _gmlp(z, n1, mc[0], mc[1], mc[2], *p)
        elif m == 1:
            wq, wo = bf(at_w_qkv[j]), bf(at_w_out[j])
            qt_l, k_l, vt_l = _qkv(xl, n1, ml[0], ml[1], wq, at_q_g[j], at_k_g[j], rope=True)
            qt_c, k_c, vt_c = _qkv(z, n1, mc[0], mc[1], wq, at_q_g[j], at_k_g[j], rope=False)
            k_all = jnp.concatenate([k_c, k_l], axis=1)
            vt_all = jnp.concatenate([vt_c, vt_l], axis=1)
            stot = nctx + seq
            ts = next(t for t in FLASH_KV_TILES if stot % t == 0)
            score_bound = (HD ** 0.5 * LOG2E) * jnp.max(jnp.abs(at_q_g[j])) * jnp.max(jnp.abs(at_k_g[j]))
            o_l = lax.cond(score_bound <= FLASH_SCORE_BOUND,
                           lambda: _flash(qt_l, k_all, vt_all, stot, ts, bounded=True),
                           lambda: _flash(qt_l, k_all, vt_all, stot, ts, bounded=False))
            pend_l = ("proj", o_l, wo, ml[2])
            if want_ctx:
                pend_c = ("proj", _flash(qt_c, k_all, vt_all, nctx, nctx), wo, mc[2])
        elif m == 2:
            p = (bf(hy_w_in[j]), hy_conv_w[j], hy_conv_b[j], hy_filt_w1[j], hy_filt_b1[j], hy_filt_w2[j],
                 hy_filt_b2[j], hy_filt_w3[j], hy_filt_freq[j], hy_skip[j], bf(hy_w_out[j]))
            pend_l = _hyena(xl, n1, ml[0], ml[1], ml[2], *p)
            if want_ctx:
                pend_c = _hyena(z, n1, mc[0], mc[1], mc[2], *p)
        else:
            win, wo = bf(ssd_w_in[j]), bf(ssd_w_out[j])
            pin = (win, ssd_conv_w[j], ssd_conv_b[j], ssd_dt_bias[j])
            zg_c, xs_c, bm_c, cm_c, dt_c = _ssd_in(z, n1, mc[0], mc[1], *pin)
            zg_l, xs_l, bm_l, cm_l, dt_l = _ssd_in(xl, n1, ml[0], ml[1], *pin)
            h0 = jnp.zeros((2, SSD_HEADS // 2, SSD_STATE, 2 * SSD_P), F32)
            y_c, h_ctx = _ssd_scan(xs_c, bm_c, cm_c, dt_c, ssd_a_log[j], h0, want_ctx)
            y_l, _ = _ssd_scan(xs_l, bm_l, cm_l, dt_l, ssd_a_log[j], h_ctx, True)
            xl = _ssd_out(xl, y_l, xs_l, zg_l, ssd_d_skip[j], ssd_norm_g[j], wo, ml[2])
            if want_ctx:
                z = _ssd_out(z, y_c, xs_c, zg_c, ssd_d_skip[j], ssd_norm_g[j], wo, mc[2])
        wi, wo2 = bf(ffn_w_in[i]), bf(ffn_w_out[i])
        xl = _ffn(xl, pend_l, norm2_g[i], ml[3], ml[4], ml[5], wi, wo2, final_g, final=(i == DEPTH - 1))
        if want_ctx:
            z = _ffn(z, pend_c, norm2_g[i], mc[3], mc[4], mc[5], wi, wo2, final_g, final=False)
    return xl[None]
```
